```python
import math
import jax, jax.numpy as jnp
from jax import lax
import numpy as np

D_MODEL = 1024
BATCH = 16
SEQ = 2048
DEPTH = 2

HEAD_DIM = 64
N_HEADS_SB = 8
N_HEADS_MOBA = 8
W_SB = N_HEADS_SB * HEAD_DIM
W_MOBA = N_HEADS_MOBA * HEAD_DIM
N_BRANCH = 2
IN_COLS = 3 * W_SB + 3 * W_MOBA + N_BRANCH * D_MODEL
SB_Q_BLOCK = 128
MOBA_BLOCK = 256
MOBA_TOPK = 3
MOBA_Q_CHUNK = 16
REL_BUCKETS = 32
REL_MAX_DIST = 128
D_FF = 3584
N_EXPERTS = 8
TOP_K = 2
D_FF_EXPERT = 3584
N_DENSE = (DEPTH + 1) // 2
N_MOE = DEPTH // 2
N_MOD = 6
EPS = 1e-6
NEG_INF = -1e30

kernel_name = 'hybrid_stickbreak_moba_moe_block'


def rms_norm(x, g):
    xf = x.astype(jnp.float32)
    y = xf * lax.rsqrt(jnp.mean(xf * xf, axis=-1, keepdims=True) + EPS)
    return (y * g.astype(jnp.float32)).astype(x.dtype)


def rel_bucket(dist):
    max_exact = REL_BUCKETS // 2
    n = jnp.maximum(dist, 0)
    nf = jnp.maximum(n, 1).astype(jnp.float32)
    large = max_exact + (jnp.log(nf / max_exact) / math.log(REL_MAX_DIST / max_exact)
                         * (REL_BUCKETS - max_exact)).astype(jnp.int32)
    large = jnp.minimum(large, REL_BUCKETS - 1)
    return jnp.where(n < max_exact, n, large)


def stick_breaking_attention(q, k, v):
    B, H, S, Dh = q.shape
    nq = S // SB_Q_BLOCK
    scale = Dh ** -0.5
    kpos = jnp.arange(S)
    qb = q.reshape(B, H, nq, SB_Q_BLOCK, Dh).transpose(2, 0, 1, 3, 4)

    def block(args):
        i, qi = args
        qpos = i * SB_Q_BLOCK + jnp.arange(SB_Q_BLOCK)
        z = jnp.einsum('bhqd,bhkd->bhqk', qi, k).astype(jnp.float32) * scale
        mask = kpos[None, :] < qpos[:, None]
        log_1m = jnp.where(mask, -jax.nn.softplus(z), 0.0)
        suffix = lax.cumsum(log_1m, axis=3, reverse=True) - log_1m
        att = jnp.where(mask, jnp.exp(jax.nn.log_sigmoid(z) + suffix), 0.0)
        return jnp.einsum('bhqk,bhkd->bhqd', att.astype(v.dtype), v)

    out = lax.map(block, (jnp.arange(nq), qb))
    return out.transpose(1, 2, 0, 3, 4).reshape(B, H, S, Dh)


def moba_attention(q, k, v, rel_bias):
    B, H, S, Dh = q.shape
    nb = -(-S // MOBA_BLOCK)
    sp = nb * MOBA_BLOCK
    pad = ((0, 0), (0, 0), (0, sp - S), (0, 0))
    kp = jnp.pad(k, pad)
    vp = jnp.pad(v, pad)
    kblk = kp.reshape(B, H, nb, MOBA_BLOCK, Dh)
    vblk = vp.reshape(B, H, nb, MOBA_BLOCK, Dh)
    kmean = jnp.mean(kblk, axis=3)
    n_sel = min(MOBA_TOPK, nb)
    scale = Dh ** -0.5
    nc = S // MOBA_Q_CHUNK
    qc = q.reshape(B, H, nc, MOBA_Q_CHUNK, Dh).transpose(2, 0, 1, 3, 4)
    bidx = jnp.arange(B)[:, None, None, None]
    hidx = jnp.arange(H)[None, :, None, None]
    hidx5 = jnp.arange(H)[None, :, None, None, None]
    blk_off = jnp.arange(MOBA_BLOCK)
    bias_t = rel_bias.T

    def chunk(args):
        i, qi = args
        qpos = i * MOBA_Q_CHUNK + jnp.arange(MOBA_Q_CHUNK)
        own = (i * MOBA_Q_CHUNK) // MOBA_BLOCK
        gate = jnp.einsum('bhqd,bhnd->bhqn', qi, kmean).astype(jnp.float32)
        gate = jnp.where(jnp.arange(nb) < own, gate, NEG_INF)
        _, sel = lax.top_k(gate, n_sel)
        sel_valid = sel < own
        ks = kblk[bidx, hidx, sel]
        vs = vblk[bidx, hidx, sel]
        s_sel = jnp.einsum('bhqd,bhqnkd->bhqnk', qi, ks).astype(jnp.float32) * scale
        spos_sel = sel[..., None] * MOBA_BLOCK + blk_off
        bias_sel = bias_t[hidx5, rel_bucket(qpos[:, None, None] - spos_sel)]
        s_sel = jnp.where(sel_valid[..., None], s_sel + bias_sel.astype(jnp.float32), NEG_INF)
        ko = lax.dynamic_slice_in_dim(kp, own * MOBA_BLOCK, MOBA_BLOCK, axis=2)
        vo = lax.dynamic_slice_in_dim(vp, own * MOBA_BLOCK, MOBA_BLOCK, axis=2)
        s_own = jnp.einsum('bhqd,bhkd->bhqk', qi, ko).astype(jnp.float32) * scale
        dist = qpos[:, None] - (own * MOBA_BLOCK + blk_off)[None, :]
        bias_own = bias_t[:, rel_bucket(dist)]
        s_own = jnp.where(dist >= 0, s_own + bias_own.astype(jnp.float32), NEG_INF)
        logits = jnp.concatenate([s_sel.reshape(B, H, MOBA_Q_CHUNK, n_sel * MOBA_BLOCK), s_own], axis=-1)
        p = jax.nn.softmax(logits, axis=-1).astype(v.dtype)
        p_sel = p[..., :n_sel * MOBA_BLOCK].reshape(B, H, MOBA_Q_CHUNK, n_sel, MOBA_BLOCK)
        p_own = p[..., n_sel * MOBA_BLOCK:]
        return (jnp.einsum('bhqnk,bhqnkd->bhqd', p_sel, vs)
                + jnp.einsum('bhqk,bhkd->bhqd', p_own, vo))

    out = lax.map(chunk, (jnp.arange(nc), qc))
    return out.transpose(1, 2, 0, 3, 4).reshape(B, H, S, Dh)


def hybrid_mixer(h, w_in, w_proj_sb, w_proj_moba, w_out, rel_bias):
    B, S, _ = h.shape
    proj = h @ w_in
    qkv_sb, qkv_moba, gates = jnp.split(proj, [3 * W_SB, 3 * W_SB + 3 * W_MOBA], axis=-1)

    def heads(t, nh):
        return t.reshape(B, S, 3, nh, HEAD_DIM).transpose(2, 0, 3, 1, 4)

    q_a, k_a, v_a = heads(qkv_sb, N_HEADS_SB)
    o_a = stick_breaking_attention(q_a, k_a, v_a)
    o_a = o_a.transpose(0, 2, 1, 3).reshape(B, S, W_SB) @ w_proj_sb
    q_b, k_b, v_b = heads(qkv_moba, N_HEADS_MOBA)
    o_b = moba_attention(q_b, k_b, v_b, rel_bias)
    o_b = o_b.transpose(0, 2, 1, 3).reshape(B, S, W_MOBA) @ w_proj_moba
    g_a, g_b = jnp.split(gates, 2, axis=-1)
    merged = jax.nn.sigmoid(g_a) * o_a + jax.nn.sigmoid(g_b) * o_b
    return merged @ w_out


def swiglu(h, w1, w3, w2):
    return (jax.nn.silu(h @ w1) * (h @ w3)) @ w2


def moe_swiglu(h, w_router, w1, w3, w2):
    logits = (h @ w_router).astype(jnp.float32)
    top_v, top_i = lax.top_k(logits, TOP_K)
    top_w = jax.nn.softmax(top_v, axis=-1)
    combine = jnp.sum(jax.nn.one_hot(top_i, N_EXPERTS, dtype=jnp.float32) * top_w[..., None], axis=-2)
    combine = combine.astype(h.dtype)
    out = jnp.zeros_like(h)
    for e in range(N_EXPERTS):
        out = out + combine[..., e:e + 1] * swiglu(h, w1[e], w3[e], w2[e])
    return out


def modulate_in(x, g_pre, shift, scale):
    return rms_norm(x, g_pre) * (1 + scale[:, None, :]) + shift[:, None, :]


def residual_out(x, y, g_post, gate):
    return x + gate[:, None, :] * rms_norm(y, g_post)


def setup_inputs(seed: int = 0) -> dict:
    key = jax.random.key(seed)
    ks = jax.random.split(key, 20)
    f32 = jnp.float32

    def w(k, shape, fan_in, mult=1.0):
        return jax.random.normal(k, shape, f32) * (mult * fan_in ** -0.5)

    def gain(k, shape):
        return 1.0 + 0.05 * jax.random.normal(k, shape, f32)

    return {
        'x': jax.random.normal(ks[0], (BATCH, SEQ, D_MODEL), f32),
        'c': jax.random.normal(ks[1], (BATCH, D_MODEL), f32),
        'w_mod': w(ks[2], (DEPTH, D_MODEL, N_MOD * D_MODEL), D_MODEL, 0.5),
        'b_mod': 0.01 * jax.random.normal(ks[3], (DEPTH, N_MOD * D_MODEL), f32),
        'g_mix_pre': gain(ks[4], (DEPTH, D_MODEL)),
        'g_mix_post': gain(ks[5], (DEPTH, D_MODEL)),
        'g_ffn_pre': gain(ks[6], (DEPTH, D_MODEL)),
        'g_ffn_post': gain(ks[7], (DEPTH, D_MODEL)),
        'w_in': w(ks[8], (DEPTH, D_MODEL, IN_COLS), D_MODEL),
        'w_proj_sb': w(ks[9], (DEPTH, W_SB, D_MODEL), W_SB),
        'w_proj_moba': w(ks[10], (DEPTH, W_MOBA, D_MODEL), W_MOBA),
        'w_out': w(ks[11], (DEPTH, D_MODEL, D_MODEL), D_MODEL),
        'rel_bias': 0.5 * jax.random.normal(ks[12], (REL_BUCKETS, N_HEADS_MOBA), f32),
        'w1_dense': w(ks[13], (N_DENSE, D_MODEL, D_FF), D_MODEL),
        'w3_dense': w(ks[14], (N_DENSE, D_MODEL, D_FF), D_MODEL),
        'w2_dense': w(ks[15], (N_DENSE, D_FF, D_MODEL), D_FF),
        'w_router': w(ks[16], (N_MOE, D_MODEL, N_EXPERTS), D_MODEL),
        'w1_moe': w(ks[17], (N_MOE, N_EXPERTS, D_MODEL, D_FF_EXPERT), D_MODEL),
        'w3_moe': w(ks[18], (N_MOE, N_EXPERTS, D_MODEL, D_FF_EXPERT), D_MODEL),
        'w2_moe': w(ks[19], (N_MOE, N_EXPERTS, D_FF_EXPERT, D_MODEL), D_FF_EXPERT),
    }


def reference(x, c, w_mod, b_mod, g_mix_pre, g_mix_post, g_ffn_pre, g_ffn_post, w_in, w_proj_sb,
              w_proj_moba, w_out, rel_bias, w1_dense, w3_dense, w2_dense, w_router, w1_moe, w3_moe,
              w2_moe):
    cond = jax.nn.silu(c)
    for l in range(DEPTH):
        mod = cond @ w_mod[l] + b_mod[l]
        sh_mix, sc_mix, gt_mix, sh_ffn, sc_ffn, gt_ffn = jnp.split(mod, N_MOD, axis=-1)
        h = modulate_in(x, g_mix_pre[l], sh_mix, sc_mix)
        y = hybrid_mixer(h, w_in[l], w_proj_sb[l], w_proj_moba[l], w_out[l], rel_bias)
        x = residual_out(x, y, g_mix_post[l], gt_mix)
        h = modulate_in(x, g_ffn_pre[l], sh_ffn, sc_ffn)
        if l % 2 == 0:
            y = swiglu(h, w1_dense[l // 2], w3_dense[l // 2], w2_dense[l // 2])
        else:
            y = moe_swiglu(h, w_router[l // 2], w1_moe[l // 2], w3_moe[l // 2], w2_moe[l // 2])
        x = residual_out(x, y, g_ffn_post[l], gt_ffn)
    return x
```

```python
import functools
import math

import jax
import jax.numpy as jnp
from jax import lax
from jax.experimental import pallas as pl
from jax.experimental.pallas import tpu as pltpu

F32 = jnp.float32
BF16 = jnp.bfloat16

EPS = 1e-6
NEG_INF = -1e30
HEAD_DIM = 64
N_HEADS = 8
W_BRANCH = N_HEADS * HEAD_DIM
LANES = 128
HEADS_PER_TILE = LANES // HEAD_DIM
N_PAIRS = N_HEADS // HEADS_PER_TILE
MOBA_BLOCK = 256
MOBA_TOPK = 3
REL_BUCKETS = 32
REL_MAX_DIST = 128
N_MOD = 6
TOP_K = 2
ATT_BLOCK = 256
VMEM_LIMIT = 56 * 1024 * 1024


def _dot(a, b):
    return jnp.dot(a, b, preferred_element_type=F32)


def _dot_nt(a, b):
    return lax.dot_general(a, b, (((1,), (1,)), ((), ())), preferred_element_type=F32)


def _split_bf16(x):
    hi = x.astype(BF16)
    lo = (x - hi.astype(F32)).astype(BF16)
    return hi, lo


def _rms(x, g):
    return x * lax.rsqrt(jnp.mean(x * x, axis=-1, keepdims=True) + EPS) * g


def _mod_kernel(c_ref, w_ref, b_ref, o_ref):
    c = c_ref[...]
    cond = c * jax.nn.sigmoid(c)
    o_ref[0] = jnp.dot(cond, w_ref[0], preferred_element_type=F32,
                       precision=lax.Precision.HIGHEST) + b_ref[0]


def _modulation(c, w_mod, b_mod):
    depth, d, n = w_mod.shape
    bsz = c.shape[0]
    tn = 1536
    return pl.pallas_call(
        _mod_kernel,
        grid=(depth, n // tn),
        in_specs=[
            pl.BlockSpec((bsz, d), lambda l, j: (0, 0)),
            pl.BlockSpec((1, d, tn), lambda l, j: (l, 0, j)),
            pl.BlockSpec((1, 1, tn), lambda l, j: (l, 0, j)),
        ],
        out_specs=pl.BlockSpec((1, bsz, tn), lambda l, j: (l, 0, j)),
        out_shape=jax.ShapeDtypeStruct((depth, bsz, n), F32),
        compiler_params=pltpu.CompilerParams(
            dimension_semantics=("arbitrary", "arbitrary"), vmem_limit_bytes=VMEM_LIMIT),
        name="modulation",
    )(c, w_mod, b_mod.reshape(depth, 1, n))


def _inproj_kernel(x_ref, g_ref, sc_ref, sh_ref, w_ref, o_ref, *, tn):
    h = _rms(x_ref[0], g_ref[...]) * (1.0 + sc_ref[0]) + sh_ref[0]
    hb = h.astype(BF16)
    for n in range(w_ref.shape[1] // tn):
        o_ref[0, :, n * tn:(n + 1) * tn] = _dot(hb, w_ref[:, n * tn:(n + 1) * tn]).astype(BF16)


def _inproj(x, g, scale, shift, w):
    bsz, s, d = x.shape
    n = w.shape[1]
    tm = 512
    return pl.pallas_call(
        functools.partial(_inproj_kernel, tn=1024),
        grid=(bsz, s // tm),
        in_specs=[
            pl.BlockSpec((1, tm, d), lambda b, t: (b, t, 0)),
            pl.BlockSpec((1, d), lambda b, t: (0, 0)),
            pl.BlockSpec((1, 1, d), lambda b, t: (b, 0, 0)),
            pl.BlockSpec((1, 1, d), lambda b, t: (b, 0, 0)),
            pl.BlockSpec((d, n), lambda b, t: (0, 0), pipeline_mode=pl.Buffered(1)),
        ],
        out_specs=pl.BlockSpec((1, tm, n), lambda b, t: (b, t, 0)),
        out_shape=jax.ShapeDtypeStruct((bsz, s, n), BF16),
        compiler_params=pltpu.CompilerParams(
            dimension_semantics=("arbitrary", "arbitrary"), vmem_limit_bytes=VMEM_LIMIT),
        name="inproj",
    )(x, g.reshape(1, d), scale.reshape(bsz, 1, d), shift.reshape(bsz, 1, d), w)


def _softplus(z):
    return jnp.maximum(z, 0.0) + jnp.log(1.0 + jnp.exp(-jnp.abs(z)))


def _sb_kernel(q_ref, k_ref, v_ref, o_ref, acc_ref, carry_ref):
    i = pl.program_id(2)
    tq = ATT_BLOCK
    lane = lax.broadcasted_iota(jnp.int32, (1, LANES), 1)
    row = lax.broadcasted_iota(jnp.int32, (tq, tq), 0)
    col = lax.broadcasted_iota(jnp.int32, (tq, tq), 1)
    upper = jnp.where(row > col, 1.0, 0.0).astype(BF16)
    causal = col < row
    q = q_ref[0]

    def suffix_in_block(sp):
        hi, lo = _split_bf16(sp)
        return _dot(hi, upper) + _dot(lo, upper)

    outs = []
    for h in range(HEADS_PER_TILE):
        qh = jnp.where(lane // HEAD_DIM == h, q, jnp.zeros_like(q))

        start = pl.multiple_of(i * tq, tq)
        z = _dot_nt(qh, k_ref[0, pl.ds(start, tq), :])
        sp = jnp.where(causal, _softplus(z), 0.0)
        att = jnp.where(causal, jnp.exp(z - sp - suffix_in_block(sp)), 0.0)
        acc_ref[...] = _dot(att.astype(BF16), v_ref[0, pl.ds(start, tq), :])
        carry_ref[...] = jnp.sum(sp, axis=1, keepdims=True)

        def body(jj, _):
            st = pl.multiple_of((i - 1 - jj) * tq, tq)
            z = _dot_nt(qh, k_ref[0, pl.ds(st, tq), :])
            sp = _softplus(z)
            carry = carry_ref[...]
            att = jnp.exp(z - sp - suffix_in_block(sp) - carry)
            acc_ref[...] += _dot(att.astype(BF16), v_ref[0, pl.ds(st, tq), :])
            carry_ref[...] = carry + jnp.sum(sp, axis=1, keepdims=True)
            return 0

        lax.fori_loop(0, i, body, 0)
        outs.append(acc_ref[...])

    o_ref[0] = jnp.where(lane < HEAD_DIM, outs[0], outs[1]).astype(BF16)


def _stick_breaking(proj, col0):
    bsz, s, _ = proj.shape
    tq = ATT_BLOCK
    return pl.pallas_call(
        _sb_kernel,
        grid=(bsz, N_PAIRS, s // tq),
        in_specs=[
            pl.BlockSpec((1, tq, LANES), lambda b, p, i: (b, i, col0 + p)),
            pl.BlockSpec((1, s, LANES), lambda b, p, i: (b, 0, col0 + N_PAIRS + p)),
            pl.BlockSpec((1, s, LANES), lambda b, p, i: (b, 0, col0 + 2 * N_PAIRS + p)),
        ],
        out_specs=pl.BlockSpec((1, tq, LANES), lambda b, p, i: (b, i, p)),
        out_shape=jax.ShapeDtypeStruct((bsz, s, W_BRANCH), BF16),
        scratch_shapes=[pltpu.VMEM((tq, LANES), F32), pltpu.VMEM((tq, 1), F32)],
        compiler_params=pltpu.CompilerParams(
            dimension_semantics=("arbitrary", "arbitrary", "arbitrary"),
            vmem_limit_bytes=VMEM_LIMIT),
        name="stick_breaking",
    )(proj, proj, proj)


def _rel_bucket(dist):
    max_exact = REL_BUCKETS // 2
    n = jnp.maximum(dist, 0)
    nf = jnp.maximum(n, 1).astype(F32)
    large = max_exact + (jnp.log(nf / max_exact) / math.log(REL_MAX_DIST / max_exact)
                         * (REL_BUCKETS - max_exact)).astype(jnp.int32)
    large = jnp.minimum(large, REL_BUCKETS - 1)
    return jnp.where(n < max_exact, n, large)


def _moba_bias_tiles(rel_bias):
    t = jnp.arange(MOBA_BLOCK)[:, None]
    s = jnp.arange(MOBA_BLOCK)[None, :]
    d_own = t - s
    own = jnp.where((d_own >= 0)[..., None], rel_bias[_rel_bucket(d_own)], NEG_INF)
    prev = rel_bias[_rel_bucket(d_own + MOBA_BLOCK)]
    return jnp.stack([own, prev], axis=0).transpose(3, 0, 1, 2).astype(F32)


def _moba_kernel(bfar_ref, q_ref, k_ref, v_ref, bias_ref, o_ref,
                 kmh_ref, kml_ref, sel_ref, m_ref, l_ref, acc_ref, *, n_blocks):
    p = pl.program_id(1)
    i = pl.program_id(2)
    tq = MOBA_BLOCK
    s_len = k_ref.shape[1]
    nb_pad = kmh_ref.shape[0]
    lane = lax.broadcasted_iota(jnp.int32, (1, LANES), 1)

    @pl.when(i == 0)
    def _():
        r = lax.broadcasted_iota(jnp.int32, (nb_pad, s_len), 0)
        c = lax.broadcasted_iota(jnp.int32, (nb_pad, s_len), 1)
        pool = jnp.where(c // tq == r, 1.0 / tq, 0.0).astype(BF16)
        hi, lo = _split_bf16(_dot(pool, k_ref[0]))
        kmh_ref[...] = hi
        kml_ref[...] = lo

    q = q_ref[0]
    blk_row = lax.broadcasted_iota(jnp.int32, (nb_pad, tq), 0)
    er = lax.broadcasted_iota(jnp.int32, (tq, tq), 0)
    ec = lax.broadcasted_iota(jnp.int32, (tq, tq), 1)
    eye = jnp.where(er == ec, 1.0, 0.0).astype(BF16)

    for h in range(HEADS_PER_TILE):
        head_lanes = lane // HEAD_DIM == h
        qh = jnp.where(head_lanes, q, jnp.zeros_like(q))

        kmh = jnp.where(head_lanes, kmh_ref[...], jnp.zeros_like(kmh_ref[...]))
        kml = jnp.where(head_lanes, kml_ref[...], jnp.zeros_like(kml_ref[...]))
        gate = _dot_nt(kmh, q) + _dot_nt(kml, q)
        cnt = jnp.zeros((nb_pad, tq), F32)
        for n in range(n_blocks):
            gn = gate[n:n + 1, :]
            beats = jnp.where(gn > gate, 1.0, jnp.where((gn == gate) & (n < blk_row), 1.0, 0.0))
            cnt = cnt + jnp.where(n < i, beats, 0.0)
        sel = jnp.where((blk_row < i) & (cnt < MOBA_TOPK), 1.0, 0.0).astype(BF16)
        selq = _dot_nt(eye, sel)
        for n in range(n_blocks):
            sel_ref[n] = (selq[:, n:n + 1] - 1.0) * (-NEG_INF)

        def tile(st, bias, first):
            s = _dot_nt(qh, k_ref[0, pl.ds(st, tq), :]) + bias
            m_tile = jnp.max(s, axis=1, keepdims=True)
            if first:
                m_new = m_tile
                pr = jnp.exp(s - m_new)
                l_ref[...] = jnp.sum(pr, axis=1, keepdims=True)
                acc_ref[...] = _dot(pr.astype(BF16), v_ref[0, pl.ds(st, tq), :])
            else:
                m_prev = m_ref[...]
                m_new = jnp.maximum(m_prev, m_tile)
                alpha = jnp.exp(m_prev - m_new)
                pr = jnp.exp(s - m_new)
                l_ref[...] = alpha * l_ref[...] + jnp.sum(pr, axis=1, keepdims=True)
                acc_ref[...] = alpha * acc_ref[...] + _dot(pr.astype(BF16),
                                                           v_ref[0, pl.ds(st, tq), :])
            m_ref[...] = m_new

        tile(pl.multiple_of(i * tq, tq), bias_ref[h, 0], True)

        @pl.when(i >= 1)
        def _():
            tile(pl.multiple_of((i - 1) * tq, tq), bias_ref[h, 1] + sel_ref[i - 1], False)

        b_far = bfar_ref[p * HEADS_PER_TILE + h]

        def far(j, _):
            tile(pl.multiple_of(j * tq, tq), sel_ref[j] + b_far, False)
            return 0

        lax.fori_loop(0, jnp.maximum(i - 1, 0), far, 0)

        out_h = acc_ref[...] / l_ref[...]
        if h == 0:
            out = out_h
        else:
            out = jnp.where(lane < h * HEAD_DIM, out, out_h)
    o_ref[0] = out.astype(BF16)


def _moba(proj, col0, rel_bias):
    bsz, s, _ = proj.shape
    tq = MOBA_BLOCK
    assert s % tq == 0
    n_blocks = s // tq
    nb_pad = 16
    assert n_blocks <= nb_pad
    assert MOBA_BLOCK >= REL_MAX_DIST
    bias_tiles = _moba_bias_tiles(rel_bias)
    b_far = rel_bias[REL_BUCKETS - 1].astype(F32)
    return pl.pallas_call(
        functools.partial(_moba_kernel, n_blocks=n_blocks),
        grid=(bsz, N_PAIRS, n_blocks),
        in_specs=[
            pl.BlockSpec(memory_space=pltpu.SMEM),
            pl.BlockSpec((1, tq, LANES), lambda b, p, i: (b, i, col0 + p)),
            pl.BlockSpec((1, s, LANES), lambda b, p, i: (b, 0, col0 + N_PAIRS + p)),
            pl.BlockSpec((1, s, LANES), lambda b, p, i: (b, 0, col0 + 2 * N_PAIRS + p)),
            pl.BlockSpec((HEADS_PER_TILE, 2, tq, tq), lambda b, p, i: (p, 0, 0, 0)),
        ],
        out_specs=pl.BlockSpec((1, tq, LANES), lambda b, p, i: (b, i, p)),
        out_shape=jax.ShapeDtypeStruct((bsz, s, W_BRANCH), BF16),
        scratch_shapes=[
            pltpu.VMEM((nb_pad, LANES), BF16),
            pltpu.VMEM((nb_pad, LANES), BF16),
            pltpu.VMEM((n_blocks, tq, 1), F32),
            pltpu.VMEM((tq, 1), F32),
            pltpu.VMEM((tq, 1), F32),
            pltpu.VMEM((tq, LANES), F32),
        ],
        compiler_params=pltpu.CompilerParams(
            dimension_semantics=("arbitrary", "arbitrary", "arbitrary"),
            vmem_limit_bytes=VMEM_LIMIT),
        name="moba",
    )(b_far, proj, proj, proj, bias_tiles)


def _merge_kernel(oa_ref, ob_ref, ga_ref, gb_ref, x_ref, wa_ref, wb_ref, wo_ref, g_ref, gate_ref,
                  o_ref):
    pa = _dot(oa_ref[0], wa_ref[...])
    pb = _dot(ob_ref[0], wb_ref[...])
    merged = (jax.nn.sigmoid(ga_ref[0].astype(F32)) * pa
              + jax.nn.sigmoid(gb_ref[0].astype(F32)) * pb)
    y = _dot(merged.astype(BF16), wo_ref[...])
    o_ref[0] = x_ref[0] + gate_ref[0] * _rms(y, g_ref[...])


def _merge(o_a, o_b, proj, gate_col, x, w_a, w_b, w_o, g_post, gate):
    bsz, s, d = x.shape
    tm = 512
    const = lambda b, t: (0, 0)
    return pl.pallas_call(
        _merge_kernel,
        grid=(bsz, s // tm),
        in_specs=[
            pl.BlockSpec((1, tm, W_BRANCH), lambda b, t: (b, t, 0)),
            pl.BlockSpec((1, tm, W_BRANCH), lambda b, t: (b, t, 0)),
            pl.BlockSpec((1, tm, d), lambda b, t: (b, t, gate_col)),
            pl.BlockSpec((1, tm, d), lambda b, t: (b, t, gate_col + 1)),
            pl.BlockSpec((1, tm, d), lambda b, t: (b, t, 0)),
            pl.BlockSpec((W_BRANCH, d), const),
            pl.BlockSpec((W_BRANCH, d), const),
            pl.BlockSpec((d, d), const),
            pl.BlockSpec((1, d), const),
            pl.BlockSpec((1, 1, d), lambda b, t: (b, 0, 0)),
        ],
        out_specs=pl.BlockSpec((1, tm, d), lambda b, t: (b, t, 0)),
        out_shape=jax.ShapeDtypeStruct((bsz, s, d), F32),
        compiler_params=pltpu.CompilerParams(
            dimension_semantics=("arbitrary", "arbitrary"), vmem_limit_bytes=VMEM_LIMIT),
        name="merge",
    )(o_a, o_b, proj, proj, x, w_a, w_b, w_o, g_post.reshape(1, d), gate.reshape(bsz, 1, d))


def _route_top2(logits, n_experts):
    lane = lax.broadcasted_iota(jnp.int32, logits.shape, 1)
    lg = jnp.where(lane < n_experts, logits, -jnp.inf)
    m1 = jnp.max(lg, axis=1, keepdims=True)
    i1 = jnp.min(jnp.where(lg == m1, lane, LANES), axis=1, keepdims=True)
    first = lane == i1
    lg2 = jnp.where(first, -jnp.inf, lg)
    m2 = jnp.max(lg2, axis=1, keepdims=True)
    i2 = jnp.min(jnp.where(lg2 == m2, lane, LANES), axis=1, keepdims=True)
    second = lane == i2
    e2 = jnp.exp(m2 - m1)
    w_first = 1.0 / (1.0 + e2)
    return jnp.where(first, w_first, jnp.where(second, e2 * w_first, 0.0))


def _ffn_kernel(*refs, n_experts):
    routed = n_experts > 1
    if routed:
        (x_ref, gpre_ref, sc_ref, sh_ref, wr_ref, w1_ref, w3_ref, w2_ref, gpost_ref, gate_ref,
         o_ref, h_ref, acc_ref, comb_ref) = refs
    else:
        (x_ref, gpre_ref, sc_ref, sh_ref, w1_ref, w3_ref, w2_ref, gpost_ref, gate_ref,
         o_ref, h_ref, acc_ref) = refs
    e = pl.program_id(2)
    k = pl.program_id(3)

    @pl.when((e == 0) & (k == 0))
    def _():
        h = _rms(x_ref[0], gpre_ref[...]) * (1.0 + sc_ref[0]) + sh_ref[0]
        h_ref[...] = h.astype(BF16)
        acc_ref[...] = jnp.zeros_like(acc_ref)
        if routed:
            logits = jnp.dot(h, wr_ref[...], preferred_element_type=F32,
                             precision=lax.Precision.HIGHEST)
            comb_ref[...] = _route_top2(logits, n_experts)

    h = h_ref[...]
    u = jax.nn.silu(_dot(h, w1_ref[0])) * _dot(h, w3_ref[0])
    if routed:
        lane = lax.broadcasted_iota(jnp.int32, comb_ref.shape, 1)
        u = u * jnp.sum(jnp.where(lane == e, comb_ref[...], 0.0), axis=1, keepdims=True)
    acc_ref[...] += _dot(u.astype(BF16), w2_ref[0])

    @pl.when((e == pl.num_programs(2) - 1) & (k == pl.num_programs(3) - 1))
    def _():
        o_ref[0] = x_ref[0] + gate_ref[0] * _rms(acc_ref[...], gpost_ref[...])


def _ffn(x, g_pre, scale, shift, w_router, w1, w3, w2, g_post, gate):
    bsz, s, d = x.shape
    n_experts, _, f = w1.shape
    tm = 1024
    tf = 512
    row = lambda b, t, e, k: (b, t, 0)
    per_b = lambda b, t, e, k: (b, 0, 0)
    const = lambda b, t, e, k: (0, 0)
    in_specs = [
        pl.BlockSpec((1, tm, d), row),
        pl.BlockSpec((1, d), const),
        pl.BlockSpec((1, 1, d), per_b),
        pl.BlockSpec((1, 1, d), per_b),
    ]
    args = [x, g_pre.reshape(1, d), scale.reshape(bsz, 1, d), shift.reshape(bsz, 1, d)]
    scratch = [pltpu.VMEM((tm, d), BF16), pltpu.VMEM((tm, d), F32)]
    if n_experts > 1:
        in_specs.append(pl.BlockSpec((d, LANES), const))
        args.append(jnp.pad(w_router, ((0, 0), (0, LANES - n_experts))))
        scratch.append(pltpu.VMEM((tm, LANES), F32))
    in_specs += [
        pl.BlockSpec((1, d, tf), lambda b, t, e, k: (e, 0, k)),
        pl.BlockSpec((1, d, tf), lambda b, t, e, k: (e, 0, k)),
        pl.BlockSpec((1, tf, d), lambda b, t, e, k: (e, k, 0)),
        pl.BlockSpec((1, d), const),
        pl.BlockSpec((1, 1, d), per_b),
    ]
    args += [w1, w3, w2, g_post.reshape(1, d), gate.reshape(bsz, 1, d)]
    return pl.pallas_call(
        functools.partial(_ffn_kernel, n_experts=n_experts),
        grid=(bsz, s // tm, n_experts, f // tf),
        in_specs=in_specs,
        out_specs=pl.BlockSpec((1, tm, d), row),
        out_shape=jax.ShapeDtypeStruct((bsz, s, d), F32),
        scratch_shapes=scratch,
        compiler_params=pltpu.CompilerParams(
            dimension_semantics=("arbitrary",) * 4, vmem_limit_bytes=VMEM_LIMIT),
        name="ffn_routed" if n_experts > 1 else "ffn_dense",
    )(*args)


def kernel(x, c, w_mod, b_mod, g_mix_pre, g_mix_post, g_ffn_pre, g_ffn_post, w_in, w_proj_sb,
           w_proj_moba, w_out, rel_bias, w1_dense, w3_dense, w2_dense, w_router, w1_moe, w3_moe,
           w2_moe):
    depth = w_mod.shape[0]
    d = x.shape[-1]
    qkv = 3 * W_BRANCH
    col = jnp.arange(w_in.shape[-1])
    is_q = (col < W_BRANCH) | ((col >= qkv) & (col < qkv + W_BRANCH))
    col_scale = jnp.where(is_q, HEAD_DIM ** -0.5, 1.0).astype(F32)

    mod = _modulation(c, w_mod, b_mod)
    for l in range(depth):
        sh_mix, sc_mix, gt_mix, sh_ffn, sc_ffn, gt_ffn = jnp.split(mod[l], N_MOD, axis=-1)
        proj = _inproj(x, g_mix_pre[l], sc_mix, sh_mix, (w_in[l] * col_scale).astype(BF16))
        o_a = _stick_breaking(proj, 0)
        o_b = _moba(proj, qkv // LANES, rel_bias)
        x = _merge(o_a, o_b, proj, 2 * qkv // d, x, w_proj_sb[l].astype(BF16),
                   w_proj_moba[l].astype(BF16), w_out[l].astype(BF16), g_mix_post[l], gt_mix)
        if l % 2 == 0:
            j = l // 2
            x = _ffn(x, g_ffn_pre[l], sc_ffn, sh_ffn, None, w1_dense[j:j + 1].astype(BF16),
                     w3_dense[j:j + 1].astype(BF16), w2_dense[j:j + 1].astype(BF16),
                     g_ffn_post[l], gt_ffn)
        else:
            j = l // 2
            x = _ffn(x, g_ffn_pre[l], sc_ffn, sh_ffn, w_router[j], w1_moe[j].astype(BF16),
                     w3_moe[j].astype(BF16), w2_moe[j].astype(BF16), g_ffn_post[l], gt_ffn)
    return x
```

```python
import functools
import math

import jax
import jax.numpy as jnp
from jax import lax
from jax.experimental import pallas as pl
from jax.experimental.pallas import tpu as pltpu

F32 = jnp.float32
BF16 = jnp.bfloat16

EPS = 1e-6
NEG_INF = -1e30
HEAD_DIM = 64
N_HEADS = 8
W_BRANCH = N_HEADS * HEAD_DIM
LANES = 128
HEADS_PER_TILE = LANES // HEAD_DIM
N_PAIRS = N_HEADS // HEADS_PER_TILE
MOBA_BLOCK = 256
MOBA_TOPK = 3
REL_BUCKETS = 32
REL_MAX_DIST = 128
N_MOD = 6
TOP_K = 2
ATT_BLOCK = 256
VMEM_LIMIT = 56 * 1024 * 1024


def _dot(a, b):
    return jnp.dot(a, b, preferred_element_type=F32)


def _dot_nt(a, b):
    return lax.dot_general(a, b, (((1,), (1,)), ((), ())), preferred_element_type=F32)


def _split_bf16(x):
    hi = x.astype(BF16)
    lo = (x - hi.astype(F32)).astype(BF16)
    return hi, lo


def _rms(x, g):
    return x * lax.rsqrt(jnp.mean(x * x, axis=-1, keepdims=True) + EPS) * g


def _mod_kernel(c_ref, w_ref, b_ref, o_ref):
    c = c_ref[...]
    cond = c * jax.nn.sigmoid(c)
    o_ref[0] = jnp.dot(cond, w_ref[0], preferred_element_type=F32,
                       precision=lax.Precision.HIGHEST) + b_ref[0]


def _modulation(c, w_mod, b_mod):
    depth, d, n = w_mod.shape
    bsz = c.shape[0]
    tn = 1536
    return pl.pallas_call(
        _mod_kernel,
        grid=(depth, n // tn),
        in_specs=[
            pl.BlockSpec((bsz, d), lambda l, j: (0, 0)),
            pl.BlockSpec((1, d, tn), lambda l, j: (l, 0, j)),
            pl.BlockSpec((1, 1, tn), lambda l, j: (l, 0, j)),
        ],
        out_specs=pl.BlockSpec((1, bsz, tn), lambda l, j: (l, 0, j)),
        out_shape=jax.ShapeDtypeStruct((depth, bsz, n), F32),
        compiler_params=pltpu.CompilerParams(
            dimension_semantics=("arbitrary", "arbitrary"), vmem_limit_bytes=VMEM_LIMIT),
        name="modulation",
    )(c, w_mod, b_mod.reshape(depth, 1, n))


def _inproj_kernel(x_ref, g_ref, sc_ref, sh_ref, w_ref, o_ref, *, tn):
    h = _rms(x_ref[0], g_ref[...]) * (1.0 + sc_ref[0]) + sh_ref[0]
    hb = h.astype(BF16)
    for n in range(w_ref.shape[1] // tn):
        o_ref[0, :, n * tn:(n + 1) * tn] = _dot(hb, w_ref[:, n * tn:(n + 1) * tn]).astype(BF16)


def _inproj(x, g, scale, shift, w):
    bsz, s, d = x.shape
    n = w.shape[1]
    tm = 512
    return pl.pallas_call(
        functools.partial(_inproj_kernel, tn=1024),
        grid=(bsz, s // tm),
        in_specs=[
            pl.BlockSpec((1, tm, d), lambda b, t: (b, t, 0)),
            pl.BlockSpec((1, d), lambda b, t: (0, 0)),
            pl.BlockSpec((1, 1, d), lambda b, t: (b, 0, 0)),
            pl.BlockSpec((1, 1, d), lambda b, t: (b, 0, 0)),
            pl.BlockSpec((d, n), lambda b, t: (0, 0), pipeline_mode=pl.Buffered(1)),
        ],
        out_specs=pl.BlockSpec((1, tm, n), lambda b, t: (b, t, 0)),
        out_shape=jax.ShapeDtypeStruct((bsz, s, n), BF16),
        compiler_params=pltpu.CompilerParams(
            dimension_semantics=("arbitrary", "arbitrary"), vmem_limit_bytes=VMEM_LIMIT),
        name="inproj",
    )(x, g.reshape(1, d), scale.reshape(bsz, 1, d), shift.reshape(bsz, 1, d), w)


def _softplus(z):
    return jnp.maximum(z, 0.0) + jnp.log(1.0 + jnp.exp(-jnp.abs(z)))


def _for_each_group(n, run):
    big = n // 4

    def loop(g, _):
        run(g * 4, 4)
        return 0

    lax.fori_loop(0, big, loop, 0)
    base = big * 4

    @pl.when((n & 2) != 0)
    def _():
        run(base, 2)

    @pl.when((n & 1) != 0)
    def _():
        run(base + (n & 2), 1)


def _sb_kernel(q_ref, k_ref, v_ref, o_ref, acc_ref, carry_ref):
    i = pl.program_id(2)
    tq = ATT_BLOCK
    lane = lax.broadcasted_iota(jnp.int32, (1, LANES), 1)
    row = lax.broadcasted_iota(jnp.int32, (tq, tq), 0)
    col = lax.broadcasted_iota(jnp.int32, (tq, tq), 1)
    upper = jnp.where(row > col, 1.0, 0.0).astype(BF16)
    causal = col < row
    q = q_ref[0]
    qh = [jnp.where(lane // HEAD_DIM == h, q, jnp.zeros_like(q)) for h in range(HEADS_PER_TILE)]

    def suffix_in_block(sp):
        hi, lo = _split_bf16(sp)
        return _dot(hi, upper) + _dot(lo, upper)

    def run(first, size, diag=False):
        for h in range(HEADS_PER_TILE):
            carry = None if diag else carry_ref[h]
            acc = None
            for n in range(size):
                blk = i - first - n - (0 if diag else 1)
                st = pl.multiple_of(blk * tq, tq)
                z = _dot_nt(qh[h], k_ref[0, pl.ds(st, tq), :])
                sp = _softplus(z)
                if diag:
                    sp = jnp.where(causal, sp, 0.0)
                logit = z - sp - suffix_in_block(sp)
                if carry is not None:
                    logit = logit - carry
                att = jnp.exp(logit)
                if diag:
                    att = jnp.where(causal, att, 0.0)
                part = _dot(att.astype(BF16), v_ref[0, pl.ds(st, tq), :])
                acc = part if acc is None else acc + part
                rs = jnp.sum(sp, axis=1, keepdims=True)
                carry = rs if carry is None else carry + rs
            acc_ref[h] = acc if diag else acc_ref[h] + acc
            carry_ref[h] = carry

    run(0, 1, diag=True)
    _for_each_group(i, run)
    o_ref[0] = jnp.where(lane < HEAD_DIM, acc_ref[0], acc_ref[1]).astype(BF16)


def _stick_breaking(proj, col0):
    bsz, s, _ = proj.shape
    tq = ATT_BLOCK
    return pl.pallas_call(
        _sb_kernel,
        grid=(bsz, N_PAIRS, s // tq),
        in_specs=[
            pl.BlockSpec((1, tq, LANES), lambda b, p, i: (b, i, col0 + p)),
            pl.BlockSpec((1, s, LANES), lambda b, p, i: (b, 0, col0 + N_PAIRS + p)),
            pl.BlockSpec((1, s, LANES), lambda b, p, i: (b, 0, col0 + 2 * N_PAIRS + p)),
        ],
        out_specs=pl.BlockSpec((1, tq, LANES), lambda b, p, i: (b, i, p)),
        out_shape=jax.ShapeDtypeStruct((bsz, s, W_BRANCH), BF16),
        scratch_shapes=[pltpu.VMEM((HEADS_PER_TILE, tq, LANES), F32),
                        pltpu.VMEM((HEADS_PER_TILE, tq, 1), F32)],
        compiler_params=pltpu.CompilerParams(
            dimension_semantics=("arbitrary", "arbitrary", "arbitrary"),
            vmem_limit_bytes=VMEM_LIMIT),
        name="stick_breaking",
    )(proj, proj, proj)


def _rel_bucket(dist):
    max_exact = REL_BUCKETS // 2
    n = jnp.maximum(dist, 0)
    nf = jnp.maximum(n, 1).astype(F32)
    large = max_exact + (jnp.log(nf / max_exact) / math.log(REL_MAX_DIST / max_exact)
                         * (REL_BUCKETS - max_exact)).astype(jnp.int32)
    large = jnp.minimum(large, REL_BUCKETS - 1)
    return jnp.where(n < max_exact, n, large)


def _moba_bias_tiles(rel_bias):
    t = jnp.arange(MOBA_BLOCK)[:, None]
    s = jnp.arange(MOBA_BLOCK)[None, :]
    d_own = t - s

    def lookup(dist):
        onehot = jax.nn.one_hot(_rel_bucket(dist), REL_BUCKETS, dtype=F32)
        return jnp.einsum("tsk,kh->hts", onehot, rel_bias.astype(F32),
                          precision=lax.Precision.HIGHEST)

    own = jnp.where((d_own >= 0)[None], lookup(d_own), NEG_INF)
    return jnp.stack([own, lookup(d_own + MOBA_BLOCK)], axis=1)


def _moba_kernel(bfar_ref, q_ref, k_ref, v_ref, bias_ref, o_ref,
                 kmh_ref, kml_ref, sel_ref, m_ref, l_ref, acc_ref, *, n_blocks):
    p = pl.program_id(1)
    i = pl.program_id(2)
    tq = MOBA_BLOCK
    s_len = k_ref.shape[1]
    nb_pad = kmh_ref.shape[0]
    lane = lax.broadcasted_iota(jnp.int32, (1, LANES), 1)

    @pl.when(i == 0)
    def _():
        r = lax.broadcasted_iota(jnp.int32, (nb_pad, s_len), 0)
        c = lax.broadcasted_iota(jnp.int32, (nb_pad, s_len), 1)
        pool = jnp.where(c // tq == r, 1.0 / tq, 0.0).astype(BF16)
        hi, lo = _split_bf16(_dot(pool, k_ref[0]))
        kmh_ref[...] = hi
        kml_ref[...] = lo

    q = q_ref[0]
    blk_row = lax.broadcasted_iota(jnp.int32, (nb_pad, tq), 0)
    er = lax.broadcasted_iota(jnp.int32, (tq, tq), 0)
    ec = lax.broadcasted_iota(jnp.int32, (tq, tq), 1)
    eye = jnp.where(er == ec, 1.0, 0.0).astype(BF16)
    head_lanes = [lane // HEAD_DIM == h for h in range(HEADS_PER_TILE)]
    qh = [jnp.where(hl, q, jnp.zeros_like(q)) for hl in head_lanes]

    for h in range(HEADS_PER_TILE):
        kmh = jnp.where(head_lanes[h], kmh_ref[...], jnp.zeros_like(kmh_ref[...]))
        kml = jnp.where(head_lanes[h], kml_ref[...], jnp.zeros_like(kml_ref[...]))
        gate = _dot_nt(kmh, q) + _dot_nt(kml, q)
        cnt = jnp.zeros((nb_pad, tq), F32)
        for n in range(n_blocks):
            gn = gate[n:n + 1, :]
            beats = jnp.where(gn > gate, 1.0, jnp.where((gn == gate) & (n < blk_row), 1.0, 0.0))
            cnt = cnt + jnp.where(n < i, beats, 0.0)
        sel = jnp.where((blk_row < i) & (cnt < MOBA_TOPK), 1.0, 0.0).astype(BF16)
        selq = _dot_nt(eye, sel)
        for n in range(n_blocks):
            sel_ref[h, n] = (selq[:, n:n + 1] - 1.0) * (-NEG_INF)

    def attend(h, blk, bias):
        st = pl.multiple_of(blk * tq, tq)
        s = _dot_nt(qh[h], k_ref[0, pl.ds(st, tq), :]) + bias
        m = jnp.max(s, axis=1, keepdims=True)
        pr = jnp.exp(s - m)
        return (m, jnp.sum(pr, axis=1, keepdims=True),
                _dot(pr.astype(BF16), v_ref[0, pl.ds(st, tq), :]))

    def merge(h, parts, init):
        ms = [part[0] for part in parts]
        if not init:
            ms.append(m_ref[h])
        m_new = functools.reduce(jnp.maximum, ms)
        if init:
            l_new = acc_new = None
        else:
            a = jnp.exp(m_ref[h] - m_new)
            l_new = a * l_ref[h]
            acc_new = a * acc_ref[h]
        for m, l, o in parts:
            w = jnp.exp(m - m_new)
            l_new = w * l if l_new is None else l_new + w * l
            acc_new = w * o if acc_new is None else acc_new + w * o
        m_ref[h] = m_new
        l_ref[h] = l_new
        acc_ref[h] = acc_new

    @pl.when(i == 0)
    def _():
        for h in range(HEADS_PER_TILE):
            merge(h, [attend(h, i, bias_ref[h, 0])], True)

    @pl.when(i >= 1)
    def _():
        for h in range(HEADS_PER_TILE):
            merge(h, [attend(h, i, bias_ref[h, 0]),
                      attend(h, i - 1, bias_ref[h, 1] + sel_ref[h, i - 1])], True)

    def far(first, size):
        for h in range(HEADS_PER_TILE):
            b_far = bfar_ref[p * HEADS_PER_TILE + h]
            merge(h, [attend(h, first + n, sel_ref[h, first + n] + b_far) for n in range(size)],
                  False)

    _for_each_group(jnp.maximum(i - 1, 0), far)

    out = [acc_ref[h] / l_ref[h] for h in range(HEADS_PER_TILE)]
    o_ref[0] = jnp.where(lane < HEAD_DIM, out[0], out[1]).astype(BF16)


def _moba(proj, col0, bias_tiles, b_far):
    bsz, s, _ = proj.shape
    tq = MOBA_BLOCK
    assert s % tq == 0
    n_blocks = s // tq
    nb_pad = 16
    assert n_blocks <= nb_pad
    return pl.pallas_call(
        functools.partial(_moba_kernel, n_blocks=n_blocks),
        grid=(bsz, N_PAIRS, n_blocks),
        in_specs=[
            pl.BlockSpec(memory_space=pltpu.SMEM),
            pl.BlockSpec((1, tq, LANES), lambda b, p, i: (b, i, col0 + p)),
            pl.BlockSpec((1, s, LANES), lambda b, p, i: (b, 0, col0 + N_PAIRS + p)),
            pl.BlockSpec((1, s, LANES), lambda b, p, i: (b, 0, col0 + 2 * N_PAIRS + p)),
            pl.BlockSpec((HEADS_PER_TILE, 2, tq, tq), lambda b, p, i: (p, 0, 0, 0)),
        ],
        out_specs=pl.BlockSpec((1, tq, LANES), lambda b, p, i: (b, i, p)),
        out_shape=jax.ShapeDtypeStruct((bsz, s, W_BRANCH), BF16),
        scratch_shapes=[
            pltpu.VMEM((nb_pad, LANES), BF16),
            pltpu.VMEM((nb_pad, LANES), BF16),
            pltpu.VMEM((HEADS_PER_TILE, n_blocks, tq, 1), F32),
            pltpu.VMEM((HEADS_PER_TILE, tq, 1), F32),
            pltpu.VMEM((HEADS_PER_TILE, tq, 1), F32),
            pltpu.VMEM((HEADS_PER_TILE, tq, LANES), F32),
        ],
        compiler_params=pltpu.CompilerParams(
            dimension_semantics=("arbitrary", "arbitrary", "arbitrary"),
            vmem_limit_bytes=VMEM_LIMIT),
        name="moba",
    )(b_far, proj, proj, proj, bias_tiles)


def _merge_kernel(oa_ref, ob_ref, ga_ref, gb_ref, x_ref, wa_ref, wb_ref, wo_ref, g_ref, gate_ref,
                  o_ref):
    pa = _dot(oa_ref[0], wa_ref[...])
    pb = _dot(ob_ref[0], wb_ref[...])
    merged = (jax.nn.sigmoid(ga_ref[0].astype(F32)) * pa
              + jax.nn.sigmoid(gb_ref[0].astype(F32)) * pb)
    y = _dot(merged.astype(BF16), wo_ref[...])
    o_ref[0] = x_ref[0] + gate_ref[0] * _rms(y, g_ref[...])


def _merge(o_a, o_b, proj, gate_col, x, w_a, w_b, w_o, g_post, gate):
    bsz, s, d = x.shape
    tm = 512
    const = lambda b, t: (0, 0)
    return pl.pallas_call(
        _merge_kernel,
        grid=(bsz, s // tm),
        in_specs=[
            pl.BlockSpec((1, tm, W_BRANCH), lambda b, t: (b, t, 0)),
            pl.BlockSpec((1, tm, W_BRANCH), lambda b, t: (b, t, 0)),
            pl.BlockSpec((1, tm, d), lambda b, t: (b, t, gate_col)),
            pl.BlockSpec((1, tm, d), lambda b, t: (b, t, gate_col + 1)),
            pl.BlockSpec((1, tm, d), lambda b, t: (b, t, 0)),
            pl.BlockSpec((W_BRANCH, d), const),
            pl.BlockSpec((W_BRANCH, d), const),
            pl.BlockSpec((d, d), const),
            pl.BlockSpec((1, d), const),
            pl.BlockSpec((1, 1, d), lambda b, t: (b, 0, 0)),
        ],
        out_specs=pl.BlockSpec((1, tm, d), lambda b, t: (b, t, 0)),
        out_shape=jax.ShapeDtypeStruct((bsz, s, d), F32),
        compiler_params=pltpu.CompilerParams(
            dimension_semantics=("arbitrary", "arbitrary"), vmem_limit_bytes=VMEM_LIMIT),
        name="merge",
    )(o_a, o_b, proj, proj, x, w_a, w_b, w_o, g_post.reshape(1, d), gate.reshape(bsz, 1, d))


def _route_top2(logits, n_experts):
    lane = lax.broadcasted_iota(jnp.int32, logits.shape, 1)
    lg = jnp.where(lane < n_experts, logits, -jnp.inf)
    m1 = jnp.max(lg, axis=1, keepdims=True)
    i1 = jnp.min(jnp.where(lg == m1, lane, LANES), axis=1, keepdims=True)
    first = lane == i1
    lg2 = jnp.where(first, -jnp.inf, lg)
    m2 = jnp.max(lg2, axis=1, keepdims=True)
    i2 = jnp.min(jnp.where(lg2 == m2, lane, LANES), axis=1, keepdims=True)
    second = lane == i2
    e2 = jnp.exp(m2 - m1)
    w_first = 1.0 / (1.0 + e2)
    return jnp.where(first, w_first, jnp.where(second, e2 * w_first, 0.0))


def _ffn_kernel(*refs, n_experts):
    routed = n_experts > 1
    if routed:
        (x_ref, gpre_ref, sc_ref, sh_ref, wr_ref, w1_ref, w3_ref, w2_ref, gpost_ref, gate_ref,
         o_ref, h_ref, acc_ref, comb_ref) = refs
    else:
        (x_ref, gpre_ref, sc_ref, sh_ref, w1_ref, w3_ref, w2_ref, gpost_ref, gate_ref,
         o_ref, h_ref, acc_ref) = refs
    e = pl.program_id(2)
    k = pl.program_id(3)

    @pl.when((e == 0) & (k == 0))
    def _():
        h = _rms(x_ref[0], gpre_ref[...]) * (1.0 + sc_ref[0]) + sh_ref[0]
        h_ref[...] = h.astype(BF16)
        acc_ref[...] = jnp.zeros_like(acc_ref)
        if routed:
            logits = jnp.dot(h, wr_ref[...], preferred_element_type=F32,
                             precision=lax.Precision.HIGHEST)
            comb_ref[...] = _route_top2(logits, n_experts)

    h = h_ref[...]
    u = jax.nn.silu(_dot(h, w1_ref[0])) * _dot(h, w3_ref[0])
    if routed:
        lane = lax.broadcasted_iota(jnp.int32, comb_ref.shape, 1)
        u = u * jnp.sum(jnp.where(lane == e, comb_ref[...], 0.0), axis=1, keepdims=True)
    acc_ref[...] += _dot(u.astype(BF16), w2_ref[0])

    @pl.when((e == pl.num_programs(2) - 1) & (k == pl.num_programs(3) - 1))
    def _():
        o_ref[0] = x_ref[0] + gate_ref[0] * _rms(acc_ref[...], gpost_ref[...])


def _ffn(x, g_pre, scale, shift, w_router, w1, w3, w2, g_post, gate):
    bsz, s, d = x.shape
    n_experts, _, f = w1.shape
    tm = 1024
    tf = 512
    row = lambda b, t, e, k: (b, t, 0)
    per_b = lambda b, t, e, k: (b, 0, 0)
    const = lambda b, t, e, k: (0, 0)
    in_specs = [
        pl.BlockSpec((1, tm, d), row),
        pl.BlockSpec((1, d), const),
        pl.BlockSpec((1, 1, d), per_b),
        pl.BlockSpec((1, 1, d), per_b),
    ]
    args = [x, g_pre.reshape(1, d), scale.reshape(bsz, 1, d), shift.reshape(bsz, 1, d)]
    scratch = [pltpu.VMEM((tm, d), BF16), pltpu.VMEM((tm, d), F32)]
    if n_experts > 1:
        in_specs.append(pl.BlockSpec((d, LANES), const))
        args.append(jnp.pad(w_router, ((0, 0), (0, LANES - n_experts))))
        scratch.append(pltpu.VMEM((tm, LANES), F32))
    in_specs += [
        pl.BlockSpec((1, d, tf), lambda b, t, e, k: (e, 0, k)),
        pl.BlockSpec((1, d, tf), lambda b, t, e, k: (e, 0, k)),
        pl.BlockSpec((1, tf, d), lambda b, t, e, k: (e, k, 0)),
        pl.BlockSpec((1, d), const),
        pl.BlockSpec((1, 1, d), per_b),
    ]
    args += [w1, w3, w2, g_post.reshape(1, d), gate.reshape(bsz, 1, d)]
    return pl.pallas_call(
        functools.partial(_ffn_kernel, n_experts=n_experts),
        grid=(bsz, s // tm, n_experts, f // tf),
        in_specs=in_specs,
        out_specs=pl.BlockSpec((1, tm, d), row),
        out_shape=jax.ShapeDtypeStruct((bsz, s, d), F32),
        scratch_shapes=scratch,
        compiler_params=pltpu.CompilerParams(
            dimension_semantics=("arbitrary",) * 4, vmem_limit_bytes=VMEM_LIMIT),
        name="ffn_routed" if n_experts > 1 else "ffn_dense",
    )(*args)


def kernel(x, c, w_mod, b_mod, g_mix_pre, g_mix_post, g_ffn_pre, g_ffn_post, w_in, w_proj_sb,
           w_proj_moba, w_out, rel_bias, w1_dense, w3_dense, w2_dense, w_router, w1_moe, w3_moe,
           w2_moe):
    depth = w_mod.shape[0]
    d = x.shape[-1]
    qkv = 3 * W_BRANCH
    col = jnp.arange(w_in.shape[-1])
    is_q = (col < W_BRANCH) | ((col >= qkv) & (col < qkv + W_BRANCH))
    col_scale = jnp.where(is_q, HEAD_DIM ** -0.5, 1.0).astype(F32)

    mod = _modulation(c, w_mod, b_mod)
    assert MOBA_BLOCK >= REL_MAX_DIST
    bias_tiles = _moba_bias_tiles(rel_bias)
    b_far = rel_bias[REL_BUCKETS - 1].astype(F32)
    for l in range(depth):
        sh_mix, sc_mix, gt_mix, sh_ffn, sc_ffn, gt_ffn = jnp.split(mod[l], N_MOD, axis=-1)
        proj = _inproj(x, g_mix_pre[l], sc_mix, sh_mix, (w_in[l] * col_scale).astype(BF16))
        o_a = _stick_breaking(proj, 0)
        o_b = _moba(proj, qkv // LANES, bias_tiles, b_far)
        x = _merge(o_a, o_b, proj, 2 * qkv // d, x, w_proj_sb[l].astype(BF16),
                   w_proj_moba[l].astype(BF16), w_out[l].astype(BF16), g_mix_post[l], gt_mix)
        if l % 2 == 0:
            j = l // 2
            x = _ffn(x, g_ffn_pre[l], sc_ffn, sh_ffn, None, w1_dense[j:j + 1].astype(BF16),
                     w3_dense[j:j + 1].astype(BF16), w2_dense[j:j + 1].astype(BF16),
                     g_ffn_post[l], gt_ffn)
        else:
            j = l // 2
            x = _ffn(x, g_ffn_pre[l], sc_ffn, sh_ffn, w_router[j], w1_moe[j].astype(BF16),
                     w3_moe[j].astype(BF16), w2_moe[j].astype(BF16), g_ffn_post[l], gt_ffn)
    return x
```

```python
import functools
import math

import jax
import jax.numpy as jnp
from jax import lax
from jax.experimental import pallas as pl
from jax.experimental.pallas import tpu as pltpu

F32 = jnp.float32
BF16 = jnp.bfloat16

EPS = 1e-6
NEG_INF = -1e30
HEAD_DIM = 64
N_HEADS = 8
W_BRANCH = N_HEADS * HEAD_DIM
LANES = 128
SUBLANES = 8
HEADS_PER_TILE = LANES // HEAD_DIM
N_PAIRS = N_HEADS // HEADS_PER_TILE
MOBA_BLOCK = 256
MOBA_TOPK = 3
REL_BUCKETS = 32
REL_MAX_DIST = 128
N_MOD = 6
TOP_K = 2
ATT_BLOCK = 256
VMEM_LIMIT = 56 * 1024 * 1024


def _dot(a, b):
    return jnp.dot(a, b, preferred_element_type=F32)


def _dot_nt(a, b):
    return lax.dot_general(a, b, (((1,), (1,)), ((), ())), preferred_element_type=F32)


def _split_bf16(x):
    hi = x.astype(BF16)
    lo = (x - hi.astype(F32)).astype(BF16)
    return hi, lo


def _rms(x, g):
    return x * lax.rsqrt(jnp.mean(x * x, axis=-1, keepdims=True) + EPS) * g


def _mod_kernel(c_ref, w_ref, b_ref, o_ref):
    c = c_ref[...]
    cond = c * jax.nn.sigmoid(c)
    o_ref[0] = jnp.dot(cond, w_ref[0], preferred_element_type=F32,
                       precision=lax.Precision.HIGHEST) + b_ref[0]


def _modulation(c, w_mod, b_mod):
    depth, d, n = w_mod.shape
    bsz = c.shape[0]
    tn = 1536
    return pl.pallas_call(
        _mod_kernel,
        grid=(depth, n // tn),
        in_specs=[
            pl.BlockSpec((bsz, d), lambda l, j: (0, 0)),
            pl.BlockSpec((1, d, tn), lambda l, j: (l, 0, j)),
            pl.BlockSpec((1, 1, tn), lambda l, j: (l, 0, j)),
        ],
        out_specs=pl.BlockSpec((1, bsz, tn), lambda l, j: (l, 0, j)),
        out_shape=jax.ShapeDtypeStruct((depth, bsz, n), F32),
        compiler_params=pltpu.CompilerParams(
            dimension_semantics=("arbitrary", "arbitrary"), vmem_limit_bytes=VMEM_LIMIT),
        name="modulation",
    )(c, w_mod, b_mod.reshape(depth, 1, n))


def _inproj_kernel(x_ref, g_ref, sc_ref, sh_ref, w_ref, o_ref, *, tn):
    h = _rms(x_ref[0], g_ref[...]) * (1.0 + sc_ref[0]) + sh_ref[0]
    hb = h.astype(BF16)
    for n in range(w_ref.shape[1] // tn):
        o_ref[0, :, n * tn:(n + 1) * tn] = _dot(hb, w_ref[:, n * tn:(n + 1) * tn]).astype(BF16)


def _inproj(x, g, scale, shift, w):
    bsz, s, d = x.shape
    n = w.shape[1]
    tm = 512
    return pl.pallas_call(
        functools.partial(_inproj_kernel, tn=1024),
        grid=(bsz, s // tm),
        in_specs=[
            pl.BlockSpec((1, tm, d), lambda b, t: (b, t, 0)),
            pl.BlockSpec((1, d), lambda b, t: (0, 0)),
            pl.BlockSpec((1, 1, d), lambda b, t: (b, 0, 0)),
            pl.BlockSpec((1, 1, d), lambda b, t: (b, 0, 0)),
            pl.BlockSpec((d, n), lambda b, t: (0, 0), pipeline_mode=pl.Buffered(1)),
        ],
        out_specs=pl.BlockSpec((1, tm, n), lambda b, t: (b, t, 0)),
        out_shape=jax.ShapeDtypeStruct((bsz, s, n), BF16),
        compiler_params=pltpu.CompilerParams(
            dimension_semantics=("arbitrary", "arbitrary"), vmem_limit_bytes=VMEM_LIMIT),
        name="inproj",
    )(x, g.reshape(1, d), scale.reshape(bsz, 1, d), shift.reshape(bsz, 1, d), w)


SOFTPLUS_LINEAR = 40.0


def _softplus(z):
    t = jnp.exp(jnp.minimum(z, SOFTPLUS_LINEAR))
    return jnp.where(z > SOFTPLUS_LINEAR, z, jnp.log(1.0 + t))


def _for_each_group(n, run):
    big = n // 4

    def loop(g, _):
        run(g * 4, 4)
        return 0

    lax.fori_loop(0, big, loop, 0)
    base = big * 4

    @pl.when((n & 2) != 0)
    def _():
        run(base, 2)

    @pl.when((n & 1) != 0)
    def _():
        run(base + (n & 2), 1)


def _sb_kernel(q_ref, k_ref, v_ref, o_ref, acc_ref, carry_ref):
    i = pl.program_id(2)
    tq = ATT_BLOCK
    lane = lax.broadcasted_iota(jnp.int32, (1, LANES), 1)
    row = lax.broadcasted_iota(jnp.int32, (tq, tq), 0)
    col = lax.broadcasted_iota(jnp.int32, (tq, tq), 1)
    upper = jnp.where(row > col, 1.0, 0.0).astype(BF16)
    causal = col < row
    q = q_ref[0]
    qh = [jnp.where(lane // HEAD_DIM == h, q, jnp.zeros_like(q)) for h in range(HEADS_PER_TILE)]

    def suffix_in_block(sp):
        hi, lo = _split_bf16(sp)
        return _dot(hi, upper) + _dot(lo, upper)

    def run(first, size, diag=False):
        for h in range(HEADS_PER_TILE):
            carry = None if diag else carry_ref[h]
            acc = None
            for n in range(size):
                blk = i - first - n - (0 if diag else 1)
                st = pl.multiple_of(blk * tq, tq)
                z = _dot_nt(qh[h], k_ref[0, pl.ds(st, tq), :])
                sp = _softplus(z)
                if diag:
                    sp = jnp.where(causal, sp, 0.0)
                logit = z - sp - suffix_in_block(sp)
                if carry is not None:
                    logit = logit - carry
                att = jnp.exp(logit)
                if diag:
                    att = jnp.where(causal, att, 0.0)
                part = _dot(att.astype(BF16), v_ref[0, pl.ds(st, tq), :])
                acc = part if acc is None else acc + part
                rs = jnp.sum(sp, axis=1, keepdims=True)
                carry = rs if carry is None else carry + rs
            acc_ref[h] = acc if diag else acc_ref[h] + acc
            carry_ref[h] = carry

    run(0, 1, diag=True)
    _for_each_group(i, run)
    o_ref[0] = jnp.where(lane < HEAD_DIM, acc_ref[0], acc_ref[1]).astype(BF16)


def _stick_breaking(proj, col0):
    bsz, s, _ = proj.shape
    tq = ATT_BLOCK
    return pl.pallas_call(
        _sb_kernel,
        grid=(bsz, N_PAIRS, s // tq),
        in_specs=[
            pl.BlockSpec((1, tq, LANES), lambda b, p, i: (b, i, col0 + p)),
            pl.BlockSpec((1, s, LANES), lambda b, p, i: (b, 0, col0 + N_PAIRS + p)),
            pl.BlockSpec((1, s, LANES), lambda b, p, i: (b, 0, col0 + 2 * N_PAIRS + p)),
        ],
        out_specs=pl.BlockSpec((1, tq, LANES), lambda b, p, i: (b, i, p)),
        out_shape=jax.ShapeDtypeStruct((bsz, s, W_BRANCH), BF16),
        scratch_shapes=[pltpu.VMEM((HEADS_PER_TILE, tq, LANES), F32),
                        pltpu.VMEM((HEADS_PER_TILE, tq, 1), F32)],
        compiler_params=pltpu.CompilerParams(
            dimension_semantics=("arbitrary", "arbitrary", "arbitrary"),
            vmem_limit_bytes=VMEM_LIMIT),
        name="stick_breaking",
    )(proj, proj, proj)


def _rel_bucket(dist):
    max_exact = REL_BUCKETS // 2
    n = jnp.maximum(dist, 0)
    nf = jnp.maximum(n, 1).astype(F32)
    large = max_exact + (jnp.log(nf / max_exact) / math.log(REL_MAX_DIST / max_exact)
                         * (REL_BUCKETS - max_exact)).astype(jnp.int32)
    large = jnp.minimum(large, REL_BUCKETS - 1)
    return jnp.where(n < max_exact, n, large)


def _moba_bias_tiles(rel_bias):
    t = jnp.arange(MOBA_BLOCK)[:, None]
    s = jnp.arange(MOBA_BLOCK)[None, :]
    d_own = t - s

    def lookup(dist):
        onehot = jax.nn.one_hot(_rel_bucket(dist), REL_BUCKETS, dtype=F32)
        return jnp.einsum("tsk,kh->hts", onehot, rel_bias.astype(F32),
                          precision=lax.Precision.HIGHEST)

    own = jnp.where((d_own >= 0)[None], lookup(d_own), NEG_INF)
    return jnp.stack([own, lookup(d_own + MOBA_BLOCK)], axis=1)


def _moba_kernel(bfar_ref, q_ref, k_ref, v_ref, bias_ref, o_ref,
                 kmh_ref, kml_ref, sel_ref, m_ref, l_ref, acc_ref, *, n_blocks):
    p = pl.program_id(1)
    i = pl.program_id(2)
    tq = MOBA_BLOCK
    s_len = k_ref.shape[1]
    nb_pad = kmh_ref.shape[0]
    lane = lax.broadcasted_iota(jnp.int32, (1, LANES), 1)

    @pl.when(i == 0)
    def _():
        r = lax.broadcasted_iota(jnp.int32, (nb_pad, s_len), 0)
        c = lax.broadcasted_iota(jnp.int32, (nb_pad, s_len), 1)
        pool = jnp.where(c // tq == r, 1.0 / tq, 0.0).astype(BF16)
        hi, lo = _split_bf16(_dot(pool, k_ref[0]))
        kmh_ref[...] = hi
        kml_ref[...] = lo

    q = q_ref[0]
    blk_row = lax.broadcasted_iota(jnp.int32, (nb_pad, tq), 0)
    er = lax.broadcasted_iota(jnp.int32, (tq, tq), 0)
    ec = lax.broadcasted_iota(jnp.int32, (tq, tq), 1)
    eye = jnp.where(er == ec, 1.0, 0.0).astype(BF16)
    head_lanes = [lane // HEAD_DIM == h for h in range(HEADS_PER_TILE)]
    qh = [jnp.where(hl, q, jnp.zeros_like(q)) for hl in head_lanes]

    for h in range(HEADS_PER_TILE):
        kmh = jnp.where(head_lanes[h], kmh_ref[...], jnp.zeros_like(kmh_ref[...]))
        kml = jnp.where(head_lanes[h], kml_ref[...], jnp.zeros_like(kml_ref[...]))
        gate = _dot_nt(kmh, q) + _dot_nt(kml, q)
        cnt = jnp.zeros((nb_pad, tq), F32)
        for n in range(n_blocks):
            gn = gate[n:n + 1, :]
            beats = jnp.where(gn > gate, 1.0, jnp.where((gn == gate) & (n < blk_row), 1.0, 0.0))
            cnt = cnt + jnp.where(n < i, beats, 0.0)
        sel = jnp.where((blk_row < i) & (cnt < MOBA_TOPK), 1.0, 0.0).astype(BF16)
        selq = _dot_nt(eye, sel)
        for n in range(n_blocks):
            sel_ref[h, n] = (selq[:, n:n + 1] - 1.0) * (-NEG_INF)

    def attend(h, blk, bias):
        st = pl.multiple_of(blk * tq, tq)
        s = _dot_nt(qh[h], k_ref[0, pl.ds(st, tq), :]) + bias
        m = jnp.max(s, axis=1, keepdims=True)
        pr = jnp.exp(s - m)
        return (m, jnp.sum(pr, axis=1, keepdims=True),
                _dot(pr.astype(BF16), v_ref[0, pl.ds(st, tq), :]))

    def merge(h, parts, init):
        ms = [part[0] for part in parts]
        if not init:
            ms.append(m_ref[h])
        m_new = functools.reduce(jnp.maximum, ms)
        if init:
            l_new = acc_new = None
        else:
            a = jnp.exp(m_ref[h] - m_new)
            l_new = a * l_ref[h]
            acc_new = a * acc_ref[h]
        for m, l, o in parts:
            w = jnp.exp(m - m_new)
            l_new = w * l if l_new is None else l_new + w * l
            acc_new = w * o if acc_new is None else acc_new + w * o
        m_ref[h] = m_new
        l_ref[h] = l_new
        acc_ref[h] = acc_new

    @pl.when(i == 0)
    def _():
        for h in range(HEADS_PER_TILE):
            merge(h, [attend(h, i, bias_ref[h, 0])], True)

    @pl.when(i >= 1)
    def _():
        for h in range(HEADS_PER_TILE):
            merge(h, [attend(h, i, bias_ref[h, 0]),
                      attend(h, i - 1, bias_ref[h, 1] + sel_ref[h, i - 1])], True)

    def far(first, size):
        for h in range(HEADS_PER_TILE):
            b_far = bfar_ref[p * HEADS_PER_TILE + h]
            merge(h, [attend(h, first + n, sel_ref[h, first + n] + b_far) for n in range(size)],
                  False)

    _for_each_group(jnp.maximum(i - 1, 0), far)

    out = [acc_ref[h] / l_ref[h] for h in range(HEADS_PER_TILE)]
    o_ref[0] = jnp.where(lane < HEAD_DIM, out[0], out[1]).astype(BF16)


def _moba(proj, col0, bias_tiles, b_far):
    bsz, s, _ = proj.shape
    tq = MOBA_BLOCK
    assert s % tq == 0
    n_blocks = s // tq
    nb_pad = 16
    assert n_blocks <= nb_pad
    return pl.pallas_call(
        functools.partial(_moba_kernel, n_blocks=n_blocks),
        grid=(bsz, N_PAIRS, n_blocks),
        in_specs=[
            pl.BlockSpec(memory_space=pltpu.SMEM),
            pl.BlockSpec((1, tq, LANES), lambda b, p, i: (b, i, col0 + p)),
            pl.BlockSpec((1, s, LANES), lambda b, p, i: (b, 0, col0 + N_PAIRS + p)),
            pl.BlockSpec((1, s, LANES), lambda b, p, i: (b, 0, col0 + 2 * N_PAIRS + p)),
            pl.BlockSpec((HEADS_PER_TILE, 2, tq, tq), lambda b, p, i: (p, 0, 0, 0)),
        ],
        out_specs=pl.BlockSpec((1, tq, LANES), lambda b, p, i: (b, i, p)),
        out_shape=jax.ShapeDtypeStruct((bsz, s, W_BRANCH), BF16),
        scratch_shapes=[
            pltpu.VMEM((nb_pad, LANES), BF16),
            pltpu.VMEM((nb_pad, LANES), BF16),
            pltpu.VMEM((HEADS_PER_TILE, n_blocks, tq, 1), F32),
            pltpu.VMEM((HEADS_PER_TILE, tq, 1), F32),
            pltpu.VMEM((HEADS_PER_TILE, tq, 1), F32),
            pltpu.VMEM((HEADS_PER_TILE, tq, LANES), F32),
        ],
        compiler_params=pltpu.CompilerParams(
            dimension_semantics=("arbitrary", "arbitrary", "arbitrary"),
            vmem_limit_bytes=VMEM_LIMIT),
        name="moba",
    )(b_far, proj, proj, proj, bias_tiles)


def _merge_kernel(oa_ref, ob_ref, ga_ref, gb_ref, x_ref, wa_ref, wb_ref, wo_ref, g_ref, gate_ref,
                  o_ref):
    pa = _dot(oa_ref[0], wa_ref[...])
    pb = _dot(ob_ref[0], wb_ref[...])
    merged = (jax.nn.sigmoid(ga_ref[0].astype(F32)) * pa
              + jax.nn.sigmoid(gb_ref[0].astype(F32)) * pb)
    y = _dot(merged.astype(BF16), wo_ref[...])
    o_ref[0] = x_ref[0] + gate_ref[0] * _rms(y, g_ref[...])


def _merge(o_a, o_b, proj, gate_col, x, w_a, w_b, w_o, g_post, gate):
    bsz, s, d = x.shape
    tm = 512
    const = lambda b, t: (0, 0)
    return pl.pallas_call(
        _merge_kernel,
        grid=(bsz, s // tm),
        in_specs=[
            pl.BlockSpec((1, tm, W_BRANCH), lambda b, t: (b, t, 0)),
            pl.BlockSpec((1, tm, W_BRANCH), lambda b, t: (b, t, 0)),
            pl.BlockSpec((1, tm, d), lambda b, t: (b, t, gate_col)),
            pl.BlockSpec((1, tm, d), lambda b, t: (b, t, gate_col + 1)),
            pl.BlockSpec((1, tm, d), lambda b, t: (b, t, 0)),
            pl.BlockSpec((W_BRANCH, d), const),
            pl.BlockSpec((W_BRANCH, d), const),
            pl.BlockSpec((d, d), const),
            pl.BlockSpec((1, d), const),
            pl.BlockSpec((1, 1, d), lambda b, t: (b, 0, 0)),
        ],
        out_specs=pl.BlockSpec((1, tm, d), lambda b, t: (b, t, 0)),
        out_shape=jax.ShapeDtypeStruct((bsz, s, d), F32),
        compiler_params=pltpu.CompilerParams(
            dimension_semantics=("arbitrary", "arbitrary"), vmem_limit_bytes=VMEM_LIMIT),
        name="merge",
    )(o_a, o_b, proj, proj, x, w_a, w_b, w_o, g_post.reshape(1, d), gate.reshape(bsz, 1, d))


FFN_ROWS = 1024
FFN_COLS = 512


def _ffn_kernel(x_ref, gpre_ref, sc_ref, sh_ref, w1_ref, w3_ref, w2_ref, gpost_ref, gate_ref,
                o_ref, h_ref, acc_ref):
    k = pl.program_id(2)

    @pl.when(k == 0)
    def _():
        h = _rms(x_ref[0], gpre_ref[...]) * (1.0 + sc_ref[0]) + sh_ref[0]
        h_ref[...] = h.astype(BF16)
        acc_ref[...] = jnp.zeros_like(acc_ref)

    h = h_ref[...]
    u = jax.nn.silu(_dot(h, w1_ref[...])) * _dot(h, w3_ref[...])
    acc_ref[...] += _dot(u.astype(BF16), w2_ref[...])

    @pl.when(k == pl.num_programs(2) - 1)
    def _():
        o_ref[0] = x_ref[0] + gate_ref[0] * _rms(acc_ref[...], gpost_ref[...])


def _ffn(x, g_pre, scale, shift, w1, w3, w2, g_post, gate):
    bsz, s, d = x.shape
    f = w1.shape[1]
    tm, tf = FFN_ROWS, FFN_COLS
    row = lambda b, t, k: (b, t, 0)
    per_b = lambda b, t, k: (b, 0, 0)
    const = lambda b, t, k: (0, 0)
    return pl.pallas_call(
        _ffn_kernel,
        grid=(bsz, s // tm, f // tf),
        in_specs=[
            pl.BlockSpec((1, tm, d), row),
            pl.BlockSpec((1, d), const),
            pl.BlockSpec((1, 1, d), per_b),
            pl.BlockSpec((1, 1, d), per_b),
            pl.BlockSpec((d, tf), lambda b, t, k: (0, k)),
            pl.BlockSpec((d, tf), lambda b, t, k: (0, k)),
            pl.BlockSpec((tf, d), lambda b, t, k: (k, 0)),
            pl.BlockSpec((1, d), const),
            pl.BlockSpec((1, 1, d), per_b),
        ],
        out_specs=pl.BlockSpec((1, tm, d), row),
        out_shape=jax.ShapeDtypeStruct((bsz, s, d), F32),
        scratch_shapes=[pltpu.VMEM((tm, d), BF16), pltpu.VMEM((tm, d), F32)],
        compiler_params=pltpu.CompilerParams(
            dimension_semantics=("arbitrary",) * 3, vmem_limit_bytes=VMEM_LIMIT),
        name="ffn_dense",
    )(x, g_pre.reshape(1, d), scale.reshape(bsz, 1, d), shift.reshape(bsz, 1, d), w1, w3, w2,
      g_post.reshape(1, d), gate.reshape(bsz, 1, d))


ROUTE_ROWS = 512
ROUTE_LANES = ("expert0", "expert1", "weight0", "weight1", "rank0", "rank1")


def _route_kernel(x_ref, gpre_ref, sc_ref, sh_ref, wr_ref, h_ref, route_ref, cnt_ref, *, n_experts):
    first_step = (pl.program_id(0) == 0) & (pl.program_id(1) == 0)

    @pl.when(first_step)
    def _():
        cnt_ref[...] = jnp.zeros_like(cnt_ref)

    h = _rms(x_ref[0], gpre_ref[...]) * (1.0 + sc_ref[0]) + sh_ref[0]
    h_ref[0] = h
    logits = jnp.dot(h, wr_ref[...], preferred_element_type=F32, precision=lax.Precision.HIGHEST)
    ts = logits.shape[0]
    lane = lax.broadcasted_iota(jnp.int32, logits.shape, 1)
    lg = jnp.where(lane < n_experts, logits, -jnp.inf)
    m1 = jnp.max(lg, axis=1, keepdims=True)
    i1 = jnp.min(jnp.where(lg == m1, lane, LANES), axis=1, keepdims=True)
    first = lane == i1
    lg2 = jnp.where(first, -jnp.inf, lg)
    m2 = jnp.max(lg2, axis=1, keepdims=True)
    i2 = jnp.min(jnp.where(lg2 == m2, lane, LANES), axis=1, keepdims=True)
    second = lane == i2
    e2 = jnp.exp(m2 - m1)
    w_first = 1.0 / (1.0 + e2)
    w_second = e2 * w_first

    onehot = jnp.where(first | second, 1.0, 0.0)
    r = lax.broadcasted_iota(jnp.int32, (ts, ts), 0)
    c = lax.broadcasted_iota(jnp.int32, (ts, ts), 1)
    lower = jnp.where(c < r, 1.0, 0.0).astype(BF16)
    prefix = _dot(lower, onehot.astype(BF16)) + cnt_ref[...]
    rank_first = jnp.sum(jnp.where(first, prefix, 0.0), axis=1, keepdims=True)
    rank_second = jnp.sum(jnp.where(second, prefix, 0.0), axis=1, keepdims=True)
    cnt_ref[...] += jnp.sum(onehot, axis=0, keepdims=True)

    cols = (i1.astype(F32), i2.astype(F32), w_first, w_second, rank_first, rank_second)
    route = jnp.zeros(logits.shape, F32)
    for n, val in enumerate(cols):
        route = jnp.where(lane == n, val, route)
    route_ref[0] = route


def _route(x, g_pre, scale, shift, w_router):
    bsz, s, d = x.shape
    n_experts = w_router.shape[1]
    ts = ROUTE_ROWS
    row = lambda b, t: (b, t, 0)
    per_b = lambda b, t: (b, 0, 0)
    const = lambda b, t: (0, 0)
    return pl.pallas_call(
        functools.partial(_route_kernel, n_experts=n_experts),
        grid=(bsz, s // ts),
        in_specs=[
            pl.BlockSpec((1, ts, d), row),
            pl.BlockSpec((1, d), const),
            pl.BlockSpec((1, 1, d), per_b),
            pl.BlockSpec((1, 1, d), per_b),
            pl.BlockSpec((d, LANES), const),
        ],
        out_specs=[
            pl.BlockSpec((1, ts, d), row),
            pl.BlockSpec((1, ts, LANES), row),
            pl.BlockSpec((1, LANES), const),
        ],
        out_shape=[
            jax.ShapeDtypeStruct((bsz, s, d), F32),
            jax.ShapeDtypeStruct((bsz, s, LANES), F32),
            jax.ShapeDtypeStruct((1, LANES), F32),
        ],
        compiler_params=pltpu.CompilerParams(
            dimension_semantics=("arbitrary", "arbitrary"), vmem_limit_bytes=VMEM_LIMIT),
        name="moe_route",
    )(x, g_pre.reshape(1, d), scale.reshape(bsz, 1, d), shift.reshape(bsz, 1, d),
      jnp.pad(w_router, ((0, 0), (0, LANES - n_experts))))


def _scatter_kernel(pos_ref, fill_ref, h_ref, zeros_ref, xs_ref, sem, *, n_experts):
    ts = h_ref.shape[0]

    def issue(r, _):
        for slot in range(TOP_K):
            dst = pos_ref[0, 0, slot * ts + r]
            pltpu.make_async_copy(h_ref.at[pl.ds(r, 1)], xs_ref.at[pl.ds(dst, 1)], sem).start()
        return 0

    lax.fori_loop(0, ts, issue, 0)
    for slot in range(TOP_K):
        pltpu.make_async_copy(h_ref, xs_ref.at[pl.ds(0, ts)], sem).wait()

    @pl.when(pl.program_id(0) == pl.num_programs(0) - 1)
    def _():
        rows = zeros_ref.shape[0]
        n_tiles = xs_ref.shape[0] // rows

        def zero_row(r):
            return pltpu.make_async_copy(zeros_ref.at[pl.ds(0, 1)], xs_ref.at[pl.ds(r, 1)], sem)

        def zero_tile(j):
            start = pl.multiple_of((fill_ref[2 * n_experts] + j) * rows, rows)
            return pltpu.make_async_copy(zeros_ref, xs_ref.at[pl.ds(start, rows)], sem)

        for e in range(n_experts):
            lax.fori_loop(fill_ref[e], fill_ref[n_experts + e],
                          lambda r, _: (zero_row(r).start(), 0)[1], 0)
        for j in range(n_experts):
            @pl.when(fill_ref[2 * n_experts] + j < n_tiles)
            def _():
                zero_tile(j).start()
        for e in range(n_experts):
            lax.fori_loop(fill_ref[e], fill_ref[n_experts + e],
                          lambda r, _: (zero_row(r).wait(), 0)[1], 0)
        for j in range(n_experts):
            @pl.when(fill_ref[2 * n_experts] + j < n_tiles)
            def _():
                zero_tile(j).wait()


def _scatter(h, pos, fill, n_rows):
    t, d = h.shape
    ts = ROUTE_ROWS
    return pl.pallas_call(
        functools.partial(_scatter_kernel, n_experts=(fill.shape[0] - 1) // 2),
        grid=(t // ts,),
        in_specs=[
            pl.BlockSpec((1, 1, TOP_K * ts), lambda i: (i, 0, 0), memory_space=pltpu.SMEM),
            pl.BlockSpec(memory_space=pltpu.SMEM),
            pl.BlockSpec((ts, d), lambda i: (i, 0)),
            pl.BlockSpec(memory_space=pl.ANY),
        ],
        out_specs=pl.BlockSpec(memory_space=pl.ANY),
        out_shape=jax.ShapeDtypeStruct((n_rows, d), F32),
        scratch_shapes=[pltpu.SemaphoreType.DMA(())],
        compiler_params=pltpu.CompilerParams(
            dimension_semantics=("arbitrary",), vmem_limit_bytes=VMEM_LIMIT),
        name="moe_scatter",
    )(pos, fill, h, jnp.zeros((FFN_ROWS, d), F32))


def _experts_kernel(exp_ref, nact_ref, xs_ref, w1_ref, w3_ref, w2_ref, ys_ref, h_ref):
    i = pl.program_id(0)
    k = pl.program_id(1)

    @pl.when(i < nact_ref[0])
    def _():
        @pl.when(k == 0)
        def _():
            h_ref[...] = xs_ref[...].astype(BF16)

        h = h_ref[...]
        u = jax.nn.silu(_dot(h, w1_ref[0])) * _dot(h, w3_ref[0])
        part = _dot(u.astype(BF16), w2_ref[0])

        @pl.when(k == 0)
        def _():
            ys_ref[...] = part

        @pl.when(k > 0)
        def _():
            ys_ref[...] += part

    @pl.when((i >= nact_ref[0]) & (k == 0))
    def _():
        ys_ref[...] = jnp.zeros_like(ys_ref)


def _experts(xs, tile_expert, n_active, w1, w3, w2):
    n_rows, d = xs.shape
    f = w1.shape[2]
    tm, tf = FFN_ROWS, FFN_COLS
    n_k = f // tf
    n_tiles = n_rows // tm

    def k_of(i, k, nact):
        return jnp.where(i < nact[0], k, n_k - 1)

    return pl.pallas_call(
        _experts_kernel,
        grid_spec=pltpu.PrefetchScalarGridSpec(
            num_scalar_prefetch=2,
            grid=(n_tiles, n_k),
            in_specs=[
                pl.BlockSpec((tm, d), lambda i, k, exp, nact: (i, 0)),
                pl.BlockSpec((1, d, tf), lambda i, k, exp, nact: (exp[i], 0, k_of(i, k, nact))),
                pl.BlockSpec((1, d, tf), lambda i, k, exp, nact: (exp[i], 0, k_of(i, k, nact))),
                pl.BlockSpec((1, tf, d), lambda i, k, exp, nact: (exp[i], k_of(i, k, nact), 0)),
            ],
            out_specs=pl.BlockSpec((tm, d), lambda i, k, exp, nact: (i, 0)),
            scratch_shapes=[pltpu.VMEM((tm, d), BF16)],
        ),
        out_shape=jax.ShapeDtypeStruct((n_rows, d), F32),
        compiler_params=pltpu.CompilerParams(
            dimension_semantics=("arbitrary", "arbitrary"), vmem_limit_bytes=VMEM_LIMIT),
        name="moe_experts",
    )(tile_expert, n_active, xs, w1, w3, w2)


def _combine_kernel(pos_ref, route_ref, x_ref, gpost_ref, gate_ref, ys_ref, o_ref, buf_ref, sem):
    ts = x_ref.shape[0]

    def issue(r, _):
        for slot in range(TOP_K):
            src = pos_ref[0, 0, slot * ts + r]
            pltpu.make_async_copy(ys_ref.at[pl.ds(src, 1)], buf_ref.at[slot, pl.ds(r, 1)],
                                  sem).start()
        return 0

    lax.fori_loop(0, ts, issue, 0)
    for slot in range(TOP_K):
        pltpu.make_async_copy(ys_ref.at[pl.ds(0, ts)], buf_ref.at[slot], sem).wait()

    w0 = ROUTE_LANES.index("weight0")
    route = route_ref[...]
    y = route[:, w0:w0 + 1] * buf_ref[0] + route[:, w0 + 1:w0 + 2] * buf_ref[1]
    o_ref[...] = x_ref[...] + gate_ref[0] * _rms(y, gpost_ref[...])


def _combine(ys, pos, route, x, g_post, gate):
    t, d = x.shape
    bsz = gate.shape[0]
    ts = ROUTE_ROWS
    steps_per_b = t // bsz // ts
    return pl.pallas_call(
        _combine_kernel,
        grid=(t // ts,),
        in_specs=[
            pl.BlockSpec((1, 1, TOP_K * ts), lambda i: (i, 0, 0), memory_space=pltpu.SMEM),
            pl.BlockSpec((ts, LANES), lambda i: (i, 0)),
            pl.BlockSpec((ts, d), lambda i: (i, 0)),
            pl.BlockSpec((1, d), lambda i: (0, 0)),
            pl.BlockSpec((1, 1, d), lambda i: (i // steps_per_b, 0, 0)),
            pl.BlockSpec(memory_space=pl.ANY),
        ],
        out_specs=pl.BlockSpec((ts, d), lambda i: (i, 0)),
        out_shape=jax.ShapeDtypeStruct((t, d), F32),
        scratch_shapes=[pltpu.VMEM((TOP_K, ts, d), F32), pltpu.SemaphoreType.DMA(())],
        compiler_params=pltpu.CompilerParams(
            dimension_semantics=("arbitrary",), vmem_limit_bytes=VMEM_LIMIT),
        name="moe_combine",
    )(pos, route, x, g_post.reshape(1, d), gate, ys)


def _moe(x, g_pre, scale, shift, w_router, w1, w3, w2, g_post, gate):
    bsz, s, d = x.shape
    t = bsz * s
    n_experts = w_router.shape[1]
    ts, tm = ROUTE_ROWS, FFN_ROWS
    h, route, counts = _route(x, g_pre, scale, shift, w_router)
    route = route.reshape(t, LANES)

    lane_of = ROUTE_LANES.index
    expert = route[:, lane_of("expert0"):lane_of("expert0") + TOP_K].astype(jnp.int32)
    rank = route[:, lane_of("rank0"):lane_of("rank0") + TOP_K].astype(jnp.int32)
    counts = counts[0, :n_experts].astype(jnp.int32)
    tiles_per = (counts + tm - 1) // tm
    tile_end = jnp.cumsum(tiles_per)
    first_row = (tile_end - tiles_per) * tm
    n_active = tile_end[-1]
    n_tiles = TOP_K * t // tm + n_experts
    is_expert = expert[..., None] == jnp.arange(n_experts, dtype=jnp.int32)
    pos = jnp.sum(jnp.where(is_expert, first_row, 0), axis=-1) + rank
    pos = pos.reshape(t // ts, ts, TOP_K).transpose(0, 2, 1).reshape(t // ts, 1, TOP_K * ts)
    fill = jnp.concatenate([first_row + counts, tile_end * tm, n_active[None]]).astype(jnp.int32)
    idx = jnp.minimum(jnp.arange(n_tiles, dtype=jnp.int32), n_active - 1)
    tile_expert = jnp.sum(idx[:, None] >= tile_end[None, :], axis=1).astype(jnp.int32)

    xs = _scatter(h.reshape(t, d), pos, fill, n_tiles * tm)
    ys = _experts(xs, tile_expert, n_active.reshape(1), w1, w3, w2)
    out = _combine(ys, pos, route, x.reshape(t, d), g_post, gate.reshape(bsz, 1, d))
    return out.reshape(bsz, s, d)


def kernel(x, c, w_mod, b_mod, g_mix_pre, g_mix_post, g_ffn_pre, g_ffn_post, w_in, w_proj_sb,
           w_proj_moba, w_out, rel_bias, w1_dense, w3_dense, w2_dense, w_router, w1_moe, w3_moe,
           w2_moe):
    depth = w_mod.shape[0]
    d = x.shape[-1]
    qkv = 3 * W_BRANCH
    col = jnp.arange(w_in.shape[-1])
    is_q = (col < W_BRANCH) | ((col >= qkv) & (col < qkv + W_BRANCH))
    col_scale = jnp.where(is_q, HEAD_DIM ** -0.5, 1.0).astype(F32)

    mod = _modulation(c, w_mod, b_mod)
    assert MOBA_BLOCK >= REL_MAX_DIST
    bias_tiles = _moba_bias_tiles(rel_bias)
    b_far = rel_bias[REL_BUCKETS - 1].astype(F32)
    for l in range(depth):
        sh_mix, sc_mix, gt_mix, sh_ffn, sc_ffn, gt_ffn = jnp.split(mod[l], N_MOD, axis=-1)
        proj = _inproj(x, g_mix_pre[l], sc_mix, sh_mix, (w_in[l] * col_scale).astype(BF16))
        o_a = _stick_breaking(proj, 0)
        o_b = _moba(proj, qkv // LANES, bias_tiles, b_far)
        x = _merge(o_a, o_b, proj, 2 * qkv // d, x, w_proj_sb[l].astype(BF16),
                   w_proj_moba[l].astype(BF16), w_out[l].astype(BF16), g_mix_post[l], gt_mix)
        j = l // 2
        if l % 2 == 0:
            x = _ffn(x, g_ffn_pre[l], sc_ffn, sh_ffn, w1_dense[j].astype(BF16),
                     w3_dense[j].astype(BF16), w2_dense[j].astype(BF16), g_ffn_post[l], gt_ffn)
        else:
            x = _moe(x, g_ffn_pre[l], sc_ffn, sh_ffn, w_router[j], w1_moe[j].astype(BF16),
                     w3_moe[j].astype(BF16), w2_moe[j].astype(BF16), g_ffn_post[l], gt_ffn)
    return x
```

```python
import functools
import math

import jax
import jax.numpy as jnp
from jax import lax
from jax.experimental import pallas as pl
from jax.experimental.pallas import tpu as pltpu

F32 = jnp.float32
BF16 = jnp.bfloat16

EPS = 1e-6
NEG_INF = -1e30
HEAD_DIM = 64
N_HEADS = 8
W_BRANCH = N_HEADS * HEAD_DIM
LANES = 128
SUBLANES = 8
HEADS_PER_TILE = LANES // HEAD_DIM
N_PAIRS = N_HEADS // HEADS_PER_TILE
MOBA_BLOCK = 256
MOBA_TOPK = 3
REL_BUCKETS = 32
REL_MAX_DIST = 128
N_MOD = 6
TOP_K = 2
ATT_BLOCK = 256
VMEM_LIMIT = 56 * 1024 * 1024


def _dot(a, b):
    return jnp.dot(a, b, preferred_element_type=F32)


def _dot_nt(a, b):
    return lax.dot_general(a, b, (((1,), (1,)), ((), ())), preferred_element_type=F32)


def _split_bf16(x):
    hi = x.astype(BF16)
    lo = (x - hi.astype(F32)).astype(BF16)
    return hi, lo


def _rms(x, g):
    return x * lax.rsqrt(jnp.mean(x * x, axis=-1, keepdims=True) + EPS) * g


def _mod_kernel(c_ref, w_ref, b_ref, o_ref):
    c = c_ref[...]
    cond = c * jax.nn.sigmoid(c)
    o_ref[0] = jnp.dot(cond, w_ref[0], preferred_element_type=F32,
                       precision=lax.Precision.HIGHEST) + b_ref[0]


def _modulation(c, w_mod, b_mod):
    depth, d, n = w_mod.shape
    bsz = c.shape[0]
    tn = 1536
    return pl.pallas_call(
        _mod_kernel,
        grid=(depth, n // tn),
        in_specs=[
            pl.BlockSpec((bsz, d), lambda l, j: (0, 0)),
            pl.BlockSpec((1, d, tn), lambda l, j: (l, 0, j)),
            pl.BlockSpec((1, 1, tn), lambda l, j: (l, 0, j)),
        ],
        out_specs=pl.BlockSpec((1, bsz, tn), lambda l, j: (l, 0, j)),
        out_shape=jax.ShapeDtypeStruct((depth, bsz, n), F32),
        compiler_params=pltpu.CompilerParams(
            dimension_semantics=("arbitrary", "arbitrary"), vmem_limit_bytes=VMEM_LIMIT),
        name="modulation",
    )(c, w_mod, b_mod.reshape(depth, 1, n))


def _inproj_kernel(x_ref, g_ref, sc_ref, sh_ref, w_ref, o_ref, *, tn):
    h = _rms(x_ref[0], g_ref[...]) * (1.0 + sc_ref[0]) + sh_ref[0]
    hb = h.astype(BF16)
    for n in range(w_ref.shape[1] // tn):
        o_ref[0, :, n * tn:(n + 1) * tn] = _dot(hb, w_ref[:, n * tn:(n + 1) * tn]).astype(BF16)


def _inproj(x, g, scale, shift, w):
    bsz, s, d = x.shape
    n = w.shape[1]
    tm = 512
    return pl.pallas_call(
        functools.partial(_inproj_kernel, tn=1024),
        grid=(bsz, s // tm),
        in_specs=[
            pl.BlockSpec((1, tm, d), lambda b, t: (b, t, 0)),
            pl.BlockSpec((1, d), lambda b, t: (0, 0)),
            pl.BlockSpec((1, 1, d), lambda b, t: (b, 0, 0)),
            pl.BlockSpec((1, 1, d), lambda b, t: (b, 0, 0)),
            pl.BlockSpec((d, n), lambda b, t: (0, 0), pipeline_mode=pl.Buffered(1)),
        ],
        out_specs=pl.BlockSpec((1, tm, n), lambda b, t: (b, t, 0)),
        out_shape=jax.ShapeDtypeStruct((bsz, s, n), BF16),
        compiler_params=pltpu.CompilerParams(
            dimension_semantics=("arbitrary", "arbitrary"), vmem_limit_bytes=VMEM_LIMIT),
        name="inproj",
    )(x, g.reshape(1, d), scale.reshape(bsz, 1, d), shift.reshape(bsz, 1, d), w)


SOFTPLUS_LINEAR = 40.0


def _softplus(z):
    t = jnp.exp(jnp.minimum(z, SOFTPLUS_LINEAR))
    return jnp.where(z > SOFTPLUS_LINEAR, z, jnp.log(1.0 + t))


def _for_each_group(n, run):
    big = n // 4

    def loop(g, _):
        run(g * 4, 4)
        return 0

    lax.fori_loop(0, big, loop, 0)
    base = big * 4

    @pl.when((n & 2) != 0)
    def _():
        run(base, 2)

    @pl.when((n & 1) != 0)
    def _():
        run(base + (n & 2), 1)


def _sb_kernel(q_ref, k_ref, v_ref, o_ref, acc_ref, carry_ref):
    i = pl.program_id(2)
    tq = ATT_BLOCK
    lane = lax.broadcasted_iota(jnp.int32, (1, LANES), 1)
    row = lax.broadcasted_iota(jnp.int32, (tq, tq), 0)
    col = lax.broadcasted_iota(jnp.int32, (tq, tq), 1)
    upper = jnp.where(row > col, 1.0, 0.0).astype(BF16)
    causal = col < row
    q = q_ref[0]
    qh = [jnp.where(lane // HEAD_DIM == h, q, jnp.zeros_like(q)) for h in range(HEADS_PER_TILE)]

    def suffix_in_block(sp):
        hi, lo = _split_bf16(sp)
        return _dot(hi, upper) + _dot(lo, upper)

    def run(first, size, diag=False):
        for h in range(HEADS_PER_TILE):
            carry = None if diag else carry_ref[h]
            acc = None
            for n in range(size):
                blk = i - first - n - (0 if diag else 1)
                st = pl.multiple_of(blk * tq, tq)
                z = _dot_nt(qh[h], k_ref[0, pl.ds(st, tq), :])
                sp = _softplus(z)
                if diag:
                    sp = jnp.where(causal, sp, 0.0)
                logit = z - sp - suffix_in_block(sp)
                if carry is not None:
                    logit = logit - carry
                att = jnp.exp(logit)
                if diag:
                    att = jnp.where(causal, att, 0.0)
                part = _dot(att.astype(BF16), v_ref[0, pl.ds(st, tq), :])
                acc = part if acc is None else acc + part
                rs = jnp.sum(sp, axis=1, keepdims=True)
                carry = rs if carry is None else carry + rs
            acc_ref[h] = acc if diag else acc_ref[h] + acc
            carry_ref[h] = carry

    run(0, 1, diag=True)
    _for_each_group(i, run)
    o_ref[0] = jnp.where(lane < HEAD_DIM, acc_ref[0], acc_ref[1]).astype(BF16)


def _stick_breaking(proj, col0):
    bsz, s, _ = proj.shape
    tq = ATT_BLOCK
    return pl.pallas_call(
        _sb_kernel,
        grid=(bsz, N_PAIRS, s // tq),
        in_specs=[
            pl.BlockSpec((1, tq, LANES), lambda b, p, i: (b, i, col0 + p)),
            pl.BlockSpec((1, s, LANES), lambda b, p, i: (b, 0, col0 + N_PAIRS + p)),
            pl.BlockSpec((1, s, LANES), lambda b, p, i: (b, 0, col0 + 2 * N_PAIRS + p)),
        ],
        out_specs=pl.BlockSpec((1, tq, LANES), lambda b, p, i: (b, i, p)),
        out_shape=jax.ShapeDtypeStruct((bsz, s, W_BRANCH), BF16),
        scratch_shapes=[pltpu.VMEM((HEADS_PER_TILE, tq, LANES), F32),
                        pltpu.VMEM((HEADS_PER_TILE, tq, 1), F32)],
        compiler_params=pltpu.CompilerParams(
            dimension_semantics=("arbitrary", "arbitrary", "arbitrary"),
            vmem_limit_bytes=VMEM_LIMIT),
        name="stick_breaking",
    )(proj, proj, proj)


def _rel_bucket(dist):
    max_exact = REL_BUCKETS // 2
    n = jnp.maximum(dist, 0)
    nf = jnp.maximum(n, 1).astype(F32)
    large = max_exact + (jnp.log(nf / max_exact) / math.log(REL_MAX_DIST / max_exact)
                         * (REL_BUCKETS - max_exact)).astype(jnp.int32)
    large = jnp.minimum(large, REL_BUCKETS - 1)
    return jnp.where(n < max_exact, n, large)


def _moba_bias_tiles(rel_bias):
    t = jnp.arange(MOBA_BLOCK)[:, None]
    s = jnp.arange(MOBA_BLOCK)[None, :]
    d_own = t - s

    def lookup(dist):
        onehot = jax.nn.one_hot(_rel_bucket(dist), REL_BUCKETS, dtype=F32)
        return jnp.einsum("tsk,kh->hts", onehot, rel_bias.astype(F32),
                          precision=lax.Precision.HIGHEST)

    own = jnp.where((d_own >= 0)[None], lookup(d_own), NEG_INF)
    return jnp.stack([own, lookup(d_own + MOBA_BLOCK)], axis=1)


def _moba_kernel(bfar_ref, q_ref, k_ref, v_ref, bias_ref, o_ref,
                 kmh_ref, kml_ref, sel_ref, m_ref, l_ref, acc_ref, *, n_blocks):
    p = pl.program_id(1)
    i = pl.program_id(2)
    tq = MOBA_BLOCK
    s_len = k_ref.shape[1]
    nb_pad = kmh_ref.shape[0]
    lane = lax.broadcasted_iota(jnp.int32, (1, LANES), 1)

    @pl.when(i == 0)
    def _():
        r = lax.broadcasted_iota(jnp.int32, (nb_pad, s_len), 0)
        c = lax.broadcasted_iota(jnp.int32, (nb_pad, s_len), 1)
        pool = jnp.where(c // tq == r, 1.0 / tq, 0.0).astype(BF16)
        hi, lo = _split_bf16(_dot(pool, k_ref[0]))
        kmh_ref[...] = hi
        kml_ref[...] = lo

    q = q_ref[0]
    blk_row = lax.broadcasted_iota(jnp.int32, (nb_pad, tq), 0)
    er = lax.broadcasted_iota(jnp.int32, (tq, tq), 0)
    ec = lax.broadcasted_iota(jnp.int32, (tq, tq), 1)
    eye = jnp.where(er == ec, 1.0, 0.0).astype(BF16)
    head_lanes = [lane // HEAD_DIM == h for h in range(HEADS_PER_TILE)]
    qh = [jnp.where(hl, q, jnp.zeros_like(q)) for hl in head_lanes]

    for h in range(HEADS_PER_TILE):
        kmh = jnp.where(head_lanes[h], kmh_ref[...], jnp.zeros_like(kmh_ref[...]))
        kml = jnp.where(head_lanes[h], kml_ref[...], jnp.zeros_like(kml_ref[...]))
        gate = _dot_nt(kmh, q) + _dot_nt(kml, q)
        cnt = jnp.zeros((nb_pad, tq), F32)
        for n in range(n_blocks):
            gn = gate[n:n + 1, :]
            beats = jnp.where(gn > gate, 1.0, jnp.where((gn == gate) & (n < blk_row), 1.0, 0.0))
            cnt = cnt + jnp.where(n < i, beats, 0.0)
        sel = jnp.where((blk_row < i) & (cnt < MOBA_TOPK), 1.0, 0.0).astype(BF16)
        selq = _dot_nt(eye, sel)
        for n in range(n_blocks):
            sel_ref[h, n] = (selq[:, n:n + 1] - 1.0) * (-NEG_INF)

    def attend(h, blk, bias):
        st = pl.multiple_of(blk * tq, tq)
        s = _dot_nt(qh[h], k_ref[0, pl.ds(st, tq), :]) + bias
        m = jnp.max(s, axis=1, keepdims=True)
        pr = jnp.exp(s - m)
        return (m, jnp.sum(pr, axis=1, keepdims=True),
                _dot(pr.astype(BF16), v_ref[0, pl.ds(st, tq), :]))

    def merge(h, parts, init):
        ms = [part[0] for part in parts]
        if not init:
            ms.append(m_ref[h])
        m_new = functools.reduce(jnp.maximum, ms)
        if init:
            l_new = acc_new = None
        else:
            a = jnp.exp(m_ref[h] - m_new)
            l_new = a * l_ref[h]
            acc_new = a * acc_ref[h]
        for m, l, o in parts:
            w = jnp.exp(m - m_new)
            l_new = w * l if l_new is None else l_new + w * l
            acc_new = w * o if acc_new is None else acc_new + w * o
        m_ref[h] = m_new
        l_ref[h] = l_new
        acc_ref[h] = acc_new

    @pl.when(i == 0)
    def _():
        for h in range(HEADS_PER_TILE):
            merge(h, [attend(h, i, bias_ref[h, 0])], True)

    @pl.when(i >= 1)
    def _():
        for h in range(HEADS_PER_TILE):
            merge(h, [attend(h, i, bias_ref[h, 0]),
                      attend(h, i - 1, bias_ref[h, 1] + sel_ref[h, i - 1])], True)

    def far(first, size):
        for h in range(HEADS_PER_TILE):
            b_far = bfar_ref[p * HEADS_PER_TILE + h]
            merge(h, [attend(h, first + n, sel_ref[h, first + n] + b_far) for n in range(size)],
                  False)

    _for_each_group(jnp.maximum(i - 1, 0), far)

    out = [acc_ref[h] / l_ref[h] for h in range(HEADS_PER_TILE)]
    o_ref[0] = jnp.where(lane < HEAD_DIM, out[0], out[1]).astype(BF16)


def _moba(proj, col0, bias_tiles, b_far):
    bsz, s, _ = proj.shape
    tq = MOBA_BLOCK
    assert s % tq == 0
    n_blocks = s // tq
    nb_pad = 16
    assert n_blocks <= nb_pad
    return pl.pallas_call(
        functools.partial(_moba_kernel, n_blocks=n_blocks),
        grid=(bsz, N_PAIRS, n_blocks),
        in_specs=[
            pl.BlockSpec(memory_space=pltpu.SMEM),
            pl.BlockSpec((1, tq, LANES), lambda b, p, i: (b, i, col0 + p)),
            pl.BlockSpec((1, s, LANES), lambda b, p, i: (b, 0, col0 + N_PAIRS + p)),
            pl.BlockSpec((1, s, LANES), lambda b, p, i: (b, 0, col0 + 2 * N_PAIRS + p)),
            pl.BlockSpec((HEADS_PER_TILE, 2, tq, tq), lambda b, p, i: (p, 0, 0, 0)),
        ],
        out_specs=pl.BlockSpec((1, tq, LANES), lambda b, p, i: (b, i, p)),
        out_shape=jax.ShapeDtypeStruct((bsz, s, W_BRANCH), BF16),
        scratch_shapes=[
            pltpu.VMEM((nb_pad, LANES), BF16),
            pltpu.VMEM((nb_pad, LANES), BF16),
            pltpu.VMEM((HEADS_PER_TILE, n_blocks, tq, 1), F32),
            pltpu.VMEM((HEADS_PER_TILE, tq, 1), F32),
            pltpu.VMEM((HEADS_PER_TILE, tq, 1), F32),
            pltpu.VMEM((HEADS_PER_TILE, tq, LANES), F32),
        ],
        compiler_params=pltpu.CompilerParams(
            dimension_semantics=("arbitrary", "arbitrary", "arbitrary"),
            vmem_limit_bytes=VMEM_LIMIT),
        name="moba",
    )(b_far, proj, proj, proj, bias_tiles)


def _merge_kernel(oa_ref, ob_ref, ga_ref, gb_ref, x_ref, wa_ref, wb_ref, wo_ref, g_ref, gate_ref,
                  o_ref):
    pa = _dot(oa_ref[0], wa_ref[...])
    pb = _dot(ob_ref[0], wb_ref[...])
    merged = (jax.nn.sigmoid(ga_ref[0].astype(F32)) * pa
              + jax.nn.sigmoid(gb_ref[0].astype(F32)) * pb)
    y = _dot(merged.astype(BF16), wo_ref[...])
    o_ref[0] = x_ref[0] + gate_ref[0] * _rms(y, g_ref[...])


def _merge(o_a, o_b, proj, gate_col, x, w_a, w_b, w_o, g_post, gate):
    bsz, s, d = x.shape
    tm = 512
    const = lambda b, t: (0, 0)
    return pl.pallas_call(
        _merge_kernel,
        grid=(bsz, s // tm),
        in_specs=[
            pl.BlockSpec((1, tm, W_BRANCH), lambda b, t: (b, t, 0)),
            pl.BlockSpec((1, tm, W_BRANCH), lambda b, t: (b, t, 0)),
            pl.BlockSpec((1, tm, d), lambda b, t: (b, t, gate_col)),
            pl.BlockSpec((1, tm, d), lambda b, t: (b, t, gate_col + 1)),
            pl.BlockSpec((1, tm, d), lambda b, t: (b, t, 0)),
            pl.BlockSpec((W_BRANCH, d), const),
            pl.BlockSpec((W_BRANCH, d), const),
            pl.BlockSpec((d, d), const),
            pl.BlockSpec((1, d), const),
            pl.BlockSpec((1, 1, d), lambda b, t: (b, 0, 0)),
        ],
        out_specs=pl.BlockSpec((1, tm, d), lambda b, t: (b, t, 0)),
        out_shape=jax.ShapeDtypeStruct((bsz, s, d), F32),
        compiler_params=pltpu.CompilerParams(
            dimension_semantics=("arbitrary", "arbitrary"), vmem_limit_bytes=VMEM_LIMIT),
        name="merge",
    )(o_a, o_b, proj, proj, x, w_a, w_b, w_o, g_post.reshape(1, d), gate.reshape(bsz, 1, d))


FFN_ROWS = 1024
FFN_COLS = 512


def _ffn_kernel(x_ref, gpre_ref, sc_ref, sh_ref, w1_ref, w3_ref, w2_ref, gpost_ref, gate_ref,
                o_ref, h_ref, acc_ref):
    k = pl.program_id(2)

    @pl.when(k == 0)
    def _():
        h = _rms(x_ref[0], gpre_ref[...]) * (1.0 + sc_ref[0]) + sh_ref[0]
        h_ref[...] = h.astype(BF16)
        acc_ref[...] = jnp.zeros_like(acc_ref)

    h = h_ref[...]
    u = jax.nn.silu(_dot(h, w1_ref[...])) * _dot(h, w3_ref[...])
    acc_ref[...] += _dot(u.astype(BF16), w2_ref[...])

    @pl.when(k == pl.num_programs(2) - 1)
    def _():
        o_ref[0] = x_ref[0] + gate_ref[0] * _rms(acc_ref[...], gpost_ref[...])


def _ffn(x, g_pre, scale, shift, w1, w3, w2, g_post, gate):
    bsz, s, d = x.shape
    f = w1.shape[1]
    tm, tf = FFN_ROWS, FFN_COLS
    row = lambda b, t, k: (b, t, 0)
    per_b = lambda b, t, k: (b, 0, 0)
    const = lambda b, t, k: (0, 0)
    return pl.pallas_call(
        _ffn_kernel,
        grid=(bsz, s // tm, f // tf),
        in_specs=[
            pl.BlockSpec((1, tm, d), row),
            pl.BlockSpec((1, d), const),
            pl.BlockSpec((1, 1, d), per_b),
            pl.BlockSpec((1, 1, d), per_b),
            pl.BlockSpec((d, tf), lambda b, t, k: (0, k)),
            pl.BlockSpec((d, tf), lambda b, t, k: (0, k)),
            pl.BlockSpec((tf, d), lambda b, t, k: (k, 0)),
            pl.BlockSpec((1, d), const),
            pl.BlockSpec((1, 1, d), per_b),
        ],
        out_specs=pl.BlockSpec((1, tm, d), row),
        out_shape=jax.ShapeDtypeStruct((bsz, s, d), F32),
        scratch_shapes=[pltpu.VMEM((tm, d), BF16), pltpu.VMEM((tm, d), F32)],
        compiler_params=pltpu.CompilerParams(
            dimension_semantics=("arbitrary",) * 3, vmem_limit_bytes=VMEM_LIMIT),
        name="ffn_dense",
    )(x, g_pre.reshape(1, d), scale.reshape(bsz, 1, d), shift.reshape(bsz, 1, d), w1, w3, w2,
      g_post.reshape(1, d), gate.reshape(bsz, 1, d))


ROUTE_ROWS = 512


def _to_token_major(ref, val):
    rows, n = val.shape[0], val.shape[1] // LANES
    for s in range(n):
        ref[pl.ds(s, rows, stride=n), :] = val[:, s * LANES:(s + 1) * LANES]


def _from_token_major(ref, n):
    rows = ref.shape[0] // n
    return jnp.concatenate([ref[pl.ds(s, rows, stride=n), :] for s in range(n)], axis=1)
ROUTE_LANES = ("expert0", "expert1", "weight0", "weight1", "rank0", "rank1")


def _route_kernel(x_ref, gpre_ref, sc_ref, sh_ref, wr_ref, h_ref, route_ref, cnt_ref, *, n_experts):
    first_step = (pl.program_id(0) == 0) & (pl.program_id(1) == 0)

    @pl.when(first_step)
    def _():
        cnt_ref[...] = jnp.zeros_like(cnt_ref)

    h = _rms(x_ref[0], gpre_ref[...]) * (1.0 + sc_ref[0]) + sh_ref[0]
    _to_token_major(h_ref, h)
    logits = jnp.dot(h, wr_ref[...], preferred_element_type=F32, precision=lax.Precision.HIGHEST)
    ts = logits.shape[0]
    lane = lax.broadcasted_iota(jnp.int32, logits.shape, 1)
    lg = jnp.where(lane < n_experts, logits, -jnp.inf)
    m1 = jnp.max(lg, axis=1, keepdims=True)
    i1 = jnp.min(jnp.where(lg == m1, lane, LANES), axis=1, keepdims=True)
    first = lane == i1
    lg2 = jnp.where(first, -jnp.inf, lg)
    m2 = jnp.max(lg2, axis=1, keepdims=True)
    i2 = jnp.min(jnp.where(lg2 == m2, lane, LANES), axis=1, keepdims=True)
    second = lane == i2
    e2 = jnp.exp(m2 - m1)
    w_first = 1.0 / (1.0 + e2)
    w_second = e2 * w_first

    onehot = jnp.where(first | second, 1.0, 0.0)
    r = lax.broadcasted_iota(jnp.int32, (ts, ts), 0)
    c = lax.broadcasted_iota(jnp.int32, (ts, ts), 1)
    lower = jnp.where(c < r, 1.0, 0.0).astype(BF16)
    prefix = _dot(lower, onehot.astype(BF16)) + cnt_ref[...]
    rank_first = jnp.sum(jnp.where(first, prefix, 0.0), axis=1, keepdims=True)
    rank_second = jnp.sum(jnp.where(second, prefix, 0.0), axis=1, keepdims=True)
    cnt_ref[...] += jnp.sum(onehot, axis=0, keepdims=True)

    cols = (i1.astype(F32), i2.astype(F32), w_first, w_second, rank_first, rank_second)
    route = jnp.zeros(logits.shape, F32)
    for n, val in enumerate(cols):
        route = jnp.where(lane == n, val, route)
    route_ref[0] = route


def _route(x, g_pre, scale, shift, w_router):
    bsz, s, d = x.shape
    n_experts = w_router.shape[1]
    ts = ROUTE_ROWS
    row = lambda b, t: (b, t, 0)
    per_b = lambda b, t: (b, 0, 0)
    const = lambda b, t: (0, 0)
    return pl.pallas_call(
        functools.partial(_route_kernel, n_experts=n_experts),
        grid=(bsz, s // ts),
        in_specs=[
            pl.BlockSpec((1, ts, d), row),
            pl.BlockSpec((1, d), const),
            pl.BlockSpec((1, 1, d), per_b),
            pl.BlockSpec((1, 1, d), per_b),
            pl.BlockSpec((d, LANES), const),
        ],
        out_specs=[
            pl.BlockSpec((ts * d // LANES, LANES), lambda b, t: (b * (s // ts) + t, 0)),
            pl.BlockSpec((1, ts, LANES), row),
            pl.BlockSpec((1, LANES), const),
        ],
        out_shape=[
            jax.ShapeDtypeStruct((bsz * s * d // LANES, LANES), F32),
            jax.ShapeDtypeStruct((bsz, s, LANES), F32),
            jax.ShapeDtypeStruct((1, LANES), F32),
        ],
        compiler_params=pltpu.CompilerParams(
            dimension_semantics=("arbitrary", "arbitrary"), vmem_limit_bytes=VMEM_LIMIT),
        name="moe_route",
    )(x, g_pre.reshape(1, d), scale.reshape(bsz, 1, d), shift.reshape(bsz, 1, d),
      jnp.pad(w_router, ((0, 0), (0, LANES - n_experts))))


def _scatter_kernel(pos_ref, fill_ref, h_ref, zeros_ref, xs_ref, sem, *, n_experts):
    ts = h_ref.shape[0]

    def issue(r, _):
        for slot in range(TOP_K):
            dst = pos_ref[0, 0, slot * ts + r]
            pltpu.make_async_copy(h_ref.at[r], xs_ref.at[dst], sem).start()
        return 0

    lax.fori_loop(0, ts, issue, 0)
    for slot in range(TOP_K):
        pltpu.make_async_copy(h_ref, xs_ref.at[pl.ds(0, ts)], sem).wait()

    @pl.when(pl.program_id(0) == pl.num_programs(0) - 1)
    def _():
        rows = zeros_ref.shape[0]
        n_tiles = xs_ref.shape[0] // rows

        def zero_row(r):
            return pltpu.make_async_copy(zeros_ref.at[0], xs_ref.at[r], sem)

        def zero_tile(j):
            start = pl.multiple_of((fill_ref[2 * n_experts] + j) * rows, rows)
            return pltpu.make_async_copy(zeros_ref, xs_ref.at[pl.ds(start, rows)], sem)

        for e in range(n_experts):
            lax.fori_loop(fill_ref[e], fill_ref[n_experts + e],
                          lambda r, _: (zero_row(r).start(), 0)[1], 0)
        for j in range(n_experts):
            @pl.when(fill_ref[2 * n_experts] + j < n_tiles)
            def _():
                zero_tile(j).start()
        for e in range(n_experts):
            lax.fori_loop(fill_ref[e], fill_ref[n_experts + e],
                          lambda r, _: (zero_row(r).wait(), 0)[1], 0)
        for j in range(n_experts):
            @pl.when(fill_ref[2 * n_experts] + j < n_tiles)
            def _():
                zero_tile(j).wait()


def _scatter(h, pos, fill, n_rows):
    t = h.shape[0]
    ts = ROUTE_ROWS
    return pl.pallas_call(
        functools.partial(_scatter_kernel, n_experts=(fill.shape[0] - 1) // 2),
        grid=(t // ts,),
        in_specs=[
            pl.BlockSpec((1, 1, TOP_K * ts), lambda i: (i, 0, 0), memory_space=pltpu.SMEM),
            pl.BlockSpec(memory_space=pltpu.SMEM),
            pl.BlockSpec((ts,) + h.shape[1:], lambda i: (i, 0, 0)),
            pl.BlockSpec(memory_space=pl.ANY),
        ],
        out_specs=pl.BlockSpec(memory_space=pl.ANY),
        out_shape=jax.ShapeDtypeStruct((n_rows,) + h.shape[1:], F32),
        scratch_shapes=[pltpu.SemaphoreType.DMA(())],
        compiler_params=pltpu.CompilerParams(
            dimension_semantics=("arbitrary",), vmem_limit_bytes=VMEM_LIMIT),
        name="moe_scatter",
    )(pos, fill, h, jnp.zeros((FFN_ROWS,) + h.shape[1:], F32))


def _experts_kernel(exp_ref, nact_ref, xs_ref, w1_ref, w3_ref, w2_ref, ys_ref, h_ref, acc_ref):
    i = pl.program_id(0)
    k = pl.program_id(1)

    @pl.when(i < nact_ref[0])
    def _():
        @pl.when(k == 0)
        def _():
            h_ref[...] = _from_token_major(xs_ref, h_ref.shape[1] // LANES).astype(BF16)

        h = h_ref[...]
        u = jax.nn.silu(_dot(h, w1_ref[0])) * _dot(h, w3_ref[0])
        part = _dot(u.astype(BF16), w2_ref[0])

        @pl.when(k == 0)
        def _():
            acc_ref[...] = part

        @pl.when(k > 0)
        def _():
            acc_ref[...] += part

        @pl.when(k == pl.num_programs(1) - 1)
        def _():
            _to_token_major(ys_ref, acc_ref[...])

    @pl.when((i >= nact_ref[0]) & (k == 0))
    def _():
        ys_ref[...] = jnp.zeros_like(ys_ref)


def _experts(xs, tile_expert, n_active, w1, w3, w2):
    d, f = w1.shape[1:]
    n_rows = xs.shape[0] * LANES // d
    tm, tf = FFN_ROWS, FFN_COLS
    n_k = f // tf
    n_tiles = n_rows // tm

    def k_of(i, k, nact):
        return jnp.where(i < nact[0], k, n_k - 1)

    return pl.pallas_call(
        _experts_kernel,
        grid_spec=pltpu.PrefetchScalarGridSpec(
            num_scalar_prefetch=2,
            grid=(n_tiles, n_k),
            in_specs=[
                pl.BlockSpec((tm * d // LANES, LANES), lambda i, k, exp, nact: (i, 0)),
                pl.BlockSpec((1, d, tf), lambda i, k, exp, nact: (exp[i], 0, k_of(i, k, nact))),
                pl.BlockSpec((1, d, tf), lambda i, k, exp, nact: (exp[i], 0, k_of(i, k, nact))),
                pl.BlockSpec((1, tf, d), lambda i, k, exp, nact: (exp[i], k_of(i, k, nact), 0)),
            ],
            out_specs=pl.BlockSpec((tm * d // LANES, LANES), lambda i, k, exp, nact: (i, 0)),
            scratch_shapes=[pltpu.VMEM((tm, d), BF16), pltpu.VMEM((tm, d), F32)],
        ),
        out_shape=jax.ShapeDtypeStruct(xs.shape, F32),
        compiler_params=pltpu.CompilerParams(
            dimension_semantics=("arbitrary", "arbitrary"), vmem_limit_bytes=VMEM_LIMIT),
        name="moe_experts",
    )(tile_expert, n_active, xs, w1, w3, w2)


def _combine_kernel(pos_ref, route_ref, x_ref, gpost_ref, gate_ref, ys_ref, o_ref, buf0_ref,
                    buf1_ref, sem):
    ts = x_ref.shape[0]
    n = ys_ref.shape[1]
    bufs = (buf0_ref, buf1_ref)

    def issue(r, _):
        dst = pl.multiple_of(r * n, n)
        for slot in range(TOP_K):
            src = pos_ref[0, 0, slot * ts + r]
            pltpu.make_async_copy(ys_ref.at[src], bufs[slot].at[pl.ds(dst, n)], sem).start()
        return 0

    lax.fori_loop(0, ts, issue, 0)
    for slot in range(TOP_K):
        pltpu.make_async_copy(bufs[slot], bufs[slot], sem).wait()

    w0 = ROUTE_LANES.index("weight0")
    route = route_ref[...]
    y = (route[:, w0:w0 + 1] * _from_token_major(buf0_ref, n)
         + route[:, w0 + 1:w0 + 2] * _from_token_major(buf1_ref, n))
    o_ref[...] = x_ref[...] + gate_ref[0] * _rms(y, gpost_ref[...])


def _combine(ys, pos, route, x, g_post, gate):
    t, d = x.shape
    bsz = gate.shape[0]
    ts = ROUTE_ROWS
    steps_per_b = t // bsz // ts
    return pl.pallas_call(
        _combine_kernel,
        grid=(t // ts,),
        in_specs=[
            pl.BlockSpec((1, 1, TOP_K * ts), lambda i: (i, 0, 0), memory_space=pltpu.SMEM),
            pl.BlockSpec((ts, LANES), lambda i: (i, 0)),
            pl.BlockSpec((ts, d), lambda i: (i, 0)),
            pl.BlockSpec((1, d), lambda i: (0, 0)),
            pl.BlockSpec((1, 1, d), lambda i: (i // steps_per_b, 0, 0)),
            pl.BlockSpec(memory_space=pl.ANY),
        ],
        out_specs=pl.BlockSpec((ts, d), lambda i: (i, 0)),
        out_shape=jax.ShapeDtypeStruct((t, d), F32),
        scratch_shapes=[pltpu.VMEM((ts * d // LANES, LANES), F32)] * TOP_K
        + [pltpu.SemaphoreType.DMA(())],
        compiler_params=pltpu.CompilerParams(
            dimension_semantics=("arbitrary",), vmem_limit_bytes=VMEM_LIMIT),
        name="moe_combine",
    )(pos, route, x, g_post.reshape(1, d), gate, ys)


def _moe(x, g_pre, scale, shift, w_router, w1, w3, w2, g_post, gate):
    bsz, s, d = x.shape
    t = bsz * s
    n_experts = w_router.shape[1]
    ts, tm = ROUTE_ROWS, FFN_ROWS
    h, route, counts = _route(x, g_pre, scale, shift, w_router)
    route = route.reshape(t, LANES)

    lane_of = ROUTE_LANES.index
    expert = route[:, lane_of("expert0"):lane_of("expert0") + TOP_K].astype(jnp.int32)
    rank = route[:, lane_of("rank0"):lane_of("rank0") + TOP_K].astype(jnp.int32)
    counts = counts[0, :n_experts].astype(jnp.int32)
    tiles_per = (counts + tm - 1) // tm
    tile_end = jnp.cumsum(tiles_per)
    first_row = (tile_end - tiles_per) * tm
    n_active = tile_end[-1]
    n_tiles = TOP_K * t // tm + n_experts
    is_expert = expert[..., None] == jnp.arange(n_experts, dtype=jnp.int32)
    pos = jnp.sum(jnp.where(is_expert, first_row, 0), axis=-1) + rank
    pos = pos.reshape(t // ts, ts, TOP_K).transpose(0, 2, 1).reshape(t // ts, 1, TOP_K * ts)
    fill = jnp.concatenate([first_row + counts, tile_end * tm, n_active[None]]).astype(jnp.int32)
    idx = jnp.minimum(jnp.arange(n_tiles, dtype=jnp.int32), n_active - 1)
    tile_expert = jnp.sum(idx[:, None] >= tile_end[None, :], axis=1).astype(jnp.int32)

    token_major = (-1, d // LANES, LANES)
    xs = _scatter(h.reshape(token_major), pos, fill, n_tiles * tm)
    ys = _experts(xs.reshape(-1, LANES), tile_expert, n_active.reshape(1), w1, w3, w2)
    out = _combine(ys.reshape(token_major), pos, route, x.reshape(t, d), g_post,
                   gate.reshape(bsz, 1, d))
    return out.reshape(bsz, s, d)


def kernel(x, c, w_mod, b_mod, g_mix_pre, g_mix_post, g_ffn_pre, g_ffn_post, w_in, w_proj_sb,
           w_proj_moba, w_out, rel_bias, w1_dense, w3_dense, w2_dense, w_router, w1_moe, w3_moe,
           w2_moe):
    depth = w_mod.shape[0]
    d = x.shape[-1]
    qkv = 3 * W_BRANCH
    col = jnp.arange(w_in.shape[-1])
    is_q = (col < W_BRANCH) | ((col >= qkv) & (col < qkv + W_BRANCH))
    col_scale = jnp.where(is_q, HEAD_DIM ** -0.5, 1.0).astype(F32)

    mod = _modulation(c, w_mod, b_mod)
    assert MOBA_BLOCK >= REL_MAX_DIST
    bias_tiles = _moba_bias_tiles(rel_bias)
    b_far = rel_bias[REL_BUCKETS - 1].astype(F32)
    for l in range(depth):
        sh_mix, sc_mix, gt_mix, sh_ffn, sc_ffn, gt_ffn = jnp.split(mod[l], N_MOD, axis=-1)
        proj = _inproj(x, g_mix_pre[l], sc_mix, sh_mix, (w_in[l] * col_scale).astype(BF16))
        o_a = _stick_breaking(proj, 0)
        o_b = _moba(proj, qkv // LANES, bias_tiles, b_far)
        x = _merge(o_a, o_b, proj, 2 * qkv // d, x, w_proj_sb[l].astype(BF16),
                   w_proj_moba[l].astype(BF16), w_out[l].astype(BF16), g_mix_post[l], gt_mix)
        j = l // 2
        if l % 2 == 0:
            x = _ffn(x, g_ffn_pre[l], sc_ffn, sh_ffn, w1_dense[j].astype(BF16),
                     w3_dense[j].astype(BF16), w2_dense[j].astype(BF16), g_ffn_post[l], gt_ffn)
        else:
            x = _moe(x, g_ffn_pre[l], sc_ffn, sh_ffn, w_router[j], w1_moe[j].astype(BF16),
                     w3_moe[j].astype(BF16), w2_moe[j].astype(BF16), g_ffn_post[l], gt_ffn)
    return x
```

```python
import functools
import math

import jax
import jax.numpy as jnp
from jax import lax
from jax.experimental import pallas as pl
from jax.experimental.pallas import tpu as pltpu

F32 = jnp.float32
BF16 = jnp.bfloat16

EPS = 1e-6
NEG_INF = -1e30
HEAD_DIM = 64
N_HEADS = 8
W_BRANCH = N_HEADS * HEAD_DIM
LANES = 128
SUBLANES = 8
HEADS_PER_TILE = LANES // HEAD_DIM
N_PAIRS = N_HEADS // HEADS_PER_TILE
MOBA_BLOCK = 256
MOBA_TOPK = 3
REL_BUCKETS = 32
REL_MAX_DIST = 128
N_MOD = 6
TOP_K = 2
ATT_BLOCK = 256
VMEM_LIMIT = 56 * 1024 * 1024


def _dot(a, b):
    return jnp.dot(a, b, preferred_element_type=F32)


def _dot_nt(a, b):
    return lax.dot_general(a, b, (((1,), (1,)), ((), ())), preferred_element_type=F32)


def _split_bf16(x):
    hi = x.astype(BF16)
    lo = (x - hi.astype(F32)).astype(BF16)
    return hi, lo


def _rms(x, g):
    return x * lax.rsqrt(jnp.mean(x * x, axis=-1, keepdims=True) + EPS) * g


def _mod_kernel(c_ref, w_ref, b_ref, o_ref):
    c = c_ref[...]
    cond = c * jax.nn.sigmoid(c)
    o_ref[0] = jnp.dot(cond, w_ref[0], preferred_element_type=F32,
                       precision=lax.Precision.HIGHEST) + b_ref[0]


def _modulation(c, w_mod, b_mod):
    depth, d, n = w_mod.shape
    bsz = c.shape[0]
    tn = 1536
    return pl.pallas_call(
        _mod_kernel,
        grid=(depth, n // tn),
        in_specs=[
            pl.BlockSpec((bsz, d), lambda l, j: (0, 0)),
            pl.BlockSpec((1, d, tn), lambda l, j: (l, 0, j)),
            pl.BlockSpec((1, 1, tn), lambda l, j: (l, 0, j)),
        ],
        out_specs=pl.BlockSpec((1, bsz, tn), lambda l, j: (l, 0, j)),
        out_shape=jax.ShapeDtypeStruct((depth, bsz, n), F32),
        compiler_params=pltpu.CompilerParams(
            dimension_semantics=("arbitrary", "arbitrary"), vmem_limit_bytes=VMEM_LIMIT),
        name="modulation",
    )(c, w_mod, b_mod.reshape(depth, 1, n))


def _inproj_kernel(x_ref, g_ref, sc_ref, sh_ref, w_ref, o_ref, *, tn):
    h = _rms(x_ref[0], g_ref[...]) * (1.0 + sc_ref[0]) + sh_ref[0]
    hb = h.astype(BF16)
    for n in range(w_ref.shape[1] // tn):
        o_ref[0, :, n * tn:(n + 1) * tn] = _dot(hb, w_ref[:, n * tn:(n + 1) * tn]).astype(BF16)


def _inproj(x, g, scale, shift, w):
    bsz, s, d = x.shape
    n = w.shape[1]
    tm = 512
    return pl.pallas_call(
        functools.partial(_inproj_kernel, tn=1024),
        grid=(bsz, s // tm),
        in_specs=[
            pl.BlockSpec((1, tm, d), lambda b, t: (b, t, 0)),
            pl.BlockSpec((1, d), lambda b, t: (0, 0)),
            pl.BlockSpec((1, 1, d), lambda b, t: (b, 0, 0)),
            pl.BlockSpec((1, 1, d), lambda b, t: (b, 0, 0)),
            pl.BlockSpec((d, n), lambda b, t: (0, 0), pipeline_mode=pl.Buffered(1)),
        ],
        out_specs=pl.BlockSpec((1, tm, n), lambda b, t: (b, t, 0)),
        out_shape=jax.ShapeDtypeStruct((bsz, s, n), BF16),
        compiler_params=pltpu.CompilerParams(
            dimension_semantics=("arbitrary", "arbitrary"), vmem_limit_bytes=VMEM_LIMIT),
        name="inproj",
    )(x, g.reshape(1, d), scale.reshape(bsz, 1, d), shift.reshape(bsz, 1, d), w)


SOFTPLUS_LINEAR = 40.0


def _softplus(z):
    t = jnp.exp(jnp.minimum(z, SOFTPLUS_LINEAR))
    return jnp.where(z > SOFTPLUS_LINEAR, z, jnp.log(1.0 + t))


def _for_each_group(n, run):
    big = n // 4

    def loop(g, _):
        run(g * 4, 4)
        return 0

    lax.fori_loop(0, big, loop, 0)
    base = big * 4

    @pl.when((n & 2) != 0)
    def _():
        run(base, 2)

    @pl.when((n & 1) != 0)
    def _():
        run(base + (n & 2), 1)


def _sb_kernel(q_ref, k_ref, v_ref, o_ref, acc_ref, carry_ref):
    i = pl.program_id(2)
    tq = ATT_BLOCK
    lane = lax.broadcasted_iota(jnp.int32, (1, LANES), 1)
    row = lax.broadcasted_iota(jnp.int32, (tq, tq), 0)
    col = lax.broadcasted_iota(jnp.int32, (tq, tq), 1)
    upper = jnp.where(row > col, 1.0, 0.0).astype(BF16)
    causal = col < row
    q = q_ref[0]
    qh = [jnp.where(lane // HEAD_DIM == h, q, jnp.zeros_like(q)) for h in range(HEADS_PER_TILE)]

    def suffix_in_block(sp):
        hi, lo = _split_bf16(sp)
        return _dot(hi, upper) + _dot(lo, upper)

    def run(first, size, diag=False):
        for h in range(HEADS_PER_TILE):
            carry = None if diag else carry_ref[h]
            acc = None
            for n in range(size):
                blk = i - first - n - (0 if diag else 1)
                st = pl.multiple_of(blk * tq, tq)
                z = _dot_nt(qh[h], k_ref[0, pl.ds(st, tq), :])
                sp = _softplus(z)
                if diag:
                    sp = jnp.where(causal, sp, 0.0)
                logit = z - sp - suffix_in_block(sp)
                if carry is not None:
                    logit = logit - carry
                att = jnp.exp(logit)
                if diag:
                    att = jnp.where(causal, att, 0.0)
                part = _dot(att.astype(BF16), v_ref[0, pl.ds(st, tq), :])
                acc = part if acc is None else acc + part
                rs = jnp.sum(sp, axis=1, keepdims=True)
                carry = rs if carry is None else carry + rs
            acc_ref[h] = acc if diag else acc_ref[h] + acc
            carry_ref[h] = carry

    run(0, 1, diag=True)
    _for_each_group(i, run)
    o_ref[0] = jnp.where(lane < HEAD_DIM, acc_ref[0], acc_ref[1]).astype(BF16)


def _stick_breaking(proj, col0):
    bsz, s, _ = proj.shape
    tq = ATT_BLOCK
    return pl.pallas_call(
        _sb_kernel,
        grid=(bsz, N_PAIRS, s // tq),
        in_specs=[
            pl.BlockSpec((1, tq, LANES), lambda b, p, i: (b, i, col0 + p)),
            pl.BlockSpec((1, s, LANES), lambda b, p, i: (b, 0, col0 + N_PAIRS + p)),
            pl.BlockSpec((1, s, LANES), lambda b, p, i: (b, 0, col0 + 2 * N_PAIRS + p)),
        ],
        out_specs=pl.BlockSpec((1, tq, LANES), lambda b, p, i: (b, i, p)),
        out_shape=jax.ShapeDtypeStruct((bsz, s, W_BRANCH), BF16),
        scratch_shapes=[pltpu.VMEM((HEADS_PER_TILE, tq, LANES), F32),
                        pltpu.VMEM((HEADS_PER_TILE, tq, 1), F32)],
        compiler_params=pltpu.CompilerParams(
            dimension_semantics=("arbitrary", "arbitrary", "arbitrary"),
            vmem_limit_bytes=VMEM_LIMIT),
        name="stick_breaking",
    )(proj, proj, proj)


def _rel_bucket(dist):
    max_exact = REL_BUCKETS // 2
    n = jnp.maximum(dist, 0)
    nf = jnp.maximum(n, 1).astype(F32)
    large = max_exact + (jnp.log(nf / max_exact) / math.log(REL_MAX_DIST / max_exact)
                         * (REL_BUCKETS - max_exact)).astype(jnp.int32)
    large = jnp.minimum(large, REL_BUCKETS - 1)
    return jnp.where(n < max_exact, n, large)


def _moba_bias_tiles(rel_bias):
    t = jnp.arange(MOBA_BLOCK)[:, None]
    s = jnp.arange(MOBA_BLOCK)[None, :]
    d_own = t - s

    def lookup(dist):
        onehot = jax.nn.one_hot(_rel_bucket(dist), REL_BUCKETS, dtype=F32)
        return jnp.einsum("tsk,kh->hts", onehot, rel_bias.astype(F32),
                          precision=lax.Precision.HIGHEST)

    own = jnp.where((d_own >= 0)[None], lookup(d_own), NEG_INF)
    return jnp.stack([own, lookup(d_own + MOBA_BLOCK)], axis=1)


def _moba_kernel(bfar_ref, q_ref, k_ref, v_ref, bias_ref, o_ref,
                 kmh_ref, kml_ref, sel_ref, m_ref, l_ref, acc_ref, *, n_blocks):
    p = pl.program_id(1)
    i = pl.program_id(2)
    tq = MOBA_BLOCK
    s_len = k_ref.shape[1]
    nb_pad = kmh_ref.shape[0]
    lane = lax.broadcasted_iota(jnp.int32, (1, LANES), 1)

    @pl.when(i == 0)
    def _():
        r = lax.broadcasted_iota(jnp.int32, (nb_pad, s_len), 0)
        c = lax.broadcasted_iota(jnp.int32, (nb_pad, s_len), 1)
        pool = jnp.where(c // tq == r, 1.0 / tq, 0.0).astype(BF16)
        hi, lo = _split_bf16(_dot(pool, k_ref[0]))
        kmh_ref[...] = hi
        kml_ref[...] = lo

    q = q_ref[0]
    blk_row = lax.broadcasted_iota(jnp.int32, (nb_pad, tq), 0)
    er = lax.broadcasted_iota(jnp.int32, (tq, tq), 0)
    ec = lax.broadcasted_iota(jnp.int32, (tq, tq), 1)
    eye = jnp.where(er == ec, 1.0, 0.0).astype(BF16)
    head_lanes = [lane // HEAD_DIM == h for h in range(HEADS_PER_TILE)]
    qh = [jnp.where(hl, q, jnp.zeros_like(q)) for hl in head_lanes]

    for h in range(HEADS_PER_TILE):
        kmh = jnp.where(head_lanes[h], kmh_ref[...], jnp.zeros_like(kmh_ref[...]))
        kml = jnp.where(head_lanes[h], kml_ref[...], jnp.zeros_like(kml_ref[...]))
        gate = _dot_nt(kmh, q) + _dot_nt(kml, q)
        cnt = jnp.zeros((nb_pad, tq), F32)
        for n in range(n_blocks):
            gn = gate[n:n + 1, :]
            beats = jnp.where(gn > gate, 1.0, jnp.where((gn == gate) & (n < blk_row), 1.0, 0.0))
            cnt = cnt + jnp.where(n < i, beats, 0.0)
        sel = jnp.where((blk_row < i) & (cnt < MOBA_TOPK), 1.0, 0.0).astype(BF16)
        selq = _dot_nt(eye, sel)
        for n in range(n_blocks):
            sel_ref[h, n] = (selq[:, n:n + 1] - 1.0) * (-NEG_INF)

    def attend(h, blk, bias):
        st = pl.multiple_of(blk * tq, tq)
        s = _dot_nt(qh[h], k_ref[0, pl.ds(st, tq), :]) + bias
        m = jnp.max(s, axis=1, keepdims=True)
        pr = jnp.exp(s - m)
        return (m, jnp.sum(pr, axis=1, keepdims=True),
                _dot(pr.astype(BF16), v_ref[0, pl.ds(st, tq), :]))

    def merge(h, parts, init):
        ms = [part[0] for part in parts]
        if not init:
            ms.append(m_ref[h])
        m_new = functools.reduce(jnp.maximum, ms)
        if init:
            l_new = acc_new = None
        else:
            a = jnp.exp(m_ref[h] - m_new)
            l_new = a * l_ref[h]
            acc_new = a * acc_ref[h]
        for m, l, o in parts:
            w = jnp.exp(m - m_new)
            l_new = w * l if l_new is None else l_new + w * l
            acc_new = w * o if acc_new is None else acc_new + w * o
        m_ref[h] = m_new
        l_ref[h] = l_new
        acc_ref[h] = acc_new

    @pl.when(i == 0)
    def _():
        for h in range(HEADS_PER_TILE):
            merge(h, [attend(h, i, bias_ref[h, 0])], True)

    @pl.when(i >= 1)
    def _():
        for h in range(HEADS_PER_TILE):
            merge(h, [attend(h, i, bias_ref[h, 0]),
                      attend(h, i - 1, bias_ref[h, 1] + sel_ref[h, i - 1])], True)

    def far(first, size):
        for h in range(HEADS_PER_TILE):
            b_far = bfar_ref[p * HEADS_PER_TILE + h]
            merge(h, [attend(h, first + n, sel_ref[h, first + n] + b_far) for n in range(size)],
                  False)

    _for_each_group(jnp.maximum(i - 1, 0), far)

    out = [acc_ref[h] / l_ref[h] for h in range(HEADS_PER_TILE)]
    o_ref[0] = jnp.where(lane < HEAD_DIM, out[0], out[1]).astype(BF16)


def _moba(proj, col0, bias_tiles, b_far):
    bsz, s, _ = proj.shape
    tq = MOBA_BLOCK
    assert s % tq == 0
    n_blocks = s // tq
    nb_pad = 16
    assert n_blocks <= nb_pad
    return pl.pallas_call(
        functools.partial(_moba_kernel, n_blocks=n_blocks),
        grid=(bsz, N_PAIRS, n_blocks),
        in_specs=[
            pl.BlockSpec(memory_space=pltpu.SMEM),
            pl.BlockSpec((1, tq, LANES), lambda b, p, i: (b, i, col0 + p)),
            pl.BlockSpec((1, s, LANES), lambda b, p, i: (b, 0, col0 + N_PAIRS + p)),
            pl.BlockSpec((1, s, LANES), lambda b, p, i: (b, 0, col0 + 2 * N_PAIRS + p)),
            pl.BlockSpec((HEADS_PER_TILE, 2, tq, tq), lambda b, p, i: (p, 0, 0, 0)),
        ],
        out_specs=pl.BlockSpec((1, tq, LANES), lambda b, p, i: (b, i, p)),
        out_shape=jax.ShapeDtypeStruct((bsz, s, W_BRANCH), BF16),
        scratch_shapes=[
            pltpu.VMEM((nb_pad, LANES), BF16),
            pltpu.VMEM((nb_pad, LANES), BF16),
            pltpu.VMEM((HEADS_PER_TILE, n_blocks, tq, 1), F32),
            pltpu.VMEM((HEADS_PER_TILE, tq, 1), F32),
            pltpu.VMEM((HEADS_PER_TILE, tq, 1), F32),
            pltpu.VMEM((HEADS_PER_TILE, tq, LANES), F32),
        ],
        compiler_params=pltpu.CompilerParams(
            dimension_semantics=("arbitrary", "arbitrary", "arbitrary"),
            vmem_limit_bytes=VMEM_LIMIT),
        name="moba",
    )(b_far, proj, proj, proj, bias_tiles)


def _merge_kernel(oa_ref, ob_ref, ga_ref, gb_ref, x_ref, wa_ref, wb_ref, wo_ref, g_ref, gate_ref,
                  o_ref):
    pa = _dot(oa_ref[0], wa_ref[...])
    pb = _dot(ob_ref[0], wb_ref[...])
    merged = (jax.nn.sigmoid(ga_ref[0].astype(F32)) * pa
              + jax.nn.sigmoid(gb_ref[0].astype(F32)) * pb)
    y = _dot(merged.astype(BF16), wo_ref[...])
    o_ref[0] = x_ref[0] + gate_ref[0] * _rms(y, g_ref[...])


def _merge(o_a, o_b, proj, gate_col, x, w_a, w_b, w_o, g_post, gate):
    bsz, s, d = x.shape
    tm = 512
    const = lambda b, t: (0, 0)
    return pl.pallas_call(
        _merge_kernel,
        grid=(bsz, s // tm),
        in_specs=[
            pl.BlockSpec((1, tm, W_BRANCH), lambda b, t: (b, t, 0)),
            pl.BlockSpec((1, tm, W_BRANCH), lambda b, t: (b, t, 0)),
            pl.BlockSpec((1, tm, d), lambda b, t: (b, t, gate_col)),
            pl.BlockSpec((1, tm, d), lambda b, t: (b, t, gate_col + 1)),
            pl.BlockSpec((1, tm, d), lambda b, t: (b, t, 0)),
            pl.BlockSpec((W_BRANCH, d), const),
            pl.BlockSpec((W_BRANCH, d), const),
            pl.BlockSpec((d, d), const),
            pl.BlockSpec((1, d), const),
            pl.BlockSpec((1, 1, d), lambda b, t: (b, 0, 0)),
        ],
        out_specs=pl.BlockSpec((1, tm, d), lambda b, t: (b, t, 0)),
        out_shape=jax.ShapeDtypeStruct((bsz, s, d), F32),
        compiler_params=pltpu.CompilerParams(
            dimension_semantics=("arbitrary", "arbitrary"), vmem_limit_bytes=VMEM_LIMIT),
        name="merge",
    )(o_a, o_b, proj, proj, x, w_a, w_b, w_o, g_post.reshape(1, d), gate.reshape(bsz, 1, d))


FFN_ROWS = 1024
FFN_COLS = 512


def _ffn_kernel(x_ref, gpre_ref, sc_ref, sh_ref, w1_ref, w3_ref, w2_ref, gpost_ref, gate_ref,
                o_ref, h_ref, acc_ref):
    k = pl.program_id(2)

    @pl.when(k == 0)
    def _():
        h = _rms(x_ref[0], gpre_ref[...]) * (1.0 + sc_ref[0]) + sh_ref[0]
        h_ref[...] = h.astype(BF16)
        acc_ref[...] = jnp.zeros_like(acc_ref)

    h = h_ref[...]
    u = jax.nn.silu(_dot(h, w1_ref[...])) * _dot(h, w3_ref[...])
    acc_ref[...] += _dot(u.astype(BF16), w2_ref[...])

    @pl.when(k == pl.num_programs(2) - 1)
    def _():
        o_ref[0] = x_ref[0] + gate_ref[0] * _rms(acc_ref[...], gpost_ref[...])


def _ffn(x, g_pre, scale, shift, w1, w3, w2, g_post, gate):
    bsz, s, d = x.shape
    f = w1.shape[1]
    tm, tf = FFN_ROWS, FFN_COLS
    row = lambda b, t, k: (b, t, 0)
    per_b = lambda b, t, k: (b, 0, 0)
    const = lambda b, t, k: (0, 0)
    return pl.pallas_call(
        _ffn_kernel,
        grid=(bsz, s // tm, f // tf),
        in_specs=[
            pl.BlockSpec((1, tm, d), row),
            pl.BlockSpec((1, d), const),
            pl.BlockSpec((1, 1, d), per_b),
            pl.BlockSpec((1, 1, d), per_b),
            pl.BlockSpec((d, tf), lambda b, t, k: (0, k)),
            pl.BlockSpec((d, tf), lambda b, t, k: (0, k)),
            pl.BlockSpec((tf, d), lambda b, t, k: (k, 0)),
            pl.BlockSpec((1, d), const),
            pl.BlockSpec((1, 1, d), per_b),
        ],
        out_specs=pl.BlockSpec((1, tm, d), row),
        out_shape=jax.ShapeDtypeStruct((bsz, s, d), F32),
        scratch_shapes=[pltpu.VMEM((tm, d), BF16), pltpu.VMEM((tm, d), F32)],
        compiler_params=pltpu.CompilerParams(
            dimension_semantics=("arbitrary",) * 3, vmem_limit_bytes=VMEM_LIMIT),
        name="ffn_dense",
    )(x, g_pre.reshape(1, d), scale.reshape(bsz, 1, d), shift.reshape(bsz, 1, d), w1, w3, w2,
      g_post.reshape(1, d), gate.reshape(bsz, 1, d))


ROUTE_ROWS = 512


def _to_token_major(ref, val):
    rows, n = val.shape[0], val.shape[1] // LANES
    for s in range(n):
        ref[pl.ds(s, rows, stride=n), :] = val[:, s * LANES:(s + 1) * LANES]


def _from_token_major(ref, n):
    rows = ref.shape[0] // n
    return jnp.concatenate([ref[pl.ds(s, rows, stride=n), :] for s in range(n)], axis=1)
ROUTE_LANES = ("expert0", "expert1", "weight0", "weight1", "rank0", "rank1")


def _route_kernel(x_ref, gpre_ref, sc_ref, sh_ref, wr_ref, h_ref, route_ref, cnt_ref, *, n_experts):
    first_step = (pl.program_id(0) == 0) & (pl.program_id(1) == 0)

    @pl.when(first_step)
    def _():
        cnt_ref[...] = jnp.zeros_like(cnt_ref)

    h = _rms(x_ref[0], gpre_ref[...]) * (1.0 + sc_ref[0]) + sh_ref[0]
    _to_token_major(h_ref, h)
    logits = jnp.dot(h, wr_ref[...], preferred_element_type=F32, precision=lax.Precision.HIGHEST)
    ts = logits.shape[0]
    lane = lax.broadcasted_iota(jnp.int32, logits.shape, 1)
    lg = jnp.where(lane < n_experts, logits, -jnp.inf)
    m1 = jnp.max(lg, axis=1, keepdims=True)
    i1 = jnp.min(jnp.where(lg == m1, lane, LANES), axis=1, keepdims=True)
    first = lane == i1
    lg2 = jnp.where(first, -jnp.inf, lg)
    m2 = jnp.max(lg2, axis=1, keepdims=True)
    i2 = jnp.min(jnp.where(lg2 == m2, lane, LANES), axis=1, keepdims=True)
    second = lane == i2
    e2 = jnp.exp(m2 - m1)
    w_first = 1.0 / (1.0 + e2)
    w_second = e2 * w_first

    onehot = jnp.where(first | second, 1.0, 0.0)
    r = lax.broadcasted_iota(jnp.int32, (ts, ts), 0)
    c = lax.broadcasted_iota(jnp.int32, (ts, ts), 1)
    lower = jnp.where(c < r, 1.0, 0.0).astype(BF16)
    prefix = _dot(lower, onehot.astype(BF16)) + cnt_ref[...]
    rank_first = jnp.sum(jnp.where(first, prefix, 0.0), axis=1, keepdims=True)
    rank_second = jnp.sum(jnp.where(second, prefix, 0.0), axis=1, keepdims=True)
    cnt_ref[...] += jnp.sum(onehot, axis=0, keepdims=True)

    cols = (i1.astype(F32), i2.astype(F32), w_first, w_second, rank_first, rank_second)
    route = jnp.zeros(logits.shape, F32)
    for n, val in enumerate(cols):
        route = jnp.where(lane == n, val, route)
    route_ref[0] = route


def _route(x, g_pre, scale, shift, w_router):
    bsz, s, d = x.shape
    n_experts = w_router.shape[1]
    ts = ROUTE_ROWS
    row = lambda b, t: (b, t, 0)
    per_b = lambda b, t: (b, 0, 0)
    const = lambda b, t: (0, 0)
    return pl.pallas_call(
        functools.partial(_route_kernel, n_experts=n_experts),
        grid=(bsz, s // ts),
        in_specs=[
            pl.BlockSpec((1, ts, d), row),
            pl.BlockSpec((1, d), const),
            pl.BlockSpec((1, 1, d), per_b),
            pl.BlockSpec((1, 1, d), per_b),
            pl.BlockSpec((d, LANES), const),
        ],
        out_specs=[
            pl.BlockSpec((ts * d // LANES, LANES), lambda b, t: (b * (s // ts) + t, 0)),
            pl.BlockSpec((1, ts, LANES), row),
            pl.BlockSpec((1, LANES), const),
        ],
        out_shape=[
            jax.ShapeDtypeStruct((bsz * s * d // LANES, LANES), F32),
            jax.ShapeDtypeStruct((bsz, s, LANES), F32),
            jax.ShapeDtypeStruct((1, LANES), F32),
        ],
        compiler_params=pltpu.CompilerParams(
            dimension_semantics=("arbitrary", "arbitrary"), vmem_limit_bytes=VMEM_LIMIT),
        name="moe_route",
    )(x, g_pre.reshape(1, d), scale.reshape(bsz, 1, d), shift.reshape(bsz, 1, d),
      jnp.pad(w_router, ((0, 0), (0, LANES - n_experts))))


def _for_each_chunk(start, length, chunk, fn):
    whole = length // chunk

    def loop(c, _):
        fn(start + c * chunk, chunk)
        return 0

    lax.fori_loop(0, whole, loop, 0)
    cur = start + whole * chunk
    size = chunk // 2
    while size >= 1:
        bit = length & size

        @pl.when(bit != 0)
        def _(cur=cur, size=size):
            fn(cur, size)

        cur = cur + bit
        size //= 2


def _scatter_kernel(pos_ref, fill_ref, h_ref, xs_ref, zero_ref, sem, *, n_experts, tile_rows):
    ts = h_ref.shape[0]

    def issue(r, _):
        for slot in range(TOP_K):
            dst = pos_ref[0, 0, slot * ts + r]
            pltpu.make_async_copy(h_ref.at[r], xs_ref.at[dst], sem).start()
        return 0

    lax.fori_loop(0, ts, issue, 0)
    for slot in range(TOP_K):
        pltpu.make_async_copy(h_ref, xs_ref.at[pl.ds(0, ts)], sem).wait()

    @pl.when(pl.program_id(0) == pl.num_programs(0) - 1)
    def _():
        zero_ref[...] = jnp.zeros_like(zero_ref)
        occupied = fill_ref[2 * n_experts] * tile_rows
        spans = [(fill_ref[e], fill_ref[n_experts + e] - fill_ref[e]) for e in range(n_experts)]
        spans.append((occupied, xs_ref.shape[0] - occupied))

        def zero_copy(first, rows):
            return pltpu.make_async_copy(zero_ref.at[pl.ds(0, rows)],
                                         xs_ref.at[pl.ds(first, rows)], sem)

        for first, length in spans:
            _for_each_chunk(first, length, zero_ref.shape[0],
                            lambda row, rows: zero_copy(row, rows).start())
        for first, length in spans:
            _for_each_chunk(first, length, zero_ref.shape[0],
                            lambda row, rows: zero_copy(row, rows).wait())


def _scatter(h, pos, fill, n_rows):
    t = h.shape[0]
    ts = ROUTE_ROWS
    zero_rows = 256
    return pl.pallas_call(
        functools.partial(_scatter_kernel, n_experts=(fill.shape[0] - 1) // 2,
                          tile_rows=FFN_ROWS),
        grid=(t // ts,),
        in_specs=[
            pl.BlockSpec((1, 1, TOP_K * ts), lambda i: (i, 0, 0), memory_space=pltpu.SMEM),
            pl.BlockSpec(memory_space=pltpu.SMEM),
            pl.BlockSpec((ts,) + h.shape[1:], lambda i: (i, 0, 0)),
        ],
        out_specs=pl.BlockSpec(memory_space=pl.ANY),
        out_shape=jax.ShapeDtypeStruct((n_rows,) + h.shape[1:], F32),
        scratch_shapes=[pltpu.VMEM((zero_rows,) + h.shape[1:], F32), pltpu.SemaphoreType.DMA(())],
        compiler_params=pltpu.CompilerParams(
            dimension_semantics=("arbitrary",), vmem_limit_bytes=VMEM_LIMIT),
        name="moe_scatter",
    )(pos, fill, h)


def _experts_kernel(exp_ref, nact_ref, xs_ref, w1_ref, w3_ref, w2_ref, ys_ref, h_ref, acc_ref):
    i = pl.program_id(0)
    k = pl.program_id(1)

    @pl.when(i < nact_ref[0])
    def _():
        @pl.when(k == 0)
        def _():
            h_ref[...] = _from_token_major(xs_ref, h_ref.shape[1] // LANES).astype(BF16)

        h = h_ref[...]
        u = jax.nn.silu(_dot(h, w1_ref[0])) * _dot(h, w3_ref[0])
        part = _dot(u.astype(BF16), w2_ref[0])

        @pl.when(k == 0)
        def _():
            acc_ref[...] = part

        @pl.when(k > 0)
        def _():
            acc_ref[...] += part

        @pl.when(k == pl.num_programs(1) - 1)
        def _():
            _to_token_major(ys_ref, acc_ref[...])

    @pl.when((i >= nact_ref[0]) & (k == 0))
    def _():
        ys_ref[...] = jnp.zeros_like(ys_ref)


def _experts(xs, tile_expert, n_active, w1, w3, w2):
    d, f = w1.shape[1:]
    n_rows = xs.shape[0] * LANES // d
    tm, tf = FFN_ROWS, FFN_COLS
    n_k = f // tf
    n_tiles = n_rows // tm

    def k_of(i, k, nact):
        return jnp.where(i < nact[0], k, n_k - 1)

    return pl.pallas_call(
        _experts_kernel,
        grid_spec=pltpu.PrefetchScalarGridSpec(
            num_scalar_prefetch=2,
            grid=(n_tiles, n_k),
            in_specs=[
                pl.BlockSpec((tm * d // LANES, LANES), lambda i, k, exp, nact: (i, 0)),
                pl.BlockSpec((1, d, tf), lambda i, k, exp, nact: (exp[i], 0, k_of(i, k, nact))),
                pl.BlockSpec((1, d, tf), lambda i, k, exp, nact: (exp[i], 0, k_of(i, k, nact))),
                pl.BlockSpec((1, tf, d), lambda i, k, exp, nact: (exp[i], k_of(i, k, nact), 0)),
            ],
            out_specs=pl.BlockSpec((tm * d // LANES, LANES), lambda i, k, exp, nact: (i, 0)),
            scratch_shapes=[pltpu.VMEM((tm, d), BF16), pltpu.VMEM((tm, d), F32)],
        ),
        out_shape=jax.ShapeDtypeStruct(xs.shape, F32),
        compiler_params=pltpu.CompilerParams(
            dimension_semantics=("arbitrary", "arbitrary"), vmem_limit_bytes=VMEM_LIMIT),
        name="moe_experts",
    )(tile_expert, n_active, xs, w1, w3, w2)


def _combine_kernel(pos_ref, route_ref, x_ref, gpost_ref, gate_ref, ys_ref, o_ref, buf0_ref,
                    buf1_ref, sem):
    ts = x_ref.shape[0]
    n = ys_ref.shape[1]
    bufs = (buf0_ref, buf1_ref)

    def issue(r, _):
        dst = pl.multiple_of(r * n, n)
        for slot in range(TOP_K):
            src = pos_ref[0, 0, slot * ts + r]
            pltpu.make_async_copy(ys_ref.at[src], bufs[slot].at[pl.ds(dst, n)], sem).start()
        return 0

    lax.fori_loop(0, ts, issue, 0)
    for slot in range(TOP_K):
        pltpu.make_async_copy(bufs[slot], bufs[slot], sem).wait()

    w0 = ROUTE_LANES.index("weight0")
    route = route_ref[...]
    y = (route[:, w0:w0 + 1] * _from_token_major(buf0_ref, n)
         + route[:, w0 + 1:w0 + 2] * _from_token_major(buf1_ref, n))
    o_ref[...] = x_ref[...] + gate_ref[0] * _rms(y, gpost_ref[...])


def _combine(ys, pos, route, x, g_post, gate):
    t, d = x.shape
    bsz = gate.shape[0]
    ts = ROUTE_ROWS
    steps_per_b = t // bsz // ts
    return pl.pallas_call(
        _combine_kernel,
        grid=(t // ts,),
        in_specs=[
            pl.BlockSpec((1, 1, TOP_K * ts), lambda i: (i, 0, 0), memory_space=pltpu.SMEM),
            pl.BlockSpec((ts, LANES), lambda i: (i, 0)),
            pl.BlockSpec((ts, d), lambda i: (i, 0)),
            pl.BlockSpec((1, d), lambda i: (0, 0)),
            pl.BlockSpec((1, 1, d), lambda i: (i // steps_per_b, 0, 0)),
            pl.BlockSpec(memory_space=pl.ANY),
        ],
        out_specs=pl.BlockSpec((ts, d), lambda i: (i, 0)),
        out_shape=jax.ShapeDtypeStruct((t, d), F32),
        scratch_shapes=[pltpu.VMEM((ts * d // LANES, LANES), F32)] * TOP_K
        + [pltpu.SemaphoreType.DMA(())],
        compiler_params=pltpu.CompilerParams(
            dimension_semantics=("arbitrary",), vmem_limit_bytes=VMEM_LIMIT),
        name="moe_combine",
    )(pos, route, x, g_post.reshape(1, d), gate, ys)


def _moe(x, g_pre, scale, shift, w_router, w1, w3, w2, g_post, gate):
    bsz, s, d = x.shape
    t = bsz * s
    n_experts = w_router.shape[1]
    ts, tm = ROUTE_ROWS, FFN_ROWS
    h, route, counts = _route(x, g_pre, scale, shift, w_router)
    route = route.reshape(t, LANES)

    lane_of = ROUTE_LANES.index
    expert = route[:, lane_of("expert0"):lane_of("expert0") + TOP_K].astype(jnp.int32)
    rank = route[:, lane_of("rank0"):lane_of("rank0") + TOP_K].astype(jnp.int32)
    counts = counts[0, :n_experts].astype(jnp.int32)
    tiles_per = (counts + tm - 1) // tm
    tile_end = jnp.cumsum(tiles_per)
    first_row = (tile_end - tiles_per) * tm
    n_active = tile_end[-1]
    n_tiles = TOP_K * t // tm + n_experts
    is_expert = expert[..., None] == jnp.arange(n_experts, dtype=jnp.int32)
    pos = jnp.sum(jnp.where(is_expert, first_row, 0), axis=-1) + rank
    pos = pos.reshape(t // ts, ts, TOP_K).transpose(0, 2, 1).reshape(t // ts, 1, TOP_K * ts)
    fill = jnp.concatenate([first_row + counts, tile_end * tm, n_active[None]]).astype(jnp.int32)
    idx = jnp.minimum(jnp.arange(n_tiles, dtype=jnp.int32), n_active - 1)
    tile_expert = jnp.sum(idx[:, None] >= tile_end[None, :], axis=1).astype(jnp.int32)

    token_major = (-1, d // LANES, LANES)
    xs = _scatter(h.reshape(token_major), pos, fill, n_tiles * tm)
    ys = _experts(xs.reshape(-1, LANES), tile_expert, n_active.reshape(1), w1, w3, w2)
    out = _combine(ys.reshape(token_major), pos, route, x.reshape(t, d), g_post,
                   gate.reshape(bsz, 1, d))
    return out.reshape(bsz, s, d)


def kernel(x, c, w_mod, b_mod, g_mix_pre, g_mix_post, g_ffn_pre, g_ffn_post, w_in, w_proj_sb,
           w_proj_moba, w_out, rel_bias, w1_dense, w3_dense, w2_dense, w_router, w1_moe, w3_moe,
           w2_moe):
    depth = w_mod.shape[0]
    d = x.shape[-1]
    qkv = 3 * W_BRANCH
    col = jnp.arange(w_in.shape[-1])
    is_q = (col < W_BRANCH) | ((col >= qkv) & (col < qkv + W_BRANCH))
    col_scale = jnp.where(is_q, HEAD_DIM ** -0.5, 1.0).astype(F32)

    mod = _modulation(c, w_mod, b_mod)
    assert MOBA_BLOCK >= REL_MAX_DIST
    bias_tiles = _moba_bias_tiles(rel_bias)
    b_far = rel_bias[REL_BUCKETS - 1].astype(F32)
    for l in range(depth):
        sh_mix, sc_mix, gt_mix, sh_ffn, sc_ffn, gt_ffn = jnp.split(mod[l], N_MOD, axis=-1)
        proj = _inproj(x, g_mix_pre[l], sc_mix, sh_mix, (w_in[l] * col_scale).astype(BF16))
        o_a = _stick_breaking(proj, 0)
        o_b = _moba(proj, qkv // LANES, bias_tiles, b_far)
        x = _merge(o_a, o_b, proj, 2 * qkv // d, x, w_proj_sb[l].astype(BF16),
                   w_proj_moba[l].astype(BF16), w_out[l].astype(BF16), g_mix_post[l], gt_mix)
        j = l // 2
        if l % 2 == 0:
            x = _ffn(x, g_ffn_pre[l], sc_ffn, sh_ffn, w1_dense[j].astype(BF16),
                     w3_dense[j].astype(BF16), w2_dense[j].astype(BF16), g_ffn_post[l], gt_ffn)
        else:
            x = _moe(x, g_ffn_pre[l], sc_ffn, sh_ffn, w_router[j], w1_moe[j].astype(BF16),
                     w3_moe[j].astype(BF16), w2_moe[j].astype(BF16), g_ffn_post[l], gt_ffn)
    return x
```

```python
import functools
import math

import jax
import jax.numpy as jnp
from jax import lax
from jax.experimental import pallas as pl
from jax.experimental.pallas import tpu as pltpu

F32 = jnp.float32
BF16 = jnp.bfloat16

EPS = 1e-6
NEG_INF = -1e30
HEAD_DIM = 64
N_HEADS = 8
W_BRANCH = N_HEADS * HEAD_DIM
LANES = 128
SUBLANES = 8
HEADS_PER_TILE = LANES // HEAD_DIM
N_PAIRS = N_HEADS // HEADS_PER_TILE
MOBA_BLOCK = 256
MOBA_TOPK = 3
REL_BUCKETS = 32
REL_MAX_DIST = 128
N_MOD = 6
TOP_K = 2
ATT_BLOCK = 256
VMEM_LIMIT = 56 * 1024 * 1024


def _dot(a, b):
    return jnp.dot(a, b, preferred_element_type=F32)


def _dot_nt(a, b):
    return lax.dot_general(a, b, (((1,), (1,)), ((), ())), preferred_element_type=F32)


def _split_bf16(x):
    hi = x.astype(BF16)
    lo = (x - hi.astype(F32)).astype(BF16)
    return hi, lo


def _rms(x, g):
    return x * lax.rsqrt(jnp.mean(x * x, axis=-1, keepdims=True) + EPS) * g


def _mod_kernel(c_ref, w_ref, b_ref, o_ref):
    c = c_ref[...]
    cond = c * jax.nn.sigmoid(c)
    o_ref[0] = jnp.dot(cond, w_ref[0], preferred_element_type=F32,
                       precision=lax.Precision.HIGHEST) + b_ref[0]


def _modulation(c, w_mod, b_mod):
    depth, d, n = w_mod.shape
    bsz = c.shape[0]
    tn = 1536
    return pl.pallas_call(
        _mod_kernel,
        grid=(depth, n // tn),
        in_specs=[
            pl.BlockSpec((bsz, d), lambda l, j: (0, 0)),
            pl.BlockSpec((1, d, tn), lambda l, j: (l, 0, j)),
            pl.BlockSpec((1, 1, tn), lambda l, j: (l, 0, j)),
        ],
        out_specs=pl.BlockSpec((1, bsz, tn), lambda l, j: (l, 0, j)),
        out_shape=jax.ShapeDtypeStruct((depth, bsz, n), F32),
        compiler_params=pltpu.CompilerParams(
            dimension_semantics=("arbitrary", "arbitrary"), vmem_limit_bytes=VMEM_LIMIT),
        name="modulation",
    )(c, w_mod, b_mod.reshape(depth, 1, n))


def _inproj_kernel(x_ref, g_ref, sc_ref, sh_ref, w_ref, o_ref, *, tn):
    h = _rms(x_ref[0], g_ref[...]) * (1.0 + sc_ref[0]) + sh_ref[0]
    hb = h.astype(BF16)
    for n in range(w_ref.shape[1] // tn):
        o_ref[0, :, n * tn:(n + 1) * tn] = _dot(hb, w_ref[:, n * tn:(n + 1) * tn]).astype(BF16)


def _inproj(x, g, scale, shift, w):
    bsz, s, d = x.shape
    n = w.shape[1]
    tm = 512
    return pl.pallas_call(
        functools.partial(_inproj_kernel, tn=1024),
        grid=(bsz, s // tm),
        in_specs=[
            pl.BlockSpec((1, tm, d), lambda b, t: (b, t, 0)),
            pl.BlockSpec((1, d), lambda b, t: (0, 0)),
            pl.BlockSpec((1, 1, d), lambda b, t: (b, 0, 0)),
            pl.BlockSpec((1, 1, d), lambda b, t: (b, 0, 0)),
            pl.BlockSpec((d, n), lambda b, t: (0, 0), pipeline_mode=pl.Buffered(1)),
        ],
        out_specs=pl.BlockSpec((1, tm, n), lambda b, t: (b, t, 0)),
        out_shape=jax.ShapeDtypeStruct((bsz, s, n), BF16),
        compiler_params=pltpu.CompilerParams(
            dimension_semantics=("arbitrary", "arbitrary"), vmem_limit_bytes=VMEM_LIMIT),
        name="inproj",
    )(x, g.reshape(1, d), scale.reshape(bsz, 1, d), shift.reshape(bsz, 1, d), w)


SOFTPLUS_LINEAR = 40.0


def _softplus(z):
    return jnp.maximum(z, jnp.log(1.0 + jnp.exp(jnp.minimum(z, SOFTPLUS_LINEAR))))


def _for_each_group(n, run):
    big = n // 4

    def loop(g, _):
        run(g * 4, 4)
        return 0

    lax.fori_loop(0, big, loop, 0)
    base = big * 4

    @pl.when((n & 2) != 0)
    def _():
        run(base, 2)

    @pl.when((n & 1) != 0)
    def _():
        run(base + (n & 2), 1)


def _store_transposed_blocks(vt_ref, v_ref):
    blk = vt_ref.shape[2]
    for n in range(vt_ref.shape[0]):
        vt_ref[n] = v_ref[0, n * blk:(n + 1) * blk, :].astype(F32).T.astype(BF16)


def _sb_kernel(q_ref, k_ref, v_ref, o_ref, acc_ref, carry_ref):
    i = pl.program_id(2)
    tq = ATT_BLOCK
    lane = lax.broadcasted_iota(jnp.int32, (1, LANES), 1)
    row = lax.broadcasted_iota(jnp.int32, (tq, tq), 0)
    col = lax.broadcasted_iota(jnp.int32, (tq, tq), 1)
    upper = jnp.where(row > col, 1.0, 0.0).astype(BF16)
    causal = jnp.concatenate([col < row] * HEADS_PER_TILE, axis=0)
    q = q_ref[0]
    q2 = jnp.concatenate([jnp.where(lane // HEAD_DIM == h, q, jnp.zeros_like(q))
                          for h in range(HEADS_PER_TILE)], axis=0)

    def suffix_in_block(sp):
        return _dot(sp.astype(BF16), upper)

    def run(first, size, diag=False):
        carry = None if diag else carry_ref[...]
        acc = None
        for n in range(size):
            blk = i - first - n - (0 if diag else 1)
            st = pl.multiple_of(blk * tq, tq)
            z = _dot_nt(q2, k_ref[0, pl.ds(st, tq), :])
            sp = _softplus(z)
            if diag:
                sp = jnp.where(causal, sp, 0.0)
            logit = z - sp - suffix_in_block(sp)
            if carry is not None:
                logit = logit - carry
            att = jnp.exp(logit)
            if diag:
                att = jnp.where(causal, att, 0.0)
            part = _dot(att.astype(BF16), v_ref[0, pl.ds(st, tq), :])
            acc = part if acc is None else acc + part
            rs = jnp.sum(sp, axis=1, keepdims=True)
            carry = rs if carry is None else carry + rs
        acc_ref[...] = acc if diag else acc_ref[...] + acc
        carry_ref[...] = carry

    run(0, 1, diag=True)
    _for_each_group(i, run)
    o_ref[0] = jnp.where(lane < HEAD_DIM, acc_ref[:tq], acc_ref[tq:]).astype(BF16)


def _stick_breaking(proj, col0):
    bsz, s, _ = proj.shape
    tq = ATT_BLOCK
    return pl.pallas_call(
        _sb_kernel,
        grid=(bsz, N_PAIRS, s // tq),
        in_specs=[
            pl.BlockSpec((1, tq, LANES), lambda b, p, i: (b, i, col0 + p)),
            pl.BlockSpec((1, s, LANES), lambda b, p, i: (b, 0, col0 + N_PAIRS + p)),
            pl.BlockSpec((1, s, LANES), lambda b, p, i: (b, 0, col0 + 2 * N_PAIRS + p)),
        ],
        out_specs=pl.BlockSpec((1, tq, LANES), lambda b, p, i: (b, i, p)),
        out_shape=jax.ShapeDtypeStruct((bsz, s, W_BRANCH), BF16),
        scratch_shapes=[pltpu.VMEM((HEADS_PER_TILE * tq, LANES), F32),
                        pltpu.VMEM((HEADS_PER_TILE * tq, 1), F32)],
        compiler_params=pltpu.CompilerParams(
            dimension_semantics=("arbitrary", "arbitrary", "arbitrary"),
            vmem_limit_bytes=VMEM_LIMIT),
        name="stick_breaking",
    )(proj, proj, proj)


def _rel_bucket(dist):
    max_exact = REL_BUCKETS // 2
    n = jnp.maximum(dist, 0)
    nf = jnp.maximum(n, 1).astype(F32)
    large = max_exact + (jnp.log(nf / max_exact) / math.log(REL_MAX_DIST / max_exact)
                         * (REL_BUCKETS - max_exact)).astype(jnp.int32)
    large = jnp.minimum(large, REL_BUCKETS - 1)
    return jnp.where(n < max_exact, n, large)


def _moba_bias_tiles(rel_bias):
    s = jnp.arange(MOBA_BLOCK)[:, None]
    t = jnp.arange(MOBA_BLOCK)[None, :]
    d_own = t - s

    def lookup(dist):
        onehot = jax.nn.one_hot(_rel_bucket(dist), REL_BUCKETS, dtype=F32)
        return jnp.einsum("stk,kh->hst", onehot, rel_bias.astype(F32),
                          precision=lax.Precision.HIGHEST)

    own = jnp.where((d_own >= 0)[None], lookup(d_own), NEG_INF)
    return jnp.stack([own, lookup(d_own + MOBA_BLOCK)], axis=1)


def _moba_kernel(bfar_ref, q_ref, k_ref, v_ref, bias_ref, o_ref,
                 kmh_ref, kml_ref, vt_ref, sel_ref, m_ref, l_ref, acc_ref, *, n_blocks):
    p = pl.program_id(1)
    i = pl.program_id(2)
    tq = MOBA_BLOCK
    s_len = k_ref.shape[1]
    nb_pad = kmh_ref.shape[0]
    lane = lax.broadcasted_iota(jnp.int32, (1, LANES), 1)

    @pl.when(i == 0)
    def _():
        r = lax.broadcasted_iota(jnp.int32, (nb_pad, s_len), 0)
        c = lax.broadcasted_iota(jnp.int32, (nb_pad, s_len), 1)
        pool = jnp.where(c // tq == r, 1.0 / tq, 0.0).astype(BF16)
        hi, lo = _split_bf16(_dot(pool, k_ref[0]))
        kmh_ref[...] = hi
        kml_ref[...] = lo
        _store_transposed_blocks(vt_ref, v_ref)

    q = q_ref[0]
    blk_row = lax.broadcasted_iota(jnp.int32, (nb_pad, tq), 0)
    head_lanes = [lane // HEAD_DIM == h for h in range(HEADS_PER_TILE)]
    qh = [jnp.where(hl, q, jnp.zeros_like(q)) for hl in head_lanes]

    for h in range(HEADS_PER_TILE):
        kmh = jnp.where(head_lanes[h], kmh_ref[...], jnp.zeros_like(kmh_ref[...]))
        kml = jnp.where(head_lanes[h], kml_ref[...], jnp.zeros_like(kml_ref[...]))
        gate = _dot_nt(kmh, q) + _dot_nt(kml, q)
        cnt = jnp.zeros((nb_pad, tq), F32)
        for n in range(n_blocks):
            gn = gate[n:n + 1, :]
            beats = jnp.where(gn > gate, 1.0, jnp.where((gn == gate) & (n < blk_row), 1.0, 0.0))
            cnt = cnt + jnp.where(n < i, beats, 0.0)
        sel_ref[h] = jnp.where((blk_row < i) & (cnt < MOBA_TOPK), 0.0, NEG_INF)

    def attend(h, blk, bias):
        st = pl.multiple_of(blk * tq, tq)
        s = _dot_nt(k_ref[0, pl.ds(st, tq), :], qh[h]) + bias
        m = jnp.max(s, axis=0, keepdims=True)
        pr = jnp.exp(s - m)
        return (m, jnp.sum(pr, axis=0, keepdims=True),
                _dot(vt_ref[blk], pr.astype(BF16)))

    def merge(h, parts, init):
        ms = [part[0] for part in parts]
        if not init:
            ms.append(m_ref[h])
        m_new = functools.reduce(jnp.maximum, ms)
        if init:
            l_new = acc_new = None
        else:
            a = jnp.exp(m_ref[h] - m_new)
            l_new = a * l_ref[h]
            acc_new = a * acc_ref[h]
        for m, l, o in parts:
            w = jnp.exp(m - m_new)
            l_new = w * l if l_new is None else l_new + w * l
            acc_new = w * o if acc_new is None else acc_new + w * o
        m_ref[h] = m_new
        l_ref[h] = l_new
        acc_ref[h] = acc_new

    @pl.when(i == 0)
    def _():
        for h in range(HEADS_PER_TILE):
            merge(h, [attend(h, i, bias_ref[h, 0])], True)

    @pl.when(i >= 1)
    def _():
        for h in range(HEADS_PER_TILE):
            merge(h, [attend(h, i, bias_ref[h, 0]),
                      attend(h, i - 1, bias_ref[h, 1] + sel_ref[h, pl.ds(i - 1, 1), :])], True)

    def far(first, size):
        for h in range(HEADS_PER_TILE):
            b_far = bfar_ref[p * HEADS_PER_TILE + h]
            merge(h, [attend(h, first + n, sel_ref[h, pl.ds(first + n, 1), :] + b_far)
                      for n in range(size)], False)

    _for_each_group(jnp.maximum(i - 1, 0), far)

    out = [acc_ref[h] / l_ref[h] for h in range(HEADS_PER_TILE)]
    dim = lax.broadcasted_iota(jnp.int32, (LANES, 1), 0)
    o_ref[0] = jnp.where(dim < HEAD_DIM, out[0], out[1]).T.astype(BF16)


def _moba(proj, col0, bias_tiles, b_far):
    bsz, s, _ = proj.shape
    tq = MOBA_BLOCK
    assert s % tq == 0
    n_blocks = s // tq
    nb_pad = 16
    assert n_blocks <= nb_pad
    return pl.pallas_call(
        functools.partial(_moba_kernel, n_blocks=n_blocks),
        grid=(bsz, N_PAIRS, n_blocks),
        in_specs=[
            pl.BlockSpec(memory_space=pltpu.SMEM),
            pl.BlockSpec((1, tq, LANES), lambda b, p, i: (b, i, col0 + p)),
            pl.BlockSpec((1, s, LANES), lambda b, p, i: (b, 0, col0 + N_PAIRS + p)),
            pl.BlockSpec((1, s, LANES), lambda b, p, i: (b, 0, col0 + 2 * N_PAIRS + p)),
            pl.BlockSpec((HEADS_PER_TILE, 2, tq, tq), lambda b, p, i: (p, 0, 0, 0)),
        ],
        out_specs=pl.BlockSpec((1, tq, LANES), lambda b, p, i: (b, i, p)),
        out_shape=jax.ShapeDtypeStruct((bsz, s, W_BRANCH), BF16),
        scratch_shapes=[
            pltpu.VMEM((nb_pad, LANES), BF16),
            pltpu.VMEM((nb_pad, LANES), BF16),
            pltpu.VMEM((n_blocks, LANES, tq), BF16),
            pltpu.VMEM((HEADS_PER_TILE, nb_pad, tq), F32),
            pltpu.VMEM((HEADS_PER_TILE, 1, tq), F32),
            pltpu.VMEM((HEADS_PER_TILE, 1, tq), F32),
            pltpu.VMEM((HEADS_PER_TILE, LANES, tq), F32),
        ],
        compiler_params=pltpu.CompilerParams(
            dimension_semantics=("arbitrary", "arbitrary", "arbitrary"),
            vmem_limit_bytes=VMEM_LIMIT),
        name="moba",
    )(b_far, proj, proj, proj, bias_tiles)


def _merge_kernel(oa_ref, ob_ref, ga_ref, gb_ref, x_ref, wa_ref, wb_ref, wo_ref, g_ref, gate_ref,
                  o_ref):
    pa = _dot(oa_ref[0], wa_ref[...])
    pb = _dot(ob_ref[0], wb_ref[...])
    merged = (jax.nn.sigmoid(ga_ref[0].astype(F32)) * pa
              + jax.nn.sigmoid(gb_ref[0].astype(F32)) * pb)
    y = _dot(merged.astype(BF16), wo_ref[...])
    o_ref[0] = x_ref[0] + gate_ref[0] * _rms(y, g_ref[...])


def _merge(o_a, o_b, proj, gate_col, x, w_a, w_b, w_o, g_post, gate):
    bsz, s, d = x.shape
    tm = 512
    const = lambda b, t: (0, 0)
    return pl.pallas_call(
        _merge_kernel,
        grid=(bsz, s // tm),
        in_specs=[
            pl.BlockSpec((1, tm, W_BRANCH), lambda b, t: (b, t, 0)),
            pl.BlockSpec((1, tm, W_BRANCH), lambda b, t: (b, t, 0)),
            pl.BlockSpec((1, tm, d), lambda b, t: (b, t, gate_col)),
            pl.BlockSpec((1, tm, d), lambda b, t: (b, t, gate_col + 1)),
            pl.BlockSpec((1, tm, d), lambda b, t: (b, t, 0)),
            pl.BlockSpec((W_BRANCH, d), const),
            pl.BlockSpec((W_BRANCH, d), const),
            pl.BlockSpec((d, d), const),
            pl.BlockSpec((1, d), const),
            pl.BlockSpec((1, 1, d), lambda b, t: (b, 0, 0)),
        ],
        out_specs=pl.BlockSpec((1, tm, d), lambda b, t: (b, t, 0)),
        out_shape=jax.ShapeDtypeStruct((bsz, s, d), F32),
        compiler_params=pltpu.CompilerParams(
            dimension_semantics=("arbitrary", "arbitrary"), vmem_limit_bytes=VMEM_LIMIT),
        name="merge",
    )(o_a, o_b, proj, proj, x, w_a, w_b, w_o, g_post.reshape(1, d), gate.reshape(bsz, 1, d))


FFN_ROWS = 1024
FFN_COLS = 512


def _ffn_kernel(x_ref, gpre_ref, sc_ref, sh_ref, w1_ref, w3_ref, w2_ref, gpost_ref, gate_ref,
                o_ref, h_ref, acc_ref):
    k = pl.program_id(2)

    @pl.when(k == 0)
    def _():
        h = _rms(x_ref[0], gpre_ref[...]) * (1.0 + sc_ref[0]) + sh_ref[0]
        h_ref[...] = h.astype(BF16)
        acc_ref[...] = jnp.zeros_like(acc_ref)

    h = h_ref[...]
    u = jax.nn.silu(_dot(h, w1_ref[...])) * _dot(h, w3_ref[...])
    acc_ref[...] += _dot(u.astype(BF16), w2_ref[...])

    @pl.when(k == pl.num_programs(2) - 1)
    def _():
        o_ref[0] = x_ref[0] + gate_ref[0] * _rms(acc_ref[...], gpost_ref[...])


def _ffn(x, g_pre, scale, shift, w1, w3, w2, g_post, gate):
    bsz, s, d = x.shape
    f = w1.shape[1]
    tm, tf = FFN_ROWS, FFN_COLS
    row = lambda b, t, k: (b, t, 0)
    per_b = lambda b, t, k: (b, 0, 0)
    const = lambda b, t, k: (0, 0)
    return pl.pallas_call(
        _ffn_kernel,
        grid=(bsz, s // tm, f // tf),
        in_specs=[
            pl.BlockSpec((1, tm, d), row),
            pl.BlockSpec((1, d), const),
            pl.BlockSpec((1, 1, d), per_b),
            pl.BlockSpec((1, 1, d), per_b),
            pl.BlockSpec((d, tf), lambda b, t, k: (0, k)),
            pl.BlockSpec((d, tf), lambda b, t, k: (0, k)),
            pl.BlockSpec((tf, d), lambda b, t, k: (k, 0)),
            pl.BlockSpec((1, d), const),
            pl.BlockSpec((1, 1, d), per_b),
        ],
        out_specs=pl.BlockSpec((1, tm, d), row),
        out_shape=jax.ShapeDtypeStruct((bsz, s, d), F32),
        scratch_shapes=[pltpu.VMEM((tm, d), BF16), pltpu.VMEM((tm, d), F32)],
        compiler_params=pltpu.CompilerParams(
            dimension_semantics=("arbitrary",) * 3, vmem_limit_bytes=VMEM_LIMIT),
        name="ffn_dense",
    )(x, g_pre.reshape(1, d), scale.reshape(bsz, 1, d), shift.reshape(bsz, 1, d), w1, w3, w2,
      g_post.reshape(1, d), gate.reshape(bsz, 1, d))


ROUTE_ROWS = 512


def _to_token_major(ref, val):
    rows, n = val.shape[0], val.shape[1] // LANES
    for s in range(n):
        ref[pl.ds(s, rows, stride=n), :] = val[:, s * LANES:(s + 1) * LANES]


def _from_token_major(ref, n):
    rows = ref.shape[0] // n
    return jnp.concatenate([ref[pl.ds(s, rows, stride=n), :] for s in range(n)], axis=1)
ROUTE_LANES = ("expert0", "expert1", "weight0", "weight1", "rank0", "rank1")


def _route_kernel(x_ref, gpre_ref, sc_ref, sh_ref, wr_ref, h_ref, route_ref, cnt_ref, *, n_experts):
    first_step = (pl.program_id(0) == 0) & (pl.program_id(1) == 0)

    @pl.when(first_step)
    def _():
        cnt_ref[...] = jnp.zeros_like(cnt_ref)

    h = _rms(x_ref[0], gpre_ref[...]) * (1.0 + sc_ref[0]) + sh_ref[0]
    _to_token_major(h_ref, h)
    logits = jnp.dot(h, wr_ref[...], preferred_element_type=F32, precision=lax.Precision.HIGHEST)
    ts = logits.shape[0]
    lane = lax.broadcasted_iota(jnp.int32, logits.shape, 1)
    lg = jnp.where(lane < n_experts, logits, -jnp.inf)
    m1 = jnp.max(lg, axis=1, keepdims=True)
    i1 = jnp.min(jnp.where(lg == m1, lane, LANES), axis=1, keepdims=True)
    first = lane == i1
    lg2 = jnp.where(first, -jnp.inf, lg)
    m2 = jnp.max(lg2, axis=1, keepdims=True)
    i2 = jnp.min(jnp.where(lg2 == m2, lane, LANES), axis=1, keepdims=True)
    second = lane == i2
    e2 = jnp.exp(m2 - m1)
    w_first = 1.0 / (1.0 + e2)
    w_second = e2 * w_first

    onehot = jnp.where(first | second, 1.0, 0.0)
    r = lax.broadcasted_iota(jnp.int32, (ts, ts), 0)
    c = lax.broadcasted_iota(jnp.int32, (ts, ts), 1)
    lower = jnp.where(c < r, 1.0, 0.0).astype(BF16)
    prefix = _dot(lower, onehot.astype(BF16)) + cnt_ref[...]
    rank_first = jnp.sum(jnp.where(first, prefix, 0.0), axis=1, keepdims=True)
    rank_second = jnp.sum(jnp.where(second, prefix, 0.0), axis=1, keepdims=True)
    cnt_ref[...] += jnp.sum(onehot, axis=0, keepdims=True)

    cols = (i1.astype(F32), i2.astype(F32), w_first, w_second, rank_first, rank_second)
    route = jnp.zeros(logits.shape, F32)
    for n, val in enumerate(cols):
        route = jnp.where(lane == n, val, route)
    route_ref[0] = route


def _route(x, g_pre, scale, shift, w_router):
    bsz, s, d = x.shape
    n_experts = w_router.shape[1]
    ts = ROUTE_ROWS
    row = lambda b, t: (b, t, 0)
    per_b = lambda b, t: (b, 0, 0)
    const = lambda b, t: (0, 0)
    return pl.pallas_call(
        functools.partial(_route_kernel, n_experts=n_experts),
        grid=(bsz, s // ts),
        in_specs=[
            pl.BlockSpec((1, ts, d), row),
            pl.BlockSpec((1, d), const),
            pl.BlockSpec((1, 1, d), per_b),
            pl.BlockSpec((1, 1, d), per_b),
            pl.BlockSpec((d, LANES), const),
        ],
        out_specs=[
            pl.BlockSpec((ts * d // LANES, LANES), lambda b, t: (b * (s // ts) + t, 0)),
            pl.BlockSpec((1, ts, LANES), row),
            pl.BlockSpec((1, LANES), const),
        ],
        out_shape=[
            jax.ShapeDtypeStruct((bsz * s * d // LANES, LANES), F32),
            jax.ShapeDtypeStruct((bsz, s, LANES), F32),
            jax.ShapeDtypeStruct((1, LANES), F32),
        ],
        compiler_params=pltpu.CompilerParams(
            dimension_semantics=("arbitrary", "arbitrary"), vmem_limit_bytes=VMEM_LIMIT),
        name="moe_route",
    )(x, g_pre.reshape(1, d), scale.reshape(bsz, 1, d), shift.reshape(bsz, 1, d),
      jnp.pad(w_router, ((0, 0), (0, LANES - n_experts))))


def _for_each_chunk(start, length, chunk, fn):
    whole = length // chunk

    def loop(c, _):
        fn(start + c * chunk, chunk)
        return 0

    lax.fori_loop(0, whole, loop, 0)
    cur = start + whole * chunk
    size = chunk // 2
    while size >= 1:
        bit = length & size

        @pl.when(bit != 0)
        def _(cur=cur, size=size):
            fn(cur, size)

        cur = cur + bit
        size //= 2


def _scatter_kernel(pos_ref, fill_ref, h_ref, xs_ref, zero_ref, sem, *, n_experts, tile_rows):
    ts = h_ref.shape[0]

    def issue(r, _):
        for slot in range(TOP_K):
            dst = pos_ref[0, 0, slot * ts + r]
            pltpu.make_async_copy(h_ref.at[r], xs_ref.at[dst], sem).start()
        return 0

    lax.fori_loop(0, ts, issue, 0)
    for slot in range(TOP_K):
        pltpu.make_async_copy(h_ref, xs_ref.at[pl.ds(0, ts)], sem).wait()

    @pl.when(pl.program_id(0) == pl.num_programs(0) - 1)
    def _():
        zero_ref[...] = jnp.zeros_like(zero_ref)
        occupied = fill_ref[2 * n_experts] * tile_rows
        spans = [(fill_ref[e], fill_ref[n_experts + e] - fill_ref[e]) for e in range(n_experts)]
        spans.append((occupied, xs_ref.shape[0] - occupied))

        def zero_copy(first, rows):
            return pltpu.make_async_copy(zero_ref.at[pl.ds(0, rows)],
                                         xs_ref.at[pl.ds(first, rows)], sem)

        for first, length in spans:
            _for_each_chunk(first, length, zero_ref.shape[0],
                            lambda row, rows: zero_copy(row, rows).start())
        for first, length in spans:
            _for_each_chunk(first, length, zero_ref.shape[0],
                            lambda row, rows: zero_copy(row, rows).wait())


def _scatter(h, pos, fill, n_rows):
    t = h.shape[0]
    ts = ROUTE_ROWS
    zero_rows = 256
    return pl.pallas_call(
        functools.partial(_scatter_kernel, n_experts=(fill.shape[0] - 1) // 2,
                          tile_rows=FFN_ROWS),
        grid=(t // ts,),
        in_specs=[
            pl.BlockSpec((1, 1, TOP_K * ts), lambda i: (i, 0, 0), memory_space=pltpu.SMEM),
            pl.BlockSpec(memory_space=pltpu.SMEM),
            pl.BlockSpec((ts,) + h.shape[1:], lambda i: (i, 0, 0)),
        ],
        out_specs=pl.BlockSpec(memory_space=pl.ANY),
        out_shape=jax.ShapeDtypeStruct((n_rows,) + h.shape[1:], F32),
        scratch_shapes=[pltpu.VMEM((zero_rows,) + h.shape[1:], F32), pltpu.SemaphoreType.DMA(())],
        compiler_params=pltpu.CompilerParams(
            dimension_semantics=("arbitrary",), vmem_limit_bytes=VMEM_LIMIT),
        name="moe_scatter",
    )(pos, fill, h)


def _experts_kernel(exp_ref, nact_ref, xs_ref, w1_ref, w3_ref, w2_ref, ys_ref, h_ref, acc_ref):
    i = pl.program_id(0)
    k = pl.program_id(1)

    @pl.when(i < nact_ref[0])
    def _():
        @pl.when(k == 0)
        def _():
            h_ref[...] = _from_token_major(xs_ref, h_ref.shape[1] // LANES).astype(BF16)

        h = h_ref[...]
        u = jax.nn.silu(_dot(h, w1_ref[0])) * _dot(h, w3_ref[0])
        part = _dot(u.astype(BF16), w2_ref[0])

        @pl.when(k == 0)
        def _():
            acc_ref[...] = part

        @pl.when(k > 0)
        def _():
            acc_ref[...] += part

        @pl.when(k == pl.num_programs(1) - 1)
        def _():
            _to_token_major(ys_ref, acc_ref[...])

    @pl.when((i >= nact_ref[0]) & (k == 0))
    def _():
        ys_ref[...] = jnp.zeros_like(ys_ref)


def _experts(xs, tile_expert, n_active, w1, w3, w2):
    d, f = w1.shape[1:]
    n_rows = xs.shape[0] * LANES // d
    tm, tf = FFN_ROWS, FFN_COLS
    n_k = f // tf
    n_tiles = n_rows // tm

    def k_of(i, k, nact):
        return jnp.where(i < nact[0], k, n_k - 1)

    return pl.pallas_call(
        _experts_kernel,
        grid_spec=pltpu.PrefetchScalarGridSpec(
            num_scalar_prefetch=2,
            grid=(n_tiles, n_k),
            in_specs=[
                pl.BlockSpec((tm * d // LANES, LANES), lambda i, k, exp, nact: (i, 0)),
                pl.BlockSpec((1, d, tf), lambda i, k, exp, nact: (exp[i], 0, k_of(i, k, nact))),
                pl.BlockSpec((1, d, tf), lambda i, k, exp, nact: (exp[i], 0, k_of(i, k, nact))),
                pl.BlockSpec((1, tf, d), lambda i, k, exp, nact: (exp[i], k_of(i, k, nact), 0)),
            ],
            out_specs=pl.BlockSpec((tm * d // LANES, LANES), lambda i, k, exp, nact: (i, 0)),
            scratch_shapes=[pltpu.VMEM((tm, d), BF16), pltpu.VMEM((tm, d), F32)],
        ),
        out_shape=jax.ShapeDtypeStruct(xs.shape, F32),
        compiler_params=pltpu.CompilerParams(
            dimension_semantics=("arbitrary", "arbitrary"), vmem_limit_bytes=VMEM_LIMIT),
        name="moe_experts",
    )(tile_expert, n_active, xs, w1, w3, w2)


def _combine_kernel(pos_ref, route_ref, x_ref, gpost_ref, gate_ref, ys_ref, o_ref, buf0_ref,
                    buf1_ref, sem):
    ts = x_ref.shape[0]
    n = ys_ref.shape[1]
    bufs = (buf0_ref, buf1_ref)

    def issue(r, _):
        dst = pl.multiple_of(r * n, n)
        for slot in range(TOP_K):
            src = pos_ref[0, 0, slot * ts + r]
            pltpu.make_async_copy(ys_ref.at[src], bufs[slot].at[pl.ds(dst, n)], sem).start()
        return 0

    lax.fori_loop(0, ts, issue, 0)
    for slot in range(TOP_K):
        pltpu.make_async_copy(bufs[slot], bufs[slot], sem).wait()

    w0 = ROUTE_LANES.index("weight0")
    route = route_ref[...]
    y = (route[:, w0:w0 + 1] * _from_token_major(buf0_ref, n)
         + route[:, w0 + 1:w0 + 2] * _from_token_major(buf1_ref, n))
    o_ref[...] = x_ref[...] + gate_ref[0] * _rms(y, gpost_ref[...])


def _combine(ys, pos, route, x, g_post, gate):
    t, d = x.shape
    bsz = gate.shape[0]
    ts = ROUTE_ROWS
    steps_per_b = t // bsz // ts
    return pl.pallas_call(
        _combine_kernel,
        grid=(t // ts,),
        in_specs=[
            pl.BlockSpec((1, 1, TOP_K * ts), lambda i: (i, 0, 0), memory_space=pltpu.SMEM),
            pl.BlockSpec((ts, LANES), lambda i: (i, 0)),
            pl.BlockSpec((ts, d), lambda i: (i, 0)),
            pl.BlockSpec((1, d), lambda i: (0, 0)),
            pl.BlockSpec((1, 1, d), lambda i: (i // steps_per_b, 0, 0)),
            pl.BlockSpec(memory_space=pl.ANY),
        ],
        out_specs=pl.BlockSpec((ts, d), lambda i: (i, 0)),
        out_shape=jax.ShapeDtypeStruct((t, d), F32),
        scratch_shapes=[pltpu.VMEM((ts * d // LANES, LANES), F32)] * TOP_K
        + [pltpu.SemaphoreType.DMA(())],
        compiler_params=pltpu.CompilerParams(
            dimension_semantics=("arbitrary",), vmem_limit_bytes=VMEM_LIMIT),
        name="moe_combine",
    )(pos, route, x, g_post.reshape(1, d), gate, ys)


def _moe(x, g_pre, scale, shift, w_router, w1, w3, w2, g_post, gate):
    bsz, s, d = x.shape
    t = bsz * s
    n_experts = w_router.shape[1]
    ts, tm = ROUTE_ROWS, FFN_ROWS
    h, route, counts = _route(x, g_pre, scale, shift, w_router)
    route = route.reshape(t, LANES)

    lane_of = ROUTE_LANES.index
    expert = route[:, lane_of("expert0"):lane_of("expert0") + TOP_K].astype(jnp.int32)
    rank = route[:, lane_of("rank0"):lane_of("rank0") + TOP_K].astype(jnp.int32)
    counts = counts[0, :n_experts].astype(jnp.int32)
    tiles_per = (counts + tm - 1) // tm
    tile_end = jnp.cumsum(tiles_per)
    first_row = (tile_end - tiles_per) * tm
    n_active = tile_end[-1]
    n_tiles = TOP_K * t // tm + n_experts
    is_expert = expert[..., None] == jnp.arange(n_experts, dtype=jnp.int32)
    pos = jnp.sum(jnp.where(is_expert, first_row, 0), axis=-1) + rank
    pos = pos.reshape(t // ts, ts, TOP_K).transpose(0, 2, 1).reshape(t // ts, 1, TOP_K * ts)
    fill = jnp.concatenate([first_row + counts, tile_end * tm, n_active[None]]).astype(jnp.int32)
    idx = jnp.minimum(jnp.arange(n_tiles, dtype=jnp.int32), n_active - 1)
    tile_expert = jnp.sum(idx[:, None] >= tile_end[None, :], axis=1).astype(jnp.int32)

    token_major = (-1, d // LANES, LANES)
    xs = _scatter(h.reshape(token_major), pos, fill, n_tiles * tm)
    ys = _experts(xs.reshape(-1, LANES), tile_expert, n_active.reshape(1), w1, w3, w2)
    out = _combine(ys.reshape(token_major), pos, route, x.reshape(t, d), g_post,
                   gate.reshape(bsz, 1, d))
    return out.reshape(bsz, s, d)


def kernel(x, c, w_mod, b_mod, g_mix_pre, g_mix_post, g_ffn_pre, g_ffn_post, w_in, w_proj_sb,
           w_proj_moba, w_out, rel_bias, w1_dense, w3_dense, w2_dense, w_router, w1_moe, w3_moe,
           w2_moe):
    depth = w_mod.shape[0]
    d = x.shape[-1]
    qkv = 3 * W_BRANCH
    col = jnp.arange(w_in.shape[-1])
    is_q = (col < W_BRANCH) | ((col >= qkv) & (col < qkv + W_BRANCH))
    col_scale = jnp.where(is_q, HEAD_DIM ** -0.5, 1.0).astype(F32)

    mod = _modulation(c, w_mod, b_mod)
    assert MOBA_BLOCK >= REL_MAX_DIST
    bias_tiles = _moba_bias_tiles(rel_bias)
    b_far = rel_bias[REL_BUCKETS - 1].astype(F32)
    for l in range(depth):
        sh_mix, sc_mix, gt_mix, sh_ffn, sc_ffn, gt_ffn = jnp.split(mod[l], N_MOD, axis=-1)
        proj = _inproj(x, g_mix_pre[l], sc_mix, sh_mix, (w_in[l] * col_scale).astype(BF16))
        o_a = _stick_breaking(proj, 0)
        o_b = _moba(proj, qkv // LANES, bias_tiles, b_far)
        x = _merge(o_a, o_b, proj, 2 * qkv // d, x, w_proj_sb[l].astype(BF16),
                   w_proj_moba[l].astype(BF16), w_out[l].astype(BF16), g_mix_post[l], gt_mix)
        j = l // 2
        if l % 2 == 0:
            x = _ffn(x, g_ffn_pre[l], sc_ffn, sh_ffn, w1_dense[j].astype(BF16),
                     w3_dense[j].astype(BF16), w2_dense[j].astype(BF16), g_ffn_post[l], gt_ffn)
        else:
            x = _moe(x, g_ffn_pre[l], sc_ffn, sh_ffn, w_router[j], w1_moe[j].astype(BF16),
                     w3_moe[j].astype(BF16), w2_moe[j].astype(BF16), g_ffn_post[l], gt_ffn)
    return x
```

```python
import functools
import math

import jax
import jax.numpy as jnp
from jax import lax
from jax.experimental import pallas as pl
from jax.experimental.pallas import tpu as pltpu

F32 = jnp.float32
BF16 = jnp.bfloat16

EPS = 1e-6
NEG_INF = -1e30
HEAD_DIM = 64
N_HEADS = 8
W_BRANCH = N_HEADS * HEAD_DIM
LANES = 128
SUBLANES = 8
HEADS_PER_TILE = LANES // HEAD_DIM
N_PAIRS = N_HEADS // HEADS_PER_TILE
MOBA_BLOCK = 256
MOBA_TOPK = 3
REL_BUCKETS = 32
REL_MAX_DIST = 128
N_MOD = 6
TOP_K = 2
ATT_BLOCK = 256
VMEM_LIMIT = 56 * 1024 * 1024


def _dot(a, b):
    return jnp.dot(a, b, preferred_element_type=F32)


def _dot_nt(a, b):
    return lax.dot_general(a, b, (((1,), (1,)), ((), ())), preferred_element_type=F32)


def _split_bf16(x):
    hi = x.astype(BF16)
    lo = (x - hi.astype(F32)).astype(BF16)
    return hi, lo


def _rms(x, g):
    return x * lax.rsqrt(jnp.mean(x * x, axis=-1, keepdims=True) + EPS) * g


def _mod_kernel(c_ref, w_ref, b_ref, o_ref):
    c = c_ref[...]
    cond = c * jax.nn.sigmoid(c)
    o_ref[0] = jnp.dot(cond, w_ref[0], preferred_element_type=F32,
                       precision=lax.Precision.HIGHEST) + b_ref[0]


def _modulation(c, w_mod, b_mod):
    depth, d, n = w_mod.shape
    bsz = c.shape[0]
    tn = 1536
    return pl.pallas_call(
        _mod_kernel,
        grid=(depth, n // tn),
        in_specs=[
            pl.BlockSpec((bsz, d), lambda l, j: (0, 0)),
            pl.BlockSpec((1, d, tn), lambda l, j: (l, 0, j)),
            pl.BlockSpec((1, 1, tn), lambda l, j: (l, 0, j)),
        ],
        out_specs=pl.BlockSpec((1, bsz, tn), lambda l, j: (l, 0, j)),
        out_shape=jax.ShapeDtypeStruct((depth, bsz, n), F32),
        compiler_params=pltpu.CompilerParams(
            dimension_semantics=("arbitrary", "arbitrary"), vmem_limit_bytes=VMEM_LIMIT),
        name="modulation",
    )(c, w_mod, b_mod.reshape(depth, 1, n))


def _inproj_kernel(x_ref, g_ref, sc_ref, sh_ref, w_ref, o_ref, *, tn):
    h = _rms(x_ref[0], g_ref[...]) * (1.0 + sc_ref[0]) + sh_ref[0]
    hb = h.astype(BF16)
    for n in range(w_ref.shape[1] // tn):
        o_ref[0, :, n * tn:(n + 1) * tn] = _dot(hb, w_ref[:, n * tn:(n + 1) * tn]).astype(BF16)


def _inproj(x, g, scale, shift, w):
    bsz, s, d = x.shape
    n = w.shape[1]
    tm = 512
    return pl.pallas_call(
        functools.partial(_inproj_kernel, tn=1024),
        grid=(bsz, s // tm),
        in_specs=[
            pl.BlockSpec((1, tm, d), lambda b, t: (b, t, 0)),
            pl.BlockSpec((1, d), lambda b, t: (0, 0)),
            pl.BlockSpec((1, 1, d), lambda b, t: (b, 0, 0)),
            pl.BlockSpec((1, 1, d), lambda b, t: (b, 0, 0)),
            pl.BlockSpec((d, n), lambda b, t: (0, 0), pipeline_mode=pl.Buffered(1)),
        ],
        out_specs=pl.BlockSpec((1, tm, n), lambda b, t: (b, t, 0)),
        out_shape=jax.ShapeDtypeStruct((bsz, s, n), BF16),
        compiler_params=pltpu.CompilerParams(
            dimension_semantics=("arbitrary", "arbitrary"), vmem_limit_bytes=VMEM_LIMIT),
        name="inproj",
    )(x, g.reshape(1, d), scale.reshape(bsz, 1, d), shift.reshape(bsz, 1, d), w)


SOFTPLUS_LINEAR = 40.0


def _softplus(z):
    return jnp.maximum(z, jnp.log(1.0 + jnp.exp(jnp.minimum(z, SOFTPLUS_LINEAR))))


def _for_each_group(n, run):
    big = n // 4

    def loop(g, _):
        run(g * 4, 4)
        return 0

    lax.fori_loop(0, big, loop, 0)
    base = big * 4

    @pl.when((n & 2) != 0)
    def _():
        run(base, 2)

    @pl.when((n & 1) != 0)
    def _():
        run(base + (n & 2), 1)


def _sb_kernel(q_ref, k_ref, v_ref, o_ref, acc_ref, carry_ref):
    i = pl.program_id(2)
    tq = ATT_BLOCK
    lane = lax.broadcasted_iota(jnp.int32, (1, LANES), 1)
    row = lax.broadcasted_iota(jnp.int32, (tq, tq), 0)
    col = lax.broadcasted_iota(jnp.int32, (tq, tq), 1)
    upper = jnp.where(row > col, 1.0, 0.0).astype(BF16)
    causal = jnp.concatenate([col < row] * HEADS_PER_TILE, axis=0)
    q = q_ref[0]
    q2 = jnp.concatenate([jnp.where(lane // HEAD_DIM == h, q, jnp.zeros_like(q))
                          for h in range(HEADS_PER_TILE)], axis=0)

    def suffix_in_block(sp):
        return _dot(sp.astype(BF16), upper)

    def run(first, size):
        diag = isinstance(first, int) and first == 0
        carry = None if diag else carry_ref[...]
        acc = None
        for n in range(size):
            st = pl.multiple_of((i - first - n) * tq, tq)
            masked = diag and n == 0
            z = _dot_nt(q2, k_ref[0, pl.ds(st, tq), :])
            sp = _softplus(z)
            if masked:
                sp = jnp.where(causal, sp, 0.0)
            logit = z - sp - suffix_in_block(sp)
            if carry is not None:
                logit = logit - carry
            att = jnp.exp(logit)
            if masked:
                att = jnp.where(causal, att, 0.0)
            part = _dot(att.astype(BF16), v_ref[0, pl.ds(st, tq), :])
            acc = part if acc is None else acc + part
            rs = jnp.sum(sp, axis=1, keepdims=True)
            carry = rs if carry is None else carry + rs
        acc_ref[...] = acc if diag else acc_ref[...] + acc
        carry_ref[...] = carry

    @pl.when(i == 0)
    def _():
        run(0, 1)

    @pl.when(i >= 1)
    def _():
        run(0, 2)

    _for_each_group(jnp.maximum(i - 1, 0), lambda first, size: run(first + 2, size))
    o_ref[0] = jnp.where(lane < HEAD_DIM, acc_ref[:tq], acc_ref[tq:]).astype(BF16)


def _stick_breaking(proj, col0):
    bsz, s, _ = proj.shape
    tq = ATT_BLOCK
    return pl.pallas_call(
        _sb_kernel,
        grid=(bsz, N_PAIRS, s // tq),
        in_specs=[
            pl.BlockSpec((1, tq, LANES), lambda b, p, i: (b, i, col0 + p)),
            pl.BlockSpec((1, s, LANES), lambda b, p, i: (b, 0, col0 + N_PAIRS + p)),
            pl.BlockSpec((1, s, LANES), lambda b, p, i: (b, 0, col0 + 2 * N_PAIRS + p)),
        ],
        out_specs=pl.BlockSpec((1, tq, LANES), lambda b, p, i: (b, i, p)),
        out_shape=jax.ShapeDtypeStruct((bsz, s, W_BRANCH), BF16),
        scratch_shapes=[pltpu.VMEM((HEADS_PER_TILE * tq, LANES), F32),
                        pltpu.VMEM((HEADS_PER_TILE * tq, 1), F32)],
        compiler_params=pltpu.CompilerParams(
            dimension_semantics=("arbitrary", "arbitrary", "arbitrary"),
            vmem_limit_bytes=VMEM_LIMIT),
        name="stick_breaking",
    )(proj, proj, proj)


def _rel_bucket(dist):
    max_exact = REL_BUCKETS // 2
    n = jnp.maximum(dist, 0)
    nf = jnp.maximum(n, 1).astype(F32)
    large = max_exact + (jnp.log(nf / max_exact) / math.log(REL_MAX_DIST / max_exact)
                         * (REL_BUCKETS - max_exact)).astype(jnp.int32)
    large = jnp.minimum(large, REL_BUCKETS - 1)
    return jnp.where(n < max_exact, n, large)


def _moba_bias_tiles(rel_bias):
    assert MOBA_BLOCK >= REL_MAX_DIST
    t = jnp.arange(MOBA_BLOCK)[:, None]
    s = jnp.arange(MOBA_BLOCK)[None, :]
    d_own = t - s
    rel = rel_bias.astype(F32) - rel_bias[REL_BUCKETS - 1].astype(F32)

    def lookup(dist):
        onehot = jax.nn.one_hot(_rel_bucket(dist), REL_BUCKETS, dtype=F32)
        return jnp.einsum("tsk,kh->hts", onehot, rel, precision=lax.Precision.HIGHEST)

    own = jnp.where((d_own >= 0)[None], lookup(d_own), NEG_INF)
    return jnp.stack([own, lookup(d_own + MOBA_BLOCK)], axis=1)


def _moba_kernel(q_ref, k_ref, v_ref, bias_ref, o_ref,
                 kmh_ref, kml_ref, ka_ref, va_ref, m_ref, accl_ref, *, n_blocks):
    i = pl.program_id(2)
    tq = MOBA_BLOCK
    s_len = k_ref.shape[1]
    nb_pad = kmh_ref.shape[0]
    lane = lax.broadcasted_iota(jnp.int32, (1, LANES), 1)

    @pl.when(i == 0)
    def _():
        r = lax.broadcasted_iota(jnp.int32, (nb_pad, s_len), 0)
        c = lax.broadcasted_iota(jnp.int32, (nb_pad, s_len), 1)
        pool = jnp.where(c // tq == r, 1.0 / tq, 0.0).astype(BF16)
        hi, lo = _split_bf16(_dot(pool, k_ref[0]))
        kmh_ref[...] = hi
        kml_ref[...] = lo
        key_blk = lax.broadcasted_iota(jnp.int32, (s_len, LANES), 0) // tq
        ka_ref[:, :LANES] = k_ref[0]
        ka_ref[:, LANES:] = jnp.where(lane == key_blk, 1.0, 0.0).astype(BF16)
        va_ref[:, :LANES] = v_ref[0]
        va_ref[:, LANES:] = jnp.ones((s_len, LANES), BF16)

    q = q_ref[0]
    blk_row = lax.broadcasted_iota(jnp.int32, (nb_pad, tq), 0)
    er = lax.broadcasted_iota(jnp.int32, (tq, tq), 0)
    ec = lax.broadcasted_iota(jnp.int32, (tq, tq), 1)
    eye = jnp.where(er == ec, 1.0, 0.0).astype(BF16)
    head_lanes = [lane // HEAD_DIM == h for h in range(HEADS_PER_TILE)]

    means = [jnp.where(hl, ref[...], jnp.zeros_like(ref[...]))
             for hl in head_lanes for ref in (kmh_ref, kml_ref)]
    gates = _dot_nt(jnp.concatenate(means, axis=0), q)
    masks = []
    for h in range(HEADS_PER_TILE):
        gate = (gates[2 * h * nb_pad:(2 * h + 1) * nb_pad]
                + gates[(2 * h + 1) * nb_pad:(2 * h + 2) * nb_pad])
        cnt = jnp.zeros((nb_pad, tq), F32)
        for n in range(n_blocks):
            gn = gate[n:n + 1, :]
            beats = jnp.where(gn > gate, 1.0, jnp.where((gn == gate) & (n < blk_row), 1.0, 0.0))
            cnt = cnt + jnp.where(n < i, beats, 0.0)
        keep = (blk_row == i) | ((blk_row < i) & (cnt < MOBA_TOPK))
        masks += [jnp.where(keep, 0.0, NEG_INF).astype(BF16),
                  jnp.zeros((LANES - nb_pad, tq), BF16)]
    mask_rows = _dot_nt(eye, jnp.concatenate(masks, axis=0)).astype(BF16)
    q_aug = jnp.concatenate(
        [jnp.concatenate([jnp.where(head_lanes[h], q, jnp.zeros_like(q)),
                          mask_rows[:, h * LANES:(h + 1) * LANES]], axis=1)
         for h in range(HEADS_PER_TILE)], axis=0)

    def attend(blk, bias):
        st = pl.multiple_of(blk * tq, tq)
        s = _dot_nt(q_aug, ka_ref[pl.ds(st, tq), :])
        if bias is not None:
            s = s + bias
        m = jnp.max(s, axis=1, keepdims=True)
        pr = jnp.exp(s - m)
        return m, _dot(pr.astype(BF16), va_ref[pl.ds(st, tq), :])

    def merge(parts, init):
        ms = [part[0] for part in parts]
        if not init:
            ms.append(m_ref[...])
        m_new = functools.reduce(jnp.maximum, ms)
        accl = None if init else jnp.exp(m_ref[...] - m_new) * accl_ref[...]
        for m, ov in parts:
            w = jnp.exp(m - m_new)
            accl = w * ov if accl is None else accl + w * ov
        m_ref[...] = m_new
        accl_ref[...] = accl

    def bias_tile(which):
        return jnp.concatenate([bias_ref[h, which] for h in range(HEADS_PER_TILE)], axis=0)

    @pl.when(i == 0)
    def _():
        merge([attend(i, bias_tile(0))], True)

    @pl.when(i >= 1)
    def _():
        merge([attend(i, bias_tile(0)), attend(i - 1, bias_tile(1))], True)

    def far(first, size):
        merge([attend(first + n, None) for n in range(size)], False)

    _for_each_group(jnp.maximum(i - 1, 0), far)

    out = accl_ref[:, :LANES] / accl_ref[:, LANES:]
    o_ref[0] = jnp.where(lane < HEAD_DIM, out[:tq], out[tq:]).astype(BF16)


def _moba(proj, col0, bias_tiles):
    bsz, s, _ = proj.shape
    tq = MOBA_BLOCK
    assert s % tq == 0
    n_blocks = s // tq
    nb_pad = 16
    assert n_blocks <= nb_pad
    return pl.pallas_call(
        functools.partial(_moba_kernel, n_blocks=n_blocks),
        grid=(bsz, N_PAIRS, n_blocks),
        in_specs=[
            pl.BlockSpec((1, tq, LANES), lambda b, p, i: (b, i, col0 + p)),
            pl.BlockSpec((1, s, LANES), lambda b, p, i: (b, 0, col0 + N_PAIRS + p)),
            pl.BlockSpec((1, s, LANES), lambda b, p, i: (b, 0, col0 + 2 * N_PAIRS + p)),
            pl.BlockSpec((HEADS_PER_TILE, 2, tq, tq), lambda b, p, i: (p, 0, 0, 0)),
        ],
        out_specs=pl.BlockSpec((1, tq, LANES), lambda b, p, i: (b, i, p)),
        out_shape=jax.ShapeDtypeStruct((bsz, s, W_BRANCH), BF16),
        scratch_shapes=[
            pltpu.VMEM((nb_pad, LANES), BF16),
            pltpu.VMEM((nb_pad, LANES), BF16),
            pltpu.VMEM((s, 2 * LANES), BF16),
            pltpu.VMEM((s, 2 * LANES), BF16),
            pltpu.VMEM((HEADS_PER_TILE * tq, 1), F32),
            pltpu.VMEM((HEADS_PER_TILE * tq, 2 * LANES), F32),
        ],
        compiler_params=pltpu.CompilerParams(
            dimension_semantics=("arbitrary", "arbitrary", "arbitrary"),
            vmem_limit_bytes=VMEM_LIMIT),
        name="moba",
    )(proj, proj, proj, bias_tiles)


def _merge_kernel(oa_ref, ob_ref, ga_ref, gb_ref, x_ref, wa_ref, wb_ref, wo_ref, g_ref, gate_ref,
                  o_ref):
    pa = _dot(oa_ref[0], wa_ref[...])
    pb = _dot(ob_ref[0], wb_ref[...])
    merged = (jax.nn.sigmoid(ga_ref[0].astype(F32)) * pa
              + jax.nn.sigmoid(gb_ref[0].astype(F32)) * pb)
    y = _dot(merged.astype(BF16), wo_ref[...])
    o_ref[0] = x_ref[0] + gate_ref[0] * _rms(y, g_ref[...])


def _merge(o_a, o_b, proj, gate_col, x, w_a, w_b, w_o, g_post, gate):
    bsz, s, d = x.shape
    tm = 512
    const = lambda b, t: (0, 0)
    return pl.pallas_call(
        _merge_kernel,
        grid=(bsz, s // tm),
        in_specs=[
            pl.BlockSpec((1, tm, W_BRANCH), lambda b, t: (b, t, 0)),
            pl.BlockSpec((1, tm, W_BRANCH), lambda b, t: (b, t, 0)),
            pl.BlockSpec((1, tm, d), lambda b, t: (b, t, gate_col)),
            pl.BlockSpec((1, tm, d), lambda b, t: (b, t, gate_col + 1)),
            pl.BlockSpec((1, tm, d), lambda b, t: (b, t, 0)),
            pl.BlockSpec((W_BRANCH, d), const),
            pl.BlockSpec((W_BRANCH, d), const),
            pl.BlockSpec((d, d), const),
            pl.BlockSpec((1, d), const),
            pl.BlockSpec((1, 1, d), lambda b, t: (b, 0, 0)),
        ],
        out_specs=pl.BlockSpec((1, tm, d), lambda b, t: (b, t, 0)),
        out_shape=jax.ShapeDtypeStruct((bsz, s, d), F32),
        compiler_params=pltpu.CompilerParams(
            dimension_semantics=("arbitrary", "arbitrary"), vmem_limit_bytes=VMEM_LIMIT),
        name="merge",
    )(o_a, o_b, proj, proj, x, w_a, w_b, w_o, g_post.reshape(1, d), gate.reshape(bsz, 1, d))


FFN_ROWS = 1024
FFN_COLS = 512


def _ffn_kernel(x_ref, gpre_ref, sc_ref, sh_ref, w1_ref, w3_ref, w2_ref, gpost_ref, gate_ref,
                o_ref, h_ref, acc_ref):
    k = pl.program_id(2)

    @pl.when(k == 0)
    def _():
        h = _rms(x_ref[0], gpre_ref[...]) * (1.0 + sc_ref[0]) + sh_ref[0]
        h_ref[...] = h.astype(BF16)
        acc_ref[...] = jnp.zeros_like(acc_ref)

    h = h_ref[...]
    u = jax.nn.silu(_dot(h, w1_ref[...])) * _dot(h, w3_ref[...])
    acc_ref[...] += _dot(u.astype(BF16), w2_ref[...])

    @pl.when(k == pl.num_programs(2) - 1)
    def _():
        o_ref[0] = x_ref[0] + gate_ref[0] * _rms(acc_ref[...], gpost_ref[...])


def _ffn(x, g_pre, scale, shift, w1, w3, w2, g_post, gate):
    bsz, s, d = x.shape
    f = w1.shape[1]
    tm, tf = FFN_ROWS, FFN_COLS
    row = lambda b, t, k: (b, t, 0)
    per_b = lambda b, t, k: (b, 0, 0)
    const = lambda b, t, k: (0, 0)
    return pl.pallas_call(
        _ffn_kernel,
        grid=(bsz, s // tm, f // tf),
        in_specs=[
            pl.BlockSpec((1, tm, d), row),
            pl.BlockSpec((1, d), const),
            pl.BlockSpec((1, 1, d), per_b),
            pl.BlockSpec((1, 1, d), per_b),
            pl.BlockSpec((d, tf), lambda b, t, k: (0, k)),
            pl.BlockSpec((d, tf), lambda b, t, k: (0, k)),
            pl.BlockSpec((tf, d), lambda b, t, k: (k, 0)),
            pl.BlockSpec((1, d), const),
            pl.BlockSpec((1, 1, d), per_b),
        ],
        out_specs=pl.BlockSpec((1, tm, d), row),
        out_shape=jax.ShapeDtypeStruct((bsz, s, d), F32),
        scratch_shapes=[pltpu.VMEM((tm, d), BF16), pltpu.VMEM((tm, d), F32)],
        compiler_params=pltpu.CompilerParams(
            dimension_semantics=("arbitrary",) * 3, vmem_limit_bytes=VMEM_LIMIT),
        name="ffn_dense",
    )(x, g_pre.reshape(1, d), scale.reshape(bsz, 1, d), shift.reshape(bsz, 1, d), w1, w3, w2,
      g_post.reshape(1, d), gate.reshape(bsz, 1, d))


ROUTE_ROWS = 512


def _to_token_major(ref, val):
    rows, n = val.shape[0], val.shape[1] // LANES
    for s in range(n):
        ref[pl.ds(s, rows, stride=n), :] = val[:, s * LANES:(s + 1) * LANES]


def _from_token_major(ref, n):
    rows = ref.shape[0] // n
    return jnp.concatenate([ref[pl.ds(s, rows, stride=n), :] for s in range(n)], axis=1)
ROUTE_LANES = ("expert0", "expert1", "weight0", "weight1", "rank0", "rank1")


def _route_kernel(x_ref, gpre_ref, sc_ref, sh_ref, wr_ref, h_ref, route_ref, cnt_ref, *, n_experts):
    first_step = (pl.program_id(0) == 0) & (pl.program_id(1) == 0)

    @pl.when(first_step)
    def _():
        cnt_ref[...] = jnp.zeros_like(cnt_ref)

    h = _rms(x_ref[0], gpre_ref[...]) * (1.0 + sc_ref[0]) + sh_ref[0]
    _to_token_major(h_ref, h)
    logits = jnp.dot(h, wr_ref[...], preferred_element_type=F32, precision=lax.Precision.HIGHEST)
    ts = logits.shape[0]
    lane = lax.broadcasted_iota(jnp.int32, logits.shape, 1)
    lg = jnp.where(lane < n_experts, logits, -jnp.inf)
    m1 = jnp.max(lg, axis=1, keepdims=True)
    i1 = jnp.min(jnp.where(lg == m1, lane, LANES), axis=1, keepdims=True)
    first = lane == i1
    lg2 = jnp.where(first, -jnp.inf, lg)
    m2 = jnp.max(lg2, axis=1, keepdims=True)
    i2 = jnp.min(jnp.where(lg2 == m2, lane, LANES), axis=1, keepdims=True)
    second = lane == i2
    e2 = jnp.exp(m2 - m1)
    w_first = 1.0 / (1.0 + e2)
    w_second = e2 * w_first

    onehot = jnp.where(first | second, 1.0, 0.0)
    r = lax.broadcasted_iota(jnp.int32, (ts, ts), 0)
    c = lax.broadcasted_iota(jnp.int32, (ts, ts), 1)
    lower = jnp.where(c < r, 1.0, 0.0).astype(BF16)
    prefix = _dot(lower, onehot.astype(BF16)) + cnt_ref[...]
    rank_first = jnp.sum(jnp.where(first, prefix, 0.0), axis=1, keepdims=True)
    rank_second = jnp.sum(jnp.where(second, prefix, 0.0), axis=1, keepdims=True)
    cnt_ref[...] += jnp.sum(onehot, axis=0, keepdims=True)

    cols = (i1.astype(F32), i2.astype(F32), w_first, w_second, rank_first, rank_second)
    route = jnp.zeros(logits.shape, F32)
    for n, val in enumerate(cols):
        route = jnp.where(lane == n, val, route)
    route_ref[0] = route


def _route(x, g_pre, scale, shift, w_router):
    bsz, s, d = x.shape
    n_experts = w_router.shape[1]
    ts = ROUTE_ROWS
    row = lambda b, t: (b, t, 0)
    per_b = lambda b, t: (b, 0, 0)
    const = lambda b, t: (0, 0)
    return pl.pallas_call(
        functools.partial(_route_kernel, n_experts=n_experts),
        grid=(bsz, s // ts),
        in_specs=[
            pl.BlockSpec((1, ts, d), row),
            pl.BlockSpec((1, d), const),
            pl.BlockSpec((1, 1, d), per_b),
            pl.BlockSpec((1, 1, d), per_b),
            pl.BlockSpec((d, LANES), const),
        ],
        out_specs=[
            pl.BlockSpec((ts * d // LANES, LANES), lambda b, t: (b * (s // ts) + t, 0)),
            pl.BlockSpec((1, ts, LANES), row),
            pl.BlockSpec((1, LANES), const),
        ],
        out_shape=[
            jax.ShapeDtypeStruct((bsz * s * d // LANES, LANES), F32),
            jax.ShapeDtypeStruct((bsz, s, LANES), F32),
            jax.ShapeDtypeStruct((1, LANES), F32),
        ],
        compiler_params=pltpu.CompilerParams(
            dimension_semantics=("arbitrary", "arbitrary"), vmem_limit_bytes=VMEM_LIMIT),
        name="moe_route",
    )(x, g_pre.reshape(1, d), scale.reshape(bsz, 1, d), shift.reshape(bsz, 1, d),
      jnp.pad(w_router, ((0, 0), (0, LANES - n_experts))))


def _for_each_chunk(start, length, chunk, fn):
    whole = length // chunk

    def loop(c, _):
        fn(start + c * chunk, chunk)
        return 0

    lax.fori_loop(0, whole, loop, 0)
    cur = start + whole * chunk
    size = chunk // 2
    while size >= 1:
        bit = length & size

        @pl.when(bit != 0)
        def _(cur=cur, size=size):
            fn(cur, size)

        cur = cur + bit
        size //= 2


def _scatter_kernel(pos_ref, fill_ref, h_ref, xs_ref, zero_ref, sem, *, n_experts, tile_rows):
    ts = h_ref.shape[0]

    def issue(r, _):
        for slot in range(TOP_K):
            dst = pos_ref[0, 0, slot * ts + r]
            pltpu.make_async_copy(h_ref.at[r], xs_ref.at[dst], sem).start()
        return 0

    lax.fori_loop(0, ts, issue, 0)
    for slot in range(TOP_K):
        pltpu.make_async_copy(h_ref, xs_ref.at[pl.ds(0, ts)], sem).wait()

    @pl.when(pl.program_id(0) == pl.num_programs(0) - 1)
    def _():
        zero_ref[...] = jnp.zeros_like(zero_ref)
        occupied = fill_ref[2 * n_experts] * tile_rows
        spans = [(fill_ref[e], fill_ref[n_experts + e] - fill_ref[e]) for e in range(n_experts)]
        spans.append((occupied, xs_ref.shape[0] - occupied))

        def zero_copy(first, rows):
            return pltpu.make_async_copy(zero_ref.at[pl.ds(0, rows)],
                                         xs_ref.at[pl.ds(first, rows)], sem)

        for first, length in spans:
            _for_each_chunk(first, length, zero_ref.shape[0],
                            lambda row, rows: zero_copy(row, rows).start())
        for first, length in spans:
            _for_each_chunk(first, length, zero_ref.shape[0],
                            lambda row, rows: zero_copy(row, rows).wait())


def _scatter(h, pos, fill, n_rows):
    t = h.shape[0]
    ts = ROUTE_ROWS
    zero_rows = 256
    return pl.pallas_call(
        functools.partial(_scatter_kernel, n_experts=(fill.shape[0] - 1) // 2,
                          tile_rows=FFN_ROWS),
        grid=(t // ts,),
        in_specs=[
            pl.BlockSpec((1, 1, TOP_K * ts), lambda i: (i, 0, 0), memory_space=pltpu.SMEM),
            pl.BlockSpec(memory_space=pltpu.SMEM),
            pl.BlockSpec((ts,) + h.shape[1:], lambda i: (i, 0, 0)),
        ],
        out_specs=pl.BlockSpec(memory_space=pl.ANY),
        out_shape=jax.ShapeDtypeStruct((n_rows,) + h.shape[1:], F32),
        scratch_shapes=[pltpu.VMEM((zero_rows,) + h.shape[1:], F32), pltpu.SemaphoreType.DMA(())],
        compiler_params=pltpu.CompilerParams(
            dimension_semantics=("arbitrary",), vmem_limit_bytes=VMEM_LIMIT),
        name="moe_scatter",
    )(pos, fill, h)


def _experts_kernel(exp_ref, nact_ref, xs_ref, w1_ref, w3_ref, w2_ref, ys_ref, h_ref, acc_ref):
    i = pl.program_id(0)
    k = pl.program_id(1)

    @pl.when(i < nact_ref[0])
    def _():
        @pl.when(k == 0)
        def _():
            h_ref[...] = _from_token_major(xs_ref, h_ref.shape[1] // LANES).astype(BF16)

        h = h_ref[...]
        u = jax.nn.silu(_dot(h, w1_ref[0])) * _dot(h, w3_ref[0])
        part = _dot(u.astype(BF16), w2_ref[0])

        @pl.when(k == 0)
        def _():
            acc_ref[...] = part

        @pl.when(k > 0)
        def _():
            acc_ref[...] += part

        @pl.when(k == pl.num_programs(1) - 1)
        def _():
            _to_token_major(ys_ref, acc_ref[...])

    @pl.when((i >= nact_ref[0]) & (k == 0))
    def _():
        ys_ref[...] = jnp.zeros_like(ys_ref)


def _experts(xs, tile_expert, n_active, w1, w3, w2):
    d, f = w1.shape[1:]
    n_rows = xs.shape[0] * LANES // d
    tm, tf = FFN_ROWS, FFN_COLS
    n_k = f // tf
    n_tiles = n_rows // tm

    def k_of(i, k, nact):
        return jnp.where(i < nact[0], k, n_k - 1)

    return pl.pallas_call(
        _experts_kernel,
        grid_spec=pltpu.PrefetchScalarGridSpec(
            num_scalar_prefetch=2,
            grid=(n_tiles, n_k),
            in_specs=[
                pl.BlockSpec((tm * d // LANES, LANES), lambda i, k, exp, nact: (i, 0)),
                pl.BlockSpec((1, d, tf), lambda i, k, exp, nact: (exp[i], 0, k_of(i, k, nact))),
                pl.BlockSpec((1, d, tf), lambda i, k, exp, nact: (exp[i], 0, k_of(i, k, nact))),
                pl.BlockSpec((1, tf, d), lambda i, k, exp, nact: (exp[i], k_of(i, k, nact), 0)),
            ],
            out_specs=pl.BlockSpec((tm * d // LANES, LANES), lambda i, k, exp, nact: (i, 0)),
            scratch_shapes=[pltpu.VMEM((tm, d), BF16), pltpu.VMEM((tm, d), F32)],
        ),
        out_shape=jax.ShapeDtypeStruct(xs.shape, F32),
        compiler_params=pltpu.CompilerParams(
            dimension_semantics=("arbitrary", "arbitrary"), vmem_limit_bytes=VMEM_LIMIT),
        name="moe_experts",
    )(tile_expert, n_active, xs, w1, w3, w2)


def _combine_kernel(pos_ref, route_ref, x_ref, gpost_ref, gate_ref, ys_ref, o_ref, buf0_ref,
                    buf1_ref, sem):
    ts = x_ref.shape[0]
    n = ys_ref.shape[1]
    bufs = (buf0_ref, buf1_ref)

    def issue(r, _):
        dst = pl.multiple_of(r * n, n)
        for slot in range(TOP_K):
            src = pos_ref[0, 0, slot * ts + r]
            pltpu.make_async_copy(ys_ref.at[src], bufs[slot].at[pl.ds(dst, n)], sem).start()
        return 0

    lax.fori_loop(0, ts, issue, 0)
    for slot in range(TOP_K):
        pltpu.make_async_copy(bufs[slot], bufs[slot], sem).wait()

    w0 = ROUTE_LANES.index("weight0")
    route = route_ref[...]
    y = (route[:, w0:w0 + 1] * _from_token_major(buf0_ref, n)
         + route[:, w0 + 1:w0 + 2] * _from_token_major(buf1_ref, n))
    o_ref[...] = x_ref[...] + gate_ref[0] * _rms(y, gpost_ref[...])


def _combine(ys, pos, route, x, g_post, gate):
    t, d = x.shape
    bsz = gate.shape[0]
    ts = ROUTE_ROWS
    steps_per_b = t // bsz // ts
    return pl.pallas_call(
        _combine_kernel,
        grid=(t // ts,),
        in_specs=[
            pl.BlockSpec((1, 1, TOP_K * ts), lambda i: (i, 0, 0), memory_space=pltpu.SMEM),
            pl.BlockSpec((ts, LANES), lambda i: (i, 0)),
            pl.BlockSpec((ts, d), lambda i: (i, 0)),
            pl.BlockSpec((1, d), lambda i: (0, 0)),
            pl.BlockSpec((1, 1, d), lambda i: (i // steps_per_b, 0, 0)),
            pl.BlockSpec(memory_space=pl.ANY),
        ],
        out_specs=pl.BlockSpec((ts, d), lambda i: (i, 0)),
        out_shape=jax.ShapeDtypeStruct((t, d), F32),
        scratch_shapes=[pltpu.VMEM((ts * d // LANES, LANES), F32)] * TOP_K
        + [pltpu.SemaphoreType.DMA(())],
        compiler_params=pltpu.CompilerParams(
            dimension_semantics=("arbitrary",), vmem_limit_bytes=VMEM_LIMIT),
        name="moe_combine",
    )(pos, route, x, g_post.reshape(1, d), gate, ys)


def _moe(x, g_pre, scale, shift, w_router, w1, w3, w2, g_post, gate):
    bsz, s, d = x.shape
    t = bsz * s
    n_experts = w_router.shape[1]
    ts, tm = ROUTE_ROWS, FFN_ROWS
    h, route, counts = _route(x, g_pre, scale, shift, w_router)
    route = route.reshape(t, LANES)

    lane_of = ROUTE_LANES.index
    expert = route[:, lane_of("expert0"):lane_of("expert0") + TOP_K].astype(jnp.int32)
    rank = route[:, lane_of("rank0"):lane_of("rank0") + TOP_K].astype(jnp.int32)
    counts = counts[0, :n_experts].astype(jnp.int32)
    tiles_per = (counts + tm - 1) // tm
    tile_end = jnp.cumsum(tiles_per)
    first_row = (tile_end - tiles_per) * tm
    n_active = tile_end[-1]
    n_tiles = TOP_K * t // tm + n_experts
    is_expert = expert[..., None] == jnp.arange(n_experts, dtype=jnp.int32)
    pos = jnp.sum(jnp.where(is_expert, first_row, 0), axis=-1) + rank
    pos = pos.reshape(t // ts, ts, TOP_K).transpose(0, 2, 1).reshape(t // ts, 1, TOP_K * ts)
    fill = jnp.concatenate([first_row + counts, tile_end * tm, n_active[None]]).astype(jnp.int32)
    idx = jnp.minimum(jnp.arange(n_tiles, dtype=jnp.int32), n_active - 1)
    tile_expert = jnp.sum(idx[:, None] >= tile_end[None, :], axis=1).astype(jnp.int32)

    token_major = (-1, d // LANES, LANES)
    xs = _scatter(h.reshape(token_major), pos, fill, n_tiles * tm)
    ys = _experts(xs.reshape(-1, LANES), tile_expert, n_active.reshape(1), w1, w3, w2)
    out = _combine(ys.reshape(token_major), pos, route, x.reshape(t, d), g_post,
                   gate.reshape(bsz, 1, d))
    return out.reshape(bsz, s, d)


def kernel(x, c, w_mod, b_mod, g_mix_pre, g_mix_post, g_ffn_pre, g_ffn_post, w_in, w_proj_sb,
           w_proj_moba, w_out, rel_bias, w1_dense, w3_dense, w2_dense, w_router, w1_moe, w3_moe,
           w2_moe):
    depth = w_mod.shape[0]
    d = x.shape[-1]
    qkv = 3 * W_BRANCH
    col = jnp.arange(w_in.shape[-1])
    is_q = (col < W_BRANCH) | ((col >= qkv) & (col < qkv + W_BRANCH))
    col_scale = jnp.where(is_q, HEAD_DIM ** -0.5, 1.0).astype(F32)

    mod = _modulation(c, w_mod, b_mod)
    bias_tiles = _moba_bias_tiles(rel_bias)
    for l in range(depth):
        sh_mix, sc_mix, gt_mix, sh_ffn, sc_ffn, gt_ffn = jnp.split(mod[l], N_MOD, axis=-1)
        proj = _inproj(x, g_mix_pre[l], sc_mix, sh_mix, (w_in[l] * col_scale).astype(BF16))
        o_a = _stick_breaking(proj, 0)
        o_b = _moba(proj, qkv // LANES, bias_tiles)
        x = _merge(o_a, o_b, proj, 2 * qkv // d, x, w_proj_sb[l].astype(BF16),
                   w_proj_moba[l].astype(BF16), w_out[l].astype(BF16), g_mix_post[l], gt_mix)
        j = l // 2
        if l % 2 == 0:
            x = _ffn(x, g_ffn_pre[l], sc_ffn, sh_ffn, w1_dense[j].astype(BF16),
                     w3_dense[j].astype(BF16), w2_dense[j].astype(BF16), g_ffn_post[l], gt_ffn)
        else:
            x = _moe(x, g_ffn_pre[l], sc_ffn, sh_ffn, w_router[j], w1_moe[j].astype(BF16),
                     w3_moe[j].astype(BF16), w2_moe[j].astype(BF16), g_ffn_post[l], gt_ffn)
    return x
```

```python
import functools
import math

import jax
import jax.numpy as jnp
from jax import lax
from jax.experimental import pallas as pl
from jax.experimental.pallas import tpu as pltpu

F32 = jnp.float32
BF16 = jnp.bfloat16

EPS = 1e-6
NEG_INF = -1e30
HEAD_DIM = 64
N_HEADS = 8
W_BRANCH = N_HEADS * HEAD_DIM
LANES = 128
SUBLANES = 8
HEADS_PER_TILE = LANES // HEAD_DIM
N_PAIRS = N_HEADS // HEADS_PER_TILE
MOBA_BLOCK = 256
MOBA_TOPK = 3
REL_BUCKETS = 32
REL_MAX_DIST = 128
N_MOD = 6
TOP_K = 2
ATT_BLOCK = 256
VMEM_LIMIT = 56 * 1024 * 1024


def _dot(a, b):
    return jnp.dot(a, b, preferred_element_type=F32)


def _dot_nt(a, b):
    return lax.dot_general(a, b, (((1,), (1,)), ((), ())), preferred_element_type=F32)


def _split_bf16(x):
    hi = x.astype(BF16)
    lo = (x - hi.astype(F32)).astype(BF16)
    return hi, lo


def _rms(x, g):
    return x * lax.rsqrt(jnp.mean(x * x, axis=-1, keepdims=True) + EPS) * g


def _mod_kernel(c_ref, w_ref, b_ref, o_ref):
    c = c_ref[...]
    cond = c * jax.nn.sigmoid(c)
    o_ref[0] = jnp.dot(cond, w_ref[0], preferred_element_type=F32,
                       precision=lax.Precision.HIGHEST) + b_ref[0]


def _modulation(c, w_mod, b_mod):
    depth, d, n = w_mod.shape
    bsz = c.shape[0]
    tn = 1536
    return pl.pallas_call(
        _mod_kernel,
        grid=(depth, n // tn),
        in_specs=[
            pl.BlockSpec((bsz, d), lambda l, j: (0, 0)),
            pl.BlockSpec((1, d, tn), lambda l, j: (l, 0, j)),
            pl.BlockSpec((1, 1, tn), lambda l, j: (l, 0, j)),
        ],
        out_specs=pl.BlockSpec((1, bsz, tn), lambda l, j: (l, 0, j)),
        out_shape=jax.ShapeDtypeStruct((depth, bsz, n), F32),
        compiler_params=pltpu.CompilerParams(
            dimension_semantics=("arbitrary", "arbitrary"), vmem_limit_bytes=VMEM_LIMIT),
        name="modulation",
    )(c, w_mod, b_mod.reshape(depth, 1, n))


def _inproj_kernel(x_ref, g_ref, sc_ref, sh_ref, w_ref, o_ref, *, tn):
    h = _rms(x_ref[0], g_ref[...]) * (1.0 + sc_ref[0]) + sh_ref[0]
    hb = h.astype(BF16)
    for n in range(w_ref.shape[1] // tn):
        o_ref[0, :, n * tn:(n + 1) * tn] = _dot(hb, w_ref[:, n * tn:(n + 1) * tn]).astype(BF16)


def _inproj(x, g, scale, shift, w):
    bsz, s, d = x.shape
    n = w.shape[1]
    tm = 512
    return pl.pallas_call(
        functools.partial(_inproj_kernel, tn=1024),
        grid=(bsz, s // tm),
        in_specs=[
            pl.BlockSpec((1, tm, d), lambda b, t: (b, t, 0)),
            pl.BlockSpec((1, d), lambda b, t: (0, 0)),
            pl.BlockSpec((1, 1, d), lambda b, t: (b, 0, 0)),
            pl.BlockSpec((1, 1, d), lambda b, t: (b, 0, 0)),
            pl.BlockSpec((d, n), lambda b, t: (0, 0), pipeline_mode=pl.Buffered(1)),
        ],
        out_specs=pl.BlockSpec((1, tm, n), lambda b, t: (b, t, 0)),
        out_shape=jax.ShapeDtypeStruct((bsz, s, n), BF16),
        compiler_params=pltpu.CompilerParams(
            dimension_semantics=("arbitrary", "arbitrary"), vmem_limit_bytes=VMEM_LIMIT),
        name="inproj",
    )(x, g.reshape(1, d), scale.reshape(bsz, 1, d), shift.reshape(bsz, 1, d), w)


SOFTPLUS_LINEAR = 40.0


def _softplus(z):
    return jnp.maximum(z, jnp.log(1.0 + jnp.exp(jnp.minimum(z, SOFTPLUS_LINEAR))))


def _for_each_group(n, run):
    big = n // 4

    def loop(g, _):
        run(g * 4, 4)
        return 0

    lax.fori_loop(0, big, loop, 0)
    base = big * 4

    @pl.when((n & 2) != 0)
    def _():
        run(base, 2)

    @pl.when((n & 1) != 0)
    def _():
        run(base + (n & 2), 1)


def _sb_kernel(q_ref, k_ref, v_ref, o_ref, acc_ref, carry_ref):
    i = pl.program_id(2)
    tq = ATT_BLOCK
    lane = lax.broadcasted_iota(jnp.int32, (1, LANES), 1)
    row = lax.broadcasted_iota(jnp.int32, (tq, tq), 0)
    col = lax.broadcasted_iota(jnp.int32, (tq, tq), 1)
    upper = jnp.where(row > col, 1.0, 0.0).astype(BF16)
    causal = jnp.concatenate([col < row] * HEADS_PER_TILE, axis=0)
    q = q_ref[0]
    q2 = jnp.concatenate([jnp.where(lane // HEAD_DIM == h, q, jnp.zeros_like(q))
                          for h in range(HEADS_PER_TILE)], axis=0)

    def suffix_in_block(sp):
        return _dot(sp.astype(BF16), upper)

    def run(first, size):
        diag = isinstance(first, int) and first == 0
        carry = None if diag else carry_ref[...]
        acc = None
        for n in range(size):
            st = pl.multiple_of((i - first - n) * tq, tq)
            masked = diag and n == 0
            z = _dot_nt(q2, k_ref[0, pl.ds(st, tq), :])
            sp = _softplus(z)
            if masked:
                sp = jnp.where(causal, sp, 0.0)
            logit = z - sp - suffix_in_block(sp)
            if carry is not None:
                logit = logit - carry
            att = jnp.exp(logit)
            if masked:
                att = jnp.where(causal, att, 0.0)
            part = _dot(att.astype(BF16), v_ref[0, pl.ds(st, tq), :])
            acc = part if acc is None else acc + part
            rs = jnp.sum(sp, axis=1, keepdims=True)
            carry = rs if carry is None else carry + rs
        acc_ref[...] = acc if diag else acc_ref[...] + acc
        carry_ref[...] = carry

    @pl.when(i == 0)
    def _():
        run(0, 1)

    @pl.when(i >= 1)
    def _():
        run(0, 2)

    _for_each_group(jnp.maximum(i - 1, 0), lambda first, size: run(first + 2, size))
    o_ref[0] = jnp.where(lane < HEAD_DIM, acc_ref[:tq], acc_ref[tq:]).astype(BF16)


def _stick_breaking(proj, col0):
    bsz, s, _ = proj.shape
    tq = ATT_BLOCK
    return pl.pallas_call(
        _sb_kernel,
        grid=(bsz, N_PAIRS, s // tq),
        in_specs=[
            pl.BlockSpec((1, tq, LANES), lambda b, p, i: (b, i, col0 + p)),
            pl.BlockSpec((1, s, LANES), lambda b, p, i: (b, 0, col0 + N_PAIRS + p)),
            pl.BlockSpec((1, s, LANES), lambda b, p, i: (b, 0, col0 + 2 * N_PAIRS + p)),
        ],
        out_specs=pl.BlockSpec((1, tq, LANES), lambda b, p, i: (b, i, p)),
        out_shape=jax.ShapeDtypeStruct((bsz, s, W_BRANCH), BF16),
        scratch_shapes=[pltpu.VMEM((HEADS_PER_TILE * tq, LANES), F32),
                        pltpu.VMEM((HEADS_PER_TILE * tq, 1), F32)],
        compiler_params=pltpu.CompilerParams(
            dimension_semantics=("arbitrary", "arbitrary", "arbitrary"),
            vmem_limit_bytes=VMEM_LIMIT),
        name="stick_breaking",
    )(proj, proj, proj)


def _sb_ks_kernel(q_ref, k_ref, v_ref, o_ref, qs_ref, acc_ref, carry_ref):
    tq = ATT_BLOCK
    n_blocks = k_ref.shape[1] // tq
    rows_per_blk = HEADS_PER_TILE * tq
    lane = lax.broadcasted_iota(jnp.int32, (1, LANES), 1)
    row = lax.broadcasted_iota(jnp.int32, (tq, tq), 0)
    col = lax.broadcasted_iota(jnp.int32, (tq, tq), 1)
    upper = jnp.where(row > col, 1.0, 0.0).astype(BF16)
    causal = jnp.concatenate([col < row] * HEADS_PER_TILE, axis=0)
    q = q_ref[0]
    for i in range(n_blocks):
        qi = q[i * tq:(i + 1) * tq]
        for h in range(HEADS_PER_TILE):
            row0 = (HEADS_PER_TILE * i + h) * tq
            qs_ref[row0:row0 + tq, :] = jnp.where(lane // HEAD_DIM == h, qi, jnp.zeros_like(qi))

    for j in reversed(range(n_blocks)):
        row0 = j * rows_per_blk
        far0 = row0 + rows_per_blk
        has_far = j + 1 < n_blocks
        z = _dot_nt(qs_ref[row0:, :], k_ref[0, j * tq:(j + 1) * tq, :])
        sp = _softplus(z)
        sp_diag = jnp.where(causal, sp[:rows_per_blk], 0.0)
        sp = jnp.concatenate([sp_diag, sp[rows_per_blk:]], axis=0) if has_far else sp_diag
        logit = z - sp - _dot(sp.astype(BF16), upper)
        att_diag = jnp.where(causal, jnp.exp(logit[:rows_per_blk]), 0.0)
        if has_far:
            att_far = jnp.exp(logit[rows_per_blk:] - carry_ref[far0:, :])
            att = jnp.concatenate([att_diag, att_far], axis=0)
        else:
            att = att_diag
        part = _dot(att.astype(BF16), v_ref[0, j * tq:(j + 1) * tq, :])
        rs = jnp.sum(sp, axis=1, keepdims=True)
        acc_ref[row0:far0, :] = part[:rows_per_blk]
        carry_ref[row0:far0, :] = rs[:rows_per_blk]
        if has_far:
            acc_ref[far0:, :] += part[rows_per_blk:]
            carry_ref[far0:, :] += rs[rows_per_blk:]

    for i in range(n_blocks):
        blk = acc_ref[i * rows_per_blk:(i + 1) * rows_per_blk, :]
        o_ref[0, i * tq:(i + 1) * tq, :] = jnp.where(lane < HEAD_DIM, blk[:tq], blk[tq:]).astype(BF16)


def _stick_breaking_ks(proj, col0):
    bsz, s, _ = proj.shape
    assert s % ATT_BLOCK == 0
    rows = HEADS_PER_TILE * s
    return pl.pallas_call(
        _sb_ks_kernel,
        grid=(bsz, N_PAIRS),
        in_specs=[
            pl.BlockSpec((1, s, LANES), lambda b, p: (b, 0, col0 + p)),
            pl.BlockSpec((1, s, LANES), lambda b, p: (b, 0, col0 + N_PAIRS + p)),
            pl.BlockSpec((1, s, LANES), lambda b, p: (b, 0, col0 + 2 * N_PAIRS + p)),
        ],
        out_specs=pl.BlockSpec((1, s, LANES), lambda b, p: (b, 0, p)),
        out_shape=jax.ShapeDtypeStruct((bsz, s, W_BRANCH), BF16),
        scratch_shapes=[pltpu.VMEM((rows, LANES), BF16),
                        pltpu.VMEM((rows, LANES), F32),
                        pltpu.VMEM((rows, 1), F32)],
        compiler_params=pltpu.CompilerParams(
            dimension_semantics=("arbitrary", "arbitrary"), vmem_limit_bytes=VMEM_LIMIT),
        name="stick_breaking",
    )(proj, proj, proj)


def _rel_bucket(dist):
    max_exact = REL_BUCKETS // 2
    n = jnp.maximum(dist, 0)
    nf = jnp.maximum(n, 1).astype(F32)
    large = max_exact + (jnp.log(nf / max_exact) / math.log(REL_MAX_DIST / max_exact)
                         * (REL_BUCKETS - max_exact)).astype(jnp.int32)
    large = jnp.minimum(large, REL_BUCKETS - 1)
    return jnp.where(n < max_exact, n, large)


def _moba_bias_tiles(rel_bias):
    assert MOBA_BLOCK >= REL_MAX_DIST
    t = jnp.arange(MOBA_BLOCK)[:, None]
    s = jnp.arange(MOBA_BLOCK)[None, :]
    d_own = t - s
    rel = rel_bias.astype(F32) - rel_bias[REL_BUCKETS - 1].astype(F32)

    def lookup(dist):
        onehot = jax.nn.one_hot(_rel_bucket(dist), REL_BUCKETS, dtype=F32)
        return jnp.einsum("tsk,kh->hts", onehot, rel, precision=lax.Precision.HIGHEST)

    own = jnp.where((d_own >= 0)[None], lookup(d_own), NEG_INF)
    return jnp.stack([own, lookup(d_own + MOBA_BLOCK)], axis=1)


def _moba_kernel(q_ref, k_ref, v_ref, bias_ref, o_ref,
                 kmh_ref, kml_ref, ka_ref, va_ref, m_ref, accl_ref, *, n_blocks):
    i = pl.program_id(2)
    tq = MOBA_BLOCK
    s_len = k_ref.shape[1]
    nb_pad = kmh_ref.shape[0]
    lane = lax.broadcasted_iota(jnp.int32, (1, LANES), 1)

    @pl.when(i == 0)
    def _():
        r = lax.broadcasted_iota(jnp.int32, (nb_pad, s_len), 0)
        c = lax.broadcasted_iota(jnp.int32, (nb_pad, s_len), 1)
        pool = jnp.where(c // tq == r, 1.0 / tq, 0.0).astype(BF16)
        hi, lo = _split_bf16(_dot(pool, k_ref[0]))
        kmh_ref[...] = hi
        kml_ref[...] = lo
        key_blk = lax.broadcasted_iota(jnp.int32, (s_len, LANES), 0) // tq
        ka_ref[:, :LANES] = k_ref[0]
        ka_ref[:, LANES:] = jnp.where(lane == key_blk, 1.0, 0.0).astype(BF16)
        va_ref[:, :LANES] = v_ref[0]
        va_ref[:, LANES:] = jnp.ones((s_len, LANES), BF16)

    q = q_ref[0]
    blk_row = lax.broadcasted_iota(jnp.int32, (nb_pad, tq), 0)
    er = lax.broadcasted_iota(jnp.int32, (tq, tq), 0)
    ec = lax.broadcasted_iota(jnp.int32, (tq, tq), 1)
    eye = jnp.where(er == ec, 1.0, 0.0).astype(BF16)
    head_lanes = [lane // HEAD_DIM == h for h in range(HEADS_PER_TILE)]

    means = [jnp.where(hl, ref[...], jnp.zeros_like(ref[...]))
             for hl in head_lanes for ref in (kmh_ref, kml_ref)]
    gates = _dot_nt(jnp.concatenate(means, axis=0), q)
    masks = []
    for h in range(HEADS_PER_TILE):
        gate = (gates[2 * h * nb_pad:(2 * h + 1) * nb_pad]
                + gates[(2 * h + 1) * nb_pad:(2 * h + 2) * nb_pad])
        cnt = jnp.zeros((nb_pad, tq), F32)
        for n in range(n_blocks):
            gn = gate[n:n + 1, :]
            beats = jnp.where(gn > gate, 1.0, jnp.where((gn == gate) & (n < blk_row), 1.0, 0.0))
            cnt = cnt + jnp.where(n < i, beats, 0.0)
        keep = (blk_row == i) | ((blk_row < i) & (cnt < MOBA_TOPK))
        masks += [jnp.where(keep, 0.0, NEG_INF).astype(BF16),
                  jnp.zeros((LANES - nb_pad, tq), BF16)]
    mask_rows = _dot_nt(eye, jnp.concatenate(masks, axis=0)).astype(BF16)
    q_aug = jnp.concatenate(
        [jnp.concatenate([jnp.where(head_lanes[h], q, jnp.zeros_like(q)),
                          mask_rows[:, h * LANES:(h + 1) * LANES]], axis=1)
         for h in range(HEADS_PER_TILE)], axis=0)

    def attend(blk, bias):
        st = pl.multiple_of(blk * tq, tq)
        s = _dot_nt(q_aug, ka_ref[pl.ds(st, tq), :])
        if bias is not None:
            s = s + bias
        m = jnp.max(s, axis=1, keepdims=True)
        pr = jnp.exp(s - m)
        return m, _dot(pr.astype(BF16), va_ref[pl.ds(st, tq), :])

    def merge(parts, init):
        ms = [part[0] for part in parts]
        if not init:
            ms.append(m_ref[...])
        m_new = functools.reduce(jnp.maximum, ms)
        accl = None if init else jnp.exp(m_ref[...] - m_new) * accl_ref[...]
        for m, ov in parts:
            w = jnp.exp(m - m_new)
            accl = w * ov if accl is None else accl + w * ov
        m_ref[...] = m_new
        accl_ref[...] = accl

    def bias_tile(which):
        return jnp.concatenate([bias_ref[h, which] for h in range(HEADS_PER_TILE)], axis=0)

    @pl.when(i == 0)
    def _():
        merge([attend(i, bias_tile(0))], True)

    @pl.when(i >= 1)
    def _():
        merge([attend(i, bias_tile(0)), attend(i - 1, bias_tile(1))], True)

    def far(first, size):
        merge([attend(first + n, None) for n in range(size)], False)

    _for_each_group(jnp.maximum(i - 1, 0), far)

    out = accl_ref[:, :LANES] / accl_ref[:, LANES:]
    o_ref[0] = jnp.where(lane < HEAD_DIM, out[:tq], out[tq:]).astype(BF16)


def _moba(proj, col0, bias_tiles):
    bsz, s, _ = proj.shape
    tq = MOBA_BLOCK
    assert s % tq == 0
    n_blocks = s // tq
    nb_pad = 16
    assert n_blocks <= nb_pad
    return pl.pallas_call(
        functools.partial(_moba_kernel, n_blocks=n_blocks),
        grid=(bsz, N_PAIRS, n_blocks),
        in_specs=[
            pl.BlockSpec((1, tq, LANES), lambda b, p, i: (b, i, col0 + p)),
            pl.BlockSpec((1, s, LANES), lambda b, p, i: (b, 0, col0 + N_PAIRS + p)),
            pl.BlockSpec((1, s, LANES), lambda b, p, i: (b, 0, col0 + 2 * N_PAIRS + p)),
            pl.BlockSpec((HEADS_PER_TILE, 2, tq, tq), lambda b, p, i: (p, 0, 0, 0)),
        ],
        out_specs=pl.BlockSpec((1, tq, LANES), lambda b, p, i: (b, i, p)),
        out_shape=jax.ShapeDtypeStruct((bsz, s, W_BRANCH), BF16),
        scratch_shapes=[
            pltpu.VMEM((nb_pad, LANES), BF16),
            pltpu.VMEM((nb_pad, LANES), BF16),
            pltpu.VMEM((s, 2 * LANES), BF16),
            pltpu.VMEM((s, 2 * LANES), BF16),
            pltpu.VMEM((HEADS_PER_TILE * tq, 1), F32),
            pltpu.VMEM((HEADS_PER_TILE * tq, 2 * LANES), F32),
        ],
        compiler_params=pltpu.CompilerParams(
            dimension_semantics=("arbitrary", "arbitrary", "arbitrary"),
            vmem_limit_bytes=VMEM_LIMIT),
        name="moba",
    )(proj, proj, proj, bias_tiles)


def _moba_ks_kernel(q_ref, k_ref, v_ref, bias_ref, o_ref,
                    ka_ref, va_ref, qa_ref, s_ref, m_ref, accl_ref, *, n_blocks):
    tq = MOBA_BLOCK
    s_len = k_ref.shape[1]
    nb_pad = 16
    rows_per_blk = HEADS_PER_TILE * tq
    lane = lax.broadcasted_iota(jnp.int32, (1, LANES), 1)
    head_lanes = [lane // HEAD_DIM == h for h in range(HEADS_PER_TILE)]
    q = q_ref[0]
    k = k_ref[0]

    key_blk = lax.broadcasted_iota(jnp.int32, (s_len, LANES), 0) // tq
    ka_ref[:, :LANES] = k
    ka_ref[:, LANES:] = jnp.where(lane == key_blk, 1.0, 0.0).astype(BF16)
    va_ref[:, :LANES] = v_ref[0]
    va_ref[:, LANES:] = jnp.ones((s_len, LANES), BF16)

    r = lax.broadcasted_iota(jnp.int32, (nb_pad, s_len), 0)
    c = lax.broadcasted_iota(jnp.int32, (nb_pad, s_len), 1)
    pool = jnp.where(c // tq == r, 1.0 / tq, 0.0).astype(BF16)
    km_hi, km_lo = _split_bf16(_dot(pool, k))
    means = [jnp.where(hl, km, jnp.zeros_like(km)) for hl in head_lanes for km in (km_hi, km_lo)]
    gates = _dot_nt(jnp.concatenate(means, axis=0), q)
    own = c // tq
    masks = []
    for h in range(HEADS_PER_TILE):
        gate = (gates[2 * h * nb_pad:(2 * h + 1) * nb_pad]
                + gates[(2 * h + 1) * nb_pad:(2 * h + 2) * nb_pad])
        cnt = jnp.zeros((nb_pad, s_len), F32)
        for n in range(n_blocks):
            gn = gate[n:n + 1, :]
            beats = jnp.where(gn > gate, 1.0, jnp.where((gn == gate) & (n < r), 1.0, 0.0))
            cnt = cnt + jnp.where(n < own, beats, 0.0)
        keep = (r == own) | ((r < own) & (cnt < MOBA_TOPK))
        masks += [jnp.where(keep, 0.0, NEG_INF).astype(BF16),
                  jnp.zeros((LANES - nb_pad, s_len), BF16)]
    masks = jnp.concatenate(masks, axis=0)
    er = lax.broadcasted_iota(jnp.int32, (tq, tq), 0)
    ec = lax.broadcasted_iota(jnp.int32, (tq, tq), 1)
    eye = jnp.where(er == ec, 1.0, 0.0).astype(BF16)
    for i in range(n_blocks):
        qi = q[i * tq:(i + 1) * tq]
        mask_rows = _dot_nt(eye, masks[:, i * tq:(i + 1) * tq]).astype(BF16)
        for h in range(HEADS_PER_TILE):
            row0 = (HEADS_PER_TILE * i + h) * tq
            qa_ref[row0:row0 + tq, :LANES] = jnp.where(head_lanes[h], qi, jnp.zeros_like(qi))
            qa_ref[row0:row0 + tq, LANES:] = mask_rows[:, h * LANES:(h + 1) * LANES]

    bias = [jnp.concatenate([bias_ref[h, which] for h in range(HEADS_PER_TILE)], axis=0)
            for which in range(2)]
    offs = [0]
    for j in range(n_blocks):
        offs.append(offs[-1] + (n_blocks - j) * rows_per_blk)
    for j in range(n_blocks):
        row0 = j * rows_per_blk
        s = _dot_nt(qa_ref[row0:, :], ka_ref[j * tq:(j + 1) * tq, :])
        parts = [s[:rows_per_blk] + bias[0]]
        if j + 1 < n_blocks:
            parts.append(s[rows_per_blk:2 * rows_per_blk] + bias[1])
        if j + 2 < n_blocks:
            parts.append(s[2 * rows_per_blk:])
        s = jnp.concatenate(parts, axis=0)
        s_ref[offs[j]:offs[j + 1], :] = s
        m = jnp.max(s, axis=1, keepdims=True)
        if j == 0:
            m_ref[...] = m
        else:
            m_ref[row0:, :] = jnp.maximum(m_ref[row0:, :], m)
    for j in range(n_blocks):
        row0 = j * rows_per_blk
        pr = jnp.exp(s_ref[offs[j]:offs[j + 1], :] - m_ref[row0:, :])
        ov = _dot(pr.astype(BF16), va_ref[j * tq:(j + 1) * tq, :])
        if j == 0:
            accl_ref[...] = ov
        else:
            accl_ref[row0:, :] += ov

    for i in range(n_blocks):
        blk = accl_ref[i * rows_per_blk:(i + 1) * rows_per_blk, :]
        out = blk[:, :LANES] / blk[:, LANES:]
        o_ref[0, i * tq:(i + 1) * tq, :] = jnp.where(lane < HEAD_DIM, out[:tq], out[tq:]).astype(BF16)


def _moba_ks(proj, col0, bias_tiles):
    bsz, s, _ = proj.shape
    tq = MOBA_BLOCK
    assert s % tq == 0
    n_blocks = s // tq
    assert n_blocks <= 16
    rows = HEADS_PER_TILE * s
    return pl.pallas_call(
        functools.partial(_moba_ks_kernel, n_blocks=n_blocks),
        grid=(bsz, N_PAIRS),
        in_specs=[
            pl.BlockSpec((1, s, LANES), lambda b, p: (b, 0, col0 + p)),
            pl.BlockSpec((1, s, LANES), lambda b, p: (b, 0, col0 + N_PAIRS + p)),
            pl.BlockSpec((1, s, LANES), lambda b, p: (b, 0, col0 + 2 * N_PAIRS + p)),
            pl.BlockSpec((HEADS_PER_TILE, 2, tq, tq), lambda b, p: (p, 0, 0, 0)),
        ],
        out_specs=pl.BlockSpec((1, s, LANES), lambda b, p: (b, 0, p)),
        out_shape=jax.ShapeDtypeStruct((bsz, s, W_BRANCH), BF16),
        scratch_shapes=[
            pltpu.VMEM((s, 2 * LANES), BF16),
            pltpu.VMEM((s, 2 * LANES), BF16),
            pltpu.VMEM((rows, 2 * LANES), BF16),
            pltpu.VMEM((rows * (n_blocks + 1) // 2, tq), F32),
            pltpu.VMEM((rows, 1), F32),
            pltpu.VMEM((rows, 2 * LANES), F32),
        ],
        compiler_params=pltpu.CompilerParams(
            dimension_semantics=("arbitrary", "arbitrary"), vmem_limit_bytes=VMEM_LIMIT),
        name="moba",
    )(proj, proj, proj, bias_tiles)


def _merge_kernel(oa_ref, ob_ref, ga_ref, gb_ref, x_ref, wa_ref, wb_ref, wo_ref, g_ref, gate_ref,
                  o_ref):
    pa = _dot(oa_ref[0], wa_ref[...])
    pb = _dot(ob_ref[0], wb_ref[...])
    merged = (jax.nn.sigmoid(ga_ref[0].astype(F32)) * pa
              + jax.nn.sigmoid(gb_ref[0].astype(F32)) * pb)
    y = _dot(merged.astype(BF16), wo_ref[...])
    o_ref[0] = x_ref[0] + gate_ref[0] * _rms(y, g_ref[...])


def _merge(o_a, o_b, proj, gate_col, x, w_a, w_b, w_o, g_post, gate):
    bsz, s, d = x.shape
    tm = 512
    const = lambda b, t: (0, 0)
    return pl.pallas_call(
        _merge_kernel,
        grid=(bsz, s // tm),
        in_specs=[
            pl.BlockSpec((1, tm, W_BRANCH), lambda b, t: (b, t, 0)),
            pl.BlockSpec((1, tm, W_BRANCH), lambda b, t: (b, t, 0)),
            pl.BlockSpec((1, tm, d), lambda b, t: (b, t, gate_col)),
            pl.BlockSpec((1, tm, d), lambda b, t: (b, t, gate_col + 1)),
            pl.BlockSpec((1, tm, d), lambda b, t: (b, t, 0)),
            pl.BlockSpec((W_BRANCH, d), const),
            pl.BlockSpec((W_BRANCH, d), const),
            pl.BlockSpec((d, d), const),
            pl.BlockSpec((1, d), const),
            pl.BlockSpec((1, 1, d), lambda b, t: (b, 0, 0)),
        ],
        out_specs=pl.BlockSpec((1, tm, d), lambda b, t: (b, t, 0)),
        out_shape=jax.ShapeDtypeStruct((bsz, s, d), F32),
        compiler_params=pltpu.CompilerParams(
            dimension_semantics=("arbitrary", "arbitrary"), vmem_limit_bytes=VMEM_LIMIT),
        name="merge",
    )(o_a, o_b, proj, proj, x, w_a, w_b, w_o, g_post.reshape(1, d), gate.reshape(bsz, 1, d))


FFN_ROWS = 1024
FFN_COLS = 512


def _ffn_kernel(x_ref, gpre_ref, sc_ref, sh_ref, w1_ref, w3_ref, w2_ref, gpost_ref, gate_ref,
                o_ref, h_ref, acc_ref):
    k = pl.program_id(2)

    @pl.when(k == 0)
    def _():
        h = _rms(x_ref[0], gpre_ref[...]) * (1.0 + sc_ref[0]) + sh_ref[0]
        h_ref[...] = h.astype(BF16)
        acc_ref[...] = jnp.zeros_like(acc_ref)

    h = h_ref[...]
    u = jax.nn.silu(_dot(h, w1_ref[...])) * _dot(h, w3_ref[...])
    acc_ref[...] += _dot(u.astype(BF16), w2_ref[...])

    @pl.when(k == pl.num_programs(2) - 1)
    def _():
        o_ref[0] = x_ref[0] + gate_ref[0] * _rms(acc_ref[...], gpost_ref[...])


def _ffn(x, g_pre, scale, shift, w1, w3, w2, g_post, gate):
    bsz, s, d = x.shape
    f = w1.shape[1]
    tm, tf = FFN_ROWS, FFN_COLS
    row = lambda b, t, k: (b, t, 0)
    per_b = lambda b, t, k: (b, 0, 0)
    const = lambda b, t, k: (0, 0)
    return pl.pallas_call(
        _ffn_kernel,
        grid=(bsz, s // tm, f // tf),
        in_specs=[
            pl.BlockSpec((1, tm, d), row),
            pl.BlockSpec((1, d), const),
            pl.BlockSpec((1, 1, d), per_b),
            pl.BlockSpec((1, 1, d), per_b),
            pl.BlockSpec((d, tf), lambda b, t, k: (0, k)),
            pl.BlockSpec((d, tf), lambda b, t, k: (0, k)),
            pl.BlockSpec((tf, d), lambda b, t, k: (k, 0)),
            pl.BlockSpec((1, d), const),
            pl.BlockSpec((1, 1, d), per_b),
        ],
        out_specs=pl.BlockSpec((1, tm, d), row),
        out_shape=jax.ShapeDtypeStruct((bsz, s, d), F32),
        scratch_shapes=[pltpu.VMEM((tm, d), BF16), pltpu.VMEM((tm, d), F32)],
        compiler_params=pltpu.CompilerParams(
            dimension_semantics=("arbitrary",) * 3, vmem_limit_bytes=VMEM_LIMIT),
        name="ffn_dense",
    )(x, g_pre.reshape(1, d), scale.reshape(bsz, 1, d), shift.reshape(bsz, 1, d), w1, w3, w2,
      g_post.reshape(1, d), gate.reshape(bsz, 1, d))


ROUTE_ROWS = 512


def _to_token_major(ref, val):
    rows, n = val.shape[0], val.shape[1] // LANES
    for s in range(n):
        ref[pl.ds(s, rows, stride=n), :] = val[:, s * LANES:(s + 1) * LANES]


def _from_token_major(ref, n):
    rows = ref.shape[0] // n
    return jnp.concatenate([ref[pl.ds(s, rows, stride=n), :] for s in range(n)], axis=1)
ROUTE_LANES = ("expert0", "expert1", "weight0", "weight1", "rank0", "rank1")


def _route_kernel(x_ref, gpre_ref, sc_ref, sh_ref, wr_ref, h_ref, route_ref, cnt_ref, *, n_experts):
    first_step = (pl.program_id(0) == 0) & (pl.program_id(1) == 0)

    @pl.when(first_step)
    def _():
        cnt_ref[...] = jnp.zeros_like(cnt_ref)

    h = _rms(x_ref[0], gpre_ref[...]) * (1.0 + sc_ref[0]) + sh_ref[0]
    _to_token_major(h_ref, h)
    logits = jnp.dot(h, wr_ref[...], preferred_element_type=F32, precision=lax.Precision.HIGHEST)
    ts = logits.shape[0]
    lane = lax.broadcasted_iota(jnp.int32, logits.shape, 1)
    lg = jnp.where(lane < n_experts, logits, -jnp.inf)
    m1 = jnp.max(lg, axis=1, keepdims=True)
    i1 = jnp.min(jnp.where(lg == m1, lane, LANES), axis=1, keepdims=True)
    first = lane == i1
    lg2 = jnp.where(first, -jnp.inf, lg)
    m2 = jnp.max(lg2, axis=1, keepdims=True)
    i2 = jnp.min(jnp.where(lg2 == m2, lane, LANES), axis=1, keepdims=True)
    second = lane == i2
    e2 = jnp.exp(m2 - m1)
    w_first = 1.0 / (1.0 + e2)
    w_second = e2 * w_first

    onehot = jnp.where(first | second, 1.0, 0.0)
    r = lax.broadcasted_iota(jnp.int32, (ts, ts), 0)
    c = lax.broadcasted_iota(jnp.int32, (ts, ts), 1)
    lower = jnp.where(c < r, 1.0, 0.0).astype(BF16)
    prefix = _dot(lower, onehot.astype(BF16)) + cnt_ref[...]
    rank_first = jnp.sum(jnp.where(first, prefix, 0.0), axis=1, keepdims=True)
    rank_second = jnp.sum(jnp.where(second, prefix, 0.0), axis=1, keepdims=True)
    cnt_ref[...] += jnp.sum(onehot, axis=0, keepdims=True)

    cols = (i1.astype(F32), i2.astype(F32), w_first, w_second, rank_first, rank_second)
    route = jnp.zeros(logits.shape, F32)
    for n, val in enumerate(cols):
        route = jnp.where(lane == n, val, route)
    route_ref[0] = route


def _route(x, g_pre, scale, shift, w_router):
    bsz, s, d = x.shape
    n_experts = w_router.shape[1]
    ts = ROUTE_ROWS
    row = lambda b, t: (b, t, 0)
    per_b = lambda b, t: (b, 0, 0)
    const = lambda b, t: (0, 0)
    return pl.pallas_call(
        functools.partial(_route_kernel, n_experts=n_experts),
        grid=(bsz, s // ts),
        in_specs=[
            pl.BlockSpec((1, ts, d), row),
            pl.BlockSpec((1, d), const),
            pl.BlockSpec((1, 1, d), per_b),
            pl.BlockSpec((1, 1, d), per_b),
            pl.BlockSpec((d, LANES), const),
        ],
        out_specs=[
            pl.BlockSpec((ts * d // LANES, LANES), lambda b, t: (b * (s // ts) + t, 0)),
            pl.BlockSpec((1, ts, LANES), row),
            pl.BlockSpec((1, LANES), const),
        ],
        out_shape=[
            jax.ShapeDtypeStruct((bsz * s * d // LANES, LANES), F32),
            jax.ShapeDtypeStruct((bsz, s, LANES), F32),
            jax.ShapeDtypeStruct((1, LANES), F32),
        ],
        compiler_params=pltpu.CompilerParams(
            dimension_semantics=("arbitrary", "arbitrary"), vmem_limit_bytes=VMEM_LIMIT),
        name="moe_route",
    )(x, g_pre.reshape(1, d), scale.reshape(bsz, 1, d), shift.reshape(bsz, 1, d),
      jnp.pad(w_router, ((0, 0), (0, LANES - n_experts))))


def _for_each_chunk(start, length, chunk, fn):
    whole = length // chunk

    def loop(c, _):
        fn(start + c * chunk, chunk)
        return 0

    lax.fori_loop(0, whole, loop, 0)
    cur = start + whole * chunk
    size = chunk // 2
    while size >= 1:
        bit = length & size

        @pl.when(bit != 0)
        def _(cur=cur, size=size):
            fn(cur, size)

        cur = cur + bit
        size //= 2


def _scatter_kernel(pos_ref, fill_ref, h_ref, xs_ref, zero_ref, sem, *, n_experts, tile_rows):
    ts = h_ref.shape[0]

    def issue(r, _):
        for slot in range(TOP_K):
            dst = pos_ref[0, 0, slot * ts + r]
            pltpu.make_async_copy(h_ref.at[r], xs_ref.at[dst], sem).start()
        return 0

    lax.fori_loop(0, ts, issue, 0)
    for slot in range(TOP_K):
        pltpu.make_async_copy(h_ref, xs_ref.at[pl.ds(0, ts)], sem).wait()

    @pl.when(pl.program_id(0) == pl.num_programs(0) - 1)
    def _():
        zero_ref[...] = jnp.zeros_like(zero_ref)
        occupied = fill_ref[2 * n_experts] * tile_rows
        spans = [(fill_ref[e], fill_ref[n_experts + e] - fill_ref[e]) for e in range(n_experts)]
        spans.append((occupied, xs_ref.shape[0] - occupied))

        def zero_copy(first, rows):
            return pltpu.make_async_copy(zero_ref.at[pl.ds(0, rows)],
                                         xs_ref.at[pl.ds(first, rows)], sem)

        for first, length in spans:
            _for_each_chunk(first, length, zero_ref.shape[0],
                            lambda row, rows: zero_copy(row, rows).start())
        for first, length in spans:
            _for_each_chunk(first, length, zero_ref.shape[0],
                            lambda row, rows: zero_copy(row, rows).wait())


def _scatter(h, pos, fill, n_rows):
    t = h.shape[0]
    ts = ROUTE_ROWS
    zero_rows = 256
    return pl.pallas_call(
        functools.partial(_scatter_kernel, n_experts=(fill.shape[0] - 1) // 2,
                          tile_rows=FFN_ROWS),
        grid=(t // ts,),
        in_specs=[
            pl.BlockSpec((1, 1, TOP_K * ts), lambda i: (i, 0, 0), memory_space=pltpu.SMEM),
            pl.BlockSpec(memory_space=pltpu.SMEM),
            pl.BlockSpec((ts,) + h.shape[1:], lambda i: (i, 0, 0)),
        ],
        out_specs=pl.BlockSpec(memory_space=pl.ANY),
        out_shape=jax.ShapeDtypeStruct((n_rows,) + h.shape[1:], F32),
        scratch_shapes=[pltpu.VMEM((zero_rows,) + h.shape[1:], F32), pltpu.SemaphoreType.DMA(())],
        compiler_params=pltpu.CompilerParams(
            dimension_semantics=("arbitrary",), vmem_limit_bytes=VMEM_LIMIT),
        name="moe_scatter",
    )(pos, fill, h)


def _experts_kernel(exp_ref, nact_ref, xs_ref, w1_ref, w3_ref, w2_ref, ys_ref, h_ref, acc_ref):
    i = pl.program_id(0)
    k = pl.program_id(1)

    @pl.when(i < nact_ref[0])
    def _():
        @pl.when(k == 0)
        def _():
            h_ref[...] = _from_token_major(xs_ref, h_ref.shape[1] // LANES).astype(BF16)

        h = h_ref[...]
        u = jax.nn.silu(_dot(h, w1_ref[0])) * _dot(h, w3_ref[0])
        part = _dot(u.astype(BF16), w2_ref[0])

        @pl.when(k == 0)
        def _():
            acc_ref[...] = part

        @pl.when(k > 0)
        def _():
            acc_ref[...] += part

        @pl.when(k == pl.num_programs(1) - 1)
        def _():
            _to_token_major(ys_ref, acc_ref[...])

    @pl.when((i >= nact_ref[0]) & (k == 0))
    def _():
        ys_ref[...] = jnp.zeros_like(ys_ref)


def _experts(xs, tile_expert, n_active, w1, w3, w2):
    d, f = w1.shape[1:]
    n_rows = xs.shape[0] * LANES // d
    tm, tf = FFN_ROWS, FFN_COLS
    n_k = f // tf
    n_tiles = n_rows // tm

    def k_of(i, k, nact):
        return jnp.where(i < nact[0], k, n_k - 1)

    return pl.pallas_call(
        _experts_kernel,
        grid_spec=pltpu.PrefetchScalarGridSpec(
            num_scalar_prefetch=2,
            grid=(n_tiles, n_k),
            in_specs=[
                pl.BlockSpec((tm * d // LANES, LANES), lambda i, k, exp, nact: (i, 0)),
                pl.BlockSpec((1, d, tf), lambda i, k, exp, nact: (exp[i], 0, k_of(i, k, nact))),
                pl.BlockSpec((1, d, tf), lambda i, k, exp, nact: (exp[i], 0, k_of(i, k, nact))),
                pl.BlockSpec((1, tf, d), lambda i, k, exp, nact: (exp[i], k_of(i, k, nact), 0)),
            ],
            out_specs=pl.BlockSpec((tm * d // LANES, LANES), lambda i, k, exp, nact: (i, 0)),
            scratch_shapes=[pltpu.VMEM((tm, d), BF16), pltpu.VMEM((tm, d), F32)],
        ),
        out_shape=jax.ShapeDtypeStruct(xs.shape, F32),
        compiler_params=pltpu.CompilerParams(
            dimension_semantics=("arbitrary", "arbitrary"), vmem_limit_bytes=VMEM_LIMIT),
        name="moe_experts",
    )(tile_expert, n_active, xs, w1, w3, w2)


def _combine_kernel(pos_ref, route_ref, x_ref, gpost_ref, gate_ref, ys_ref, o_ref, buf0_ref,
                    buf1_ref, sem):
    ts = x_ref.shape[0]
    n = ys_ref.shape[1]
    bufs = (buf0_ref, buf1_ref)

    def issue(r, _):
        dst = pl.multiple_of(r * n, n)
        for slot in range(TOP_K):
            src = pos_ref[0, 0, slot * ts + r]
            pltpu.make_async_copy(ys_ref.at[src], bufs[slot].at[pl.ds(dst, n)], sem).start()
        return 0

    lax.fori_loop(0, ts, issue, 0)
    for slot in range(TOP_K):
        pltpu.make_async_copy(bufs[slot], bufs[slot], sem).wait()

    w0 = ROUTE_LANES.index("weight0")
    route = route_ref[...]
    y = (route[:, w0:w0 + 1] * _from_token_major(buf0_ref, n)
         + route[:, w0 + 1:w0 + 2] * _from_token_major(buf1_ref, n))
    o_ref[...] = x_ref[...] + gate_ref[0] * _rms(y, gpost_ref[...])


def _combine(ys, pos, route, x, g_post, gate):
    t, d = x.shape
    bsz = gate.shape[0]
    ts = ROUTE_ROWS
    steps_per_b = t // bsz // ts
    return pl.pallas_call(
        _combine_kernel,
        grid=(t // ts,),
        in_specs=[
            pl.BlockSpec((1, 1, TOP_K * ts), lambda i: (i, 0, 0), memory_space=pltpu.SMEM),
            pl.BlockSpec((ts, LANES), lambda i: (i, 0)),
            pl.BlockSpec((ts, d), lambda i: (i, 0)),
            pl.BlockSpec((1, d), lambda i: (0, 0)),
            pl.BlockSpec((1, 1, d), lambda i: (i // steps_per_b, 0, 0)),
            pl.BlockSpec(memory_space=pl.ANY),
        ],
        out_specs=pl.BlockSpec((ts, d), lambda i: (i, 0)),
        out_shape=jax.ShapeDtypeStruct((t, d), F32),
        scratch_shapes=[pltpu.VMEM((ts * d // LANES, LANES), F32)] * TOP_K
        + [pltpu.SemaphoreType.DMA(())],
        compiler_params=pltpu.CompilerParams(
            dimension_semantics=("arbitrary",), vmem_limit_bytes=VMEM_LIMIT),
        name="moe_combine",
    )(pos, route, x, g_post.reshape(1, d), gate, ys)


def _moe(x, g_pre, scale, shift, w_router, w1, w3, w2, g_post, gate):
    bsz, s, d = x.shape
    t = bsz * s
    n_experts = w_router.shape[1]
    ts, tm = ROUTE_ROWS, FFN_ROWS
    h, route, counts = _route(x, g_pre, scale, shift, w_router)
    route = route.reshape(t, LANES)

    lane_of = ROUTE_LANES.index
    expert = route[:, lane_of("expert0"):lane_of("expert0") + TOP_K].astype(jnp.int32)
    rank = route[:, lane_of("rank0"):lane_of("rank0") + TOP_K].astype(jnp.int32)
    counts = counts[0, :n_experts].astype(jnp.int32)
    tiles_per = (counts + tm - 1) // tm
    tile_end = jnp.cumsum(tiles_per)
    first_row = (tile_end - tiles_per) * tm
    n_active = tile_end[-1]
    n_tiles = TOP_K * t // tm + n_experts
    is_expert = expert[..., None] == jnp.arange(n_experts, dtype=jnp.int32)
    pos = jnp.sum(jnp.where(is_expert, first_row, 0), axis=-1) + rank
    pos = pos.reshape(t // ts, ts, TOP_K).transpose(0, 2, 1).reshape(t // ts, 1, TOP_K * ts)
    fill = jnp.concatenate([first_row + counts, tile_end * tm, n_active[None]]).astype(jnp.int32)
    idx = jnp.minimum(jnp.arange(n_tiles, dtype=jnp.int32), n_active - 1)
    tile_expert = jnp.sum(idx[:, None] >= tile_end[None, :], axis=1).astype(jnp.int32)

    token_major = (-1, d // LANES, LANES)
    xs = _scatter(h.reshape(token_major), pos, fill, n_tiles * tm)
    ys = _experts(xs.reshape(-1, LANES), tile_expert, n_active.reshape(1), w1, w3, w2)
    out = _combine(ys.reshape(token_major), pos, route, x.reshape(t, d), g_post,
                   gate.reshape(bsz, 1, d))
    return out.reshape(bsz, s, d)


def kernel(x, c, w_mod, b_mod, g_mix_pre, g_mix_post, g_ffn_pre, g_ffn_post, w_in, w_proj_sb,
           w_proj_moba, w_out, rel_bias, w1_dense, w3_dense, w2_dense, w_router, w1_moe, w3_moe,
           w2_moe):
    depth = w_mod.shape[0]
    d = x.shape[-1]
    qkv = 3 * W_BRANCH
    col = jnp.arange(w_in.shape[-1])
    is_q = (col < W_BRANCH) | ((col >= qkv) & (col < qkv + W_BRANCH))
    col_scale = jnp.where(is_q, HEAD_DIM ** -0.5, 1.0).astype(F32)

    mod = _modulation(c, w_mod, b_mod)
    bias_tiles = _moba_bias_tiles(rel_bias)
    for l in range(depth):
        sh_mix, sc_mix, gt_mix, sh_ffn, sc_ffn, gt_ffn = jnp.split(mod[l], N_MOD, axis=-1)
        proj = _inproj(x, g_mix_pre[l], sc_mix, sh_mix, (w_in[l] * col_scale).astype(BF16))
        o_a = _stick_breaking_ks(proj, 0)
        o_b = _moba_ks(proj, qkv // LANES, bias_tiles)
        x = _merge(o_a, o_b, proj, 2 * qkv // d, x, w_proj_sb[l].astype(BF16),
                   w_proj_moba[l].astype(BF16), w_out[l].astype(BF16), g_mix_post[l], gt_mix)
        j = l // 2
        if l % 2 == 0:
            x = _ffn(x, g_ffn_pre[l], sc_ffn, sh_ffn, w1_dense[j].astype(BF16),
                     w3_dense[j].astype(BF16), w2_dense[j].astype(BF16), g_ffn_post[l], gt_ffn)
        else:
            x = _moe(x, g_ffn_pre[l], sc_ffn, sh_ffn, w_router[j], w1_moe[j].astype(BF16),
                     w3_moe[j].astype(BF16), w2_moe[j].astype(BF16), g_ffn_post[l], gt_ffn)
    return x
```

```python
import functools
import math

import jax
import jax.numpy as jnp
from jax import lax
from jax.experimental import pallas as pl
from jax.experimental.pallas import tpu as pltpu

F32 = jnp.float32
BF16 = jnp.bfloat16

EPS = 1e-6
NEG_INF = -1e30
HEAD_DIM = 64
N_HEADS = 8
W_BRANCH = N_HEADS * HEAD_DIM
LANES = 128
SUBLANES = 8
HEADS_PER_TILE = LANES // HEAD_DIM
N_PAIRS = N_HEADS // HEADS_PER_TILE
MOBA_BLOCK = 256
MOBA_TOPK = 3
REL_BUCKETS = 32
REL_MAX_DIST = 128
N_MOD = 6
TOP_K = 2
ATT_BLOCK = 256
VMEM_LIMIT = 56 * 1024 * 1024


def _dot(a, b):
    return jnp.dot(a, b, preferred_element_type=F32)


def _dot_nt(a, b):
    return lax.dot_general(a, b, (((1,), (1,)), ((), ())), preferred_element_type=F32)


def _split_bf16(x):
    hi = x.astype(BF16)
    lo = (x - hi.astype(F32)).astype(BF16)
    return hi, lo


def _rms(x, g):
    return x * lax.rsqrt(jnp.mean(x * x, axis=-1, keepdims=True) + EPS) * g


def _mod_kernel(c_ref, w_ref, b_ref, o_ref):
    c = c_ref[...]
    cond = c * jax.nn.sigmoid(c)
    o_ref[0] = jnp.dot(cond, w_ref[0], preferred_element_type=F32,
                       precision=lax.Precision.HIGHEST) + b_ref[0]


def _modulation(c, w_mod, b_mod):
    depth, d, n = w_mod.shape
    bsz = c.shape[0]
    tn = 1536
    return pl.pallas_call(
        _mod_kernel,
        grid=(depth, n // tn),
        in_specs=[
            pl.BlockSpec((bsz, d), lambda l, j: (0, 0)),
            pl.BlockSpec((1, d, tn), lambda l, j: (l, 0, j)),
            pl.BlockSpec((1, 1, tn), lambda l, j: (l, 0, j)),
        ],
        out_specs=pl.BlockSpec((1, bsz, tn), lambda l, j: (l, 0, j)),
        out_shape=jax.ShapeDtypeStruct((depth, bsz, n), F32),
        compiler_params=pltpu.CompilerParams(
            dimension_semantics=("arbitrary", "arbitrary"), vmem_limit_bytes=VMEM_LIMIT),
        name="modulation",
    )(c, w_mod, b_mod.reshape(depth, 1, n))


def _inproj_kernel(x_ref, g_ref, sc_ref, sh_ref, w_ref, o_ref, *, tn):
    h = _rms(x_ref[0], g_ref[...]) * (1.0 + sc_ref[0]) + sh_ref[0]
    hb = h.astype(BF16)
    for n in range(w_ref.shape[1] // tn):
        o_ref[0, :, n * tn:(n + 1) * tn] = _dot(hb, w_ref[:, n * tn:(n + 1) * tn]).astype(BF16)


def _inproj(x, g, scale, shift, w):
    bsz, s, d = x.shape
    n = w.shape[1]
    tm = 512
    return pl.pallas_call(
        functools.partial(_inproj_kernel, tn=1024),
        grid=(bsz, s // tm),
        in_specs=[
            pl.BlockSpec((1, tm, d), lambda b, t: (b, t, 0)),
            pl.BlockSpec((1, d), lambda b, t: (0, 0)),
            pl.BlockSpec((1, 1, d), lambda b, t: (b, 0, 0)),
            pl.BlockSpec((1, 1, d), lambda b, t: (b, 0, 0)),
            pl.BlockSpec((d, n), lambda b, t: (0, 0), pipeline_mode=pl.Buffered(1)),
        ],
        out_specs=pl.BlockSpec((1, tm, n), lambda b, t: (b, t, 0)),
        out_shape=jax.ShapeDtypeStruct((bsz, s, n), BF16),
        compiler_params=pltpu.CompilerParams(
            dimension_semantics=("arbitrary", "arbitrary"), vmem_limit_bytes=VMEM_LIMIT),
        name="inproj",
    )(x, g.reshape(1, d), scale.reshape(bsz, 1, d), shift.reshape(bsz, 1, d), w)


SOFTPLUS_LINEAR = 40.0


def _softplus(z):
    return jnp.maximum(z, jnp.log(1.0 + jnp.exp(jnp.minimum(z, SOFTPLUS_LINEAR))))


def _sb_kernel(q_ref, k_ref, v_ref, o_ref, qs_ref, acc_ref, carry_ref):
    tq = ATT_BLOCK
    n_blocks = k_ref.shape[1] // tq
    rows_per_blk = HEADS_PER_TILE * tq
    lane = lax.broadcasted_iota(jnp.int32, (1, LANES), 1)
    row = lax.broadcasted_iota(jnp.int32, (tq, tq), 0)
    col = lax.broadcasted_iota(jnp.int32, (tq, tq), 1)
    upper = jnp.where(row > col, 1.0, 0.0).astype(BF16)
    causal = jnp.concatenate([col < row] * HEADS_PER_TILE, axis=0)
    q = q_ref[0]
    for i in range(n_blocks):
        qi = q[i * tq:(i + 1) * tq]
        for h in range(HEADS_PER_TILE):
            row0 = (HEADS_PER_TILE * i + h) * tq
            qs_ref[row0:row0 + tq, :] = jnp.where(lane // HEAD_DIM == h, qi, jnp.zeros_like(qi))

    for j in reversed(range(n_blocks)):
        row0 = j * rows_per_blk
        far0 = row0 + rows_per_blk
        has_far = j + 1 < n_blocks
        z = _dot_nt(qs_ref[row0:, :], k_ref[0, j * tq:(j + 1) * tq, :])
        sp = _softplus(z)
        sp_diag = jnp.where(causal, sp[:rows_per_blk], 0.0)
        sp = jnp.concatenate([sp_diag, sp[rows_per_blk:]], axis=0) if has_far else sp_diag
        logit = z - sp - _dot(sp.astype(BF16), upper)
        att_diag = jnp.where(causal, jnp.exp(logit[:rows_per_blk]), 0.0)
        if has_far:
            att_far = jnp.exp(logit[rows_per_blk:] - carry_ref[far0:, :])
            att = jnp.concatenate([att_diag, att_far], axis=0)
        else:
            att = att_diag
        part = _dot(att.astype(BF16), v_ref[0, j * tq:(j + 1) * tq, :])
        rs = jnp.sum(sp, axis=1, keepdims=True)
        acc_ref[row0:far0, :] = part[:rows_per_blk]
        carry_ref[row0:far0, :] = rs[:rows_per_blk]
        if has_far:
            acc_ref[far0:, :] += part[rows_per_blk:]
            carry_ref[far0:, :] += rs[rows_per_blk:]

    for i in range(n_blocks):
        blk = acc_ref[i * rows_per_blk:(i + 1) * rows_per_blk, :]
        o_ref[0, i * tq:(i + 1) * tq, :] = jnp.where(lane < HEAD_DIM, blk[:tq], blk[tq:]).astype(BF16)


def _stick_breaking(proj, col0):
    bsz, s, _ = proj.shape
    assert s % ATT_BLOCK == 0
    rows = HEADS_PER_TILE * s
    return pl.pallas_call(
        _sb_kernel,
        grid=(bsz, N_PAIRS),
        in_specs=[
            pl.BlockSpec((1, s, LANES), lambda b, p: (b, 0, col0 + p)),
            pl.BlockSpec((1, s, LANES), lambda b, p: (b, 0, col0 + N_PAIRS + p)),
            pl.BlockSpec((1, s, LANES), lambda b, p: (b, 0, col0 + 2 * N_PAIRS + p)),
        ],
        out_specs=pl.BlockSpec((1, s, LANES), lambda b, p: (b, 0, p)),
        out_shape=jax.ShapeDtypeStruct((bsz, s, W_BRANCH), BF16),
        scratch_shapes=[pltpu.VMEM((rows, LANES), BF16),
                        pltpu.VMEM((rows, LANES), F32),
                        pltpu.VMEM((rows, 1), F32)],
        compiler_params=pltpu.CompilerParams(
            dimension_semantics=("arbitrary", "arbitrary"), vmem_limit_bytes=VMEM_LIMIT),
        name="stick_breaking",
    )(proj, proj, proj)


def _rel_bucket(dist):
    max_exact = REL_BUCKETS // 2
    n = jnp.maximum(dist, 0)
    nf = jnp.maximum(n, 1).astype(F32)
    large = max_exact + (jnp.log(nf / max_exact) / math.log(REL_MAX_DIST / max_exact)
                         * (REL_BUCKETS - max_exact)).astype(jnp.int32)
    large = jnp.minimum(large, REL_BUCKETS - 1)
    return jnp.where(n < max_exact, n, large)


def _moba_bias_tiles(rel_bias):
    assert MOBA_BLOCK >= REL_MAX_DIST
    t = jnp.arange(MOBA_BLOCK)[:, None]
    s = jnp.arange(MOBA_BLOCK)[None, :]
    d_own = t - s
    rel = rel_bias.astype(F32) - rel_bias[REL_BUCKETS - 1].astype(F32)

    def lookup(dist):
        onehot = jax.nn.one_hot(_rel_bucket(dist), REL_BUCKETS, dtype=F32)
        return jnp.einsum("tsk,kh->hts", onehot, rel, precision=lax.Precision.HIGHEST)

    own = jnp.where((d_own >= 0)[None], lookup(d_own), NEG_INF)
    return jnp.stack([own, lookup(d_own + MOBA_BLOCK)], axis=1)


def _moba_kernel(q_ref, k_ref, v_ref, bias_ref, o_ref,
                 ka_ref, va_ref, qa_ref, s_ref, m_ref, mrep_ref, accl_ref, *, n_blocks):
    tq = MOBA_BLOCK
    s_len = k_ref.shape[1]
    nb_pad = 16
    rows_per_blk = HEADS_PER_TILE * tq
    lane = lax.broadcasted_iota(jnp.int32, (1, LANES), 1)
    head_lanes = [lane // HEAD_DIM == h for h in range(HEADS_PER_TILE)]
    q = q_ref[0]
    k = k_ref[0]

    key_blk = lax.broadcasted_iota(jnp.int32, (s_len, LANES), 0) // tq
    ka_ref[:, :LANES] = k
    ka_ref[:, LANES:] = jnp.where(lane == key_blk, 1.0, 0.0).astype(BF16)
    va_ref[:, :LANES] = v_ref[0]
    va_ref[:, LANES:] = jnp.ones((s_len, LANES), BF16)

    r = lax.broadcasted_iota(jnp.int32, (nb_pad, s_len), 0)
    c = lax.broadcasted_iota(jnp.int32, (nb_pad, s_len), 1)
    pool = jnp.where(c // tq == r, 1.0 / tq, 0.0).astype(BF16)
    km_hi, km_lo = _split_bf16(_dot(pool, k))
    means = [jnp.where(hl, km, jnp.zeros_like(km)) for hl in head_lanes for km in (km_hi, km_lo)]
    gates = _dot_nt(jnp.concatenate(means, axis=0), q)
    own = c // tq
    masks = []
    for h in range(HEADS_PER_TILE):
        gate = (gates[2 * h * nb_pad:(2 * h + 1) * nb_pad]
                + gates[(2 * h + 1) * nb_pad:(2 * h + 2) * nb_pad])
        cnt = jnp.zeros((nb_pad, s_len), F32)
        for n in range(n_blocks):
            gn = gate[n:n + 1, :]
            beats = jnp.where(gn > gate, 1.0, jnp.where((gn == gate) & (n < r), 1.0, 0.0))
            cnt = cnt + jnp.where(n < own, beats, 0.0)
        keep = (r == own) | ((r < own) & (cnt < MOBA_TOPK))
        masks += [jnp.where(keep, 0.0, NEG_INF).astype(BF16),
                  jnp.zeros((LANES - nb_pad, s_len), BF16)]
    masks = jnp.concatenate(masks, axis=0)
    er = lax.broadcasted_iota(jnp.int32, (tq, tq), 0)
    ec = lax.broadcasted_iota(jnp.int32, (tq, tq), 1)
    eye = jnp.where(er == ec, 1.0, 0.0).astype(BF16)
    for i in range(n_blocks):
        qi = q[i * tq:(i + 1) * tq]
        mask_rows = _dot_nt(eye, masks[:, i * tq:(i + 1) * tq]).astype(BF16)
        for h in range(HEADS_PER_TILE):
            row0 = (HEADS_PER_TILE * i + h) * tq
            qa_ref[row0:row0 + tq, :LANES] = jnp.where(head_lanes[h], qi, jnp.zeros_like(qi))
            qa_ref[row0:row0 + tq, LANES:] = mask_rows[:, h * LANES:(h + 1) * LANES]

    bias = [jnp.concatenate([bias_ref[h, which] for h in range(HEADS_PER_TILE)], axis=0)
            for which in range(2)]
    offs = [0]
    for j in range(n_blocks):
        offs.append(offs[-1] + (n_blocks - j) * rows_per_blk)
    for j in range(n_blocks):
        row0 = j * rows_per_blk
        s = _dot_nt(qa_ref[row0:, :], ka_ref[j * tq:(j + 1) * tq, :])
        parts = [s[:rows_per_blk] + bias[0]]
        if j + 1 < n_blocks:
            parts.append(s[rows_per_blk:2 * rows_per_blk] + bias[1])
        if j + 2 < n_blocks:
            parts.append(s[2 * rows_per_blk:])
        s = jnp.concatenate(parts, axis=0)
        s_ref[offs[j]:offs[j + 1], :] = s
        m = jnp.max(s, axis=1, keepdims=True)
        if j == 0:
            m_ref[...] = m
        else:
            m_ref[row0:, :] = jnp.maximum(m_ref[row0:, :], m)
    mrep_ref[...] = jnp.broadcast_to(m_ref[...], mrep_ref.shape)
    for j in range(n_blocks):
        row0 = j * rows_per_blk
        m = mrep_ref[row0:, :]
        pr = jnp.exp(s_ref[offs[j]:offs[j + 1], :] - jnp.concatenate([m] * (tq // LANES), axis=1))
        ov = _dot(pr.astype(BF16), va_ref[j * tq:(j + 1) * tq, :])
        if j == 0:
            accl_ref[...] = ov
        else:
            accl_ref[row0:, :] += ov

    for i in range(n_blocks):
        blk = accl_ref[i * rows_per_blk:(i + 1) * rows_per_blk, :]
        out = blk[:, :LANES] / blk[:, LANES:]
        o_ref[0, i * tq:(i + 1) * tq, :] = jnp.where(lane < HEAD_DIM, out[:tq], out[tq:]).astype(BF16)


def _moba(proj, col0, bias_tiles):
    bsz, s, _ = proj.shape
    tq = MOBA_BLOCK
    assert s % tq == 0
    n_blocks = s // tq
    assert n_blocks <= 16
    rows = HEADS_PER_TILE * s
    return pl.pallas_call(
        functools.partial(_moba_kernel, n_blocks=n_blocks),
        grid=(bsz, N_PAIRS),
        in_specs=[
            pl.BlockSpec((1, s, LANES), lambda b, p: (b, 0, col0 + p)),
            pl.BlockSpec((1, s, LANES), lambda b, p: (b, 0, col0 + N_PAIRS + p)),
            pl.BlockSpec((1, s, LANES), lambda b, p: (b, 0, col0 + 2 * N_PAIRS + p)),
            pl.BlockSpec((HEADS_PER_TILE, 2, tq, tq), lambda b, p: (p, 0, 0, 0)),
        ],
        out_specs=pl.BlockSpec((1, s, LANES), lambda b, p: (b, 0, p)),
        out_shape=jax.ShapeDtypeStruct((bsz, s, W_BRANCH), BF16),
        scratch_shapes=[
            pltpu.VMEM((s, 2 * LANES), BF16),
            pltpu.VMEM((s, 2 * LANES), BF16),
            pltpu.VMEM((rows, 2 * LANES), BF16),
            pltpu.VMEM((rows * (n_blocks + 1) // 2, tq), F32),
            pltpu.VMEM((rows, 1), F32),
            pltpu.VMEM((rows, LANES), F32),
            pltpu.VMEM((rows, 2 * LANES), F32),
        ],
        compiler_params=pltpu.CompilerParams(
            dimension_semantics=("arbitrary", "arbitrary"), vmem_limit_bytes=VMEM_LIMIT),
        name="moba",
    )(proj, proj, proj, bias_tiles)


def _merge_kernel(oa_ref, ob_ref, ga_ref, gb_ref, x_ref, wa_ref, wb_ref, wo_ref, g_ref, gate_ref,
                  o_ref):
    pa = _dot(oa_ref[0], wa_ref[...])
    pb = _dot(ob_ref[0], wb_ref[...])
    merged = (jax.nn.sigmoid(ga_ref[0].astype(F32)) * pa
              + jax.nn.sigmoid(gb_ref[0].astype(F32)) * pb)
    y = _dot(merged.astype(BF16), wo_ref[...])
    o_ref[0] = x_ref[0] + gate_ref[0] * _rms(y, g_ref[...])


def _merge(o_a, o_b, proj, gate_col, x, w_a, w_b, w_o, g_post, gate):
    bsz, s, d = x.shape
    tm = 512
    const = lambda b, t: (0, 0)
    return pl.pallas_call(
        _merge_kernel,
        grid=(bsz, s // tm),
        in_specs=[
            pl.BlockSpec((1, tm, W_BRANCH), lambda b, t: (b, t, 0)),
            pl.BlockSpec((1, tm, W_BRANCH), lambda b, t: (b, t, 0)),
            pl.BlockSpec((1, tm, d), lambda b, t: (b, t, gate_col)),
            pl.BlockSpec((1, tm, d), lambda b, t: (b, t, gate_col + 1)),
            pl.BlockSpec((1, tm, d), lambda b, t: (b, t, 0)),
            pl.BlockSpec((W_BRANCH, d), const),
            pl.BlockSpec((W_BRANCH, d), const),
            pl.BlockSpec((d, d), const),
            pl.BlockSpec((1, d), const),
            pl.BlockSpec((1, 1, d), lambda b, t: (b, 0, 0)),
        ],
        out_specs=pl.BlockSpec((1, tm, d), lambda b, t: (b, t, 0)),
        out_shape=jax.ShapeDtypeStruct((bsz, s, d), F32),
        compiler_params=pltpu.CompilerParams(
            dimension_semantics=("arbitrary", "arbitrary"), vmem_limit_bytes=VMEM_LIMIT),
        name="merge",
    )(o_a, o_b, proj, proj, x, w_a, w_b, w_o, g_post.reshape(1, d), gate.reshape(bsz, 1, d))


FFN_ROWS = 1024
FFN_COLS = 512


def _ffn_kernel(x_ref, gpre_ref, sc_ref, sh_ref, w1_ref, w3_ref, w2_ref, gpost_ref, gate_ref,
                o_ref, h_ref, acc_ref):
    k = pl.program_id(2)

    @pl.when(k == 0)
    def _():
        h = _rms(x_ref[0], gpre_ref[...]) * (1.0 + sc_ref[0]) + sh_ref[0]
        h_ref[...] = h.astype(BF16)
        acc_ref[...] = jnp.zeros_like(acc_ref)

    h = h_ref[...]
    u = jax.nn.silu(_dot(h, w1_ref[...])) * _dot(h, w3_ref[...])
    acc_ref[...] += _dot(u.astype(BF16), w2_ref[...])

    @pl.when(k == pl.num_programs(2) - 1)
    def _():
        o_ref[0] = x_ref[0] + gate_ref[0] * _rms(acc_ref[...], gpost_ref[...])


def _ffn(x, g_pre, scale, shift, w1, w3, w2, g_post, gate):
    bsz, s, d = x.shape
    f = w1.shape[1]
    tm, tf = FFN_ROWS, FFN_COLS
    row = lambda b, t, k: (b, t, 0)
    per_b = lambda b, t, k: (b, 0, 0)
    const = lambda b, t, k: (0, 0)
    return pl.pallas_call(
        _ffn_kernel,
        grid=(bsz, s // tm, f // tf),
        in_specs=[
            pl.BlockSpec((1, tm, d), row),
            pl.BlockSpec((1, d), const),
            pl.BlockSpec((1, 1, d), per_b),
            pl.BlockSpec((1, 1, d), per_b),
            pl.BlockSpec((d, tf), lambda b, t, k: (0, k)),
            pl.BlockSpec((d, tf), lambda b, t, k: (0, k)),
            pl.BlockSpec((tf, d), lambda b, t, k: (k, 0)),
            pl.BlockSpec((1, d), const),
            pl.BlockSpec((1, 1, d), per_b),
        ],
        out_specs=pl.BlockSpec((1, tm, d), row),
        out_shape=jax.ShapeDtypeStruct((bsz, s, d), F32),
        scratch_shapes=[pltpu.VMEM((tm, d), BF16), pltpu.VMEM((tm, d), F32)],
        compiler_params=pltpu.CompilerParams(
            dimension_semantics=("arbitrary",) * 3, vmem_limit_bytes=VMEM_LIMIT),
        name="ffn_dense",
    )(x, g_pre.reshape(1, d), scale.reshape(bsz, 1, d), shift.reshape(bsz, 1, d), w1, w3, w2,
      g_post.reshape(1, d), gate.reshape(bsz, 1, d))


ROUTE_ROWS = 512


def _to_token_major(ref, val):
    rows, n = val.shape[0], val.shape[1] // LANES
    for s in range(n):
        ref[pl.ds(s, rows, stride=n), :] = val[:, s * LANES:(s + 1) * LANES]


def _from_token_major(ref, n):
    rows = ref.shape[0] // n
    return jnp.concatenate([ref[pl.ds(s, rows, stride=n), :] for s in range(n)], axis=1)
ROUTE_LANES = ("expert0", "expert1", "weight0", "weight1", "rank0", "rank1")


def _route_kernel(x_ref, gpre_ref, sc_ref, sh_ref, wr_ref, h_ref, route_ref, cnt_ref, *, n_experts):
    first_step = (pl.program_id(0) == 0) & (pl.program_id(1) == 0)

    @pl.when(first_step)
    def _():
        cnt_ref[...] = jnp.zeros_like(cnt_ref)

    h = _rms(x_ref[0], gpre_ref[...]) * (1.0 + sc_ref[0]) + sh_ref[0]
    _to_token_major(h_ref, h)
    logits = jnp.dot(h, wr_ref[...], preferred_element_type=F32, precision=lax.Precision.HIGHEST)
    ts = logits.shape[0]
    lane = lax.broadcasted_iota(jnp.int32, logits.shape, 1)
    lg = jnp.where(lane < n_experts, logits, -jnp.inf)
    m1 = jnp.max(lg, axis=1, keepdims=True)
    i1 = jnp.min(jnp.where(lg == m1, lane, LANES), axis=1, keepdims=True)
    first = lane == i1
    lg2 = jnp.where(first, -jnp.inf, lg)
    m2 = jnp.max(lg2, axis=1, keepdims=True)
    i2 = jnp.min(jnp.where(lg2 == m2, lane, LANES), axis=1, keepdims=True)
    second = lane == i2
    e2 = jnp.exp(m2 - m1)
    w_first = 1.0 / (1.0 + e2)
    w_second = e2 * w_first

    onehot = jnp.where(first | second, 1.0, 0.0)
    r = lax.broadcasted_iota(jnp.int32, (ts, ts), 0)
    c = lax.broadcasted_iota(jnp.int32, (ts, ts), 1)
    lower = jnp.where(c < r, 1.0, 0.0).astype(BF16)
    prefix = _dot(lower, onehot.astype(BF16)) + cnt_ref[...]
    rank_first = jnp.sum(jnp.where(first, prefix, 0.0), axis=1, keepdims=True)
    rank_second = jnp.sum(jnp.where(second, prefix, 0.0), axis=1, keepdims=True)
    cnt_ref[...] += jnp.sum(onehot, axis=0, keepdims=True)

    cols = (i1.astype(F32), i2.astype(F32), w_first, w_second, rank_first, rank_second)
    route = jnp.zeros(logits.shape, F32)
    for n, val in enumerate(cols):
        route = jnp.where(lane == n, val, route)
    route_ref[0] = route


def _route(x, g_pre, scale, shift, w_router):
    bsz, s, d = x.shape
    n_experts = w_router.shape[1]
    ts = ROUTE_ROWS
    row = lambda b, t: (b, t, 0)
    per_b = lambda b, t: (b, 0, 0)
    const = lambda b, t: (0, 0)
    return pl.pallas_call(
        functools.partial(_route_kernel, n_experts=n_experts),
        grid=(bsz, s // ts),
        in_specs=[
            pl.BlockSpec((1, ts, d), row),
            pl.BlockSpec((1, d), const),
            pl.BlockSpec((1, 1, d), per_b),
            pl.BlockSpec((1, 1, d), per_b),
            pl.BlockSpec((d, LANES), const),
        ],
        out_specs=[
            pl.BlockSpec((ts * d // LANES, LANES), lambda b, t: (b * (s // ts) + t, 0)),
            pl.BlockSpec((1, ts, LANES), row),
            pl.BlockSpec((1, LANES), const),
        ],
        out_shape=[
            jax.ShapeDtypeStruct((bsz * s * d // LANES, LANES), F32),
            jax.ShapeDtypeStruct((bsz, s, LANES), F32),
            jax.ShapeDtypeStruct((1, LANES), F32),
        ],
        compiler_params=pltpu.CompilerParams(
            dimension_semantics=("arbitrary", "arbitrary"), vmem_limit_bytes=VMEM_LIMIT),
        name="moe_route",
    )(x, g_pre.reshape(1, d), scale.reshape(bsz, 1, d), shift.reshape(bsz, 1, d),
      jnp.pad(w_router, ((0, 0), (0, LANES - n_experts))))


def _for_each_chunk(start, length, chunk, fn):
    whole = length // chunk

    def loop(c, _):
        fn(start + c * chunk, chunk)
        return 0

    lax.fori_loop(0, whole, loop, 0)
    cur = start + whole * chunk
    size = chunk // 2
    while size >= 1:
        bit = length & size

        @pl.when(bit != 0)
        def _(cur=cur, size=size):
            fn(cur, size)

        cur = cur + bit
        size //= 2


def _scatter_kernel(pos_ref, fill_ref, h_ref, xs_ref, zero_ref, sem, *, n_experts, tile_rows):
    ts = h_ref.shape[0]

    def issue(r, _):
        for slot in range(TOP_K):
            dst = pos_ref[0, 0, slot * ts + r]
            pltpu.make_async_copy(h_ref.at[r], xs_ref.at[dst], sem).start()
        return 0

    lax.fori_loop(0, ts, issue, 0)
    for slot in range(TOP_K):
        pltpu.make_async_copy(h_ref, xs_ref.at[pl.ds(0, ts)], sem).wait()

    @pl.when(pl.program_id(0) == pl.num_programs(0) - 1)
    def _():
        zero_ref[...] = jnp.zeros_like(zero_ref)
        occupied = fill_ref[2 * n_experts] * tile_rows
        spans = [(fill_ref[e], fill_ref[n_experts + e] - fill_ref[e]) for e in range(n_experts)]
        spans.append((occupied, xs_ref.shape[0] - occupied))

        def zero_copy(first, rows):
            return pltpu.make_async_copy(zero_ref.at[pl.ds(0, rows)],
                                         xs_ref.at[pl.ds(first, rows)], sem)

        for first, length in spans:
            _for_each_chunk(first, length, zero_ref.shape[0],
                            lambda row, rows: zero_copy(row, rows).start())
        for first, length in spans:
            _for_each_chunk(first, length, zero_ref.shape[0],
                            lambda row, rows: zero_copy(row, rows).wait())


def _scatter(h, pos, fill, n_rows):
    t = h.shape[0]
    ts = ROUTE_ROWS
    zero_rows = 256
    return pl.pallas_call(
        functools.partial(_scatter_kernel, n_experts=(fill.shape[0] - 1) // 2,
                          tile_rows=FFN_ROWS),
        grid=(t // ts,),
        in_specs=[
            pl.BlockSpec((1, 1, TOP_K * ts), lambda i: (i, 0, 0), memory_space=pltpu.SMEM),
            pl.BlockSpec(memory_space=pltpu.SMEM),
            pl.BlockSpec((ts,) + h.shape[1:], lambda i: (i, 0, 0)),
        ],
        out_specs=pl.BlockSpec(memory_space=pl.ANY),
        out_shape=jax.ShapeDtypeStruct((n_rows,) + h.shape[1:], F32),
        scratch_shapes=[pltpu.VMEM((zero_rows,) + h.shape[1:], F32), pltpu.SemaphoreType.DMA(())],
        compiler_params=pltpu.CompilerParams(
            dimension_semantics=("arbitrary",), vmem_limit_bytes=VMEM_LIMIT),
        name="moe_scatter",
    )(pos, fill, h)


def _experts_kernel(exp_ref, nact_ref, xs_ref, w1_ref, w3_ref, w2_ref, ys_ref, h_ref, acc_ref):
    i = pl.program_id(0)
    k = pl.program_id(1)

    @pl.when(i < nact_ref[0])
    def _():
        @pl.when(k == 0)
        def _():
            h_ref[...] = _from_token_major(xs_ref, h_ref.shape[1] // LANES).astype(BF16)

        h = h_ref[...]
        u = jax.nn.silu(_dot(h, w1_ref[0])) * _dot(h, w3_ref[0])
        part = _dot(u.astype(BF16), w2_ref[0])

        @pl.when(k == 0)
        def _():
            acc_ref[...] = part

        @pl.when(k > 0)
        def _():
            acc_ref[...] += part

        @pl.when(k == pl.num_programs(1) - 1)
        def _():
            _to_token_major(ys_ref, acc_ref[...])

    @pl.when((i >= nact_ref[0]) & (k == 0))
    def _():
        ys_ref[...] = jnp.zeros_like(ys_ref)


def _experts(xs, tile_expert, n_active, w1, w3, w2):
    d, f = w1.shape[1:]
    n_rows = xs.shape[0] * LANES // d
    tm, tf = FFN_ROWS, FFN_COLS
    n_k = f // tf
    n_tiles = n_rows // tm

    def k_of(i, k, nact):
        return jnp.where(i < nact[0], k, n_k - 1)

    return pl.pallas_call(
        _experts_kernel,
        grid_spec=pltpu.PrefetchScalarGridSpec(
            num_scalar_prefetch=2,
            grid=(n_tiles, n_k),
            in_specs=[
                pl.BlockSpec((tm * d // LANES, LANES), lambda i, k, exp, nact: (i, 0)),
                pl.BlockSpec((1, d, tf), lambda i, k, exp, nact: (exp[i], 0, k_of(i, k, nact))),
                pl.BlockSpec((1, d, tf), lambda i, k, exp, nact: (exp[i], 0, k_of(i, k, nact))),
                pl.BlockSpec((1, tf, d), lambda i, k, exp, nact: (exp[i], k_of(i, k, nact), 0)),
            ],
            out_specs=pl.BlockSpec((tm * d // LANES, LANES), lambda i, k, exp, nact: (i, 0)),
            scratch_shapes=[pltpu.VMEM((tm, d), BF16), pltpu.VMEM((tm, d), F32)],
        ),
        out_shape=jax.ShapeDtypeStruct(xs.shape, F32),
        compiler_params=pltpu.CompilerParams(
            dimension_semantics=("arbitrary", "arbitrary"), vmem_limit_bytes=VMEM_LIMIT),
        name="moe_experts",
    )(tile_expert, n_active, xs, w1, w3, w2)


def _combine_kernel(pos_ref, route_ref, x_ref, gpost_ref, gate_ref, ys_ref, o_ref, buf0_ref,
                    buf1_ref, sem):
    ts = x_ref.shape[0]
    n = ys_ref.shape[1]
    bufs = (buf0_ref, buf1_ref)

    def issue(r, _):
        dst = pl.multiple_of(r * n, n)
        for slot in range(TOP_K):
            src = pos_ref[0, 0, slot * ts + r]
            pltpu.make_async_copy(ys_ref.at[src], bufs[slot].at[pl.ds(dst, n)],
                                  sem).start(priority=slot)
        return 0

    lax.fori_loop(0, ts, issue, 0)
    for slot in range(TOP_K):
        pltpu.make_async_copy(bufs[slot], bufs[slot], sem).wait()

    w0 = ROUTE_LANES.index("weight0")
    route = route_ref[...]
    y = (route[:, w0:w0 + 1] * _from_token_major(buf0_ref, n)
         + route[:, w0 + 1:w0 + 2] * _from_token_major(buf1_ref, n))
    o_ref[...] = x_ref[...] + gate_ref[0] * _rms(y, gpost_ref[...])


def _combine(ys, pos, route, x, g_post, gate):
    t, d = x.shape
    bsz = gate.shape[0]
    ts = ROUTE_ROWS
    steps_per_b = t // bsz // ts
    return pl.pallas_call(
        _combine_kernel,
        grid=(t // ts,),
        in_specs=[
            pl.BlockSpec((1, 1, TOP_K * ts), lambda i: (i, 0, 0), memory_space=pltpu.SMEM),
            pl.BlockSpec((ts, LANES), lambda i: (i, 0)),
            pl.BlockSpec((ts, d), lambda i: (i, 0)),
            pl.BlockSpec((1, d), lambda i: (0, 0)),
            pl.BlockSpec((1, 1, d), lambda i: (i // steps_per_b, 0, 0)),
            pl.BlockSpec(memory_space=pl.ANY),
        ],
        out_specs=pl.BlockSpec((ts, d), lambda i: (i, 0)),
        out_shape=jax.ShapeDtypeStruct((t, d), F32),
        scratch_shapes=[pltpu.VMEM((ts * d // LANES, LANES), F32)] * TOP_K
        + [pltpu.SemaphoreType.DMA(())],
        compiler_params=pltpu.CompilerParams(
            dimension_semantics=("arbitrary",), vmem_limit_bytes=VMEM_LIMIT),
        name="moe_combine",
    )(pos, route, x, g_post.reshape(1, d), gate, ys)


def _moe(x, g_pre, scale, shift, w_router, w1, w3, w2, g_post, gate):
    bsz, s, d = x.shape
    t = bsz * s
    n_experts = w_router.shape[1]
    ts, tm = ROUTE_ROWS, FFN_ROWS
    h, route, counts = _route(x, g_pre, scale, shift, w_router)
    route = route.reshape(t, LANES)

    lane_of = ROUTE_LANES.index
    expert = route[:, lane_of("expert0"):lane_of("expert0") + TOP_K].astype(jnp.int32)
    rank = route[:, lane_of("rank0"):lane_of("rank0") + TOP_K].astype(jnp.int32)
    counts = counts[0, :n_experts].astype(jnp.int32)
    tiles_per = (counts + tm - 1) // tm
    tile_end = jnp.cumsum(tiles_per)
    first_row = (tile_end - tiles_per) * tm
    n_active = tile_end[-1]
    n_tiles = TOP_K * t // tm + n_experts
    is_expert = expert[..., None] == jnp.arange(n_experts, dtype=jnp.int32)
    pos = jnp.sum(jnp.where(is_expert, first_row, 0), axis=-1) + rank
    pos = pos.reshape(t // ts, ts, TOP_K).transpose(0, 2, 1).reshape(t // ts, 1, TOP_K * ts)
    fill = jnp.concatenate([first_row + counts, tile_end * tm, n_active[None]]).astype(jnp.int32)
    idx = jnp.minimum(jnp.arange(n_tiles, dtype=jnp.int32), n_active - 1)
    tile_expert = jnp.sum(idx[:, None] >= tile_end[None, :], axis=1).astype(jnp.int32)

    token_major = (-1, d // LANES, LANES)
    xs = _scatter(h.reshape(token_major), pos, fill, n_tiles * tm)
    ys = _experts(xs.reshape(-1, LANES), tile_expert, n_active.reshape(1), w1, w3, w2)
    out = _combine(ys.reshape(token_major), pos, route, x.reshape(t, d), g_post,
                   gate.reshape(bsz, 1, d))
    return out.reshape(bsz, s, d)


def kernel(x, c, w_mod, b_mod, g_mix_pre, g_mix_post, g_ffn_pre, g_ffn_post, w_in, w_proj_sb,
           w_proj_moba, w_out, rel_bias, w1_dense, w3_dense, w2_dense, w_router, w1_moe, w3_moe,
           w2_moe):
    depth = w_mod.shape[0]
    d = x.shape[-1]
    qkv = 3 * W_BRANCH
    col = jnp.arange(w_in.shape[-1])
    is_q = (col < W_BRANCH) | ((col >= qkv) & (col < qkv + W_BRANCH))
    col_scale = jnp.where(is_q, HEAD_DIM ** -0.5, 1.0).astype(F32)

    mod = _modulation(c, w_mod, b_mod)
    bias_tiles = _moba_bias_tiles(rel_bias)
    for l in range(depth):
        sh_mix, sc_mix, gt_mix, sh_ffn, sc_ffn, gt_ffn = jnp.split(mod[l], N_MOD, axis=-1)
        proj = _inproj(x, g_mix_pre[l], sc_mix, sh_mix, (w_in[l] * col_scale).astype(BF16))
        o_a = _stick_breaking(proj, 0)
        o_b = _moba(proj, qkv // LANES, bias_tiles)
        x = _merge(o_a, o_b, proj, 2 * qkv // d, x, w_proj_sb[l].astype(BF16),
                   w_proj_moba[l].astype(BF16), w_out[l].astype(BF16), g_mix_post[l], gt_mix)
        j = l // 2
        if l % 2 == 0:
            x = _ffn(x, g_ffn_pre[l], sc_ffn, sh_ffn, w1_dense[j].astype(BF16),
                     w3_dense[j].astype(BF16), w2_dense[j].astype(BF16), g_ffn_post[l], gt_ffn)
        else:
            x = _moe(x, g_ffn_pre[l], sc_ffn, sh_ffn, w_router[j], w1_moe[j].astype(BF16),
                     w3_moe[j].astype(BF16), w2_moe[j].astype(BF16), g_ffn_post[l], gt_ffn)
    return x
```

```python
import functools
import math

import jax
import jax.numpy as jnp
from jax import lax
from jax.experimental import pallas as pl
from jax.experimental.pallas import tpu as pltpu

F32 = jnp.float32
BF16 = jnp.bfloat16

EPS = 1e-6
NEG_INF = -1e30
HEAD_DIM = 64
N_HEADS = 8
W_BRANCH = N_HEADS * HEAD_DIM
LANES = 128
SUBLANES = 8
HEADS_PER_TILE = LANES // HEAD_DIM
N_PAIRS = N_HEADS // HEADS_PER_TILE
MOBA_BLOCK = 256
MOBA_TOPK = 3
REL_BUCKETS = 32
REL_MAX_DIST = 128
N_MOD = 6
TOP_K = 2
ATT_BLOCK = 256
VMEM_LIMIT = 56 * 1024 * 1024


def _dot(a, b):
    return jnp.dot(a, b, preferred_element_type=F32)


def _dot_nt(a, b):
    return lax.dot_general(a, b, (((1,), (1,)), ((), ())), preferred_element_type=F32)


def _split_bf16(x):
    hi = x.astype(BF16)
    lo = (x - hi.astype(F32)).astype(BF16)
    return hi, lo


def _rms(x, g):
    return x * lax.rsqrt(jnp.mean(x * x, axis=-1, keepdims=True) + EPS) * g


def _mod_kernel(c_ref, w_ref, b_ref, o_ref):
    c = c_ref[...]
    cond = c * jax.nn.sigmoid(c)
    o_ref[0] = jnp.dot(cond, w_ref[0], preferred_element_type=F32,
                       precision=lax.Precision.HIGHEST) + b_ref[0]


def _modulation(c, w_mod, b_mod):
    depth, d, n = w_mod.shape
    bsz = c.shape[0]
    tn = 1536
    return pl.pallas_call(
        _mod_kernel,
        grid=(depth, n // tn),
        in_specs=[
            pl.BlockSpec((bsz, d), lambda l, j: (0, 0)),
            pl.BlockSpec((1, d, tn), lambda l, j: (l, 0, j)),
            pl.BlockSpec((1, 1, tn), lambda l, j: (l, 0, j)),
        ],
        out_specs=pl.BlockSpec((1, bsz, tn), lambda l, j: (l, 0, j)),
        out_shape=jax.ShapeDtypeStruct((depth, bsz, n), F32),
        compiler_params=pltpu.CompilerParams(
            dimension_semantics=("arbitrary", "arbitrary"), vmem_limit_bytes=VMEM_LIMIT),
        name="modulation",
    )(c, w_mod, b_mod.reshape(depth, 1, n))


def _inproj_kernel(x_ref, g_ref, sc_ref, sh_ref, w_ref, o_ref, *, tn):
    h = _rms(x_ref[0], g_ref[...]) * (1.0 + sc_ref[0]) + sh_ref[0]
    hb = h.astype(BF16)
    for n in range(w_ref.shape[1] // tn):
        o_ref[0, :, n * tn:(n + 1) * tn] = _dot(hb, w_ref[:, n * tn:(n + 1) * tn]).astype(BF16)


def _inproj(x, g, scale, shift, w):
    bsz, s, d = x.shape
    n = w.shape[1]
    tm = 512
    return pl.pallas_call(
        functools.partial(_inproj_kernel, tn=1024),
        grid=(bsz, s // tm),
        in_specs=[
            pl.BlockSpec((1, tm, d), lambda b, t: (b, t, 0)),
            pl.BlockSpec((1, d), lambda b, t: (0, 0)),
            pl.BlockSpec((1, 1, d), lambda b, t: (b, 0, 0)),
            pl.BlockSpec((1, 1, d), lambda b, t: (b, 0, 0)),
            pl.BlockSpec((d, n), lambda b, t: (0, 0), pipeline_mode=pl.Buffered(1)),
        ],
        out_specs=pl.BlockSpec((1, tm, n), lambda b, t: (b, t, 0)),
        out_shape=jax.ShapeDtypeStruct((bsz, s, n), BF16),
        compiler_params=pltpu.CompilerParams(
            dimension_semantics=("arbitrary", "arbitrary"), vmem_limit_bytes=VMEM_LIMIT),
        name="inproj",
    )(x, g.reshape(1, d), scale.reshape(bsz, 1, d), shift.reshape(bsz, 1, d), w)


SOFTPLUS_LINEAR = 40.0


def _softplus(z):
    return jnp.maximum(z, jnp.log(1.0 + jnp.exp(jnp.minimum(z, SOFTPLUS_LINEAR))))


def _sb_kernel(q_ref, k_ref, v_ref, o_ref, qs_ref, acc_ref, carry_ref):
    tq = ATT_BLOCK
    n_blocks = k_ref.shape[1] // tq
    rows_per_blk = HEADS_PER_TILE * tq
    lane = lax.broadcasted_iota(jnp.int32, (1, LANES), 1)
    row = lax.broadcasted_iota(jnp.int32, (tq, tq), 0)
    col = lax.broadcasted_iota(jnp.int32, (tq, tq), 1)
    upper = jnp.where(row > col, 1.0, 0.0).astype(BF16)
    causal = jnp.concatenate([col < row] * HEADS_PER_TILE, axis=0)
    q = q_ref[0]
    for i in range(n_blocks):
        qi = q[i * tq:(i + 1) * tq]
        for h in range(HEADS_PER_TILE):
            row0 = (HEADS_PER_TILE * i + h) * tq
            qs_ref[row0:row0 + tq, :] = jnp.where(lane // HEAD_DIM == h, qi, jnp.zeros_like(qi))

    for j in reversed(range(n_blocks)):
        row0 = j * rows_per_blk
        far0 = row0 + rows_per_blk
        has_far = j + 1 < n_blocks
        z = _dot_nt(qs_ref[row0:, :], k_ref[0, j * tq:(j + 1) * tq, :])
        sp = _softplus(z)
        sp_diag = jnp.where(causal, sp[:rows_per_blk], 0.0)
        sp = jnp.concatenate([sp_diag, sp[rows_per_blk:]], axis=0) if has_far else sp_diag
        logit = z - sp - _dot(sp.astype(BF16), upper)
        att_diag = jnp.where(causal, jnp.exp(logit[:rows_per_blk]), 0.0)
        if has_far:
            att_far = jnp.exp(logit[rows_per_blk:] - carry_ref[far0:, :])
            att = jnp.concatenate([att_diag, att_far], axis=0)
        else:
            att = att_diag
        part = _dot(att.astype(BF16), v_ref[0, j * tq:(j + 1) * tq, :])
        rs = jnp.sum(sp, axis=1, keepdims=True)
        acc_ref[row0:far0, :] = part[:rows_per_blk]
        carry_ref[row0:far0, :] = rs[:rows_per_blk]
        if has_far:
            acc_ref[far0:, :] += part[rows_per_blk:]
            carry_ref[far0:, :] += rs[rows_per_blk:]

    for i in range(n_blocks):
        blk = acc_ref[i * rows_per_blk:(i + 1) * rows_per_blk, :]
        o_ref[0, i * tq:(i + 1) * tq, :] = jnp.where(lane < HEAD_DIM, blk[:tq], blk[tq:]).astype(BF16)


def _stick_breaking(proj, col0):
    bsz, s, _ = proj.shape
    assert s % ATT_BLOCK == 0
    rows = HEADS_PER_TILE * s
    return pl.pallas_call(
        _sb_kernel,
        grid=(bsz, N_PAIRS),
        in_specs=[
            pl.BlockSpec((1, s, LANES), lambda b, p: (b, 0, col0 + p)),
            pl.BlockSpec((1, s, LANES), lambda b, p: (b, 0, col0 + N_PAIRS + p)),
            pl.BlockSpec((1, s, LANES), lambda b, p: (b, 0, col0 + 2 * N_PAIRS + p)),
        ],
        out_specs=pl.BlockSpec((1, s, LANES), lambda b, p: (b, 0, p)),
        out_shape=jax.ShapeDtypeStruct((bsz, s, W_BRANCH), BF16),
        scratch_shapes=[pltpu.VMEM((rows, LANES), BF16),
                        pltpu.VMEM((rows, LANES), F32),
                        pltpu.VMEM((rows, 1), F32)],
        compiler_params=pltpu.CompilerParams(
            dimension_semantics=("arbitrary", "arbitrary"), vmem_limit_bytes=VMEM_LIMIT),
        name="stick_breaking",
    )(proj, proj, proj)


def _rel_bucket(dist):
    max_exact = REL_BUCKETS // 2
    n = jnp.maximum(dist, 0)
    nf = jnp.maximum(n, 1).astype(F32)
    large = max_exact + (jnp.log(nf / max_exact) / math.log(REL_MAX_DIST / max_exact)
                         * (REL_BUCKETS - max_exact)).astype(jnp.int32)
    large = jnp.minimum(large, REL_BUCKETS - 1)
    return jnp.where(n < max_exact, n, large)


def _moba_bias_tiles(rel_bias):
    assert MOBA_BLOCK >= REL_MAX_DIST
    t = jnp.arange(MOBA_BLOCK)[:, None]
    s = jnp.arange(MOBA_BLOCK)[None, :]
    d_own = t - s
    rel = rel_bias.astype(F32) - rel_bias[REL_BUCKETS - 1].astype(F32)

    def lookup(dist):
        onehot = jax.nn.one_hot(_rel_bucket(dist), REL_BUCKETS, dtype=F32)
        return jnp.einsum("tsk,kh->hts", onehot, rel, precision=lax.Precision.HIGHEST)

    own = jnp.where((d_own >= 0)[None], lookup(d_own), NEG_INF)
    return jnp.stack([own, lookup(d_own + MOBA_BLOCK)], axis=1)


def _moba_kernel(q_ref, k_ref, v_ref, bias_ref, o_ref,
                 ka_ref, va_ref, qa_ref, s_ref, m_ref, mrep_ref, accl_ref, *, n_blocks):
    tq = MOBA_BLOCK
    s_len = k_ref.shape[1]
    nb_pad = 16
    rows_per_blk = HEADS_PER_TILE * tq
    lane = lax.broadcasted_iota(jnp.int32, (1, LANES), 1)
    head_lanes = [lane // HEAD_DIM == h for h in range(HEADS_PER_TILE)]
    q = q_ref[0]
    k = k_ref[0]

    key_blk = lax.broadcasted_iota(jnp.int32, (s_len, LANES), 0) // tq
    ka_ref[:, :LANES] = k
    ka_ref[:, LANES:] = jnp.where(lane == key_blk, 1.0, 0.0).astype(BF16)
    va_ref[:, :LANES] = v_ref[0]
    va_ref[:, LANES:] = jnp.ones((s_len, LANES), BF16)

    r = lax.broadcasted_iota(jnp.int32, (nb_pad, s_len), 0)
    c = lax.broadcasted_iota(jnp.int32, (nb_pad, s_len), 1)
    pool = jnp.where(c // tq == r, 1.0 / tq, 0.0).astype(BF16)
    km_hi, km_lo = _split_bf16(_dot(pool, k))
    means = [jnp.where(hl, km, jnp.zeros_like(km)) for hl in head_lanes for km in (km_hi, km_lo)]
    gates = _dot_nt(jnp.concatenate(means, axis=0), q)
    own = c // tq
    masks = []
    for h in range(HEADS_PER_TILE):
        gate = (gates[2 * h * nb_pad:(2 * h + 1) * nb_pad]
                + gates[(2 * h + 1) * nb_pad:(2 * h + 2) * nb_pad])
        cnt = jnp.zeros((nb_pad, s_len), F32)
        for n in range(n_blocks):
            gn = gate[n:n + 1, :]
            beats = jnp.where(gn > gate, 1.0, jnp.where((gn == gate) & (n < r), 1.0, 0.0))
            cnt = cnt + jnp.where(n < own, beats, 0.0)
        keep = (r == own) | ((r < own) & (cnt < MOBA_TOPK))
        masks += [jnp.where(keep, 0.0, NEG_INF).astype(BF16),
                  jnp.zeros((LANES - nb_pad, s_len), BF16)]
    masks = jnp.concatenate(masks, axis=0)
    er = lax.broadcasted_iota(jnp.int32, (tq, tq), 0)
    ec = lax.broadcasted_iota(jnp.int32, (tq, tq), 1)
    eye = jnp.where(er == ec, 1.0, 0.0).astype(BF16)
    for i in range(n_blocks):
        qi = q[i * tq:(i + 1) * tq]
        mask_rows = _dot_nt(eye, masks[:, i * tq:(i + 1) * tq]).astype(BF16)
        for h in range(HEADS_PER_TILE):
            row0 = (HEADS_PER_TILE * i + h) * tq
            qa_ref[row0:row0 + tq, :LANES] = jnp.where(head_lanes[h], qi, jnp.zeros_like(qi))
            qa_ref[row0:row0 + tq, LANES:] = mask_rows[:, h * LANES:(h + 1) * LANES]

    bias = [jnp.concatenate([bias_ref[h, which] for h in range(HEADS_PER_TILE)], axis=0)
            for which in range(2)]
    offs = [0]
    for j in range(n_blocks):
        offs.append(offs[-1] + (n_blocks - j) * rows_per_blk)
    for j in range(n_blocks):
        row0 = j * rows_per_blk
        s = _dot_nt(qa_ref[row0:, :], ka_ref[j * tq:(j + 1) * tq, :])
        parts = [s[:rows_per_blk] + bias[0]]
        if j + 1 < n_blocks:
            parts.append(s[rows_per_blk:2 * rows_per_blk] + bias[1])
        if j + 2 < n_blocks:
            parts.append(s[2 * rows_per_blk:])
        s = jnp.concatenate(parts, axis=0)
        s_ref[offs[j]:offs[j + 1], :] = s
        m = jnp.max(s, axis=1, keepdims=True)
        if j == 0:
            m_ref[...] = m
        else:
            m_ref[row0:, :] = jnp.maximum(m_ref[row0:, :], m)
    mrep_ref[...] = jnp.broadcast_to(m_ref[...], mrep_ref.shape)
    for j in range(n_blocks):
        row0 = j * rows_per_blk
        m = mrep_ref[row0:, :]
        pr = jnp.exp(s_ref[offs[j]:offs[j + 1], :] - jnp.concatenate([m] * (tq // LANES), axis=1))
        ov = _dot(pr.astype(BF16), va_ref[j * tq:(j + 1) * tq, :])
        if j == 0:
            accl_ref[...] = ov
        else:
            accl_ref[row0:, :] += ov

    for i in range(n_blocks):
        blk = accl_ref[i * rows_per_blk:(i + 1) * rows_per_blk, :]
        out = blk[:, :LANES] / blk[:, LANES:]
        o_ref[0, i * tq:(i + 1) * tq, :] = jnp.where(lane < HEAD_DIM, out[:tq], out[tq:]).astype(BF16)


def _moba(proj, col0, bias_tiles):
    bsz, s, _ = proj.shape
    tq = MOBA_BLOCK
    assert s % tq == 0
    n_blocks = s // tq
    assert n_blocks <= 16
    rows = HEADS_PER_TILE * s
    return pl.pallas_call(
        functools.partial(_moba_kernel, n_blocks=n_blocks),
        grid=(bsz, N_PAIRS),
        in_specs=[
            pl.BlockSpec((1, s, LANES), lambda b, p: (b, 0, col0 + p)),
            pl.BlockSpec((1, s, LANES), lambda b, p: (b, 0, col0 + N_PAIRS + p)),
            pl.BlockSpec((1, s, LANES), lambda b, p: (b, 0, col0 + 2 * N_PAIRS + p)),
            pl.BlockSpec((HEADS_PER_TILE, 2, tq, tq), lambda b, p: (p, 0, 0, 0)),
        ],
        out_specs=pl.BlockSpec((1, s, LANES), lambda b, p: (b, 0, p)),
        out_shape=jax.ShapeDtypeStruct((bsz, s, W_BRANCH), BF16),
        scratch_shapes=[
            pltpu.VMEM((s, 2 * LANES), BF16),
            pltpu.VMEM((s, 2 * LANES), BF16),
            pltpu.VMEM((rows, 2 * LANES), BF16),
            pltpu.VMEM((rows * (n_blocks + 1) // 2, tq), F32),
            pltpu.VMEM((rows, 1), F32),
            pltpu.VMEM((rows, LANES), F32),
            pltpu.VMEM((rows, 2 * LANES), F32),
        ],
        compiler_params=pltpu.CompilerParams(
            dimension_semantics=("arbitrary", "arbitrary"), vmem_limit_bytes=VMEM_LIMIT),
        name="moba",
    )(proj, proj, proj, bias_tiles)


def _merge_kernel(oa_ref, ob_ref, ga_ref, gb_ref, x_ref, wa_ref, wb_ref, wo_ref, g_ref, gate_ref,
                  o_ref):
    pa = _dot(oa_ref[0], wa_ref[...])
    pb = _dot(ob_ref[0], wb_ref[...])
    merged = (jax.nn.sigmoid(ga_ref[0].astype(F32)) * pa
              + jax.nn.sigmoid(gb_ref[0].astype(F32)) * pb)
    y = _dot(merged.astype(BF16), wo_ref[...])
    o_ref[0] = x_ref[0] + gate_ref[0] * _rms(y, g_ref[...])


def _merge(o_a, o_b, proj, gate_col, x, w_a, w_b, w_o, g_post, gate):
    bsz, s, d = x.shape
    tm = 512
    const = lambda b, t: (0, 0)
    return pl.pallas_call(
        _merge_kernel,
        grid=(bsz, s // tm),
        in_specs=[
            pl.BlockSpec((1, tm, W_BRANCH), lambda b, t: (b, t, 0)),
            pl.BlockSpec((1, tm, W_BRANCH), lambda b, t: (b, t, 0)),
            pl.BlockSpec((1, tm, d), lambda b, t: (b, t, gate_col)),
            pl.BlockSpec((1, tm, d), lambda b, t: (b, t, gate_col + 1)),
            pl.BlockSpec((1, tm, d), lambda b, t: (b, t, 0)),
            pl.BlockSpec((W_BRANCH, d), const),
            pl.BlockSpec((W_BRANCH, d), const),
            pl.BlockSpec((d, d), const),
            pl.BlockSpec((1, d), const),
            pl.BlockSpec((1, 1, d), lambda b, t: (b, 0, 0)),
        ],
        out_specs=pl.BlockSpec((1, tm, d), lambda b, t: (b, t, 0)),
        out_shape=jax.ShapeDtypeStruct((bsz, s, d), F32),
        compiler_params=pltpu.CompilerParams(
            dimension_semantics=("arbitrary", "arbitrary"), vmem_limit_bytes=VMEM_LIMIT),
        name="merge",
    )(o_a, o_b, proj, proj, x, w_a, w_b, w_o, g_post.reshape(1, d), gate.reshape(bsz, 1, d))


FFN_ROWS = 1024
FFN_COLS = 512


def _ffn_kernel(x_ref, gpre_ref, sc_ref, sh_ref, w1_ref, w3_ref, w2_ref, gpost_ref, gate_ref,
                o_ref, h_ref, acc_ref):
    k = pl.program_id(2)

    @pl.when(k == 0)
    def _():
        h = _rms(x_ref[0], gpre_ref[...]) * (1.0 + sc_ref[0]) + sh_ref[0]
        h_ref[...] = h.astype(BF16)
        acc_ref[...] = jnp.zeros_like(acc_ref)

    h = h_ref[...]
    u = jax.nn.silu(_dot(h, w1_ref[...].astype(BF16))) * _dot(h, w3_ref[...].astype(BF16))
    acc_ref[...] += _dot(u.astype(BF16), w2_ref[...].astype(BF16))

    @pl.when(k == pl.num_programs(2) - 1)
    def _():
        o_ref[0] = x_ref[0] + gate_ref[0] * _rms(acc_ref[...], gpost_ref[...])


def _ffn(x, g_pre, scale, shift, w1, w3, w2, g_post, gate):
    bsz, s, d = x.shape
    f = w1.shape[1]
    tm, tf = FFN_ROWS, FFN_COLS
    row = lambda b, t, k: (b, t, 0)
    per_b = lambda b, t, k: (b, 0, 0)
    const = lambda b, t, k: (0, 0)
    return pl.pallas_call(
        _ffn_kernel,
        grid=(bsz, s // tm, f // tf),
        in_specs=[
            pl.BlockSpec((1, tm, d), row),
            pl.BlockSpec((1, d), const),
            pl.BlockSpec((1, 1, d), per_b),
            pl.BlockSpec((1, 1, d), per_b),
            pl.BlockSpec((d, tf), lambda b, t, k: (0, k)),
            pl.BlockSpec((d, tf), lambda b, t, k: (0, k)),
            pl.BlockSpec((tf, d), lambda b, t, k: (k, 0)),
            pl.BlockSpec((1, d), const),
            pl.BlockSpec((1, 1, d), per_b),
        ],
        out_specs=pl.BlockSpec((1, tm, d), row),
        out_shape=jax.ShapeDtypeStruct((bsz, s, d), F32),
        scratch_shapes=[pltpu.VMEM((tm, d), BF16), pltpu.VMEM((tm, d), F32)],
        compiler_params=pltpu.CompilerParams(
            dimension_semantics=("arbitrary",) * 3, vmem_limit_bytes=VMEM_LIMIT),
        name="ffn_dense",
    )(x, g_pre.reshape(1, d), scale.reshape(bsz, 1, d), shift.reshape(bsz, 1, d), w1, w3, w2,
      g_post.reshape(1, d), gate.reshape(bsz, 1, d))


ROUTE_ROWS = 512


def _to_token_major(ref, val):
    rows, n = val.shape[0], val.shape[1] // LANES
    for s in range(n):
        ref[pl.ds(s, rows, stride=n), :] = val[:, s * LANES:(s + 1) * LANES]


def _from_token_major(ref, n):
    rows = ref.shape[0] // n
    return jnp.concatenate([ref[pl.ds(s, rows, stride=n), :] for s in range(n)], axis=1)
ROUTE_LANES = ("expert0", "expert1", "weight0", "weight1", "rank0", "rank1")


def _route_kernel(x_ref, gpre_ref, sc_ref, sh_ref, wr_ref, h_ref, route_ref, cnt_ref, *, n_experts):
    first_step = (pl.program_id(0) == 0) & (pl.program_id(1) == 0)

    @pl.when(first_step)
    def _():
        cnt_ref[...] = jnp.zeros_like(cnt_ref)

    h = _rms(x_ref[0], gpre_ref[...]) * (1.0 + sc_ref[0]) + sh_ref[0]
    _to_token_major(h_ref, h)
    logits = jnp.dot(h, wr_ref[...], preferred_element_type=F32, precision=lax.Precision.HIGHEST)
    ts = logits.shape[0]
    lane = lax.broadcasted_iota(jnp.int32, logits.shape, 1)
    lg = jnp.where(lane < n_experts, logits, -jnp.inf)
    m1 = jnp.max(lg, axis=1, keepdims=True)
    i1 = jnp.min(jnp.where(lg == m1, lane, LANES), axis=1, keepdims=True)
    first = lane == i1
    lg2 = jnp.where(first, -jnp.inf, lg)
    m2 = jnp.max(lg2, axis=1, keepdims=True)
    i2 = jnp.min(jnp.where(lg2 == m2, lane, LANES), axis=1, keepdims=True)
    second = lane == i2
    e2 = jnp.exp(m2 - m1)
    w_first = 1.0 / (1.0 + e2)
    w_second = e2 * w_first

    onehot = jnp.where(first | second, 1.0, 0.0)
    r = lax.broadcasted_iota(jnp.int32, (ts, ts), 0)
    c = lax.broadcasted_iota(jnp.int32, (ts, ts), 1)
    lower = jnp.where(c < r, 1.0, 0.0).astype(BF16)
    prefix = _dot(lower, onehot.astype(BF16)) + cnt_ref[...]
    rank_first = jnp.sum(jnp.where(first, prefix, 0.0), axis=1, keepdims=True)
    rank_second = jnp.sum(jnp.where(second, prefix, 0.0), axis=1, keepdims=True)
    cnt_ref[...] += jnp.sum(onehot, axis=0, keepdims=True)

    cols = (i1.astype(F32), i2.astype(F32), w_first, w_second, rank_first, rank_second)
    route = jnp.zeros(logits.shape, F32)
    for n, val in enumerate(cols):
        route = jnp.where(lane == n, val, route)
    route_ref[0] = route


def _route(x, g_pre, scale, shift, w_router):
    bsz, s, d = x.shape
    n_experts = w_router.shape[1]
    ts = ROUTE_ROWS
    row = lambda b, t: (b, t, 0)
    per_b = lambda b, t: (b, 0, 0)
    const = lambda b, t: (0, 0)
    return pl.pallas_call(
        functools.partial(_route_kernel, n_experts=n_experts),
        grid=(bsz, s // ts),
        in_specs=[
            pl.BlockSpec((1, ts, d), row),
            pl.BlockSpec((1, d), const),
            pl.BlockSpec((1, 1, d), per_b),
            pl.BlockSpec((1, 1, d), per_b),
            pl.BlockSpec((d, LANES), const),
        ],
        out_specs=[
            pl.BlockSpec((ts * d // LANES, LANES), lambda b, t: (b * (s // ts) + t, 0)),
            pl.BlockSpec((1, ts, LANES), row),
            pl.BlockSpec((1, LANES), const),
        ],
        out_shape=[
            jax.ShapeDtypeStruct((bsz * s * d // LANES, LANES), F32),
            jax.ShapeDtypeStruct((bsz, s, LANES), F32),
            jax.ShapeDtypeStruct((1, LANES), F32),
        ],
        compiler_params=pltpu.CompilerParams(
            dimension_semantics=("arbitrary", "arbitrary"), vmem_limit_bytes=VMEM_LIMIT),
        name="moe_route",
    )(x, g_pre.reshape(1, d), scale.reshape(bsz, 1, d), shift.reshape(bsz, 1, d),
      jnp.pad(w_router, ((0, 0), (0, LANES - n_experts))))


def _for_each_chunk(start, length, chunk, fn):
    whole = length // chunk

    def loop(c, _):
        fn(start + c * chunk, chunk)
        return 0

    lax.fori_loop(0, whole, loop, 0)
    cur = start + whole * chunk
    size = chunk // 2
    while size >= 1:
        bit = length & size

        @pl.when(bit != 0)
        def _(cur=cur, size=size):
            fn(cur, size)

        cur = cur + bit
        size //= 2


def _scatter_kernel(pos_ref, fill_ref, h_ref, xs_ref, zero_ref, sem, *, n_experts, tile_rows):
    ts = h_ref.shape[0]

    def issue(r, _):
        for slot in range(TOP_K):
            dst = pos_ref[0, 0, slot * ts + r]
            pltpu.make_async_copy(h_ref.at[r], xs_ref.at[dst], sem).start(priority=slot)
        return 0

    lax.fori_loop(0, ts, issue, 0)
    for slot in range(TOP_K):
        pltpu.make_async_copy(h_ref, xs_ref.at[pl.ds(0, ts)], sem).wait()

    @pl.when(pl.program_id(0) == pl.num_programs(0) - 1)
    def _():
        zero_ref[...] = jnp.zeros_like(zero_ref)
        occupied = fill_ref[2 * n_experts] * tile_rows
        spans = [(fill_ref[e], fill_ref[n_experts + e] - fill_ref[e]) for e in range(n_experts)]
        spans.append((occupied, xs_ref.shape[0] - occupied))

        def zero_copy(first, rows):
            return pltpu.make_async_copy(zero_ref.at[pl.ds(0, rows)],
                                         xs_ref.at[pl.ds(first, rows)], sem)

        for first, length in spans:
            _for_each_chunk(first, length, zero_ref.shape[0],
                            lambda row, rows: zero_copy(row, rows).start())
        for first, length in spans:
            _for_each_chunk(first, length, zero_ref.shape[0],
                            lambda row, rows: zero_copy(row, rows).wait())


def _scatter(h, pos, fill, n_rows):
    t = h.shape[0]
    ts = ROUTE_ROWS
    zero_rows = 256
    return pl.pallas_call(
        functools.partial(_scatter_kernel, n_experts=(fill.shape[0] - 1) // 2,
                          tile_rows=FFN_ROWS),
        grid=(t // ts,),
        in_specs=[
            pl.BlockSpec((1, 1, TOP_K * ts), lambda i: (i, 0, 0), memory_space=pltpu.SMEM),
            pl.BlockSpec(memory_space=pltpu.SMEM),
            pl.BlockSpec((ts,) + h.shape[1:], lambda i: (i, 0, 0)),
        ],
        out_specs=pl.BlockSpec(memory_space=pl.ANY),
        out_shape=jax.ShapeDtypeStruct((n_rows,) + h.shape[1:], F32),
        scratch_shapes=[pltpu.VMEM((zero_rows,) + h.shape[1:], F32), pltpu.SemaphoreType.DMA(())],
        compiler_params=pltpu.CompilerParams(
            dimension_semantics=("arbitrary",), vmem_limit_bytes=VMEM_LIMIT),
        name="moe_scatter",
    )(pos, fill, h)


def _experts_kernel(exp_ref, nact_ref, xs_ref, w1_ref, w3_ref, w2_ref, ys_ref, h_ref, acc_ref):
    i = pl.program_id(0)
    k = pl.program_id(1)

    @pl.when(i < nact_ref[0])
    def _():
        @pl.when(k == 0)
        def _():
            h_ref[...] = _from_token_major(xs_ref, h_ref.shape[1] // LANES).astype(BF16)

        h = h_ref[...]
        u = jax.nn.silu(_dot(h, w1_ref[0].astype(BF16))) * _dot(h, w3_ref[0].astype(BF16))
        part = _dot(u.astype(BF16), w2_ref[0].astype(BF16))

        @pl.when(k == 0)
        def _():
            acc_ref[...] = part

        @pl.when(k > 0)
        def _():
            acc_ref[...] += part

        @pl.when(k == pl.num_programs(1) - 1)
        def _():
            _to_token_major(ys_ref, acc_ref[...])

    @pl.when((i >= nact_ref[0]) & (k == 0))
    def _():
        ys_ref[...] = jnp.zeros_like(ys_ref)


def _experts(xs, tile_expert, n_active, w1, w3, w2):
    d, f = w1.shape[1:]
    n_rows = xs.shape[0] * LANES // d
    tm, tf = FFN_ROWS, FFN_COLS
    n_k = f // tf
    n_tiles = n_rows // tm

    def k_of(i, k, nact):
        return jnp.where(i < nact[0], k, n_k - 1)

    return pl.pallas_call(
        _experts_kernel,
        grid_spec=pltpu.PrefetchScalarGridSpec(
            num_scalar_prefetch=2,
            grid=(n_tiles, n_k),
            in_specs=[
                pl.BlockSpec((tm * d // LANES, LANES), lambda i, k, exp, nact: (i, 0)),
                pl.BlockSpec((1, d, tf), lambda i, k, exp, nact: (exp[i], 0, k_of(i, k, nact))),
                pl.BlockSpec((1, d, tf), lambda i, k, exp, nact: (exp[i], 0, k_of(i, k, nact))),
                pl.BlockSpec((1, tf, d), lambda i, k, exp, nact: (exp[i], k_of(i, k, nact), 0)),
            ],
            out_specs=pl.BlockSpec((tm * d // LANES, LANES), lambda i, k, exp, nact: (i, 0)),
            scratch_shapes=[pltpu.VMEM((tm, d), BF16), pltpu.VMEM((tm, d), F32)],
        ),
        out_shape=jax.ShapeDtypeStruct(xs.shape, F32),
        compiler_params=pltpu.CompilerParams(
            dimension_semantics=("arbitrary", "arbitrary"), vmem_limit_bytes=VMEM_LIMIT),
        name="moe_experts",
    )(tile_expert, n_active, xs, w1, w3, w2)


def _combine_kernel(pos_ref, route_ref, x_ref, gpost_ref, gate_ref, ys_ref, o_ref, buf0_ref,
                    buf1_ref, sem):
    ts = x_ref.shape[0]
    n = ys_ref.shape[1]
    bufs = (buf0_ref, buf1_ref)

    def issue(r, _):
        dst = pl.multiple_of(r * n, n)
        for slot in range(TOP_K):
            src = pos_ref[0, 0, slot * ts + r]
            pltpu.make_async_copy(ys_ref.at[src], bufs[slot].at[pl.ds(dst, n)],
                                  sem).start(priority=slot)
        return 0

    lax.fori_loop(0, ts, issue, 0)
    for slot in range(TOP_K):
        pltpu.make_async_copy(bufs[slot], bufs[slot], sem).wait()

    w0 = ROUTE_LANES.index("weight0")
    route = route_ref[...]
    y = (route[:, w0:w0 + 1] * _from_token_major(buf0_ref, n)
         + route[:, w0 + 1:w0 + 2] * _from_token_major(buf1_ref, n))
    o_ref[...] = x_ref[...] + gate_ref[0] * _rms(y, gpost_ref[...])


def _combine(ys, pos, route, x, g_post, gate):
    t, d = x.shape
    bsz = gate.shape[0]
    ts = ROUTE_ROWS
    steps_per_b = t // bsz // ts
    return pl.pallas_call(
        _combine_kernel,
        grid=(t // ts,),
        in_specs=[
            pl.BlockSpec((1, 1, TOP_K * ts), lambda i: (i, 0, 0), memory_space=pltpu.SMEM),
            pl.BlockSpec((ts, LANES), lambda i: (i, 0)),
            pl.BlockSpec((ts, d), lambda i: (i, 0)),
            pl.BlockSpec((1, d), lambda i: (0, 0)),
            pl.BlockSpec((1, 1, d), lambda i: (i // steps_per_b, 0, 0)),
            pl.BlockSpec(memory_space=pl.ANY),
        ],
        out_specs=pl.BlockSpec((ts, d), lambda i: (i, 0)),
        out_shape=jax.ShapeDtypeStruct((t, d), F32),
        scratch_shapes=[pltpu.VMEM((ts * d // LANES, LANES), F32)] * TOP_K
        + [pltpu.SemaphoreType.DMA(())],
        compiler_params=pltpu.CompilerParams(
            dimension_semantics=("arbitrary",), vmem_limit_bytes=VMEM_LIMIT),
        name="moe_combine",
    )(pos, route, x, g_post.reshape(1, d), gate, ys)


def _moe(x, g_pre, scale, shift, w_router, w1, w3, w2, g_post, gate):
    bsz, s, d = x.shape
    t = bsz * s
    n_experts = w_router.shape[1]
    ts, tm = ROUTE_ROWS, FFN_ROWS
    h, route, counts = _route(x, g_pre, scale, shift, w_router)
    route = route.reshape(t, LANES)

    lane_of = ROUTE_LANES.index
    expert = route[:, lane_of("expert0"):lane_of("expert0") + TOP_K].astype(jnp.int32)
    rank = route[:, lane_of("rank0"):lane_of("rank0") + TOP_K].astype(jnp.int32)
    counts = counts[0, :n_experts].astype(jnp.int32)
    tiles_per = (counts + tm - 1) // tm
    tile_end = jnp.cumsum(tiles_per)
    first_row = (tile_end - tiles_per) * tm
    n_active = tile_end[-1]
    n_tiles = TOP_K * t // tm + n_experts
    is_expert = expert[..., None] == jnp.arange(n_experts, dtype=jnp.int32)
    pos = jnp.sum(jnp.where(is_expert, first_row, 0), axis=-1) + rank
    pos = pos.reshape(t // ts, ts, TOP_K).transpose(0, 2, 1).reshape(t // ts, 1, TOP_K * ts)
    fill = jnp.concatenate([first_row + counts, tile_end * tm, n_active[None]]).astype(jnp.int32)
    idx = jnp.minimum(jnp.arange(n_tiles, dtype=jnp.int32), n_active - 1)
    tile_expert = jnp.sum(idx[:, None] >= tile_end[None, :], axis=1).astype(jnp.int32)

    token_major = (-1, d // LANES, LANES)
    xs = _scatter(h.reshape(token_major), pos, fill, n_tiles * tm)
    ys = _experts(xs.reshape(-1, LANES), tile_expert, n_active.reshape(1), w1, w3, w2)
    out = _combine(ys.reshape(token_major), pos, route, x.reshape(t, d), g_post,
                   gate.reshape(bsz, 1, d))
    return out.reshape(bsz, s, d)


def kernel(x, c, w_mod, b_mod, g_mix_pre, g_mix_post, g_ffn_pre, g_ffn_post, w_in, w_proj_sb,
           w_proj_moba, w_out, rel_bias, w1_dense, w3_dense, w2_dense, w_router, w1_moe, w3_moe,
           w2_moe):
    depth = w_mod.shape[0]
    d = x.shape[-1]
    qkv = 3 * W_BRANCH
    col = jnp.arange(w_in.shape[-1])
    is_q = (col < W_BRANCH) | ((col >= qkv) & (col < qkv + W_BRANCH))
    col_scale = jnp.where(is_q, HEAD_DIM ** -0.5, 1.0).astype(F32)

    mod = _modulation(c, w_mod, b_mod)
    bias_tiles = _moba_bias_tiles(rel_bias)
    for l in range(depth):
        sh_mix, sc_mix, gt_mix, sh_ffn, sc_ffn, gt_ffn = jnp.split(mod[l], N_MOD, axis=-1)
        proj = _inproj(x, g_mix_pre[l], sc_mix, sh_mix, (w_in[l] * col_scale).astype(BF16))
        o_a = _stick_breaking(proj, 0)
        o_b = _moba(proj, qkv // LANES, bias_tiles)
        x = _merge(o_a, o_b, proj, 2 * qkv // d, x, w_proj_sb[l].astype(BF16),
                   w_proj_moba[l].astype(BF16), w_out[l].astype(BF16), g_mix_post[l], gt_mix)
        j = l // 2
        if l % 2 == 0:
            x = _ffn(x, g_ffn_pre[l], sc_ffn, sh_ffn, w1_dense[j], w3_dense[j], w2_dense[j],
                     g_ffn_post[l], gt_ffn)
        else:
            x = _moe(x, g_ffn_pre[l], sc_ffn, sh_ffn, w_router[j], w1_moe[j], w3_moe[j],
                     w2_moe[j], g_ffn_post[l], gt_ffn)
    return x
```

```python
import functools
import math

import jax
import jax.numpy as jnp
from jax import lax
from jax.experimental import pallas as pl
from jax.experimental.pallas import tpu as pltpu

F32 = jnp.float32
BF16 = jnp.bfloat16

EPS = 1e-6
NEG_INF = -1e30
HEAD_DIM = 64
N_HEADS = 8
W_BRANCH = N_HEADS * HEAD_DIM
LANES = 128
SUBLANES = 8
HEADS_PER_TILE = LANES // HEAD_DIM
N_PAIRS = N_HEADS // HEADS_PER_TILE
MOBA_BLOCK = 256
MOBA_TOPK = 3
REL_BUCKETS = 32
REL_MAX_DIST = 128
N_MOD = 6
TOP_K = 2
ATT_BLOCK = 256
VMEM_LIMIT = 56 * 1024 * 1024


def _dot(a, b):
    return jnp.dot(a, b, preferred_element_type=F32)


def _dot_nt(a, b):
    return lax.dot_general(a, b, (((1,), (1,)), ((), ())), preferred_element_type=F32)


def _split_bf16(x):
    hi = x.astype(BF16)
    lo = (x - hi.astype(F32)).astype(BF16)
    return hi, lo


def _rms(x, g):
    return x * lax.rsqrt(jnp.mean(x * x, axis=-1, keepdims=True) + EPS) * g


def _mod_kernel(c_ref, w_ref, b_ref, o_ref):
    c = c_ref[...]
    cond = c * jax.nn.sigmoid(c)
    o_ref[0] = jnp.dot(cond, w_ref[0], preferred_element_type=F32,
                       precision=lax.Precision.HIGHEST) + b_ref[0]


def _modulation(c, w_mod, b_mod):
    depth, d, n = w_mod.shape
    bsz = c.shape[0]
    tn = 1536
    return pl.pallas_call(
        _mod_kernel,
        grid=(depth, n // tn),
        in_specs=[
            pl.BlockSpec((bsz, d), lambda l, j: (0, 0)),
            pl.BlockSpec((1, d, tn), lambda l, j: (l, 0, j)),
            pl.BlockSpec((1, 1, tn), lambda l, j: (l, 0, j)),
        ],
        out_specs=pl.BlockSpec((1, bsz, tn), lambda l, j: (l, 0, j)),
        out_shape=jax.ShapeDtypeStruct((depth, bsz, n), F32),
        compiler_params=pltpu.CompilerParams(
            dimension_semantics=("arbitrary", "arbitrary"), vmem_limit_bytes=VMEM_LIMIT),
        name="modulation",
    )(c, w_mod, b_mod.reshape(depth, 1, n))


def _inproj_kernel(x_ref, g_ref, sc_ref, sh_ref, w_ref, o_ref, *, tn):
    h = _rms(x_ref[0], g_ref[...]) * (1.0 + sc_ref[0]) + sh_ref[0]
    hb = h.astype(BF16)
    for n in range(w_ref.shape[1] // tn):
        o_ref[0, :, n * tn:(n + 1) * tn] = _dot(hb, w_ref[:, n * tn:(n + 1) * tn]).astype(BF16)


def _inproj(x, g, scale, shift, w):
    bsz, s, d = x.shape
    n = w.shape[1]
    tm = 512
    return pl.pallas_call(
        functools.partial(_inproj_kernel, tn=1024),
        grid=(bsz, s // tm),
        in_specs=[
            pl.BlockSpec((1, tm, d), lambda b, t: (b, t, 0)),
            pl.BlockSpec((1, d), lambda b, t: (0, 0)),
            pl.BlockSpec((1, 1, d), lambda b, t: (b, 0, 0)),
            pl.BlockSpec((1, 1, d), lambda b, t: (b, 0, 0)),
            pl.BlockSpec((d, n), lambda b, t: (0, 0), pipeline_mode=pl.Buffered(1)),
        ],
        out_specs=pl.BlockSpec((1, tm, n), lambda b, t: (b, t, 0)),
        out_shape=jax.ShapeDtypeStruct((bsz, s, n), BF16),
        compiler_params=pltpu.CompilerParams(
            dimension_semantics=("arbitrary", "arbitrary"), vmem_limit_bytes=VMEM_LIMIT),
        name="inproj",
    )(x, g.reshape(1, d), scale.reshape(bsz, 1, d), shift.reshape(bsz, 1, d), w)


SOFTPLUS_LINEAR = 40.0


def _softplus(z):
    return jnp.maximum(z, jnp.log(1.0 + jnp.exp(jnp.minimum(z, SOFTPLUS_LINEAR))))


def _sb_kernel(q_ref, k_ref, v_ref, o_ref, qs_ref, acc_ref, carry_ref):
    tq = ATT_BLOCK
    n_blocks = k_ref.shape[1] // tq
    rows_per_blk = HEADS_PER_TILE * tq
    lane = lax.broadcasted_iota(jnp.int32, (1, LANES), 1)
    row = lax.broadcasted_iota(jnp.int32, (tq, tq), 0)
    col = lax.broadcasted_iota(jnp.int32, (tq, tq), 1)
    upper = jnp.where(row > col, 1.0, 0.0).astype(BF16)
    causal = jnp.concatenate([col < row] * HEADS_PER_TILE, axis=0)
    q = q_ref[0]
    for i in range(n_blocks):
        qi = q[i * tq:(i + 1) * tq]
        for h in range(HEADS_PER_TILE):
            row0 = (HEADS_PER_TILE * i + h) * tq
            qs_ref[row0:row0 + tq, :] = jnp.where(lane // HEAD_DIM == h, qi, jnp.zeros_like(qi))

    for j in reversed(range(n_blocks)):
        row0 = j * rows_per_blk
        far0 = row0 + rows_per_blk
        has_far = j + 1 < n_blocks
        z = _dot_nt(qs_ref[row0:, :], k_ref[0, j * tq:(j + 1) * tq, :])
        sp = _softplus(z)
        sp_diag = jnp.where(causal, sp[:rows_per_blk], 0.0)
        sp = jnp.concatenate([sp_diag, sp[rows_per_blk:]], axis=0) if has_far else sp_diag
        logit = z - sp - _dot(sp.astype(BF16), upper)
        att_diag = jnp.where(causal, jnp.exp(logit[:rows_per_blk]), 0.0)
        if has_far:
            att_far = jnp.exp(logit[rows_per_blk:] - carry_ref[far0:, :])
            att = jnp.concatenate([att_diag, att_far], axis=0)
        else:
            att = att_diag
        part = _dot(att.astype(BF16), v_ref[0, j * tq:(j + 1) * tq, :])
        rs = jnp.sum(sp, axis=1, keepdims=True)
        acc_ref[row0:far0, :] = part[:rows_per_blk]
        carry_ref[row0:far0, :] = rs[:rows_per_blk]
        if has_far:
            acc_ref[far0:, :] += part[rows_per_blk:]
            carry_ref[far0:, :] += rs[rows_per_blk:]

    for i in range(n_blocks):
        blk = acc_ref[i * rows_per_blk:(i + 1) * rows_per_blk, :]
        o_ref[0, i * tq:(i + 1) * tq, :] = jnp.where(lane < HEAD_DIM, blk[:tq], blk[tq:]).astype(BF16)


def _stick_breaking(proj, col0):
    bsz, s, _ = proj.shape
    assert s % ATT_BLOCK == 0
    rows = HEADS_PER_TILE * s
    return pl.pallas_call(
        _sb_kernel,
        grid=(bsz, N_PAIRS),
        in_specs=[
            pl.BlockSpec((1, s, LANES), lambda b, p: (b, 0, col0 + p)),
            pl.BlockSpec((1, s, LANES), lambda b, p: (b, 0, col0 + N_PAIRS + p)),
            pl.BlockSpec((1, s, LANES), lambda b, p: (b, 0, col0 + 2 * N_PAIRS + p)),
        ],
        out_specs=pl.BlockSpec((1, s, LANES), lambda b, p: (b, 0, p)),
        out_shape=jax.ShapeDtypeStruct((bsz, s, W_BRANCH), BF16),
        scratch_shapes=[pltpu.VMEM((rows, LANES), BF16),
                        pltpu.VMEM((rows, LANES), F32),
                        pltpu.VMEM((rows, 1), F32)],
        compiler_params=pltpu.CompilerParams(
            dimension_semantics=("arbitrary", "arbitrary"), vmem_limit_bytes=VMEM_LIMIT),
        name="stick_breaking",
    )(proj, proj, proj)


def _rel_bucket(dist):
    max_exact = REL_BUCKETS // 2
    n = jnp.maximum(dist, 0)
    nf = jnp.maximum(n, 1).astype(F32)
    large = max_exact + (jnp.log(nf / max_exact) / math.log(REL_MAX_DIST / max_exact)
                         * (REL_BUCKETS - max_exact)).astype(jnp.int32)
    large = jnp.minimum(large, REL_BUCKETS - 1)
    return jnp.where(n < max_exact, n, large)


def _moba_bias_tiles(rel_bias):
    assert MOBA_BLOCK >= REL_MAX_DIST
    t = jnp.arange(MOBA_BLOCK)[:, None]
    s = jnp.arange(MOBA_BLOCK)[None, :]
    d_own = t - s
    rel = rel_bias.astype(F32) - rel_bias[REL_BUCKETS - 1].astype(F32)

    def lookup(dist):
        onehot = jax.nn.one_hot(_rel_bucket(dist), REL_BUCKETS, dtype=F32)
        return jnp.einsum("tsk,kh->hts", onehot, rel, precision=lax.Precision.HIGHEST)

    own = jnp.where((d_own >= 0)[None], lookup(d_own), NEG_INF)
    return jnp.stack([own, lookup(d_own + MOBA_BLOCK)], axis=1)


def _moba_kernel(q_ref, k_ref, v_ref, bias_ref, o_ref,
                 ka_ref, va_ref, qa_ref, s_ref, m_ref, mrep_ref, accl_ref, *, n_blocks):
    tq = MOBA_BLOCK
    s_len = k_ref.shape[1]
    nb_pad = 16
    rows_per_blk = HEADS_PER_TILE * tq
    lane = lax.broadcasted_iota(jnp.int32, (1, LANES), 1)
    head_lanes = [lane // HEAD_DIM == h for h in range(HEADS_PER_TILE)]
    q = q_ref[0]
    k = k_ref[0]

    key_blk = lax.broadcasted_iota(jnp.int32, (s_len, LANES), 0) // tq
    ka_ref[:, :LANES] = k
    ka_ref[:, LANES:] = jnp.where(lane == key_blk, 1.0, 0.0).astype(BF16)
    va_ref[:, :LANES] = v_ref[0]
    va_ref[:, LANES:] = jnp.ones((s_len, LANES), BF16)

    r = lax.broadcasted_iota(jnp.int32, (nb_pad, s_len), 0)
    c = lax.broadcasted_iota(jnp.int32, (nb_pad, s_len), 1)
    pool = jnp.where(c // tq == r, 1.0 / tq, 0.0).astype(BF16)
    km_hi, km_lo = _split_bf16(_dot(pool, k))
    means = [jnp.where(hl, km, jnp.zeros_like(km)) for hl in head_lanes for km in (km_hi, km_lo)]
    gates = _dot_nt(jnp.concatenate(means, axis=0), q)
    own = c // tq
    masks = []
    for h in range(HEADS_PER_TILE):
        gate = (gates[2 * h * nb_pad:(2 * h + 1) * nb_pad]
                + gates[(2 * h + 1) * nb_pad:(2 * h + 2) * nb_pad])
        cnt = jnp.zeros((nb_pad, s_len), F32)
        for n in range(n_blocks):
            gn = gate[n:n + 1, :]
            beats = jnp.where(gn > gate, 1.0, jnp.where((gn == gate) & (n < r), 1.0, 0.0))
            cnt = cnt + jnp.where(n < own, beats, 0.0)
        keep = (r == own) | ((r < own) & (cnt < MOBA_TOPK))
        masks += [jnp.where(keep, 0.0, NEG_INF).astype(BF16),
                  jnp.zeros((LANES - nb_pad, s_len), BF16)]
    masks = jnp.concatenate(masks, axis=0)
    er = lax.broadcasted_iota(jnp.int32, (tq, tq), 0)
    ec = lax.broadcasted_iota(jnp.int32, (tq, tq), 1)
    eye = jnp.where(er == ec, 1.0, 0.0).astype(BF16)
    for i in range(n_blocks):
        qi = q[i * tq:(i + 1) * tq]
        mask_rows = _dot_nt(eye, masks[:, i * tq:(i + 1) * tq]).astype(BF16)
        for h in range(HEADS_PER_TILE):
            row0 = (HEADS_PER_TILE * i + h) * tq
            qa_ref[row0:row0 + tq, :LANES] = jnp.where(head_lanes[h], qi, jnp.zeros_like(qi))
            qa_ref[row0:row0 + tq, LANES:] = mask_rows[:, h * LANES:(h + 1) * LANES]

    bias = [jnp.concatenate([bias_ref[h, which] for h in range(HEADS_PER_TILE)], axis=0)
            for which in range(2)]
    offs = [0]
    for j in range(n_blocks):
        offs.append(offs[-1] + (n_blocks - j) * rows_per_blk)
    for j in range(n_blocks):
        row0 = j * rows_per_blk
        s = _dot_nt(qa_ref[row0:, :], ka_ref[j * tq:(j + 1) * tq, :])
        parts = [s[:rows_per_blk] + bias[0]]
        if j + 1 < n_blocks:
            parts.append(s[rows_per_blk:2 * rows_per_blk] + bias[1])
        if j + 2 < n_blocks:
            parts.append(s[2 * rows_per_blk:])
        s = jnp.concatenate(parts, axis=0)
        s_ref[offs[j]:offs[j + 1], :] = s
        m = jnp.max(s, axis=1, keepdims=True)
        if j == 0:
            m_ref[...] = m
        else:
            m_ref[row0:, :] = jnp.maximum(m_ref[row0:, :], m)
    mrep_ref[...] = jnp.broadcast_to(m_ref[...], mrep_ref.shape)
    for j in range(n_blocks):
        row0 = j * rows_per_blk
        m = mrep_ref[row0:, :]
        pr = jnp.exp(s_ref[offs[j]:offs[j + 1], :] - jnp.concatenate([m] * (tq // LANES), axis=1))
        ov = _dot(pr.astype(BF16), va_ref[j * tq:(j + 1) * tq, :])
        if j == 0:
            accl_ref[...] = ov
        else:
            accl_ref[row0:, :] += ov

    for i in range(n_blocks):
        blk = accl_ref[i * rows_per_blk:(i + 1) * rows_per_blk, :]
        out = blk[:, :LANES] / blk[:, LANES:]
        o_ref[0, i * tq:(i + 1) * tq, :] = jnp.where(lane < HEAD_DIM, out[:tq], out[tq:]).astype(BF16)


def _moba(proj, col0, bias_tiles):
    bsz, s, _ = proj.shape
    tq = MOBA_BLOCK
    assert s % tq == 0
    n_blocks = s // tq
    assert n_blocks <= 16
    rows = HEADS_PER_TILE * s
    return pl.pallas_call(
        functools.partial(_moba_kernel, n_blocks=n_blocks),
        grid=(bsz, N_PAIRS),
        in_specs=[
            pl.BlockSpec((1, s, LANES), lambda b, p: (b, 0, col0 + p)),
            pl.BlockSpec((1, s, LANES), lambda b, p: (b, 0, col0 + N_PAIRS + p)),
            pl.BlockSpec((1, s, LANES), lambda b, p: (b, 0, col0 + 2 * N_PAIRS + p)),
            pl.BlockSpec((HEADS_PER_TILE, 2, tq, tq), lambda b, p: (p, 0, 0, 0)),
        ],
        out_specs=pl.BlockSpec((1, s, LANES), lambda b, p: (b, 0, p)),
        out_shape=jax.ShapeDtypeStruct((bsz, s, W_BRANCH), BF16),
        scratch_shapes=[
            pltpu.VMEM((s, 2 * LANES), BF16),
            pltpu.VMEM((s, 2 * LANES), BF16),
            pltpu.VMEM((rows, 2 * LANES), BF16),
            pltpu.VMEM((rows * (n_blocks + 1) // 2, tq), F32),
            pltpu.VMEM((rows, 1), F32),
            pltpu.VMEM((rows, LANES), F32),
            pltpu.VMEM((rows, 2 * LANES), F32),
        ],
        compiler_params=pltpu.CompilerParams(
            dimension_semantics=("arbitrary", "arbitrary"), vmem_limit_bytes=VMEM_LIMIT),
        name="moba",
    )(proj, proj, proj, bias_tiles)


def _merge_kernel(oa_ref, ob_ref, ga_ref, gb_ref, x_ref, wa_ref, wb_ref, wo_ref, g_ref, gate_ref,
                  o_ref):
    pa = _dot(oa_ref[0], wa_ref[...])
    pb = _dot(ob_ref[0], wb_ref[...])
    merged = (jax.nn.sigmoid(ga_ref[0].astype(F32)) * pa
              + jax.nn.sigmoid(gb_ref[0].astype(F32)) * pb)
    y = _dot(merged.astype(BF16), wo_ref[...])
    o_ref[0] = x_ref[0] + gate_ref[0] * _rms(y, g_ref[...])


def _merge(o_a, o_b, proj, gate_col, x, w_a, w_b, w_o, g_post, gate):
    bsz, s, d = x.shape
    tm = 512
    const = lambda b, t: (0, 0)
    return pl.pallas_call(
        _merge_kernel,
        grid=(bsz, s // tm),
        in_specs=[
            pl.BlockSpec((1, tm, W_BRANCH), lambda b, t: (b, t, 0)),
            pl.BlockSpec((1, tm, W_BRANCH), lambda b, t: (b, t, 0)),
            pl.BlockSpec((1, tm, d), lambda b, t: (b, t, gate_col)),
            pl.BlockSpec((1, tm, d), lambda b, t: (b, t, gate_col + 1)),
            pl.BlockSpec((1, tm, d), lambda b, t: (b, t, 0)),
            pl.BlockSpec((W_BRANCH, d), const),
            pl.BlockSpec((W_BRANCH, d), const),
            pl.BlockSpec((d, d), const),
            pl.BlockSpec((1, d), const),
            pl.BlockSpec((1, 1, d), lambda b, t: (b, 0, 0)),
        ],
        out_specs=pl.BlockSpec((1, tm, d), lambda b, t: (b, t, 0)),
        out_shape=jax.ShapeDtypeStruct((bsz, s, d), F32),
        compiler_params=pltpu.CompilerParams(
            dimension_semantics=("arbitrary", "arbitrary"), vmem_limit_bytes=VMEM_LIMIT),
        name="merge",
    )(o_a, o_b, proj, proj, x, w_a, w_b, w_o, g_post.reshape(1, d), gate.reshape(bsz, 1, d))


FFN_ROWS = 1024
FFN_COLS = 512


def _ffn_kernel(x_ref, gpre_ref, sc_ref, sh_ref, w1_ref, w3_ref, w2_ref, gpost_ref, gate_ref,
                o_ref, h_ref, acc_ref):
    k = pl.program_id(2)

    @pl.when(k == 0)
    def _():
        h = _rms(x_ref[0], gpre_ref[...]) * (1.0 + sc_ref[0]) + sh_ref[0]
        h_ref[...] = h.astype(BF16)
        acc_ref[...] = jnp.zeros_like(acc_ref)

    h = h_ref[...]
    u = jax.nn.silu(_dot(h, w1_ref[...].astype(BF16))) * _dot(h, w3_ref[...].astype(BF16))
    acc_ref[...] += _dot(u.astype(BF16), w2_ref[...].astype(BF16))

    @pl.when(k == pl.num_programs(2) - 1)
    def _():
        o_ref[0] = x_ref[0] + gate_ref[0] * _rms(acc_ref[...], gpost_ref[...])


def _ffn(x, g_pre, scale, shift, w1, w3, w2, g_post, gate):
    bsz, s, d = x.shape
    f = w1.shape[1]
    tm, tf = FFN_ROWS, FFN_COLS
    row = lambda b, t, k: (b, t, 0)
    per_b = lambda b, t, k: (b, 0, 0)
    const = lambda b, t, k: (0, 0)
    return pl.pallas_call(
        _ffn_kernel,
        grid=(bsz, s // tm, f // tf),
        in_specs=[
            pl.BlockSpec((1, tm, d), row),
            pl.BlockSpec((1, d), const),
            pl.BlockSpec((1, 1, d), per_b),
            pl.BlockSpec((1, 1, d), per_b),
            pl.BlockSpec((d, tf), lambda b, t, k: (0, k)),
            pl.BlockSpec((d, tf), lambda b, t, k: (0, k)),
            pl.BlockSpec((tf, d), lambda b, t, k: (k, 0)),
            pl.BlockSpec((1, d), const),
            pl.BlockSpec((1, 1, d), per_b),
        ],
        out_specs=pl.BlockSpec((1, tm, d), row),
        out_shape=jax.ShapeDtypeStruct((bsz, s, d), F32),
        scratch_shapes=[pltpu.VMEM((tm, d), BF16), pltpu.VMEM((tm, d), F32)],
        compiler_params=pltpu.CompilerParams(
            dimension_semantics=("arbitrary",) * 3, vmem_limit_bytes=VMEM_LIMIT),
        name="ffn_dense",
    )(x, g_pre.reshape(1, d), scale.reshape(bsz, 1, d), shift.reshape(bsz, 1, d), w1, w3, w2,
      g_post.reshape(1, d), gate.reshape(bsz, 1, d))


ROUTE_ROWS = 512


def _to_token_major(ref, val):
    rows, n = val.shape[0], val.shape[1] // LANES
    for s in range(n):
        ref[pl.ds(s, rows, stride=n), :] = val[:, s * LANES:(s + 1) * LANES]


def _from_token_major(ref, n):
    rows = ref.shape[0] // n
    return jnp.concatenate([ref[pl.ds(s, rows, stride=n), :] for s in range(n)], axis=1)
ROUTE_LANES = ("expert0", "expert1", "weight0", "weight1", "rank0", "rank1")


def _route_kernel(x_ref, gpre_ref, sc_ref, sh_ref, wr_ref, h_ref, route_ref, cnt_ref, *, n_experts):
    first_step = (pl.program_id(0) == 0) & (pl.program_id(1) == 0)

    @pl.when(first_step)
    def _():
        cnt_ref[...] = jnp.zeros_like(cnt_ref)

    h = _rms(x_ref[0], gpre_ref[...]) * (1.0 + sc_ref[0]) + sh_ref[0]
    _to_token_major(h_ref, h)
    h_hi, h_lo = _split_bf16(h)
    w_hi, w_lo = _split_bf16(wr_ref[...])
    logits = _dot(h_hi, w_hi) + (_dot(h_hi, w_lo) + _dot(h_lo, w_hi))
    ts = logits.shape[0]
    lane = lax.broadcasted_iota(jnp.int32, logits.shape, 1)
    lg = jnp.where(lane < n_experts, logits, -jnp.inf)
    m1 = jnp.max(lg, axis=1, keepdims=True)
    i1 = jnp.min(jnp.where(lg == m1, lane, LANES), axis=1, keepdims=True)
    first = lane == i1
    lg2 = jnp.where(first, -jnp.inf, lg)
    m2 = jnp.max(lg2, axis=1, keepdims=True)
    i2 = jnp.min(jnp.where(lg2 == m2, lane, LANES), axis=1, keepdims=True)
    second = lane == i2
    e2 = jnp.exp(m2 - m1)
    w_first = 1.0 / (1.0 + e2)
    w_second = e2 * w_first

    onehot = jnp.where(first | second, 1.0, 0.0)
    r = lax.broadcasted_iota(jnp.int32, (ts, ts), 0)
    c = lax.broadcasted_iota(jnp.int32, (ts, ts), 1)
    lower = jnp.where(c < r, 1.0, 0.0).astype(BF16)
    prefix = _dot(lower, onehot.astype(BF16)) + cnt_ref[...]
    rank_first = jnp.sum(jnp.where(first, prefix, 0.0), axis=1, keepdims=True)
    rank_second = jnp.sum(jnp.where(second, prefix, 0.0), axis=1, keepdims=True)
    cnt_ref[...] += jnp.sum(onehot, axis=0, keepdims=True)

    cols = (i1.astype(F32), i2.astype(F32), w_first, w_second, rank_first, rank_second)
    route = jnp.zeros(logits.shape, F32)
    for n, val in enumerate(cols):
        route = jnp.where(lane == n, val, route)
    route_ref[0] = route


def _route(x, g_pre, scale, shift, w_router):
    bsz, s, d = x.shape
    n_experts = w_router.shape[1]
    ts = ROUTE_ROWS
    row = lambda b, t: (b, t, 0)
    per_b = lambda b, t: (b, 0, 0)
    const = lambda b, t: (0, 0)
    return pl.pallas_call(
        functools.partial(_route_kernel, n_experts=n_experts),
        grid=(bsz, s // ts),
        in_specs=[
            pl.BlockSpec((1, ts, d), row),
            pl.BlockSpec((1, d), const),
            pl.BlockSpec((1, 1, d), per_b),
            pl.BlockSpec((1, 1, d), per_b),
            pl.BlockSpec((d, LANES), const),
        ],
        out_specs=[
            pl.BlockSpec((ts * d // LANES, LANES), lambda b, t: (b * (s // ts) + t, 0)),
            pl.BlockSpec((1, ts, LANES), row),
            pl.BlockSpec((1, LANES), const),
        ],
        out_shape=[
            jax.ShapeDtypeStruct((bsz * s * d // LANES, LANES), F32),
            jax.ShapeDtypeStruct((bsz, s, LANES), F32),
            jax.ShapeDtypeStruct((1, LANES), F32),
        ],
        compiler_params=pltpu.CompilerParams(
            dimension_semantics=("arbitrary", "arbitrary"), vmem_limit_bytes=VMEM_LIMIT),
        name="moe_route",
    )(x, g_pre.reshape(1, d), scale.reshape(bsz, 1, d), shift.reshape(bsz, 1, d),
      jnp.pad(w_router, ((0, 0), (0, LANES - n_experts))))


def _for_each_chunk(start, length, chunk, fn):
    whole = length // chunk

    def loop(c, _):
        fn(start + c * chunk, chunk)
        return 0

    lax.fori_loop(0, whole, loop, 0)
    cur = start + whole * chunk
    size = chunk // 2
    while size >= 1:
        bit = length & size

        @pl.when(bit != 0)
        def _(cur=cur, size=size):
            fn(cur, size)

        cur = cur + bit
        size //= 2


def _scatter_kernel(pos_ref, fill_ref, h_ref, xs_ref, zero_ref, sem, *, n_experts, tile_rows):
    ts = h_ref.shape[0]

    def issue(r, _):
        for slot in range(TOP_K):
            dst = pos_ref[0, 0, slot * ts + r]
            pltpu.make_async_copy(h_ref.at[r], xs_ref.at[dst], sem).start(priority=slot)
        return 0

    lax.fori_loop(0, ts, issue, 0, unroll=4)
    for slot in range(TOP_K):
        pltpu.make_async_copy(h_ref, xs_ref.at[pl.ds(0, ts)], sem).wait()

    @pl.when(pl.program_id(0) == pl.num_programs(0) - 1)
    def _():
        zero_ref[...] = jnp.zeros_like(zero_ref)
        occupied = fill_ref[2 * n_experts] * tile_rows
        spans = [(fill_ref[e], fill_ref[n_experts + e] - fill_ref[e]) for e in range(n_experts)]
        spans.append((occupied, xs_ref.shape[0] - occupied))

        def zero_copy(first, rows):
            return pltpu.make_async_copy(zero_ref.at[pl.ds(0, rows)],
                                         xs_ref.at[pl.ds(first, rows)], sem)

        for first, length in spans:
            _for_each_chunk(first, length, zero_ref.shape[0],
                            lambda row, rows: zero_copy(row, rows).start())
        for first, length in spans:
            _for_each_chunk(first, length, zero_ref.shape[0],
                            lambda row, rows: zero_copy(row, rows).wait())


def _scatter(h, pos, fill, n_rows):
    t = h.shape[0]
    ts = ROUTE_ROWS
    zero_rows = 256
    return pl.pallas_call(
        functools.partial(_scatter_kernel, n_experts=(fill.shape[0] - 1) // 2,
                          tile_rows=FFN_ROWS),
        grid=(t // ts,),
        in_specs=[
            pl.BlockSpec((1, 1, TOP_K * ts), lambda i: (i, 0, 0), memory_space=pltpu.SMEM),
            pl.BlockSpec(memory_space=pltpu.SMEM),
            pl.BlockSpec((ts,) + h.shape[1:], lambda i: (i, 0, 0)),
        ],
        out_specs=pl.BlockSpec(memory_space=pl.ANY),
        out_shape=jax.ShapeDtypeStruct((n_rows,) + h.shape[1:], F32),
        scratch_shapes=[pltpu.VMEM((zero_rows,) + h.shape[1:], F32), pltpu.SemaphoreType.DMA(())],
        compiler_params=pltpu.CompilerParams(
            dimension_semantics=("arbitrary",), vmem_limit_bytes=VMEM_LIMIT),
        name="moe_scatter",
    )(pos, fill, h)


def _experts_kernel(exp_ref, nact_ref, xs_ref, w1_ref, w3_ref, w2_ref, ys_ref, h_ref, acc_ref):
    i = pl.program_id(0)
    k = pl.program_id(1)

    @pl.when(i < nact_ref[0])
    def _():
        @pl.when(k == 0)
        def _():
            h_ref[...] = _from_token_major(xs_ref, h_ref.shape[1] // LANES).astype(BF16)

        h = h_ref[...]
        u = jax.nn.silu(_dot(h, w1_ref[0].astype(BF16))) * _dot(h, w3_ref[0].astype(BF16))
        part = _dot(u.astype(BF16), w2_ref[0].astype(BF16))

        @pl.when(k == 0)
        def _():
            acc_ref[...] = part

        @pl.when(k > 0)
        def _():
            acc_ref[...] += part

        @pl.when(k == pl.num_programs(1) - 1)
        def _():
            _to_token_major(ys_ref, acc_ref[...])

    @pl.when((i >= nact_ref[0]) & (k == 0))
    def _():
        ys_ref[...] = jnp.zeros_like(ys_ref)


def _experts(xs, tile_expert, n_active, w1, w3, w2):
    d, f = w1.shape[1:]
    n_rows = xs.shape[0] * LANES // d
    tm, tf = FFN_ROWS, FFN_COLS
    n_k = f // tf
    n_tiles = n_rows // tm

    def k_of(i, k, nact):
        return jnp.where(i < nact[0], k, n_k - 1)

    return pl.pallas_call(
        _experts_kernel,
        grid_spec=pltpu.PrefetchScalarGridSpec(
            num_scalar_prefetch=2,
            grid=(n_tiles, n_k),
            in_specs=[
                pl.BlockSpec((tm * d // LANES, LANES), lambda i, k, exp, nact: (i, 0)),
                pl.BlockSpec((1, d, tf), lambda i, k, exp, nact: (exp[i], 0, k_of(i, k, nact))),
                pl.BlockSpec((1, d, tf), lambda i, k, exp, nact: (exp[i], 0, k_of(i, k, nact))),
                pl.BlockSpec((1, tf, d), lambda i, k, exp, nact: (exp[i], k_of(i, k, nact), 0)),
            ],
            out_specs=pl.BlockSpec((tm * d // LANES, LANES), lambda i, k, exp, nact: (i, 0)),
            scratch_shapes=[pltpu.VMEM((tm, d), BF16), pltpu.VMEM((tm, d), F32)],
        ),
        out_shape=jax.ShapeDtypeStruct(xs.shape, F32),
        compiler_params=pltpu.CompilerParams(
            dimension_semantics=("arbitrary", "arbitrary"), vmem_limit_bytes=VMEM_LIMIT),
        name="moe_experts",
    )(tile_expert, n_active, xs, w1, w3, w2)


def _combine_kernel(pos_ref, route_ref, x_ref, gpost_ref, gate_ref, ys_ref, o_ref, buf0_ref,
                    buf1_ref, sem):
    ts = x_ref.shape[0]
    n = ys_ref.shape[1]
    bufs = (buf0_ref, buf1_ref)

    def issue(r, _):
        dst = pl.multiple_of(r * n, n)
        for slot in range(TOP_K):
            src = pos_ref[0, 0, slot * ts + r]
            pltpu.make_async_copy(ys_ref.at[src], bufs[slot].at[pl.ds(dst, n)],
                                  sem).start(priority=slot)
        return 0

    lax.fori_loop(0, ts, issue, 0, unroll=4)
    for slot in range(TOP_K):
        pltpu.make_async_copy(bufs[slot], bufs[slot], sem).wait()

    w0 = ROUTE_LANES.index("weight0")
    route = route_ref[...]
    y = (route[:, w0:w0 + 1] * _from_token_major(buf0_ref, n)
         + route[:, w0 + 1:w0 + 2] * _from_token_major(buf1_ref, n))
    o_ref[...] = x_ref[...] + gate_ref[0] * _rms(y, gpost_ref[...])


def _combine(ys, pos, route, x, g_post, gate):
    t, d = x.shape
    bsz = gate.shape[0]
    ts = ROUTE_ROWS
    steps_per_b = t // bsz // ts
    return pl.pallas_call(
        _combine_kernel,
        grid=(t // ts,),
        in_specs=[
            pl.BlockSpec((1, 1, TOP_K * ts), lambda i: (i, 0, 0), memory_space=pltpu.SMEM),
            pl.BlockSpec((ts, LANES), lambda i: (i, 0)),
            pl.BlockSpec((ts, d), lambda i: (i, 0)),
            pl.BlockSpec((1, d), lambda i: (0, 0)),
            pl.BlockSpec((1, 1, d), lambda i: (i // steps_per_b, 0, 0)),
            pl.BlockSpec(memory_space=pl.ANY),
        ],
        out_specs=pl.BlockSpec((ts, d), lambda i: (i, 0)),
        out_shape=jax.ShapeDtypeStruct((t, d), F32),
        scratch_shapes=[pltpu.VMEM((ts * d // LANES, LANES), F32)] * TOP_K
        + [pltpu.SemaphoreType.DMA(())],
        compiler_params=pltpu.CompilerParams(
            dimension_semantics=("arbitrary",), vmem_limit_bytes=VMEM_LIMIT),
        name="moe_combine",
    )(pos, route, x, g_post.reshape(1, d), gate, ys)


def _moe(x, g_pre, scale, shift, w_router, w1, w3, w2, g_post, gate):
    bsz, s, d = x.shape
    t = bsz * s
    n_experts = w_router.shape[1]
    ts, tm = ROUTE_ROWS, FFN_ROWS
    h, route, counts = _route(x, g_pre, scale, shift, w_router)
    route = route.reshape(t, LANES)

    lane_of = ROUTE_LANES.index
    expert = route[:, lane_of("expert0"):lane_of("expert0") + TOP_K].astype(jnp.int32)
    rank = route[:, lane_of("rank0"):lane_of("rank0") + TOP_K].astype(jnp.int32)
    counts = counts[0, :n_experts].astype(jnp.int32)
    tiles_per = (counts + tm - 1) // tm
    tile_end = jnp.cumsum(tiles_per)
    first_row = (tile_end - tiles_per) * tm
    n_active = tile_end[-1]
    n_tiles = TOP_K * t // tm + n_experts
    is_expert = expert[..., None] == jnp.arange(n_experts, dtype=jnp.int32)
    pos = jnp.sum(jnp.where(is_expert, first_row, 0), axis=-1) + rank
    pos = pos.reshape(t // ts, ts, TOP_K).transpose(0, 2, 1).reshape(t // ts, 1, TOP_K * ts)
    fill = jnp.concatenate([first_row + counts, tile_end * tm, n_active[None]]).astype(jnp.int32)
    idx = jnp.minimum(jnp.arange(n_tiles, dtype=jnp.int32), n_active - 1)
    tile_expert = jnp.sum(idx[:, None] >= tile_end[None, :], axis=1).astype(jnp.int32)

    token_major = (-1, d // LANES, LANES)
    xs = _scatter(h.reshape(token_major), pos, fill, n_tiles * tm)
    ys = _experts(xs.reshape(-1, LANES), tile_expert, n_active.reshape(1), w1, w3, w2)
    out = _combine(ys.reshape(token_major), pos, route, x.reshape(t, d), g_post,
                   gate.reshape(bsz, 1, d))
    return out.reshape(bsz, s, d)


def kernel(x, c, w_mod, b_mod, g_mix_pre, g_mix_post, g_ffn_pre, g_ffn_post, w_in, w_proj_sb,
           w_proj_moba, w_out, rel_bias, w1_dense, w3_dense, w2_dense, w_router, w1_moe, w3_moe,
           w2_moe):
    depth = w_mod.shape[0]
    d = x.shape[-1]
    qkv = 3 * W_BRANCH
    col = jnp.arange(w_in.shape[-1])
    is_q = (col < W_BRANCH) | ((col >= qkv) & (col < qkv + W_BRANCH))
    col_scale = jnp.where(is_q, HEAD_DIM ** -0.5, 1.0).astype(F32)

    mod = _modulation(c, w_mod, b_mod)
    bias_tiles = _moba_bias_tiles(rel_bias)
    for l in range(depth):
        sh_mix, sc_mix, gt_mix, sh_ffn, sc_ffn, gt_ffn = jnp.split(mod[l], N_MOD, axis=-1)
        proj = _inproj(x, g_mix_pre[l], sc_mix, sh_mix, (w_in[l] * col_scale).astype(BF16))
        o_a = _stick_breaking(proj, 0)
        o_b = _moba(proj, qkv // LANES, bias_tiles)
        x = _merge(o_a, o_b, proj, 2 * qkv // d, x, w_proj_sb[l].astype(BF16),
                   w_proj_moba[l].astype(BF16), w_out[l].astype(BF16), g_mix_post[l], gt_mix)
        j = l // 2
        if l % 2 == 0:
            x = _ffn(x, g_ffn_pre[l], sc_ffn, sh_ffn, w1_dense[j], w3_dense[j], w2_dense[j],
                     g_ffn_post[l], gt_ffn)
        else:
            x = _moe(x, g_ffn_pre[l], sc_ffn, sh_ffn, w_router[j], w1_moe[j], w3_moe[j],
                     w2_moe[j], g_ffn_post[l], gt_ffn)
    return x
```

```python
import functools
import math

import jax
import jax.numpy as jnp
from jax import lax
from jax.experimental import pallas as pl
from jax.experimental.pallas import tpu as pltpu

F32 = jnp.float32
BF16 = jnp.bfloat16

EPS = 1e-6
NEG_INF = -1e30
HEAD_DIM = 64
N_HEADS = 8
W_BRANCH = N_HEADS * HEAD_DIM
LANES = 128
SUBLANES = 8
HEADS_PER_TILE = LANES // HEAD_DIM
N_PAIRS = N_HEADS // HEADS_PER_TILE
MOBA_BLOCK = 256
MOBA_TOPK = 3
REL_BUCKETS = 32
REL_MAX_DIST = 128
N_MOD = 6
TOP_K = 2
ATT_BLOCK = 256
VMEM_LIMIT = 56 * 1024 * 1024


def _dot(a, b):
    return jnp.dot(a, b, preferred_element_type=F32)


def _dot_nt(a, b):
    return lax.dot_general(a, b, (((1,), (1,)), ((), ())), preferred_element_type=F32)


def _split_bf16(x):
    hi = x.astype(BF16)
    lo = (x - hi.astype(F32)).astype(BF16)
    return hi, lo


def _rms(x, g):
    return x * lax.rsqrt(jnp.mean(x * x, axis=-1, keepdims=True) + EPS) * g


def _mod_kernel(c_ref, w_ref, b_ref, o_ref):
    c = c_ref[...]
    cond = c * jax.nn.sigmoid(c)
    o_ref[0] = jnp.dot(cond, w_ref[0], preferred_element_type=F32,
                       precision=lax.Precision.HIGHEST) + b_ref[0]


def _modulation(c, w_mod, b_mod):
    depth, d, n = w_mod.shape
    bsz = c.shape[0]
    tn = 1536
    return pl.pallas_call(
        _mod_kernel,
        grid=(depth, n // tn),
        in_specs=[
            pl.BlockSpec((bsz, d), lambda l, j: (0, 0)),
            pl.BlockSpec((1, d, tn), lambda l, j: (l, 0, j)),
            pl.BlockSpec((1, 1, tn), lambda l, j: (l, 0, j)),
        ],
        out_specs=pl.BlockSpec((1, bsz, tn), lambda l, j: (l, 0, j)),
        out_shape=jax.ShapeDtypeStruct((depth, bsz, n), F32),
        compiler_params=pltpu.CompilerParams(
            dimension_semantics=("arbitrary", "arbitrary"), vmem_limit_bytes=VMEM_LIMIT),
        name="modulation",
    )(c, w_mod, b_mod.reshape(depth, 1, n))


def _inproj_kernel(x_ref, g_ref, sc_ref, sh_ref, w_ref, o_ref, *, tn):
    h = _rms(x_ref[0], g_ref[...]) * (1.0 + sc_ref[0]) + sh_ref[0]
    hb = h.astype(BF16)
    for n in range(w_ref.shape[1] // tn):
        o_ref[0, :, n * tn:(n + 1) * tn] = _dot(hb, w_ref[:, n * tn:(n + 1) * tn]).astype(BF16)


def _inproj(x, g, scale, shift, w):
    bsz, s, d = x.shape
    n = w.shape[1]
    tm = 512
    return pl.pallas_call(
        functools.partial(_inproj_kernel, tn=1024),
        grid=(bsz, s // tm),
        in_specs=[
            pl.BlockSpec((1, tm, d), lambda b, t: (b, t, 0)),
            pl.BlockSpec((1, d), lambda b, t: (0, 0)),
            pl.BlockSpec((1, 1, d), lambda b, t: (b, 0, 0)),
            pl.BlockSpec((1, 1, d), lambda b, t: (b, 0, 0)),
            pl.BlockSpec((d, n), lambda b, t: (0, 0), pipeline_mode=pl.Buffered(1)),
        ],
        out_specs=pl.BlockSpec((1, tm, n), lambda b, t: (b, t, 0)),
        out_shape=jax.ShapeDtypeStruct((bsz, s, n), BF16),
        compiler_params=pltpu.CompilerParams(
            dimension_semantics=("arbitrary", "arbitrary"), vmem_limit_bytes=VMEM_LIMIT),
        name="inproj",
    )(x, g.reshape(1, d), scale.reshape(bsz, 1, d), shift.reshape(bsz, 1, d), w)


SOFTPLUS_LINEAR = 40.0


def _softplus(z):
    return jnp.maximum(z, jnp.log(1.0 + jnp.exp(jnp.minimum(z, SOFTPLUS_LINEAR))))


def _sb_kernel(q_ref, k_ref, v_ref, o_ref, qs_ref, acc_ref, carry_ref):
    tq = ATT_BLOCK
    n_blocks = k_ref.shape[1] // tq
    rows_per_blk = HEADS_PER_TILE * tq
    lane = lax.broadcasted_iota(jnp.int32, (1, LANES), 1)
    row = lax.broadcasted_iota(jnp.int32, (tq, tq), 0)
    col = lax.broadcasted_iota(jnp.int32, (tq, tq), 1)
    upper = jnp.where(row > col, 1.0, 0.0).astype(BF16)
    causal = jnp.concatenate([col < row] * HEADS_PER_TILE, axis=0)
    q = q_ref[0]
    for i in range(n_blocks):
        qi = q[i * tq:(i + 1) * tq]
        for h in range(HEADS_PER_TILE):
            row0 = (HEADS_PER_TILE * i + h) * tq
            qs_ref[row0:row0 + tq, :] = jnp.where(lane // HEAD_DIM == h, qi, jnp.zeros_like(qi))

    for j in reversed(range(n_blocks)):
        row0 = j * rows_per_blk
        far0 = row0 + rows_per_blk
        has_far = j + 1 < n_blocks
        z = _dot_nt(qs_ref[row0:, :], k_ref[0, j * tq:(j + 1) * tq, :])
        sp = _softplus(z)
        sp_diag = jnp.where(causal, sp[:rows_per_blk], 0.0)
        sp = jnp.concatenate([sp_diag, sp[rows_per_blk:]], axis=0) if has_far else sp_diag
        logit = z - sp - _dot(sp.astype(BF16), upper)
        att_diag = jnp.where(causal, jnp.exp(logit[:rows_per_blk]), 0.0)
        if has_far:
            att_far = jnp.exp(logit[rows_per_blk:] - carry_ref[far0:, :])
            att = jnp.concatenate([att_diag, att_far], axis=0)
        else:
            att = att_diag
        part = _dot(att.astype(BF16), v_ref[0, j * tq:(j + 1) * tq, :])
        rs = jnp.sum(sp, axis=1, keepdims=True)
        acc_ref[row0:far0, :] = part[:rows_per_blk]
        carry_ref[row0:far0, :] = rs[:rows_per_blk]
        if has_far:
            acc_ref[far0:, :] += part[rows_per_blk:]
            carry_ref[far0:, :] += rs[rows_per_blk:]

    for i in range(n_blocks):
        blk = acc_ref[i * rows_per_blk:(i + 1) * rows_per_blk, :]
        o_ref[0, i * tq:(i + 1) * tq, :] = jnp.where(lane < HEAD_DIM, blk[:tq], blk[tq:]).astype(BF16)


def _stick_breaking(proj, col0):
    bsz, s, _ = proj.shape
    assert s % ATT_BLOCK == 0
    rows = HEADS_PER_TILE * s
    return pl.pallas_call(
        _sb_kernel,
        grid=(bsz, N_PAIRS),
        in_specs=[
            pl.BlockSpec((1, s, LANES), lambda b, p: (b, 0, col0 + p)),
            pl.BlockSpec((1, s, LANES), lambda b, p: (b, 0, col0 + N_PAIRS + p)),
            pl.BlockSpec((1, s, LANES), lambda b, p: (b, 0, col0 + 2 * N_PAIRS + p)),
        ],
        out_specs=pl.BlockSpec((1, s, LANES), lambda b, p: (b, 0, p)),
        out_shape=jax.ShapeDtypeStruct((bsz, s, W_BRANCH), BF16),
        scratch_shapes=[pltpu.VMEM((rows, LANES), BF16),
                        pltpu.VMEM((rows, LANES), F32),
                        pltpu.VMEM((rows, 1), F32)],
        compiler_params=pltpu.CompilerParams(
            dimension_semantics=("arbitrary", "arbitrary"), vmem_limit_bytes=VMEM_LIMIT),
        name="stick_breaking",
    )(proj, proj, proj)


def _rel_bucket(dist):
    max_exact = REL_BUCKETS // 2
    n = jnp.maximum(dist, 0)
    nf = jnp.maximum(n, 1).astype(F32)
    large = max_exact + (jnp.log(nf / max_exact) / math.log(REL_MAX_DIST / max_exact)
                         * (REL_BUCKETS - max_exact)).astype(jnp.int32)
    large = jnp.minimum(large, REL_BUCKETS - 1)
    return jnp.where(n < max_exact, n, large)


def _moba_bias_tiles(rel_bias):
    assert MOBA_BLOCK >= REL_MAX_DIST
    t = jnp.arange(MOBA_BLOCK)[:, None]
    s = jnp.arange(MOBA_BLOCK)[None, :]
    d_own = t - s
    rel = rel_bias.astype(F32) - rel_bias[REL_BUCKETS - 1].astype(F32)

    def lookup(dist):
        onehot = jax.nn.one_hot(_rel_bucket(dist), REL_BUCKETS, dtype=F32)
        return jnp.einsum("tsk,kh->hts", onehot, rel, precision=lax.Precision.HIGHEST)

    own = jnp.where((d_own >= 0)[None], lookup(d_own), NEG_INF)
    return jnp.stack([own, lookup(d_own + MOBA_BLOCK)], axis=1)


def _moba_kernel(q_ref, k_ref, v_ref, bias_ref, o_ref,
                 ka_ref, va_ref, qa_ref, m_ref, mrep_ref, accl_ref, *, n_blocks):
    tq = MOBA_BLOCK
    s_len = k_ref.shape[1]
    nb_pad = 16
    rows_per_blk = HEADS_PER_TILE * tq
    lane = lax.broadcasted_iota(jnp.int32, (1, LANES), 1)
    head_lanes = [lane // HEAD_DIM == h for h in range(HEADS_PER_TILE)]
    q = q_ref[0]
    k = k_ref[0]

    key_blk = lax.broadcasted_iota(jnp.int32, (s_len, LANES), 0) // tq
    ka_ref[:, :LANES] = k
    ka_ref[:, LANES:] = jnp.where(lane == key_blk, 1.0, 0.0).astype(BF16)
    va_ref[:, :LANES] = v_ref[0]
    va_ref[:, LANES:] = jnp.ones((s_len, LANES), BF16)

    r = lax.broadcasted_iota(jnp.int32, (nb_pad, s_len), 0)
    c = lax.broadcasted_iota(jnp.int32, (nb_pad, s_len), 1)
    pool = jnp.where(c // tq == r, 1.0 / tq, 0.0).astype(BF16)
    km_hi, km_lo = _split_bf16(_dot(pool, k))
    means = [jnp.where(hl, km, jnp.zeros_like(km)) for hl in head_lanes for km in (km_hi, km_lo)]
    gates = _dot_nt(jnp.concatenate(means, axis=0), q)
    own = c // tq
    masks = []
    for h in range(HEADS_PER_TILE):
        gate = (gates[2 * h * nb_pad:(2 * h + 1) * nb_pad]
                + gates[(2 * h + 1) * nb_pad:(2 * h + 2) * nb_pad])
        cnt = jnp.zeros((nb_pad, s_len), F32)
        for n in range(n_blocks):
            gn = gate[n:n + 1, :]
            beats = jnp.where(gn > gate, 1.0, jnp.where((gn == gate) & (n < r), 1.0, 0.0))
            cnt = cnt + jnp.where(n < own, beats, 0.0)
        keep = (r == own) | ((r < own) & (cnt < MOBA_TOPK))
        masks += [jnp.where(keep, 0.0, NEG_INF).astype(BF16),
                  jnp.zeros((LANES - nb_pad, s_len), BF16)]
    masks = jnp.concatenate(masks, axis=0)
    er = lax.broadcasted_iota(jnp.int32, (tq, tq), 0)
    ec = lax.broadcasted_iota(jnp.int32, (tq, tq), 1)
    eye = jnp.where(er == ec, 1.0, 0.0).astype(BF16)
    for i in range(n_blocks):
        qi = q[i * tq:(i + 1) * tq]
        mask_rows = _dot_nt(eye, masks[:, i * tq:(i + 1) * tq]).astype(BF16)
        for h in range(HEADS_PER_TILE):
            row0 = (HEADS_PER_TILE * i + h) * tq
            qa_ref[row0:row0 + tq, :LANES] = jnp.where(head_lanes[h], qi, jnp.zeros_like(qi))
            qa_ref[row0:row0 + tq, LANES:] = mask_rows[:, h * LANES:(h + 1) * LANES]

    bias = [jnp.concatenate([bias_ref[h, which] for h in range(HEADS_PER_TILE)], axis=0)
            for which in range(2)]
    def scores(j):
        row0 = j * rows_per_blk
        s = _dot_nt(qa_ref[row0:, :], ka_ref[j * tq:(j + 1) * tq, :])
        parts = [s[:rows_per_blk] + bias[0]]
        if j + 1 < n_blocks:
            parts.append(s[rows_per_blk:2 * rows_per_blk] + bias[1])
        if j + 2 < n_blocks:
            parts.append(s[2 * rows_per_blk:])
        return jnp.concatenate(parts, axis=0)

    for j in range(n_blocks):
        row0 = j * rows_per_blk
        m = jnp.max(scores(j), axis=1, keepdims=True)
        if j == 0:
            m_ref[...] = m
        else:
            m_ref[row0:, :] = jnp.maximum(m_ref[row0:, :], m)
    mrep_ref[...] = jnp.broadcast_to(m_ref[...], mrep_ref.shape)
    for j in range(n_blocks):
        row0 = j * rows_per_blk
        m = mrep_ref[row0:, :]
        pr = jnp.exp(scores(j) - jnp.concatenate([m] * (tq // LANES), axis=1))
        ov = _dot(pr.astype(BF16), va_ref[j * tq:(j + 1) * tq, :])
        if j == 0:
            accl_ref[...] = ov
        else:
            accl_ref[row0:, :] += ov

    for i in range(n_blocks):
        blk = accl_ref[i * rows_per_blk:(i + 1) * rows_per_blk, :]
        out = blk[:, :LANES] / blk[:, LANES:]
        o_ref[0, i * tq:(i + 1) * tq, :] = jnp.where(lane < HEAD_DIM, out[:tq], out[tq:]).astype(BF16)


def _moba(proj, col0, bias_tiles):
    bsz, s, _ = proj.shape
    tq = MOBA_BLOCK
    assert s % tq == 0
    n_blocks = s // tq
    assert n_blocks <= 16
    rows = HEADS_PER_TILE * s
    return pl.pallas_call(
        functools.partial(_moba_kernel, n_blocks=n_blocks),
        grid=(bsz, N_PAIRS),
        in_specs=[
            pl.BlockSpec((1, s, LANES), lambda b, p: (b, 0, col0 + p)),
            pl.BlockSpec((1, s, LANES), lambda b, p: (b, 0, col0 + N_PAIRS + p)),
            pl.BlockSpec((1, s, LANES), lambda b, p: (b, 0, col0 + 2 * N_PAIRS + p)),
            pl.BlockSpec((HEADS_PER_TILE, 2, tq, tq), lambda b, p: (p, 0, 0, 0)),
        ],
        out_specs=pl.BlockSpec((1, s, LANES), lambda b, p: (b, 0, p)),
        out_shape=jax.ShapeDtypeStruct((bsz, s, W_BRANCH), BF16),
        scratch_shapes=[
            pltpu.VMEM((s, 2 * LANES), BF16),
            pltpu.VMEM((s, 2 * LANES), BF16),
            pltpu.VMEM((rows, 2 * LANES), BF16),
            pltpu.VMEM((rows, 1), F32),
            pltpu.VMEM((rows, LANES), F32),
            pltpu.VMEM((rows, 2 * LANES), F32),
        ],
        compiler_params=pltpu.CompilerParams(
            dimension_semantics=("arbitrary", "arbitrary"), vmem_limit_bytes=VMEM_LIMIT),
        name="moba",
    )(proj, proj, proj, bias_tiles)


def _merge_kernel(oa_ref, ob_ref, ga_ref, gb_ref, x_ref, wa_ref, wb_ref, wo_ref, g_ref, gate_ref,
                  o_ref):
    pa = _dot(oa_ref[0], wa_ref[...])
    pb = _dot(ob_ref[0], wb_ref[...])
    merged = (jax.nn.sigmoid(ga_ref[0].astype(F32)) * pa
              + jax.nn.sigmoid(gb_ref[0].astype(F32)) * pb)
    y = _dot(merged.astype(BF16), wo_ref[...])
    o_ref[0] = x_ref[0] + gate_ref[0] * _rms(y, g_ref[...])


def _merge(o_a, o_b, proj, gate_col, x, w_a, w_b, w_o, g_post, gate):
    bsz, s, d = x.shape
    tm = 512
    const = lambda b, t: (0, 0)
    return pl.pallas_call(
        _merge_kernel,
        grid=(bsz, s // tm),
        in_specs=[
            pl.BlockSpec((1, tm, W_BRANCH), lambda b, t: (b, t, 0)),
            pl.BlockSpec((1, tm, W_BRANCH), lambda b, t: (b, t, 0)),
            pl.BlockSpec((1, tm, d), lambda b, t: (b, t, gate_col)),
            pl.BlockSpec((1, tm, d), lambda b, t: (b, t, gate_col + 1)),
            pl.BlockSpec((1, tm, d), lambda b, t: (b, t, 0)),
            pl.BlockSpec((W_BRANCH, d), const),
            pl.BlockSpec((W_BRANCH, d), const),
            pl.BlockSpec((d, d), const),
            pl.BlockSpec((1, d), const),
            pl.BlockSpec((1, 1, d), lambda b, t: (b, 0, 0)),
        ],
        out_specs=pl.BlockSpec((1, tm, d), lambda b, t: (b, t, 0)),
        out_shape=jax.ShapeDtypeStruct((bsz, s, d), F32),
        compiler_params=pltpu.CompilerParams(
            dimension_semantics=("arbitrary", "arbitrary"), vmem_limit_bytes=VMEM_LIMIT),
        name="merge",
    )(o_a, o_b, proj, proj, x, w_a, w_b, w_o, g_post.reshape(1, d), gate.reshape(bsz, 1, d))


FFN_ROWS = 1024
FFN_COLS = 512


def _ffn_kernel(x_ref, gpre_ref, sc_ref, sh_ref, w1_ref, w3_ref, w2_ref, gpost_ref, gate_ref,
                o_ref, h_ref, acc_ref):
    k = pl.program_id(2)

    @pl.when(k == 0)
    def _():
        h = _rms(x_ref[0], gpre_ref[...]) * (1.0 + sc_ref[0]) + sh_ref[0]
        h_ref[...] = h.astype(BF16)
        acc_ref[...] = jnp.zeros_like(acc_ref)

    h = h_ref[...]
    u = jax.nn.silu(_dot(h, w1_ref[...].astype(BF16))) * _dot(h, w3_ref[...].astype(BF16))
    acc_ref[...] += _dot(u.astype(BF16), w2_ref[...].astype(BF16))

    @pl.when(k == pl.num_programs(2) - 1)
    def _():
        o_ref[0] = x_ref[0] + gate_ref[0] * _rms(acc_ref[...], gpost_ref[...])


def _ffn(x, g_pre, scale, shift, w1, w3, w2, g_post, gate):
    bsz, s, d = x.shape
    f = w1.shape[1]
    tm, tf = FFN_ROWS, FFN_COLS
    row = lambda b, t, k: (b, t, 0)
    per_b = lambda b, t, k: (b, 0, 0)
    const = lambda b, t, k: (0, 0)
    return pl.pallas_call(
        _ffn_kernel,
        grid=(bsz, s // tm, f // tf),
        in_specs=[
            pl.BlockSpec((1, tm, d), row),
            pl.BlockSpec((1, d), const),
            pl.BlockSpec((1, 1, d), per_b),
            pl.BlockSpec((1, 1, d), per_b),
            pl.BlockSpec((d, tf), lambda b, t, k: (0, k)),
            pl.BlockSpec((d, tf), lambda b, t, k: (0, k)),
            pl.BlockSpec((tf, d), lambda b, t, k: (k, 0)),
            pl.BlockSpec((1, d), const),
            pl.BlockSpec((1, 1, d), per_b),
        ],
        out_specs=pl.BlockSpec((1, tm, d), row),
        out_shape=jax.ShapeDtypeStruct((bsz, s, d), F32),
        scratch_shapes=[pltpu.VMEM((tm, d), BF16), pltpu.VMEM((tm, d), F32)],
        compiler_params=pltpu.CompilerParams(
            dimension_semantics=("arbitrary",) * 3, vmem_limit_bytes=VMEM_LIMIT),
        name="ffn_dense",
    )(x, g_pre.reshape(1, d), scale.reshape(bsz, 1, d), shift.reshape(bsz, 1, d), w1, w3, w2,
      g_post.reshape(1, d), gate.reshape(bsz, 1, d))


ROUTE_ROWS = 512


def _to_token_major(ref, val):
    rows, n = val.shape[0], val.shape[1] // LANES
    for s in range(n):
        ref[pl.ds(s, rows, stride=n), :] = val[:, s * LANES:(s + 1) * LANES]


def _from_token_major(ref, n):
    rows = ref.shape[0] // n
    return jnp.concatenate([ref[pl.ds(s, rows, stride=n), :] for s in range(n)], axis=1)
ROUTE_LANES = ("expert0", "expert1", "weight0", "weight1", "rank0", "rank1")


def _route_kernel(x_ref, gpre_ref, sc_ref, sh_ref, wr_ref, h_ref, route_ref, cnt_ref, *, n_experts):
    first_step = (pl.program_id(0) == 0) & (pl.program_id(1) == 0)

    @pl.when(first_step)
    def _():
        cnt_ref[...] = jnp.zeros_like(cnt_ref)

    h = _rms(x_ref[0], gpre_ref[...]) * (1.0 + sc_ref[0]) + sh_ref[0]
    _to_token_major(h_ref, h)
    h_hi, h_lo = _split_bf16(h)
    w_hi, w_lo = _split_bf16(wr_ref[...])
    logits = _dot(h_hi, w_hi) + (_dot(h_hi, w_lo) + _dot(h_lo, w_hi))
    ts = logits.shape[0]
    lane = lax.broadcasted_iota(jnp.int32, logits.shape, 1)
    lg = jnp.where(lane < n_experts, logits, -jnp.inf)
    m1 = jnp.max(lg, axis=1, keepdims=True)
    i1 = jnp.min(jnp.where(lg == m1, lane, LANES), axis=1, keepdims=True)
    first = lane == i1
    lg2 = jnp.where(first, -jnp.inf, lg)
    m2 = jnp.max(lg2, axis=1, keepdims=True)
    i2 = jnp.min(jnp.where(lg2 == m2, lane, LANES), axis=1, keepdims=True)
    second = lane == i2
    e2 = jnp.exp(m2 - m1)
    w_first = 1.0 / (1.0 + e2)
    w_second = e2 * w_first

    onehot = jnp.where(first | second, 1.0, 0.0)
    r = lax.broadcasted_iota(jnp.int32, (ts, ts), 0)
    c = lax.broadcasted_iota(jnp.int32, (ts, ts), 1)
    lower = jnp.where(c < r, 1.0, 0.0).astype(BF16)
    prefix = _dot(lower, onehot.astype(BF16)) + cnt_ref[...]
    rank_first = jnp.sum(jnp.where(first, prefix, 0.0), axis=1, keepdims=True)
    rank_second = jnp.sum(jnp.where(second, prefix, 0.0), axis=1, keepdims=True)
    cnt_ref[...] += jnp.sum(onehot, axis=0, keepdims=True)

    cols = (i1.astype(F32), i2.astype(F32), w_first, w_second, rank_first, rank_second)
    route = jnp.zeros(logits.shape, F32)
    for n, val in enumerate(cols):
        route = jnp.where(lane == n, val, route)
    route_ref[0] = route


def _route(x, g_pre, scale, shift, w_router):
    bsz, s, d = x.shape
    n_experts = w_router.shape[1]
    ts = ROUTE_ROWS
    row = lambda b, t: (b, t, 0)
    per_b = lambda b, t: (b, 0, 0)
    const = lambda b, t: (0, 0)
    return pl.pallas_call(
        functools.partial(_route_kernel, n_experts=n_experts),
        grid=(bsz, s // ts),
        in_specs=[
            pl.BlockSpec((1, ts, d), row),
            pl.BlockSpec((1, d), const),
            pl.BlockSpec((1, 1, d), per_b),
            pl.BlockSpec((1, 1, d), per_b),
            pl.BlockSpec((d, LANES), const),
        ],
        out_specs=[
            pl.BlockSpec((ts * d // LANES, LANES), lambda b, t: (b * (s // ts) + t, 0)),
            pl.BlockSpec((1, ts, LANES), row),
            pl.BlockSpec((1, LANES), const),
        ],
        out_shape=[
            jax.ShapeDtypeStruct((bsz * s * d // LANES, LANES), F32),
            jax.ShapeDtypeStruct((bsz, s, LANES), F32),
            jax.ShapeDtypeStruct((1, LANES), F32),
        ],
        compiler_params=pltpu.CompilerParams(
            dimension_semantics=("arbitrary", "arbitrary"), vmem_limit_bytes=VMEM_LIMIT),
        name="moe_route",
    )(x, g_pre.reshape(1, d), scale.reshape(bsz, 1, d), shift.reshape(bsz, 1, d),
      jnp.pad(w_router, ((0, 0), (0, LANES - n_experts))))


def _for_each_chunk(start, length, chunk, fn):
    whole = length // chunk

    def loop(c, _):
        fn(start + c * chunk, chunk)
        return 0

    lax.fori_loop(0, whole, loop, 0)
    cur = start + whole * chunk
    size = chunk // 2
    while size >= 1:
        bit = length & size

        @pl.when(bit != 0)
        def _(cur=cur, size=size):
            fn(cur, size)

        cur = cur + bit
        size //= 2


def _scatter_kernel(pos_ref, fill_ref, h_ref, xs_ref, zero_ref, sem, *, n_experts, tile_rows):
    ts = h_ref.shape[0]

    def issue(r, _):
        for slot in range(TOP_K):
            dst = pos_ref[0, 0, slot * ts + r]
            pltpu.make_async_copy(h_ref.at[r], xs_ref.at[dst], sem).start(priority=slot)
        return 0

    lax.fori_loop(0, ts, issue, 0, unroll=4)
    for slot in range(TOP_K):
        pltpu.make_async_copy(h_ref, xs_ref.at[pl.ds(0, ts)], sem).wait()

    @pl.when(pl.program_id(0) == pl.num_programs(0) - 1)
    def _():
        zero_ref[...] = jnp.zeros_like(zero_ref)
        occupied = fill_ref[2 * n_experts] * tile_rows
        spans = [(fill_ref[e], fill_ref[n_experts + e] - fill_ref[e]) for e in range(n_experts)]
        spans.append((occupied, xs_ref.shape[0] - occupied))

        def zero_copy(first, rows):
            return pltpu.make_async_copy(zero_ref.at[pl.ds(0, rows)],
                                         xs_ref.at[pl.ds(first, rows)], sem)

        for first, length in spans:
            _for_each_chunk(first, length, zero_ref.shape[0],
                            lambda row, rows: zero_copy(row, rows).start())
        for first, length in spans:
            _for_each_chunk(first, length, zero_ref.shape[0],
                            lambda row, rows: zero_copy(row, rows).wait())


def _scatter(h, pos, fill, n_rows):
    t = h.shape[0]
    ts = ROUTE_ROWS
    zero_rows = 256
    return pl.pallas_call(
        functools.partial(_scatter_kernel, n_experts=(fill.shape[0] - 1) // 2,
                          tile_rows=FFN_ROWS),
        grid=(t // ts,),
        in_specs=[
            pl.BlockSpec((1, 1, TOP_K * ts), lambda i: (i, 0, 0), memory_space=pltpu.SMEM),
            pl.BlockSpec(memory_space=pltpu.SMEM),
            pl.BlockSpec((ts,) + h.shape[1:], lambda i: (i, 0, 0)),
        ],
        out_specs=pl.BlockSpec(memory_space=pl.ANY),
        out_shape=jax.ShapeDtypeStruct((n_rows,) + h.shape[1:], F32),
        scratch_shapes=[pltpu.VMEM((zero_rows,) + h.shape[1:], F32), pltpu.SemaphoreType.DMA(())],
        compiler_params=pltpu.CompilerParams(
            dimension_semantics=("arbitrary",), vmem_limit_bytes=VMEM_LIMIT),
        name="moe_scatter",
    )(pos, fill, h)


def _experts_kernel(exp_ref, nact_ref, xs_ref, w1_ref, w3_ref, w2_ref, ys_ref, h_ref, acc_ref):
    i = pl.program_id(0)
    k = pl.program_id(1)

    @pl.when(i < nact_ref[0])
    def _():
        @pl.when(k == 0)
        def _():
            h_ref[...] = _from_token_major(xs_ref, h_ref.shape[1] // LANES).astype(BF16)

        h = h_ref[...]
        u = jax.nn.silu(_dot(h, w1_ref[0].astype(BF16))) * _dot(h, w3_ref[0].astype(BF16))
        part = _dot(u.astype(BF16), w2_ref[0].astype(BF16))

        @pl.when(k == 0)
        def _():
            acc_ref[...] = part

        @pl.when(k > 0)
        def _():
            acc_ref[...] += part

        @pl.when(k == pl.num_programs(1) - 1)
        def _():
            _to_token_major(ys_ref, acc_ref[...])

    @pl.when((i >= nact_ref[0]) & (k == 0))
    def _():
        ys_ref[...] = jnp.zeros_like(ys_ref)


def _experts(xs, tile_expert, n_active, w1, w3, w2):
    d, f = w1.shape[1:]
    n_rows = xs.shape[0] * LANES // d
    tm, tf = FFN_ROWS, FFN_COLS
    n_k = f // tf
    n_tiles = n_rows // tm

    def k_of(i, k, nact):
        return jnp.where(i < nact[0], k, n_k - 1)

    return pl.pallas_call(
        _experts_kernel,
        grid_spec=pltpu.PrefetchScalarGridSpec(
            num_scalar_prefetch=2,
            grid=(n_tiles, n_k),
            in_specs=[
                pl.BlockSpec((tm * d // LANES, LANES), lambda i, k, exp, nact: (i, 0)),
                pl.BlockSpec((1, d, tf), lambda i, k, exp, nact: (exp[i], 0, k_of(i, k, nact))),
                pl.BlockSpec((1, d, tf), lambda i, k, exp, nact: (exp[i], 0, k_of(i, k, nact))),
                pl.BlockSpec((1, tf, d), lambda i, k, exp, nact: (exp[i], k_of(i, k, nact), 0)),
            ],
            out_specs=pl.BlockSpec((tm * d // LANES, LANES), lambda i, k, exp, nact: (i, 0)),
            scratch_shapes=[pltpu.VMEM((tm, d), BF16), pltpu.VMEM((tm, d), F32)],
        ),
        out_shape=jax.ShapeDtypeStruct(xs.shape, F32),
        compiler_params=pltpu.CompilerParams(
            dimension_semantics=("arbitrary", "arbitrary"), vmem_limit_bytes=VMEM_LIMIT),
        name="moe_experts",
    )(tile_expert, n_active, xs, w1, w3, w2)


def _combine_kernel(pos_ref, route_ref, x_ref, gpost_ref, gate_ref, ys_ref, o_ref, buf0_ref,
                    buf1_ref, sem):
    ts = x_ref.shape[0]
    n = ys_ref.shape[1]
    bufs = (buf0_ref, buf1_ref)

    def issue(r, _):
        dst = pl.multiple_of(r * n, n)
        for slot in range(TOP_K):
            src = pos_ref[0, 0, slot * ts + r]
            pltpu.make_async_copy(ys_ref.at[src], bufs[slot].at[pl.ds(dst, n)],
                                  sem).start(priority=slot)
        return 0

    lax.fori_loop(0, ts, issue, 0, unroll=4)
    for slot in range(TOP_K):
        pltpu.make_async_copy(bufs[slot], bufs[slot], sem).wait()

    w0 = ROUTE_LANES.index("weight0")
    route = route_ref[...]
    y = (route[:, w0:w0 + 1] * _from_token_major(buf0_ref, n)
         + route[:, w0 + 1:w0 + 2] * _from_token_major(buf1_ref, n))
    o_ref[...] = x_ref[...] + gate_ref[0] * _rms(y, gpost_ref[...])


def _combine(ys, pos, route, x, g_post, gate):
    t, d = x.shape
    bsz = gate.shape[0]
    ts = ROUTE_ROWS
    steps_per_b = t // bsz // ts
    return pl.pallas_call(
        _combine_kernel,
        grid=(t // ts,),
        in_specs=[
            pl.BlockSpec((1, 1, TOP_K * ts), lambda i: (i, 0, 0), memory_space=pltpu.SMEM),
            pl.BlockSpec((ts, LANES), lambda i: (i, 0)),
            pl.BlockSpec((ts, d), lambda i: (i, 0)),
            pl.BlockSpec((1, d), lambda i: (0, 0)),
            pl.BlockSpec((1, 1, d), lambda i: (i // steps_per_b, 0, 0)),
            pl.BlockSpec(memory_space=pl.ANY),
        ],
        out_specs=pl.BlockSpec((ts, d), lambda i: (i, 0)),
        out_shape=jax.ShapeDtypeStruct((t, d), F32),
        scratch_shapes=[pltpu.VMEM((ts * d // LANES, LANES), F32)] * TOP_K
        + [pltpu.SemaphoreType.DMA(())],
        compiler_params=pltpu.CompilerParams(
            dimension_semantics=("arbitrary",), vmem_limit_bytes=VMEM_LIMIT),
        name="moe_combine",
    )(pos, route, x, g_post.reshape(1, d), gate, ys)


def _moe(x, g_pre, scale, shift, w_router, w1, w3, w2, g_post, gate):
    bsz, s, d = x.shape
    t = bsz * s
    n_experts = w_router.shape[1]
    ts, tm = ROUTE_ROWS, FFN_ROWS
    h, route, counts = _route(x, g_pre, scale, shift, w_router)
    route = route.reshape(t, LANES)

    lane_of = ROUTE_LANES.index
    expert = route[:, lane_of("expert0"):lane_of("expert0") + TOP_K].astype(jnp.int32)
    rank = route[:, lane_of("rank0"):lane_of("rank0") + TOP_K].astype(jnp.int32)
    counts = counts[0, :n_experts].astype(jnp.int32)
    tiles_per = (counts + tm - 1) // tm
    tile_end = jnp.cumsum(tiles_per)
    first_row = (tile_end - tiles_per) * tm
    n_active = tile_end[-1]
    n_tiles = TOP_K * t // tm + n_experts
    is_expert = expert[..., None] == jnp.arange(n_experts, dtype=jnp.int32)
    pos = jnp.sum(jnp.where(is_expert, first_row, 0), axis=-1) + rank
    pos = pos.reshape(t // ts, ts, TOP_K).transpose(0, 2, 1).reshape(t // ts, 1, TOP_K * ts)
    fill = jnp.concatenate([first_row + counts, tile_end * tm, n_active[None]]).astype(jnp.int32)
    idx = jnp.minimum(jnp.arange(n_tiles, dtype=jnp.int32), n_active - 1)
    tile_expert = jnp.sum(idx[:, None] >= tile_end[None, :], axis=1).astype(jnp.int32)

    token_major = (-1, d // LANES, LANES)
    xs = _scatter(h.reshape(token_major), pos, fill, n_tiles * tm)
    ys = _experts(xs.reshape(-1, LANES), tile_expert, n_active.reshape(1), w1, w3, w2)
    out = _combine(ys.reshape(token_major), pos, route, x.reshape(t, d), g_post,
                   gate.reshape(bsz, 1, d))
    return out.reshape(bsz, s, d)


def kernel(x, c, w_mod, b_mod, g_mix_pre, g_mix_post, g_ffn_pre, g_ffn_post, w_in, w_proj_sb,
           w_proj_moba, w_out, rel_bias, w1_dense, w3_dense, w2_dense, w_router, w1_moe, w3_moe,
           w2_moe):
    depth = w_mod.shape[0]
    d = x.shape[-1]
    qkv = 3 * W_BRANCH
    col = jnp.arange(w_in.shape[-1])
    is_q = (col < W_BRANCH) | ((col >= qkv) & (col < qkv + W_BRANCH))
    col_scale = jnp.where(is_q, HEAD_DIM ** -0.5, 1.0).astype(F32)

    mod = _modulation(c, w_mod, b_mod)
    bias_tiles = _moba_bias_tiles(rel_bias)
    for l in range(depth):
        sh_mix, sc_mix, gt_mix, sh_ffn, sc_ffn, gt_ffn = jnp.split(mod[l], N_MOD, axis=-1)
        proj = _inproj(x, g_mix_pre[l], sc_mix, sh_mix, (w_in[l] * col_scale).astype(BF16))
        o_a = _stick_breaking(proj, 0)
        o_b = _moba(proj, qkv // LANES, bias_tiles)
        x = _merge(o_a, o_b, proj, 2 * qkv // d, x, w_proj_sb[l].astype(BF16),
                   w_proj_moba[l].astype(BF16), w_out[l].astype(BF16), g_mix_post[l], gt_mix)
        j = l // 2
        if l % 2 == 0:
            x = _ffn(x, g_ffn_pre[l], sc_ffn, sh_ffn, w1_dense[j], w3_dense[j], w2_dense[j],
                     g_ffn_post[l], gt_ffn)
        else:
            x = _moe(x, g_ffn_pre[l], sc_ffn, sh_ffn, w_router[j], w1_moe[j], w3_moe[j],
                     w2_moe[j], g_ffn_post[l], gt_ffn)
    return x
```

```python
import functools
import math

import jax
import jax.numpy as jnp
from jax import lax
from jax.experimental import pallas as pl
from jax.experimental.pallas import tpu as pltpu

F32 = jnp.float32
BF16 = jnp.bfloat16

EPS = 1e-6
NEG_INF = -1e30
HEAD_DIM = 64
N_HEADS = 8
W_BRANCH = N_HEADS * HEAD_DIM
LANES = 128
SUBLANES = 8
HEADS_PER_TILE = LANES // HEAD_DIM
N_PAIRS = N_HEADS // HEADS_PER_TILE
MOBA_BLOCK = 256
MOBA_TOPK = 3
REL_BUCKETS = 32
REL_MAX_DIST = 128
N_MOD = 6
TOP_K = 2
ATT_BLOCK = 256
VMEM_LIMIT = 56 * 1024 * 1024


def _dot(a, b):
    return jnp.dot(a, b, preferred_element_type=F32)


def _dot_nt(a, b):
    return lax.dot_general(a, b, (((1,), (1,)), ((), ())), preferred_element_type=F32)


def _split_bf16(x):
    hi = x.astype(BF16)
    lo = (x - hi.astype(F32)).astype(BF16)
    return hi, lo


def _rms(x, g):
    return x * lax.rsqrt(jnp.mean(x * x, axis=-1, keepdims=True) + EPS) * g


def _mod_kernel(c_ref, w_ref, b_ref, o_ref):
    c = c_ref[...]
    cond = c * jax.nn.sigmoid(c)
    o_ref[0] = jnp.dot(cond, w_ref[0], preferred_element_type=F32,
                       precision=lax.Precision.HIGHEST) + b_ref[0]


def _modulation(c, w_mod, b_mod):
    depth, d, n = w_mod.shape
    bsz = c.shape[0]
    tn = 1536
    return pl.pallas_call(
        _mod_kernel,
        grid=(depth, n // tn),
        in_specs=[
            pl.BlockSpec((bsz, d), lambda l, j: (0, 0)),
            pl.BlockSpec((1, d, tn), lambda l, j: (l, 0, j)),
            pl.BlockSpec((1, 1, tn), lambda l, j: (l, 0, j)),
        ],
        out_specs=pl.BlockSpec((1, bsz, tn), lambda l, j: (l, 0, j)),
        out_shape=jax.ShapeDtypeStruct((depth, bsz, n), F32),
        compiler_params=pltpu.CompilerParams(
            dimension_semantics=("arbitrary", "arbitrary"), vmem_limit_bytes=VMEM_LIMIT),
        name="modulation",
    )(c, w_mod, b_mod.reshape(depth, 1, n))


def _inproj_kernel(x_ref, g_ref, sc_ref, sh_ref, w_ref, o_ref, *, tn):
    h = _rms(x_ref[0], g_ref[...]) * (1.0 + sc_ref[0]) + sh_ref[0]
    hb = h.astype(BF16)
    for n in range(w_ref.shape[1] // tn):
        o_ref[0, :, n * tn:(n + 1) * tn] = _dot(hb, w_ref[:, n * tn:(n + 1) * tn]).astype(BF16)


def _inproj(x, g, scale, shift, w):
    bsz, s, d = x.shape
    n = w.shape[1]
    tm = 512
    return pl.pallas_call(
        functools.partial(_inproj_kernel, tn=1024),
        grid=(bsz, s // tm),
        in_specs=[
            pl.BlockSpec((1, tm, d), lambda b, t: (b, t, 0)),
            pl.BlockSpec((1, d), lambda b, t: (0, 0)),
            pl.BlockSpec((1, 1, d), lambda b, t: (b, 0, 0)),
            pl.BlockSpec((1, 1, d), lambda b, t: (b, 0, 0)),
            pl.BlockSpec((d, n), lambda b, t: (0, 0), pipeline_mode=pl.Buffered(1)),
        ],
        out_specs=pl.BlockSpec((1, tm, n), lambda b, t: (b, t, 0)),
        out_shape=jax.ShapeDtypeStruct((bsz, s, n), BF16),
        compiler_params=pltpu.CompilerParams(
            dimension_semantics=("arbitrary", "arbitrary"), vmem_limit_bytes=VMEM_LIMIT),
        name="inproj",
    )(x, g.reshape(1, d), scale.reshape(bsz, 1, d), shift.reshape(bsz, 1, d), w)


SOFTPLUS_LINEAR = 40.0


def _softplus(z):
    return jnp.maximum(z, jnp.log(1.0 + jnp.exp(jnp.minimum(z, SOFTPLUS_LINEAR))))


def _sb_kernel(q_ref, k_ref, v_ref, o_ref, qs_ref, acc_ref, carry_ref):
    tq = ATT_BLOCK
    n_blocks = k_ref.shape[1] // tq
    rows_per_blk = HEADS_PER_TILE * tq
    lane = lax.broadcasted_iota(jnp.int32, (1, LANES), 1)
    row = lax.broadcasted_iota(jnp.int32, (tq, tq), 0)
    col = lax.broadcasted_iota(jnp.int32, (tq, tq), 1)
    upper = jnp.where(row > col, 1.0, 0.0).astype(BF16)
    causal = jnp.concatenate([col < row] * HEADS_PER_TILE, axis=0)
    q = q_ref[0]
    for i in range(n_blocks):
        qi = q[i * tq:(i + 1) * tq]
        for h in range(HEADS_PER_TILE):
            row0 = (HEADS_PER_TILE * i + h) * tq
            qs_ref[row0:row0 + tq, :] = jnp.where(lane // HEAD_DIM == h, qi, jnp.zeros_like(qi))

    for j in reversed(range(n_blocks)):
        row0 = j * rows_per_blk
        far0 = row0 + rows_per_blk
        has_far = j + 1 < n_blocks
        z = _dot_nt(qs_ref[row0:, :], k_ref[0, j * tq:(j + 1) * tq, :])
        sp = _softplus(z)
        sp_diag = jnp.where(causal, sp[:rows_per_blk], 0.0)
        sp = jnp.concatenate([sp_diag, sp[rows_per_blk:]], axis=0) if has_far else sp_diag
        logit = z - sp - _dot(sp.astype(BF16), upper)
        att_diag = jnp.where(causal, jnp.exp(logit[:rows_per_blk]), 0.0)
        if has_far:
            att_far = jnp.exp(logit[rows_per_blk:] - carry_ref[far0:, :])
            att = jnp.concatenate([att_diag, att_far], axis=0)
        else:
            att = att_diag
        part = _dot(att.astype(BF16), v_ref[0, j * tq:(j + 1) * tq, :])
        rs = jnp.sum(sp, axis=1, keepdims=True)
        acc_ref[row0:far0, :] = part[:rows_per_blk]
        carry_ref[row0:far0, :] = rs[:rows_per_blk]
        if has_far:
            acc_ref[far0:, :] += part[rows_per_blk:]
            carry_ref[far0:, :] += rs[rows_per_blk:]

    for i in range(n_blocks):
        blk = acc_ref[i * rows_per_blk:(i + 1) * rows_per_blk, :]
        o_ref[0, i * tq:(i + 1) * tq, :] = jnp.where(lane < HEAD_DIM, blk[:tq], blk[tq:]).astype(BF16)


def _stick_breaking(proj, col0):
    bsz, s, _ = proj.shape
    assert s % ATT_BLOCK == 0
    rows = HEADS_PER_TILE * s
    return pl.pallas_call(
        _sb_kernel,
        grid=(bsz, N_PAIRS),
        in_specs=[
            pl.BlockSpec((1, s, LANES), lambda b, p: (b, 0, col0 + p)),
            pl.BlockSpec((1, s, LANES), lambda b, p: (b, 0, col0 + N_PAIRS + p)),
            pl.BlockSpec((1, s, LANES), lambda b, p: (b, 0, col0 + 2 * N_PAIRS + p)),
        ],
        out_specs=pl.BlockSpec((1, s, LANES), lambda b, p: (b, 0, p)),
        out_shape=jax.ShapeDtypeStruct((bsz, s, W_BRANCH), BF16),
        scratch_shapes=[pltpu.VMEM((rows, LANES), BF16),
                        pltpu.VMEM((rows, LANES), F32),
                        pltpu.VMEM((rows, 1), F32)],
        compiler_params=pltpu.CompilerParams(
            dimension_semantics=("arbitrary", "arbitrary"), vmem_limit_bytes=VMEM_LIMIT),
        name="stick_breaking",
    )(proj, proj, proj)


def _rel_bucket(dist):
    max_exact = REL_BUCKETS // 2
    n = jnp.maximum(dist, 0)
    nf = jnp.maximum(n, 1).astype(F32)
    large = max_exact + (jnp.log(nf / max_exact) / math.log(REL_MAX_DIST / max_exact)
                         * (REL_BUCKETS - max_exact)).astype(jnp.int32)
    large = jnp.minimum(large, REL_BUCKETS - 1)
    return jnp.where(n < max_exact, n, large)


def _moba_bias_tiles(rel_bias):
    assert MOBA_BLOCK >= REL_MAX_DIST
    t = jnp.arange(MOBA_BLOCK)[:, None]
    s = jnp.arange(MOBA_BLOCK)[None, :]
    d_own = t - s
    rel = rel_bias.astype(F32) - rel_bias[REL_BUCKETS - 1].astype(F32)

    def lookup(dist):
        onehot = jax.nn.one_hot(_rel_bucket(dist), REL_BUCKETS, dtype=F32)
        return jnp.einsum("tsk,kh->hts", onehot, rel, precision=lax.Precision.HIGHEST)

    own = jnp.where((d_own >= 0)[None], lookup(d_own), NEG_INF)
    return jnp.stack([own, lookup(d_own + MOBA_BLOCK)], axis=1)


def _moba_kernel(q_ref, k_ref, v_ref, bias_ref, o_ref,
                 ka_ref, va_ref, qa_ref, m_ref, mrep_ref, accl_ref, *, n_blocks):
    tq = MOBA_BLOCK
    s_len = k_ref.shape[1]
    nb_pad = 16
    rows_per_blk = HEADS_PER_TILE * tq
    lane = lax.broadcasted_iota(jnp.int32, (1, LANES), 1)
    head_lanes = [lane // HEAD_DIM == h for h in range(HEADS_PER_TILE)]
    q = q_ref[0]
    k = k_ref[0]

    key_blk = lax.broadcasted_iota(jnp.int32, (s_len, LANES), 0) // tq
    ka_ref[:, :LANES] = k
    ka_ref[:, LANES:] = jnp.where(lane == key_blk, 1.0, 0.0).astype(BF16)
    va_ref[:, :LANES] = v_ref[0]
    va_ref[:, LANES:] = jnp.ones((s_len, LANES), BF16)

    r = lax.broadcasted_iota(jnp.int32, (nb_pad, s_len), 0)
    c = lax.broadcasted_iota(jnp.int32, (nb_pad, s_len), 1)
    pool = jnp.where(c // tq == r, 1.0 / tq, 0.0).astype(BF16)
    km_hi, km_lo = _split_bf16(_dot(pool, k))
    means = [jnp.where(hl, km, jnp.zeros_like(km)) for hl in head_lanes for km in (km_hi, km_lo)]
    gates = _dot_nt(jnp.concatenate(means, axis=0), q)
    own = c // tq
    masks = []
    for h in range(HEADS_PER_TILE):
        gate = (gates[2 * h * nb_pad:(2 * h + 1) * nb_pad]
                + gates[(2 * h + 1) * nb_pad:(2 * h + 2) * nb_pad])
        cnt = jnp.zeros((nb_pad, s_len), F32)
        for n in range(n_blocks):
            gn = gate[n:n + 1, :]
            beats = jnp.where(gn > gate, 1.0, jnp.where((gn == gate) & (n < r), 1.0, 0.0))
            cnt = cnt + jnp.where(n < own, beats, 0.0)
        keep = (r == own) | ((r < own) & (cnt < MOBA_TOPK))
        masks += [jnp.where(keep, 0.0, NEG_INF).astype(BF16),
                  jnp.zeros((LANES - nb_pad, s_len), BF16)]
    masks = jnp.concatenate(masks, axis=0)
    er = lax.broadcasted_iota(jnp.int32, (tq, tq), 0)
    ec = lax.broadcasted_iota(jnp.int32, (tq, tq), 1)
    eye = jnp.where(er == ec, 1.0, 0.0).astype(BF16)
    for i in range(n_blocks):
        qi = q[i * tq:(i + 1) * tq]
        mask_rows = _dot_nt(eye, masks[:, i * tq:(i + 1) * tq]).astype(BF16)
        for h in range(HEADS_PER_TILE):
            row0 = (HEADS_PER_TILE * i + h) * tq
            qa_ref[row0:row0 + tq, :LANES] = jnp.where(head_lanes[h], qi, jnp.zeros_like(qi))
            qa_ref[row0:row0 + tq, LANES:] = mask_rows[:, h * LANES:(h + 1) * LANES]

    bias = [jnp.concatenate([bias_ref[h, which] for h in range(HEADS_PER_TILE)], axis=0)
            for which in range(2)]
    def scores(j):
        row0 = j * rows_per_blk
        s = _dot_nt(qa_ref[row0:, :], ka_ref[j * tq:(j + 1) * tq, :])
        parts = [s[:rows_per_blk] + bias[0]]
        if j + 1 < n_blocks:
            parts.append(s[rows_per_blk:2 * rows_per_blk] + bias[1])
        if j + 2 < n_blocks:
            parts.append(s[2 * rows_per_blk:])
        return jnp.concatenate(parts, axis=0)

    for j in range(n_blocks):
        row0 = j * rows_per_blk
        m = jnp.max(scores(j), axis=1, keepdims=True)
        if j == 0:
            m_ref[...] = m
        else:
            m_ref[row0:, :] = jnp.maximum(m_ref[row0:, :], m)
    mrep_ref[...] = jnp.broadcast_to(m_ref[...], mrep_ref.shape)
    for j in range(n_blocks):
        row0 = j * rows_per_blk
        m = mrep_ref[row0:, :]
        pr = jnp.exp(scores(j) - jnp.concatenate([m] * (tq // LANES), axis=1))
        ov = _dot(pr.astype(BF16), va_ref[j * tq:(j + 1) * tq, :])
        if j == 0:
            accl_ref[...] = ov
        else:
            accl_ref[row0:, :] += ov

    for i in range(n_blocks):
        blk = accl_ref[i * rows_per_blk:(i + 1) * rows_per_blk, :]
        out = blk[:, :LANES] / blk[:, LANES:]
        o_ref[0, i * tq:(i + 1) * tq, :] = jnp.where(lane < HEAD_DIM, out[:tq], out[tq:]).astype(BF16)


def _moba(proj, col0, bias_tiles):
    bsz, s, _ = proj.shape
    tq = MOBA_BLOCK
    assert s % tq == 0
    n_blocks = s // tq
    assert n_blocks <= 16
    rows = HEADS_PER_TILE * s
    return pl.pallas_call(
        functools.partial(_moba_kernel, n_blocks=n_blocks),
        grid=(bsz, N_PAIRS),
        in_specs=[
            pl.BlockSpec((1, s, LANES), lambda b, p: (b, 0, col0 + p)),
            pl.BlockSpec((1, s, LANES), lambda b, p: (b, 0, col0 + N_PAIRS + p)),
            pl.BlockSpec((1, s, LANES), lambda b, p: (b, 0, col0 + 2 * N_PAIRS + p)),
            pl.BlockSpec((HEADS_PER_TILE, 2, tq, tq), lambda b, p: (p, 0, 0, 0)),
        ],
        out_specs=pl.BlockSpec((1, s, LANES), lambda b, p: (b, 0, p)),
        out_shape=jax.ShapeDtypeStruct((bsz, s, W_BRANCH), BF16),
        scratch_shapes=[
            pltpu.VMEM((s, 2 * LANES), BF16),
            pltpu.VMEM((s, 2 * LANES), BF16),
            pltpu.VMEM((rows, 2 * LANES), BF16),
            pltpu.VMEM((rows, 1), F32),
            pltpu.VMEM((rows, LANES), F32),
            pltpu.VMEM((rows, 2 * LANES), F32),
        ],
        compiler_params=pltpu.CompilerParams(
            dimension_semantics=("arbitrary", "arbitrary"), vmem_limit_bytes=VMEM_LIMIT),
        name="moba",
    )(proj, proj, proj, bias_tiles)


def _merge_kernel(oa_ref, ob_ref, ga_ref, gb_ref, x_ref, wa_ref, wb_ref, wo_ref, g_ref, gate_ref,
                  o_ref):
    pa = _dot(oa_ref[0], wa_ref[...])
    pb = _dot(ob_ref[0], wb_ref[...])
    merged = (jax.nn.sigmoid(ga_ref[0].astype(F32)) * pa
              + jax.nn.sigmoid(gb_ref[0].astype(F32)) * pb)
    y = _dot(merged.astype(BF16), wo_ref[...])
    o_ref[0] = x_ref[0] + gate_ref[0] * _rms(y, g_ref[...])


def _merge(o_a, o_b, proj, gate_col, x, w_a, w_b, w_o, g_post, gate):
    bsz, s, d = x.shape
    tm = 512
    const = lambda b, t: (0, 0)
    return pl.pallas_call(
        _merge_kernel,
        grid=(bsz, s // tm),
        in_specs=[
            pl.BlockSpec((1, tm, W_BRANCH), lambda b, t: (b, t, 0)),
            pl.BlockSpec((1, tm, W_BRANCH), lambda b, t: (b, t, 0)),
            pl.BlockSpec((1, tm, d), lambda b, t: (b, t, gate_col)),
            pl.BlockSpec((1, tm, d), lambda b, t: (b, t, gate_col + 1)),
            pl.BlockSpec((1, tm, d), lambda b, t: (b, t, 0)),
            pl.BlockSpec((W_BRANCH, d), const),
            pl.BlockSpec((W_BRANCH, d), const),
            pl.BlockSpec((d, d), const),
            pl.BlockSpec((1, d), const),
            pl.BlockSpec((1, 1, d), lambda b, t: (b, 0, 0)),
        ],
        out_specs=pl.BlockSpec((1, tm, d), lambda b, t: (b, t, 0)),
        out_shape=jax.ShapeDtypeStruct((bsz, s, d), F32),
        compiler_params=pltpu.CompilerParams(
            dimension_semantics=("arbitrary", "arbitrary"), vmem_limit_bytes=VMEM_LIMIT),
        name="merge",
    )(o_a, o_b, proj, proj, x, w_a, w_b, w_o, g_post.reshape(1, d), gate.reshape(bsz, 1, d))


FFN_ROWS = 1024
FFN_COLS = 512


def _ffn_kernel(x_ref, gpre_ref, sc_ref, sh_ref, w1_ref, w3_ref, w2_ref, gpost_ref, gate_ref,
                o_ref, h_ref, acc_ref):
    k = pl.program_id(2)

    @pl.when(k == 0)
    def _():
        h = _rms(x_ref[0], gpre_ref[...]) * (1.0 + sc_ref[0]) + sh_ref[0]
        h_ref[...] = h.astype(BF16)
        acc_ref[...] = jnp.zeros_like(acc_ref)

    h = h_ref[...]
    u = jax.nn.silu(_dot(h, w1_ref[...].astype(BF16))) * _dot(h, w3_ref[...].astype(BF16))
    acc_ref[...] += _dot(u.astype(BF16), w2_ref[...].astype(BF16))

    @pl.when(k == pl.num_programs(2) - 1)
    def _():
        o_ref[0] = x_ref[0] + gate_ref[0] * _rms(acc_ref[...], gpost_ref[...])


def _ffn(x, g_pre, scale, shift, w1, w3, w2, g_post, gate):
    bsz, s, d = x.shape
    f = w1.shape[1]
    tm, tf = FFN_ROWS, FFN_COLS
    row = lambda b, t, k: (b, t, 0)
    per_b = lambda b, t, k: (b, 0, 0)
    const = lambda b, t, k: (0, 0)
    return pl.pallas_call(
        _ffn_kernel,
        grid=(bsz, s // tm, f // tf),
        in_specs=[
            pl.BlockSpec((1, tm, d), row),
            pl.BlockSpec((1, d), const),
            pl.BlockSpec((1, 1, d), per_b),
            pl.BlockSpec((1, 1, d), per_b),
            pl.BlockSpec((d, tf), lambda b, t, k: (0, k)),
            pl.BlockSpec((d, tf), lambda b, t, k: (0, k)),
            pl.BlockSpec((tf, d), lambda b, t, k: (k, 0)),
            pl.BlockSpec((1, d), const),
            pl.BlockSpec((1, 1, d), per_b),
        ],
        out_specs=pl.BlockSpec((1, tm, d), row),
        out_shape=jax.ShapeDtypeStruct((bsz, s, d), F32),
        scratch_shapes=[pltpu.VMEM((tm, d), BF16), pltpu.VMEM((tm, d), F32)],
        compiler_params=pltpu.CompilerParams(
            dimension_semantics=("arbitrary",) * 3, vmem_limit_bytes=VMEM_LIMIT),
        name="ffn_dense",
    )(x, g_pre.reshape(1, d), scale.reshape(bsz, 1, d), shift.reshape(bsz, 1, d), w1, w3, w2,
      g_post.reshape(1, d), gate.reshape(bsz, 1, d))


ROUTE_ROWS = 512


def _to_token_major(ref, val):
    rows, n = val.shape[0], val.shape[1] // LANES
    for s in range(n):
        ref[pl.ds(s, rows, stride=n), :] = val[:, s * LANES:(s + 1) * LANES]


def _from_token_major(ref, n):
    rows = ref.shape[0] // n
    return jnp.concatenate([ref[pl.ds(s, rows, stride=n), :] for s in range(n)], axis=1)
ROUTE_LANES = ("expert0", "expert1", "weight0", "weight1", "rank0", "rank1")


def _route_kernel(x_ref, gpre_ref, sc_ref, sh_ref, wr_ref, h_ref, route_ref, cnt_ref, *, n_experts):
    first_step = (pl.program_id(0) == 0) & (pl.program_id(1) == 0)

    @pl.when(first_step)
    def _():
        cnt_ref[...] = jnp.zeros_like(cnt_ref)

    h = _rms(x_ref[0], gpre_ref[...]) * (1.0 + sc_ref[0]) + sh_ref[0]
    _to_token_major(h_ref, h)
    h_hi, h_lo = _split_bf16(h)
    w_hi, w_lo = _split_bf16(wr_ref[...])
    logits = _dot(h_hi, w_hi) + (_dot(h_hi, w_lo) + _dot(h_lo, w_hi))
    ts = logits.shape[0]
    lane = lax.broadcasted_iota(jnp.int32, logits.shape, 1)
    lg = jnp.where(lane < n_experts, logits, -jnp.inf)
    m1 = jnp.max(lg, axis=1, keepdims=True)
    i1 = jnp.min(jnp.where(lg == m1, lane, LANES), axis=1, keepdims=True)
    first = lane == i1
    lg2 = jnp.where(first, -jnp.inf, lg)
    m2 = jnp.max(lg2, axis=1, keepdims=True)
    i2 = jnp.min(jnp.where(lg2 == m2, lane, LANES), axis=1, keepdims=True)
    second = lane == i2
    e2 = jnp.exp(m2 - m1)
    w_first = 1.0 / (1.0 + e2)
    w_second = e2 * w_first

    onehot = jnp.where(first | second, 1.0, 0.0)
    r = lax.broadcasted_iota(jnp.int32, (ts, ts), 0)
    c = lax.broadcasted_iota(jnp.int32, (ts, ts), 1)
    lower = jnp.where(c < r, 1.0, 0.0).astype(BF16)
    prefix = _dot(lower, onehot.astype(BF16)) + cnt_ref[...]
    rank_first = jnp.sum(jnp.where(first, prefix, 0.0), axis=1, keepdims=True)
    rank_second = jnp.sum(jnp.where(second, prefix, 0.0), axis=1, keepdims=True)
    cnt_ref[...] += jnp.sum(onehot, axis=0, keepdims=True)

    cols = (i1.astype(F32), i2.astype(F32), w_first, w_second, rank_first, rank_second)
    route = jnp.zeros(logits.shape, F32)
    for n, val in enumerate(cols):
        route = jnp.where(lane == n, val, route)
    route_ref[0] = route


def _route(x, g_pre, scale, shift, w_router):
    bsz, s, d = x.shape
    n_experts = w_router.shape[1]
    ts = ROUTE_ROWS
    row = lambda b, t: (b, t, 0)
    per_b = lambda b, t: (b, 0, 0)
    const = lambda b, t: (0, 0)
    return pl.pallas_call(
        functools.partial(_route_kernel, n_experts=n_experts),
        grid=(bsz, s // ts),
        in_specs=[
            pl.BlockSpec((1, ts, d), row),
            pl.BlockSpec((1, d), const),
            pl.BlockSpec((1, 1, d), per_b),
            pl.BlockSpec((1, 1, d), per_b),
            pl.BlockSpec((d, LANES), const),
        ],
        out_specs=[
            pl.BlockSpec((ts * d // LANES, LANES), lambda b, t: (b * (s // ts) + t, 0)),
            pl.BlockSpec((1, ts, LANES), row),
            pl.BlockSpec((1, LANES), const),
        ],
        out_shape=[
            jax.ShapeDtypeStruct((bsz * s * d // LANES, LANES), F32),
            jax.ShapeDtypeStruct((bsz, s, LANES), F32),
            jax.ShapeDtypeStruct((1, LANES), F32),
        ],
        compiler_params=pltpu.CompilerParams(
            dimension_semantics=("arbitrary", "arbitrary"), vmem_limit_bytes=VMEM_LIMIT),
        name="moe_route",
    )(x, g_pre.reshape(1, d), scale.reshape(bsz, 1, d), shift.reshape(bsz, 1, d),
      jnp.pad(w_router, ((0, 0), (0, LANES - n_experts))))


def _for_each_chunk(start, length, chunk, fn):
    whole = length // chunk

    def loop(c, _):
        fn(start + c * chunk, chunk)
        return 0

    lax.fori_loop(0, whole, loop, 0)
    cur = start + whole * chunk
    size = chunk // 2
    while size >= 1:
        bit = length & size

        @pl.when(bit != 0)
        def _(cur=cur, size=size):
            fn(cur, size)

        cur = cur + bit
        size //= 2


def _scatter_kernel(pos_ref, fill_ref, h_ref, xs_ref, zero_ref, sem, *, n_experts, tile_rows):
    ts = h_ref.shape[0]

    def issue(r, _):
        for slot in range(TOP_K):
            dst = pos_ref[0, 0, slot * ts + r]
            pltpu.make_async_copy(h_ref.at[r], xs_ref.at[dst], sem).start(priority=slot)
        return 0

    lax.fori_loop(0, ts, issue, 0, unroll=4)
    for slot in range(TOP_K):
        pltpu.make_async_copy(h_ref, xs_ref.at[pl.ds(0, ts)], sem).wait()

    @pl.when(pl.program_id(0) == pl.num_programs(0) - 1)
    def _():
        zero_ref[...] = jnp.zeros_like(zero_ref)
        occupied = fill_ref[2 * n_experts] * tile_rows
        spans = [(fill_ref[e], fill_ref[n_experts + e] - fill_ref[e]) for e in range(n_experts)]
        spans.append((occupied, xs_ref.shape[0] - occupied))

        def zero_copy(first, rows):
            return pltpu.make_async_copy(zero_ref.at[pl.ds(0, rows)],
                                         xs_ref.at[pl.ds(first, rows)], sem)

        for first, length in spans:
            _for_each_chunk(first, length, zero_ref.shape[0],
                            lambda row, rows: zero_copy(row, rows).start())
        for first, length in spans:
            _for_each_chunk(first, length, zero_ref.shape[0],
                            lambda row, rows: zero_copy(row, rows).wait())


def _scatter(h, pos, fill, n_rows):
    t = h.shape[0]
    ts = ROUTE_ROWS
    zero_rows = 256
    return pl.pallas_call(
        functools.partial(_scatter_kernel, n_experts=(fill.shape[0] - 1) // 2,
                          tile_rows=FFN_ROWS),
        grid=(t // ts,),
        in_specs=[
            pl.BlockSpec((1, 1, TOP_K * ts), lambda i: (i, 0, 0), memory_space=pltpu.SMEM),
            pl.BlockSpec(memory_space=pltpu.SMEM),
            pl.BlockSpec((ts,) + h.shape[1:], lambda i: (i, 0, 0)),
        ],
        out_specs=pl.BlockSpec(memory_space=pl.ANY),
        out_shape=jax.ShapeDtypeStruct((n_rows,) + h.shape[1:], F32),
        scratch_shapes=[pltpu.VMEM((zero_rows,) + h.shape[1:], F32), pltpu.SemaphoreType.DMA(())],
        compiler_params=pltpu.CompilerParams(
            dimension_semantics=("arbitrary",), vmem_limit_bytes=VMEM_LIMIT),
        name="moe_scatter",
    )(pos, fill, h)


def _experts_kernel(exp_ref, nact_ref, xs_ref, w1_ref, w3_ref, w2_ref, ys_ref, h_ref, acc_ref):
    i = pl.program_id(0)
    k = pl.program_id(1)

    @pl.when(i < nact_ref[0])
    def _():
        @pl.when(k == 0)
        def _():
            h_ref[...] = _from_token_major(xs_ref, h_ref.shape[1] // LANES).astype(BF16)

        h = h_ref[...]
        u = jax.nn.silu(_dot(h, w1_ref[0].astype(BF16))) * _dot(h, w3_ref[0].astype(BF16))
        part = _dot(u.astype(BF16), w2_ref[0].astype(BF16))

        @pl.when(k == 0)
        def _():
            acc_ref[...] = part

        @pl.when(k > 0)
        def _():
            acc_ref[...] += part

        @pl.when(k == pl.num_programs(1) - 1)
        def _():
            _to_token_major(ys_ref, acc_ref[...])

    @pl.when((i >= nact_ref[0]) & (k == 0))
    def _():
        ys_ref[...] = jnp.zeros_like(ys_ref)


def _experts(xs, tile_expert, n_active, w1, w3, w2):
    d, f = w1.shape[1:]
    n_rows = xs.shape[0] * LANES // d
    tm, tf = FFN_ROWS, FFN_COLS
    n_k = f // tf
    n_tiles = n_rows // tm

    def k_of(i, k, nact):
        return jnp.where(i < nact[0], k, n_k - 1)

    return pl.pallas_call(
        _experts_kernel,
        grid_spec=pltpu.PrefetchScalarGridSpec(
            num_scalar_prefetch=2,
            grid=(n_tiles, n_k),
            in_specs=[
                pl.BlockSpec((tm * d // LANES, LANES), lambda i, k, exp, nact: (i, 0)),
                pl.BlockSpec((1, d, tf), lambda i, k, exp, nact: (exp[i], 0, k_of(i, k, nact))),
                pl.BlockSpec((1, d, tf), lambda i, k, exp, nact: (exp[i], 0, k_of(i, k, nact))),
                pl.BlockSpec((1, tf, d), lambda i, k, exp, nact: (exp[i], k_of(i, k, nact), 0)),
            ],
            out_specs=pl.BlockSpec((tm * d // LANES, LANES), lambda i, k, exp, nact: (i, 0)),
            scratch_shapes=[pltpu.VMEM((tm, d), BF16), pltpu.VMEM((tm, d), F32)],
        ),
        out_shape=jax.ShapeDtypeStruct(xs.shape, F32),
        compiler_params=pltpu.CompilerParams(
            dimension_semantics=("arbitrary", "arbitrary"), vmem_limit_bytes=VMEM_LIMIT),
        name="moe_experts",
    )(tile_expert, n_active, xs, w1, w3, w2)


def _combine_kernel(pos_ref, next_pos_ref, route_ref, x_ref, gpost_ref, gate_ref, ys_ref, o_ref,
                    *scratch):
    ts = x_ref.shape[0]
    n = ys_ref.shape[1]
    bufs = (scratch[:TOP_K], scratch[TOP_K:2 * TOP_K])
    sem = scratch[2 * TOP_K]
    i = pl.program_id(0)
    last = pl.num_programs(0) - 1

    def gather(rows_ref, parity):
        def issue(r, _):
            dst = pl.multiple_of(r * n, n)
            for slot in range(TOP_K):
                src = rows_ref[0, 0, slot * ts + r]
                pltpu.make_async_copy(ys_ref.at[src], bufs[parity][slot].at[pl.ds(dst, n)],
                                      sem.at[parity]).start(priority=slot)
            return 0

        lax.fori_loop(0, ts, issue, 0, unroll=4)

    def finish(parity):
        for slot in range(TOP_K):
            buf = bufs[parity][slot]
            pltpu.make_async_copy(buf, buf, sem.at[parity]).wait()
        w0 = ROUTE_LANES.index("weight0")
        route = route_ref[...]
        y = (route[:, w0:w0 + 1] * _from_token_major(bufs[parity][0], n)
             + route[:, w0 + 1:w0 + 2] * _from_token_major(bufs[parity][1], n))
        o_ref[...] = x_ref[...] + gate_ref[0] * _rms(y, gpost_ref[...])

    @pl.when(i == 0)
    def _():
        gather(pos_ref, 0)

    for parity in range(2):
        @pl.when(i % 2 == parity)
        def _(parity=parity):
            @pl.when(i < last)
            def _():
                gather(next_pos_ref, 1 - parity)

            finish(parity)


def _combine(ys, pos, route, x, g_post, gate):
    t, d = x.shape
    bsz = gate.shape[0]
    ts = ROUTE_ROWS
    steps_per_b = t // bsz // ts
    n_steps = t // ts
    return pl.pallas_call(
        _combine_kernel,
        grid=(n_steps,),
        in_specs=[
            pl.BlockSpec((1, 1, TOP_K * ts), lambda i: (i, 0, 0), memory_space=pltpu.SMEM),
            pl.BlockSpec((1, 1, TOP_K * ts), lambda i: (jnp.minimum(i + 1, n_steps - 1), 0, 0),
                         memory_space=pltpu.SMEM),
            pl.BlockSpec((ts, LANES), lambda i: (i, 0)),
            pl.BlockSpec((ts, d), lambda i: (i, 0)),
            pl.BlockSpec((1, d), lambda i: (0, 0)),
            pl.BlockSpec((1, 1, d), lambda i: (i // steps_per_b, 0, 0)),
            pl.BlockSpec(memory_space=pl.ANY),
        ],
        out_specs=pl.BlockSpec((ts, d), lambda i: (i, 0)),
        out_shape=jax.ShapeDtypeStruct((t, d), F32),
        scratch_shapes=[pltpu.VMEM((ts * d // LANES, LANES), F32)] * (2 * TOP_K)
        + [pltpu.SemaphoreType.DMA((2,))],
        compiler_params=pltpu.CompilerParams(
            dimension_semantics=("arbitrary",), vmem_limit_bytes=VMEM_LIMIT),
        name="moe_combine",
    )(pos, pos, route, x, g_post.reshape(1, d), gate, ys)


def _moe(x, g_pre, scale, shift, w_router, w1, w3, w2, g_post, gate):
    bsz, s, d = x.shape
    t = bsz * s
    n_experts = w_router.shape[1]
    ts, tm = ROUTE_ROWS, FFN_ROWS
    h, route, counts = _route(x, g_pre, scale, shift, w_router)
    route = route.reshape(t, LANES)

    lane_of = ROUTE_LANES.index
    expert = route[:, lane_of("expert0"):lane_of("expert0") + TOP_K].astype(jnp.int32)
    rank = route[:, lane_of("rank0"):lane_of("rank0") + TOP_K].astype(jnp.int32)
    counts = counts[0, :n_experts].astype(jnp.int32)
    tiles_per = (counts + tm - 1) // tm
    tile_end = jnp.cumsum(tiles_per)
    first_row = (tile_end - tiles_per) * tm
    n_active = tile_end[-1]
    n_tiles = TOP_K * t // tm + n_experts
    is_expert = expert[..., None] == jnp.arange(n_experts, dtype=jnp.int32)
    pos = jnp.sum(jnp.where(is_expert, first_row, 0), axis=-1) + rank
    pos = pos.reshape(t // ts, ts, TOP_K).transpose(0, 2, 1).reshape(t // ts, 1, TOP_K * ts)
    fill = jnp.concatenate([first_row + counts, tile_end * tm, n_active[None]]).astype(jnp.int32)
    idx = jnp.minimum(jnp.arange(n_tiles, dtype=jnp.int32), n_active - 1)
    tile_expert = jnp.sum(idx[:, None] >= tile_end[None, :], axis=1).astype(jnp.int32)

    token_major = (-1, d // LANES, LANES)
    xs = _scatter(h.reshape(token_major), pos, fill, n_tiles * tm)
    ys = _experts(xs.reshape(-1, LANES), tile_expert, n_active.reshape(1), w1, w3, w2)
    out = _combine(ys.reshape(token_major), pos, route, x.reshape(t, d), g_post,
                   gate.reshape(bsz, 1, d))
    return out.reshape(bsz, s, d)


def kernel(x, c, w_mod, b_mod, g_mix_pre, g_mix_post, g_ffn_pre, g_ffn_post, w_in, w_proj_sb,
           w_proj_moba, w_out, rel_bias, w1_dense, w3_dense, w2_dense, w_router, w1_moe, w3_moe,
           w2_moe):
    depth = w_mod.shape[0]
    d = x.shape[-1]
    qkv = 3 * W_BRANCH
    col = jnp.arange(w_in.shape[-1])
    is_q = (col < W_BRANCH) | ((col >= qkv) & (col < qkv + W_BRANCH))
    col_scale = jnp.where(is_q, HEAD_DIM ** -0.5, 1.0).astype(F32)

    mod = _modulation(c, w_mod, b_mod)
    bias_tiles = _moba_bias_tiles(rel_bias)
    for l in range(depth):
        sh_mix, sc_mix, gt_mix, sh_ffn, sc_ffn, gt_ffn = jnp.split(mod[l], N_MOD, axis=-1)
        proj = _inproj(x, g_mix_pre[l], sc_mix, sh_mix, (w_in[l] * col_scale).astype(BF16))
        o_a = _stick_breaking(proj, 0)
        o_b = _moba(proj, qkv // LANES, bias_tiles)
        x = _merge(o_a, o_b, proj, 2 * qkv // d, x, w_proj_sb[l].astype(BF16),
                   w_proj_moba[l].astype(BF16), w_out[l].astype(BF16), g_mix_post[l], gt_mix)
        j = l // 2
        if l % 2 == 0:
            x = _ffn(x, g_ffn_pre[l], sc_ffn, sh_ffn, w1_dense[j], w3_dense[j], w2_dense[j],
                     g_ffn_post[l], gt_ffn)
        else:
            x = _moe(x, g_ffn_pre[l], sc_ffn, sh_ffn, w_router[j], w1_moe[j], w3_moe[j],
                     w2_moe[j], g_ffn_post[l], gt_ffn)
    return x
```

```python
import functools
import math

import jax
import jax.numpy as jnp
from jax import lax
from jax.experimental import pallas as pl
from jax.experimental.pallas import tpu as pltpu

F32 = jnp.float32
BF16 = jnp.bfloat16

EPS = 1e-6
NEG_INF = -1e30
HEAD_DIM = 64
N_HEADS = 8
W_BRANCH = N_HEADS * HEAD_DIM
LANES = 128
SUBLANES = 8
HEADS_PER_TILE = LANES // HEAD_DIM
N_PAIRS = N_HEADS // HEADS_PER_TILE
MOBA_BLOCK = 256
MOBA_TOPK = 3
REL_BUCKETS = 32
REL_MAX_DIST = 128
N_MOD = 6
TOP_K = 2
ATT_BLOCK = 256
PROJ_ROWS = 1024
VMEM_LIMIT = 56 * 1024 * 1024


def _dot(a, b):
    return jnp.dot(a, b, preferred_element_type=F32)


def _dot_nt(a, b):
    return lax.dot_general(a, b, (((1,), (1,)), ((), ())), preferred_element_type=F32)


def _split_bf16(x):
    hi = x.astype(BF16)
    lo = (x - hi.astype(F32)).astype(BF16)
    return hi, lo


def _rms(x, g):
    return x * lax.rsqrt(jnp.mean(x * x, axis=-1, keepdims=True) + EPS) * g


def _mod_kernel(c_ref, w_ref, b_ref, o_ref):
    c = c_ref[...]
    cond = c * jax.nn.sigmoid(c)
    o_ref[0] = jnp.dot(cond, w_ref[0], preferred_element_type=F32,
                       precision=lax.Precision.HIGHEST) + b_ref[0]


def _modulation(c, w_mod, b_mod):
    depth, d, n = w_mod.shape
    bsz = c.shape[0]
    tn = 1536
    return pl.pallas_call(
        _mod_kernel,
        grid=(depth, n // tn),
        in_specs=[
            pl.BlockSpec((bsz, d), lambda l, j: (0, 0)),
            pl.BlockSpec((1, d, tn), lambda l, j: (l, 0, j)),
            pl.BlockSpec((1, 1, tn), lambda l, j: (l, 0, j)),
        ],
        out_specs=pl.BlockSpec((1, bsz, tn), lambda l, j: (l, 0, j)),
        out_shape=jax.ShapeDtypeStruct((depth, bsz, n), F32),
        compiler_params=pltpu.CompilerParams(
            dimension_semantics=("arbitrary", "arbitrary"), vmem_limit_bytes=VMEM_LIMIT),
        name="modulation",
    )(c, w_mod, b_mod.reshape(depth, 1, n))


def _inproj_kernel(x_ref, g_ref, sc_ref, sh_ref, w_ref, o_ref, *, tn):
    h = _rms(x_ref[0], g_ref[...]) * (1.0 + sc_ref[0]) + sh_ref[0]
    hb = h.astype(BF16)
    for n in range(w_ref.shape[1] // tn):
        o_ref[0, :, n * tn:(n + 1) * tn] = _dot(hb, w_ref[:, n * tn:(n + 1) * tn]).astype(BF16)


def _inproj(x, g, scale, shift, w):
    bsz, s, d = x.shape
    n = w.shape[1]
    tm = PROJ_ROWS
    return pl.pallas_call(
        functools.partial(_inproj_kernel, tn=1024),
        grid=(bsz, s // tm),
        in_specs=[
            pl.BlockSpec((1, tm, d), lambda b, t: (b, t, 0)),
            pl.BlockSpec((1, d), lambda b, t: (0, 0)),
            pl.BlockSpec((1, 1, d), lambda b, t: (b, 0, 0)),
            pl.BlockSpec((1, 1, d), lambda b, t: (b, 0, 0)),
            pl.BlockSpec((d, n), lambda b, t: (0, 0), pipeline_mode=pl.Buffered(1)),
        ],
        out_specs=pl.BlockSpec((1, tm, n), lambda b, t: (b, t, 0)),
        out_shape=jax.ShapeDtypeStruct((bsz, s, n), BF16),
        compiler_params=pltpu.CompilerParams(
            dimension_semantics=("arbitrary", "arbitrary"), vmem_limit_bytes=VMEM_LIMIT),
        name="inproj",
    )(x, g.reshape(1, d), scale.reshape(bsz, 1, d), shift.reshape(bsz, 1, d), w)


SOFTPLUS_LINEAR = 40.0


def _softplus(z):
    return jnp.maximum(z, jnp.log(1.0 + jnp.exp(jnp.minimum(z, SOFTPLUS_LINEAR))))


def _sb_kernel(q_ref, k_ref, v_ref, o_ref, qs_ref, acc_ref, carry_ref):
    tq = ATT_BLOCK
    n_blocks = k_ref.shape[1] // tq
    rows_per_blk = HEADS_PER_TILE * tq
    lane = lax.broadcasted_iota(jnp.int32, (1, LANES), 1)
    row = lax.broadcasted_iota(jnp.int32, (tq, tq), 0)
    col = lax.broadcasted_iota(jnp.int32, (tq, tq), 1)
    upper = jnp.where(row > col, 1.0, 0.0).astype(BF16)
    causal = jnp.concatenate([col < row] * HEADS_PER_TILE, axis=0)
    q = q_ref[0]
    for i in range(n_blocks):
        qi = q[i * tq:(i + 1) * tq]
        for h in range(HEADS_PER_TILE):
            row0 = (HEADS_PER_TILE * i + h) * tq
            qs_ref[row0:row0 + tq, :] = jnp.where(lane // HEAD_DIM == h, qi, jnp.zeros_like(qi))

    for j in reversed(range(n_blocks)):
        row0 = j * rows_per_blk
        far0 = row0 + rows_per_blk
        has_far = j + 1 < n_blocks
        z = _dot_nt(qs_ref[row0:, :], k_ref[0, j * tq:(j + 1) * tq, :])
        sp = _softplus(z)
        sp_diag = jnp.where(causal, sp[:rows_per_blk], 0.0)
        sp = jnp.concatenate([sp_diag, sp[rows_per_blk:]], axis=0) if has_far else sp_diag
        logit = z - sp - _dot(sp.astype(BF16), upper)
        att_diag = jnp.where(causal, jnp.exp(logit[:rows_per_blk]), 0.0)
        if has_far:
            att_far = jnp.exp(logit[rows_per_blk:] - carry_ref[far0:, :])
            att = jnp.concatenate([att_diag, att_far], axis=0)
        else:
            att = att_diag
        part = _dot(att.astype(BF16), v_ref[0, j * tq:(j + 1) * tq, :])
        rs = jnp.sum(sp, axis=1, keepdims=True)
        acc_ref[row0:far0, :] = part[:rows_per_blk]
        carry_ref[row0:far0, :] = rs[:rows_per_blk]
        if has_far:
            acc_ref[far0:, :] += part[rows_per_blk:]
            carry_ref[far0:, :] += rs[rows_per_blk:]

    for i in range(n_blocks):
        blk = acc_ref[i * rows_per_blk:(i + 1) * rows_per_blk, :]
        o_ref[0, i * tq:(i + 1) * tq, :] = jnp.where(lane < HEAD_DIM, blk[:tq], blk[tq:]).astype(BF16)


def _stick_breaking(proj, col0):
    bsz, s, _ = proj.shape
    assert s % ATT_BLOCK == 0
    rows = HEADS_PER_TILE * s
    return pl.pallas_call(
        _sb_kernel,
        grid=(bsz, N_PAIRS),
        in_specs=[
            pl.BlockSpec((1, s, LANES), lambda b, p: (b, 0, col0 + p)),
            pl.BlockSpec((1, s, LANES), lambda b, p: (b, 0, col0 + N_PAIRS + p)),
            pl.BlockSpec((1, s, LANES), lambda b, p: (b, 0, col0 + 2 * N_PAIRS + p)),
        ],
        out_specs=pl.BlockSpec((1, s, LANES), lambda b, p: (b, 0, p)),
        out_shape=jax.ShapeDtypeStruct((bsz, s, W_BRANCH), BF16),
        scratch_shapes=[pltpu.VMEM((rows, LANES), BF16),
                        pltpu.VMEM((rows, LANES), F32),
                        pltpu.VMEM((rows, 1), F32)],
        compiler_params=pltpu.CompilerParams(
            dimension_semantics=("arbitrary", "arbitrary"), vmem_limit_bytes=VMEM_LIMIT),
        name="stick_breaking",
    )(proj, proj, proj)


def _rel_bucket(dist):
    max_exact = REL_BUCKETS // 2
    n = jnp.maximum(dist, 0)
    nf = jnp.maximum(n, 1).astype(F32)
    large = max_exact + (jnp.log(nf / max_exact) / math.log(REL_MAX_DIST / max_exact)
                         * (REL_BUCKETS - max_exact)).astype(jnp.int32)
    large = jnp.minimum(large, REL_BUCKETS - 1)
    return jnp.where(n < max_exact, n, large)


def _moba_bias_tiles(rel_bias):
    assert MOBA_BLOCK >= REL_MAX_DIST
    t = jnp.arange(MOBA_BLOCK)[:, None]
    s = jnp.arange(MOBA_BLOCK)[None, :]
    d_own = t - s
    rel = rel_bias.astype(F32) - rel_bias[REL_BUCKETS - 1].astype(F32)

    def lookup(dist):
        onehot = jax.nn.one_hot(_rel_bucket(dist), REL_BUCKETS, dtype=F32)
        return jnp.einsum("tsk,kh->hts", onehot, rel, precision=lax.Precision.HIGHEST)

    own = jnp.where((d_own >= 0)[None], lookup(d_own), NEG_INF)
    return jnp.stack([own, lookup(d_own + MOBA_BLOCK)], axis=1)


def _moba_kernel(q_ref, k_ref, v_ref, bias_ref, o_ref,
                 ka_ref, va_ref, qa_ref, m_ref, mrep_ref, accl_ref, *, n_blocks):
    tq = MOBA_BLOCK
    s_len = k_ref.shape[1]
    nb_pad = 16
    rows_per_blk = HEADS_PER_TILE * tq
    lane = lax.broadcasted_iota(jnp.int32, (1, LANES), 1)
    head_lanes = [lane // HEAD_DIM == h for h in range(HEADS_PER_TILE)]
    q = q_ref[0]
    k = k_ref[0]

    key_blk = lax.broadcasted_iota(jnp.int32, (s_len, LANES), 0) // tq
    ka_ref[:, :LANES] = k
    ka_ref[:, LANES:] = jnp.where(lane == key_blk, 1.0, 0.0).astype(BF16)
    va_ref[:, :LANES] = v_ref[0]
    va_ref[:, LANES:] = jnp.ones((s_len, LANES), BF16)

    r = lax.broadcasted_iota(jnp.int32, (nb_pad, s_len), 0)
    c = lax.broadcasted_iota(jnp.int32, (nb_pad, s_len), 1)
    pool = jnp.where(c // tq == r, 1.0 / tq, 0.0).astype(BF16)
    km_hi, km_lo = _split_bf16(_dot(pool, k))
    means = [jnp.where(hl, km, jnp.zeros_like(km)) for hl in head_lanes for km in (km_hi, km_lo)]
    gates = _dot_nt(jnp.concatenate(means, axis=0), q)
    own = c // tq
    masks = []
    for h in range(HEADS_PER_TILE):
        gate = (gates[2 * h * nb_pad:(2 * h + 1) * nb_pad]
                + gates[(2 * h + 1) * nb_pad:(2 * h + 2) * nb_pad])
        cnt = jnp.zeros((nb_pad, s_len), F32)
        for n in range(n_blocks):
            gn = gate[n:n + 1, :]
            beats = jnp.where(gn > gate, 1.0, jnp.where((gn == gate) & (n < r), 1.0, 0.0))
            cnt = cnt + jnp.where(n < own, beats, 0.0)
        keep = (r == own) | ((r < own) & (cnt < MOBA_TOPK))
        masks += [jnp.where(keep, 0.0, NEG_INF).astype(BF16),
                  jnp.zeros((LANES - nb_pad, s_len), BF16)]
    masks = jnp.concatenate(masks, axis=0)
    er = lax.broadcasted_iota(jnp.int32, (tq, tq), 0)
    ec = lax.broadcasted_iota(jnp.int32, (tq, tq), 1)
    eye = jnp.where(er == ec, 1.0, 0.0).astype(BF16)
    for i in range(n_blocks):
        qi = q[i * tq:(i + 1) * tq]
        mask_rows = _dot_nt(eye, masks[:, i * tq:(i + 1) * tq]).astype(BF16)
        for h in range(HEADS_PER_TILE):
            row0 = (HEADS_PER_TILE * i + h) * tq
            qa_ref[row0:row0 + tq, :LANES] = jnp.where(head_lanes[h], qi, jnp.zeros_like(qi))
            qa_ref[row0:row0 + tq, LANES:] = mask_rows[:, h * LANES:(h + 1) * LANES]

    bias = [jnp.concatenate([bias_ref[h, which] for h in range(HEADS_PER_TILE)], axis=0)
            for which in range(2)]
    def scores(j):
        row0 = j * rows_per_blk
        s = _dot_nt(qa_ref[row0:, :], ka_ref[j * tq:(j + 1) * tq, :])
        parts = [s[:rows_per_blk] + bias[0]]
        if j + 1 < n_blocks:
            parts.append(s[rows_per_blk:2 * rows_per_blk] + bias[1])
        if j + 2 < n_blocks:
            parts.append(s[2 * rows_per_blk:])
        return jnp.concatenate(parts, axis=0)

    for j in range(n_blocks):
        row0 = j * rows_per_blk
        m = jnp.max(scores(j), axis=1, keepdims=True)
        if j == 0:
            m_ref[...] = m
        else:
            m_ref[row0:, :] = jnp.maximum(m_ref[row0:, :], m)
    mrep_ref[...] = jnp.broadcast_to(m_ref[...], mrep_ref.shape)
    for j in range(n_blocks):
        row0 = j * rows_per_blk
        m = mrep_ref[row0:, :]
        pr = jnp.exp(scores(j) - jnp.concatenate([m] * (tq // LANES), axis=1))
        ov = _dot(pr.astype(BF16), va_ref[j * tq:(j + 1) * tq, :])
        if j == 0:
            accl_ref[...] = ov
        else:
            accl_ref[row0:, :] += ov

    for i in range(n_blocks):
        blk = accl_ref[i * rows_per_blk:(i + 1) * rows_per_blk, :]
        out = blk[:, :LANES] / blk[:, LANES:]
        o_ref[0, i * tq:(i + 1) * tq, :] = jnp.where(lane < HEAD_DIM, out[:tq], out[tq:]).astype(BF16)


def _moba(proj, col0, bias_tiles):
    bsz, s, _ = proj.shape
    tq = MOBA_BLOCK
    assert s % tq == 0
    n_blocks = s // tq
    assert n_blocks <= 16
    rows = HEADS_PER_TILE * s
    return pl.pallas_call(
        functools.partial(_moba_kernel, n_blocks=n_blocks),
        grid=(bsz, N_PAIRS),
        in_specs=[
            pl.BlockSpec((1, s, LANES), lambda b, p: (b, 0, col0 + p)),
            pl.BlockSpec((1, s, LANES), lambda b, p: (b, 0, col0 + N_PAIRS + p)),
            pl.BlockSpec((1, s, LANES), lambda b, p: (b, 0, col0 + 2 * N_PAIRS + p)),
            pl.BlockSpec((HEADS_PER_TILE, 2, tq, tq), lambda b, p: (p, 0, 0, 0)),
        ],
        out_specs=pl.BlockSpec((1, s, LANES), lambda b, p: (b, 0, p)),
        out_shape=jax.ShapeDtypeStruct((bsz, s, W_BRANCH), BF16),
        scratch_shapes=[
            pltpu.VMEM((s, 2 * LANES), BF16),
            pltpu.VMEM((s, 2 * LANES), BF16),
            pltpu.VMEM((rows, 2 * LANES), BF16),
            pltpu.VMEM((rows, 1), F32),
            pltpu.VMEM((rows, LANES), F32),
            pltpu.VMEM((rows, 2 * LANES), F32),
        ],
        compiler_params=pltpu.CompilerParams(
            dimension_semantics=("arbitrary", "arbitrary"), vmem_limit_bytes=VMEM_LIMIT),
        name="moba",
    )(proj, proj, proj, bias_tiles)


def _merge_kernel(oa_ref, ob_ref, ga_ref, gb_ref, x_ref, wa_ref, wb_ref, wo_ref, g_ref, gate_ref,
                  o_ref):
    pa = _dot(oa_ref[0], wa_ref[...])
    pb = _dot(ob_ref[0], wb_ref[...])
    merged = (jax.nn.sigmoid(ga_ref[0].astype(F32)) * pa
              + jax.nn.sigmoid(gb_ref[0].astype(F32)) * pb)
    y = _dot(merged.astype(BF16), wo_ref[...])
    o_ref[0] = x_ref[0] + gate_ref[0] * _rms(y, g_ref[...])


def _merge(o_a, o_b, proj, gate_col, x, w_a, w_b, w_o, g_post, gate):
    bsz, s, d = x.shape
    tm = PROJ_ROWS
    const = lambda b, t: (0, 0)
    return pl.pallas_call(
        _merge_kernel,
        grid=(bsz, s // tm),
        in_specs=[
            pl.BlockSpec((1, tm, W_BRANCH), lambda b, t: (b, t, 0)),
            pl.BlockSpec((1, tm, W_BRANCH), lambda b, t: (b, t, 0)),
            pl.BlockSpec((1, tm, d), lambda b, t: (b, t, gate_col)),
            pl.BlockSpec((1, tm, d), lambda b, t: (b, t, gate_col + 1)),
            pl.BlockSpec((1, tm, d), lambda b, t: (b, t, 0)),
            pl.BlockSpec((W_BRANCH, d), const),
            pl.BlockSpec((W_BRANCH, d), const),
            pl.BlockSpec((d, d), const),
            pl.BlockSpec((1, d), const),
            pl.BlockSpec((1, 1, d), lambda b, t: (b, 0, 0)),
        ],
        out_specs=pl.BlockSpec((1, tm, d), lambda b, t: (b, t, 0)),
        out_shape=jax.ShapeDtypeStruct((bsz, s, d), F32),
        compiler_params=pltpu.CompilerParams(
            dimension_semantics=("arbitrary", "arbitrary"), vmem_limit_bytes=VMEM_LIMIT),
        name="merge",
    )(o_a, o_b, proj, proj, x, w_a, w_b, w_o, g_post.reshape(1, d), gate.reshape(bsz, 1, d))


FFN_ROWS = 1024
FFN_COLS = 512


def _ffn_kernel(x_ref, gpre_ref, sc_ref, sh_ref, w1_ref, w3_ref, w2_ref, gpost_ref, gate_ref,
                o_ref, h_ref, acc_ref):
    k = pl.program_id(2)

    @pl.when(k == 0)
    def _():
        h = _rms(x_ref[0], gpre_ref[...]) * (1.0 + sc_ref[0]) + sh_ref[0]
        h_ref[...] = h.astype(BF16)
        acc_ref[...] = jnp.zeros_like(acc_ref)

    h = h_ref[...]
    u = jax.nn.silu(_dot(h, w1_ref[...].astype(BF16))) * _dot(h, w3_ref[...].astype(BF16))
    acc_ref[...] += _dot(u.astype(BF16), w2_ref[...].astype(BF16))

    @pl.when(k == pl.num_programs(2) - 1)
    def _():
        o_ref[0] = x_ref[0] + gate_ref[0] * _rms(acc_ref[...], gpost_ref[...])


def _ffn(x, g_pre, scale, shift, w1, w3, w2, g_post, gate):
    bsz, s, d = x.shape
    f = w1.shape[1]
    tm, tf = FFN_ROWS, FFN_COLS
    row = lambda b, t, k: (b, t, 0)
    per_b = lambda b, t, k: (b, 0, 0)
    const = lambda b, t, k: (0, 0)
    return pl.pallas_call(
        _ffn_kernel,
        grid=(bsz, s // tm, f // tf),
        in_specs=[
            pl.BlockSpec((1, tm, d), row),
            pl.BlockSpec((1, d), const),
            pl.BlockSpec((1, 1, d), per_b),
            pl.BlockSpec((1, 1, d), per_b),
            pl.BlockSpec((d, tf), lambda b, t, k: (0, k)),
            pl.BlockSpec((d, tf), lambda b, t, k: (0, k)),
            pl.BlockSpec((tf, d), lambda b, t, k: (k, 0)),
            pl.BlockSpec((1, d), const),
            pl.BlockSpec((1, 1, d), per_b),
        ],
        out_specs=pl.BlockSpec((1, tm, d), row),
        out_shape=jax.ShapeDtypeStruct((bsz, s, d), F32),
        scratch_shapes=[pltpu.VMEM((tm, d), BF16), pltpu.VMEM((tm, d), F32)],
        compiler_params=pltpu.CompilerParams(
            dimension_semantics=("arbitrary",) * 3, vmem_limit_bytes=VMEM_LIMIT),
        name="ffn_dense",
    )(x, g_pre.reshape(1, d), scale.reshape(bsz, 1, d), shift.reshape(bsz, 1, d), w1, w3, w2,
      g_post.reshape(1, d), gate.reshape(bsz, 1, d))


ROUTE_ROWS = 1024


def _to_token_major(ref, val):
    rows, n = val.shape[0], val.shape[1] // LANES
    for s in range(n):
        ref[pl.ds(s, rows, stride=n), :] = val[:, s * LANES:(s + 1) * LANES]


def _from_token_major(ref, n):
    rows = ref.shape[0] // n
    return jnp.concatenate([ref[pl.ds(s, rows, stride=n), :] for s in range(n)], axis=1)
ROUTE_LANES = ("expert0", "expert1", "weight0", "weight1", "rank0", "rank1")


def _route_kernel(x_ref, gpre_ref, sc_ref, sh_ref, wr_ref, h_ref, route_ref, cnt_ref, *, n_experts):
    first_step = (pl.program_id(0) == 0) & (pl.program_id(1) == 0)

    @pl.when(first_step)
    def _():
        cnt_ref[...] = jnp.zeros_like(cnt_ref)

    h = _rms(x_ref[0], gpre_ref[...]) * (1.0 + sc_ref[0]) + sh_ref[0]
    _to_token_major(h_ref, h)
    h_hi, h_lo = _split_bf16(h)
    w_hi, w_lo = _split_bf16(wr_ref[...])
    logits = _dot(h_hi, w_hi) + (_dot(h_hi, w_lo) + _dot(h_lo, w_hi))
    ts = logits.shape[0]
    lane = lax.broadcasted_iota(jnp.int32, logits.shape, 1)
    lg = jnp.where(lane < n_experts, logits, -jnp.inf)
    m1 = jnp.max(lg, axis=1, keepdims=True)
    i1 = jnp.min(jnp.where(lg == m1, lane, LANES), axis=1, keepdims=True)
    first = lane == i1
    lg2 = jnp.where(first, -jnp.inf, lg)
    m2 = jnp.max(lg2, axis=1, keepdims=True)
    i2 = jnp.min(jnp.where(lg2 == m2, lane, LANES), axis=1, keepdims=True)
    second = lane == i2
    e2 = jnp.exp(m2 - m1)
    w_first = 1.0 / (1.0 + e2)
    w_second = e2 * w_first

    onehot = jnp.where(first | second, 1.0, 0.0)
    r = lax.broadcasted_iota(jnp.int32, (ts, ts), 0)
    c = lax.broadcasted_iota(jnp.int32, (ts, ts), 1)
    lower = jnp.where(c < r, 1.0, 0.0).astype(BF16)
    prefix = _dot(lower, onehot.astype(BF16)) + cnt_ref[...]
    rank_first = jnp.sum(jnp.where(first, prefix, 0.0), axis=1, keepdims=True)
    rank_second = jnp.sum(jnp.where(second, prefix, 0.0), axis=1, keepdims=True)
    cnt_ref[...] += jnp.sum(onehot, axis=0, keepdims=True)

    cols = (i1.astype(F32), i2.astype(F32), w_first, w_second, rank_first, rank_second)
    route = jnp.zeros(logits.shape, F32)
    for n, val in enumerate(cols):
        route = jnp.where(lane == n, val, route)
    route_ref[0] = route


def _route(x, g_pre, scale, shift, w_router):
    bsz, s, d = x.shape
    n_experts = w_router.shape[1]
    ts = ROUTE_ROWS
    row = lambda b, t: (b, t, 0)
    per_b = lambda b, t: (b, 0, 0)
    const = lambda b, t: (0, 0)
    return pl.pallas_call(
        functools.partial(_route_kernel, n_experts=n_experts),
        grid=(bsz, s // ts),
        in_specs=[
            pl.BlockSpec((1, ts, d), row),
            pl.BlockSpec((1, d), const),
            pl.BlockSpec((1, 1, d), per_b),
            pl.BlockSpec((1, 1, d), per_b),
            pl.BlockSpec((d, LANES), const),
        ],
        out_specs=[
            pl.BlockSpec((ts * d // LANES, LANES), lambda b, t: (b * (s // ts) + t, 0)),
            pl.BlockSpec((1, ts, LANES), row),
            pl.BlockSpec((1, LANES), const),
        ],
        out_shape=[
            jax.ShapeDtypeStruct((bsz * s * d // LANES, LANES), F32),
            jax.ShapeDtypeStruct((bsz, s, LANES), F32),
            jax.ShapeDtypeStruct((1, LANES), F32),
        ],
        compiler_params=pltpu.CompilerParams(
            dimension_semantics=("arbitrary", "arbitrary"), vmem_limit_bytes=VMEM_LIMIT),
        name="moe_route",
    )(x, g_pre.reshape(1, d), scale.reshape(bsz, 1, d), shift.reshape(bsz, 1, d),
      jnp.pad(w_router, ((0, 0), (0, LANES - n_experts))))


def _for_each_chunk(start, length, chunk, fn):
    whole = length // chunk

    def loop(c, _):
        fn(start + c * chunk, chunk)
        return 0

    lax.fori_loop(0, whole, loop, 0)
    cur = start + whole * chunk
    size = chunk // 2
    while size >= 1:
        bit = length & size

        @pl.when(bit != 0)
        def _(cur=cur, size=size):
            fn(cur, size)

        cur = cur + bit
        size //= 2


def _scatter_kernel(pos_ref, fill_ref, h_ref, xs_ref, zero_ref, sem, *, n_experts, tile_rows):
    ts = h_ref.shape[0]

    def issue(r, _):
        for slot in range(TOP_K):
            dst = pos_ref[0, 0, slot * ts + r]
            pltpu.make_async_copy(h_ref.at[r], xs_ref.at[dst], sem).start(priority=slot)
        return 0

    lax.fori_loop(0, ts, issue, 0, unroll=4)
    for slot in range(TOP_K):
        pltpu.make_async_copy(h_ref, xs_ref.at[pl.ds(0, ts)], sem).wait()

    @pl.when(pl.program_id(0) == pl.num_programs(0) - 1)
    def _():
        zero_ref[...] = jnp.zeros_like(zero_ref)
        occupied = fill_ref[2 * n_experts] * tile_rows
        spans = [(fill_ref[e], fill_ref[n_experts + e] - fill_ref[e]) for e in range(n_experts)]
        spans.append((occupied, xs_ref.shape[0] - occupied))

        def zero_copy(first, rows):
            return pltpu.make_async_copy(zero_ref.at[pl.ds(0, rows)],
                                         xs_ref.at[pl.ds(first, rows)], sem)

        for first, length in spans:
            _for_each_chunk(first, length, zero_ref.shape[0],
                            lambda row, rows: zero_copy(row, rows).start())
        for first, length in spans:
            _for_each_chunk(first, length, zero_ref.shape[0],
                            lambda row, rows: zero_copy(row, rows).wait())


def _scatter(h, pos, fill, n_rows):
    t = h.shape[0]
    ts = ROUTE_ROWS
    zero_rows = 256
    return pl.pallas_call(
        functools.partial(_scatter_kernel, n_experts=(fill.shape[0] - 1) // 2,
                          tile_rows=FFN_ROWS),
        grid=(t // ts,),
        in_specs=[
            pl.BlockSpec((1, 1, TOP_K * ts), lambda i: (i, 0, 0), memory_space=pltpu.SMEM),
            pl.BlockSpec(memory_space=pltpu.SMEM),
            pl.BlockSpec((ts,) + h.shape[1:], lambda i: (i, 0, 0)),
        ],
        out_specs=pl.BlockSpec(memory_space=pl.ANY),
        out_shape=jax.ShapeDtypeStruct((n_rows,) + h.shape[1:], F32),
        scratch_shapes=[pltpu.VMEM((zero_rows,) + h.shape[1:], F32), pltpu.SemaphoreType.DMA(())],
        compiler_params=pltpu.CompilerParams(
            dimension_semantics=("arbitrary",), vmem_limit_bytes=VMEM_LIMIT),
        name="moe_scatter",
    )(pos, fill, h)


def _experts_kernel(exp_ref, nact_ref, xs_ref, w1_ref, w3_ref, w2_ref, ys_ref, h_ref, acc_ref):
    i = pl.program_id(0)
    k = pl.program_id(1)

    @pl.when(i < nact_ref[0])
    def _():
        @pl.when(k == 0)
        def _():
            h_ref[...] = _from_token_major(xs_ref, h_ref.shape[1] // LANES).astype(BF16)

        h = h_ref[...]
        u = jax.nn.silu(_dot(h, w1_ref[0].astype(BF16))) * _dot(h, w3_ref[0].astype(BF16))
        part = _dot(u.astype(BF16), w2_ref[0].astype(BF16))

        @pl.when(k == 0)
        def _():
            acc_ref[...] = part

        @pl.when(k > 0)
        def _():
            acc_ref[...] += part

        @pl.when(k == pl.num_programs(1) - 1)
        def _():
            _to_token_major(ys_ref, acc_ref[...])

    @pl.when((i >= nact_ref[0]) & (k == 0))
    def _():
        ys_ref[...] = jnp.zeros_like(ys_ref)


def _experts(xs, tile_expert, n_active, w1, w3, w2):
    d, f = w1.shape[1:]
    n_rows = xs.shape[0] * LANES // d
    tm, tf = FFN_ROWS, FFN_COLS
    n_k = f // tf
    n_tiles = n_rows // tm

    def k_of(i, k, nact):
        return jnp.where(i < nact[0], k, n_k - 1)

    return pl.pallas_call(
        _experts_kernel,
        grid_spec=pltpu.PrefetchScalarGridSpec(
            num_scalar_prefetch=2,
            grid=(n_tiles, n_k),
            in_specs=[
                pl.BlockSpec((tm * d // LANES, LANES), lambda i, k, exp, nact: (i, 0)),
                pl.BlockSpec((1, d, tf), lambda i, k, exp, nact: (exp[i], 0, k_of(i, k, nact))),
                pl.BlockSpec((1, d, tf), lambda i, k, exp, nact: (exp[i], 0, k_of(i, k, nact))),
                pl.BlockSpec((1, tf, d), lambda i, k, exp, nact: (exp[i], k_of(i, k, nact), 0)),
            ],
            out_specs=pl.BlockSpec((tm * d // LANES, LANES), lambda i, k, exp, nact: (i, 0)),
            scratch_shapes=[pltpu.VMEM((tm, d), BF16), pltpu.VMEM((tm, d), F32)],
        ),
        out_shape=jax.ShapeDtypeStruct(xs.shape, F32),
        compiler_params=pltpu.CompilerParams(
            dimension_semantics=("arbitrary", "arbitrary"), vmem_limit_bytes=VMEM_LIMIT),
        name="moe_experts",
    )(tile_expert, n_active, xs, w1, w3, w2)


def _combine_kernel(pos_ref, next_pos_ref, route_ref, x_ref, gpost_ref, gate_ref, ys_ref, o_ref,
                    *scratch):
    ts = x_ref.shape[0]
    n = ys_ref.shape[1]
    bufs = (scratch[:TOP_K], scratch[TOP_K:2 * TOP_K])
    sem = scratch[2 * TOP_K]
    i = pl.program_id(0)
    last = pl.num_programs(0) - 1

    def gather(rows_ref, parity):
        def issue(r, _):
            dst = pl.multiple_of(r * n, n)
            for slot in range(TOP_K):
                src = rows_ref[0, 0, slot * ts + r]
                pltpu.make_async_copy(ys_ref.at[src], bufs[parity][slot].at[pl.ds(dst, n)],
                                      sem.at[parity]).start(priority=slot)
            return 0

        lax.fori_loop(0, ts, issue, 0, unroll=4)

    def finish(parity):
        for slot in range(TOP_K):
            buf = bufs[parity][slot]
            pltpu.make_async_copy(buf, buf, sem.at[parity]).wait()
        w0 = ROUTE_LANES.index("weight0")
        route = route_ref[...]
        y = (route[:, w0:w0 + 1] * _from_token_major(bufs[parity][0], n)
             + route[:, w0 + 1:w0 + 2] * _from_token_major(bufs[parity][1], n))
        o_ref[...] = x_ref[...] + gate_ref[0] * _rms(y, gpost_ref[...])

    @pl.when(i == 0)
    def _():
        gather(pos_ref, 0)

    for parity in range(2):
        @pl.when(i % 2 == parity)
        def _(parity=parity):
            @pl.when(i < last)
            def _():
                gather(next_pos_ref, 1 - parity)

            finish(parity)


def _combine(ys, pos, route, x, g_post, gate):
    t, d = x.shape
    bsz = gate.shape[0]
    ts = ROUTE_ROWS
    steps_per_b = t // bsz // ts
    n_steps = t // ts
    return pl.pallas_call(
        _combine_kernel,
        grid=(n_steps,),
        in_specs=[
            pl.BlockSpec((1, 1, TOP_K * ts), lambda i: (i, 0, 0), memory_space=pltpu.SMEM),
            pl.BlockSpec((1, 1, TOP_K * ts), lambda i: (jnp.minimum(i + 1, n_steps - 1), 0, 0),
                         memory_space=pltpu.SMEM),
            pl.BlockSpec((ts, LANES), lambda i: (i, 0)),
            pl.BlockSpec((ts, d), lambda i: (i, 0)),
            pl.BlockSpec((1, d), lambda i: (0, 0)),
            pl.BlockSpec((1, 1, d), lambda i: (i // steps_per_b, 0, 0)),
            pl.BlockSpec(memory_space=pl.ANY),
        ],
        out_specs=pl.BlockSpec((ts, d), lambda i: (i, 0)),
        out_shape=jax.ShapeDtypeStruct((t, d), F32),
        scratch_shapes=[pltpu.VMEM((ts * d // LANES, LANES), F32)] * (2 * TOP_K)
        + [pltpu.SemaphoreType.DMA((2,))],
        compiler_params=pltpu.CompilerParams(
            dimension_semantics=("arbitrary",), vmem_limit_bytes=VMEM_LIMIT),
        name="moe_combine",
    )(pos, pos, route, x, g_post.reshape(1, d), gate, ys)


def _moe(x, g_pre, scale, shift, w_router, w1, w3, w2, g_post, gate):
    bsz, s, d = x.shape
    t = bsz * s
    n_experts = w_router.shape[1]
    ts, tm = ROUTE_ROWS, FFN_ROWS
    h, route, counts = _route(x, g_pre, scale, shift, w_router)
    route = route.reshape(t, LANES)

    lane_of = ROUTE_LANES.index
    expert = route[:, lane_of("expert0"):lane_of("expert0") + TOP_K].astype(jnp.int32)
    rank = route[:, lane_of("rank0"):lane_of("rank0") + TOP_K].astype(jnp.int32)
    counts = counts[0, :n_experts].astype(jnp.int32)
    tiles_per = (counts + tm - 1) // tm
    tile_end = jnp.cumsum(tiles_per)
    first_row = (tile_end - tiles_per) * tm
    n_active = tile_end[-1]
    n_tiles = TOP_K * t // tm + n_experts
    is_expert = expert[..., None] == jnp.arange(n_experts, dtype=jnp.int32)
    pos = jnp.sum(jnp.where(is_expert, first_row, 0), axis=-1) + rank
    pos = pos.reshape(t // ts, ts, TOP_K).transpose(0, 2, 1).reshape(t // ts, 1, TOP_K * ts)
    fill = jnp.concatenate([first_row + counts, tile_end * tm, n_active[None]]).astype(jnp.int32)
    idx = jnp.minimum(jnp.arange(n_tiles, dtype=jnp.int32), n_active - 1)
    tile_expert = jnp.sum(idx[:, None] >= tile_end[None, :], axis=1).astype(jnp.int32)

    token_major = (-1, d // LANES, LANES)
    xs = _scatter(h.reshape(token_major), pos, fill, n_tiles * tm)
    ys = _experts(xs.reshape(-1, LANES), tile_expert, n_active.reshape(1), w1, w3, w2)
    out = _combine(ys.reshape(token_major), pos, route, x.reshape(t, d), g_post,
                   gate.reshape(bsz, 1, d))
    return out.reshape(bsz, s, d)


def kernel(x, c, w_mod, b_mod, g_mix_pre, g_mix_post, g_ffn_pre, g_ffn_post, w_in, w_proj_sb,
           w_proj_moba, w_out, rel_bias, w1_dense, w3_dense, w2_dense, w_router, w1_moe, w3_moe,
           w2_moe):
    depth = w_mod.shape[0]
    d = x.shape[-1]
    qkv = 3 * W_BRANCH
    col = jnp.arange(w_in.shape[-1])
    is_q = (col < W_BRANCH) | ((col >= qkv) & (col < qkv + W_BRANCH))
    col_scale = jnp.where(is_q, HEAD_DIM ** -0.5, 1.0).astype(F32)

    mod = _modulation(c, w_mod, b_mod)
    bias_tiles = _moba_bias_tiles(rel_bias)
    for l in range(depth):
        sh_mix, sc_mix, gt_mix, sh_ffn, sc_ffn, gt_ffn = jnp.split(mod[l], N_MOD, axis=-1)
        proj = _inproj(x, g_mix_pre[l], sc_mix, sh_mix, (w_in[l] * col_scale).astype(BF16))
        o_a = _stick_breaking(proj, 0)
        o_b = _moba(proj, qkv // LANES, bias_tiles)
        x = _merge(o_a, o_b, proj, 2 * qkv // d, x, w_proj_sb[l].astype(BF16),
                   w_proj_moba[l].astype(BF16), w_out[l].astype(BF16), g_mix_post[l], gt_mix)
        j = l // 2
        if l % 2 == 0:
            x = _ffn(x, g_ffn_pre[l], sc_ffn, sh_ffn, w1_dense[j], w3_dense[j], w2_dense[j],
                     g_ffn_post[l], gt_ffn)
        else:
            x = _moe(x, g_ffn_pre[l], sc_ffn, sh_ffn, w_router[j], w1_moe[j], w3_moe[j],
                     w2_moe[j], g_ffn_post[l], gt_ffn)
    return x
```

```python
import functools
import math

import jax
import jax.numpy as jnp
from jax import lax
from jax.experimental import pallas as pl
from jax.experimental.pallas import tpu as pltpu

F32 = jnp.float32
BF16 = jnp.bfloat16

EPS = 1e-6
NEG_INF = -1e30
HEAD_DIM = 64
N_HEADS = 8
W_BRANCH = N_HEADS * HEAD_DIM
LANES = 128
SUBLANES = 8
HEADS_PER_TILE = LANES // HEAD_DIM
N_PAIRS = N_HEADS // HEADS_PER_TILE
MOBA_BLOCK = 256
MOBA_TOPK = 3
REL_BUCKETS = 32
REL_MAX_DIST = 128
N_MOD = 6
TOP_K = 2
ATT_BLOCK = 256
PROJ_ROWS = 1024
VMEM_LIMIT = 56 * 1024 * 1024


def _dot(a, b):
    return jnp.dot(a, b, preferred_element_type=F32)


def _dot_nt(a, b):
    return lax.dot_general(a, b, (((1,), (1,)), ((), ())), preferred_element_type=F32)


def _split_bf16(x):
    hi = x.astype(BF16)
    lo = (x - hi.astype(F32)).astype(BF16)
    return hi, lo


def _rms(x, g):
    return x * lax.rsqrt(jnp.mean(x * x, axis=-1, keepdims=True) + EPS) * g


def _mod_kernel(c_ref, w_ref, b_ref, o_ref):
    c = c_ref[...]
    cond = c * jax.nn.sigmoid(c)
    o_ref[0] = jnp.dot(cond, w_ref[0], preferred_element_type=F32,
                       precision=lax.Precision.HIGHEST) + b_ref[0]


def _modulation(c, w_mod, b_mod):
    depth, d, n = w_mod.shape
    bsz = c.shape[0]
    tn = 1536
    return pl.pallas_call(
        _mod_kernel,
        grid=(depth, n // tn),
        in_specs=[
            pl.BlockSpec((bsz, d), lambda l, j: (0, 0)),
            pl.BlockSpec((1, d, tn), lambda l, j: (l, 0, j)),
            pl.BlockSpec((1, 1, tn), lambda l, j: (l, 0, j)),
        ],
        out_specs=pl.BlockSpec((1, bsz, tn), lambda l, j: (l, 0, j)),
        out_shape=jax.ShapeDtypeStruct((depth, bsz, n), F32),
        compiler_params=pltpu.CompilerParams(
            dimension_semantics=("arbitrary", "arbitrary"), vmem_limit_bytes=VMEM_LIMIT),
        name="modulation",
    )(c, w_mod, b_mod.reshape(depth, 1, n))


def _inproj_kernel(x_ref, g_ref, sc_ref, sh_ref, w_ref, o_ref, *, tn):
    h = _rms(x_ref[0], g_ref[...]) * (1.0 + sc_ref[0]) + sh_ref[0]
    hb = h.astype(BF16)
    for n in range(w_ref.shape[1] // tn):
        o_ref[0, :, n * tn:(n + 1) * tn] = _dot(hb, w_ref[:, n * tn:(n + 1) * tn]).astype(BF16)


def _inproj(x, g, scale, shift, w):
    bsz, s, d = x.shape
    n = w.shape[1]
    tm = PROJ_ROWS
    return pl.pallas_call(
        functools.partial(_inproj_kernel, tn=1024),
        grid=(bsz, s // tm),
        in_specs=[
            pl.BlockSpec((1, tm, d), lambda b, t: (b, t, 0)),
            pl.BlockSpec((1, d), lambda b, t: (0, 0)),
            pl.BlockSpec((1, 1, d), lambda b, t: (b, 0, 0)),
            pl.BlockSpec((1, 1, d), lambda b, t: (b, 0, 0)),
            pl.BlockSpec((d, n), lambda b, t: (0, 0), pipeline_mode=pl.Buffered(1)),
        ],
        out_specs=pl.BlockSpec((1, tm, n), lambda b, t: (b, t, 0)),
        out_shape=jax.ShapeDtypeStruct((bsz, s, n), BF16),
        compiler_params=pltpu.CompilerParams(
            dimension_semantics=("arbitrary", "arbitrary"), vmem_limit_bytes=VMEM_LIMIT),
        name="inproj",
    )(x, g.reshape(1, d), scale.reshape(bsz, 1, d), shift.reshape(bsz, 1, d), w)


SOFTPLUS_LINEAR = 40.0


def _softplus(z):
    return jnp.maximum(z, jnp.log(1.0 + jnp.exp(jnp.minimum(z, SOFTPLUS_LINEAR))))


def _sb_kernel(q_ref, k_ref, v_ref, o_ref, qs_ref, acc_ref, carry_ref):
    tq = ATT_BLOCK
    n_blocks = k_ref.shape[1] // tq
    rows_per_blk = HEADS_PER_TILE * tq
    lane = lax.broadcasted_iota(jnp.int32, (1, LANES), 1)
    row = lax.broadcasted_iota(jnp.int32, (tq, tq), 0)
    col = lax.broadcasted_iota(jnp.int32, (tq, tq), 1)
    upper = jnp.where(row > col, 1.0, 0.0).astype(BF16)
    causal = jnp.concatenate([col < row] * HEADS_PER_TILE, axis=0)
    q = q_ref[0]
    for i in range(n_blocks):
        qi = q[i * tq:(i + 1) * tq]
        for h in range(HEADS_PER_TILE):
            row0 = (HEADS_PER_TILE * i + h) * tq
            qs_ref[row0:row0 + tq, :] = jnp.where(lane // HEAD_DIM == h, qi, jnp.zeros_like(qi))

    for j in reversed(range(n_blocks)):
        row0 = j * rows_per_blk
        far0 = row0 + rows_per_blk
        has_far = j + 1 < n_blocks
        z = _dot_nt(qs_ref[row0:, :], k_ref[0, j * tq:(j + 1) * tq, :])
        sp = _softplus(z)
        sp_diag = jnp.where(causal, sp[:rows_per_blk], 0.0)
        sp = jnp.concatenate([sp_diag, sp[rows_per_blk:]], axis=0) if has_far else sp_diag
        logit = z - sp - _dot(sp.astype(BF16), upper)
        att_diag = jnp.where(causal, jnp.exp(logit[:rows_per_blk]), 0.0)
        if has_far:
            att_far = jnp.exp(logit[rows_per_blk:] - carry_ref[far0:, :])
            att = jnp.concatenate([att_diag, att_far], axis=0)
        else:
            att = att_diag
        part = _dot(att.astype(BF16), v_ref[0, j * tq:(j + 1) * tq, :])
        rs = jnp.sum(sp, axis=1, keepdims=True)
        acc_ref[row0:far0, :] = part[:rows_per_blk]
        carry_ref[row0:far0, :] = rs[:rows_per_blk]
        if has_far:
            acc_ref[far0:, :] += part[rows_per_blk:]
            carry_ref[far0:, :] += rs[rows_per_blk:]

    for i in range(n_blocks):
        blk = acc_ref[i * rows_per_blk:(i + 1) * rows_per_blk, :]
        o_ref[0, i * tq:(i + 1) * tq, :] = jnp.where(lane < HEAD_DIM, blk[:tq], blk[tq:]).astype(BF16)


def _stick_breaking(proj, col0):
    bsz, s, _ = proj.shape
    assert s % ATT_BLOCK == 0
    rows = HEADS_PER_TILE * s
    return pl.pallas_call(
        _sb_kernel,
        grid=(bsz, N_PAIRS),
        in_specs=[
            pl.BlockSpec((1, s, LANES), lambda b, p: (b, 0, col0 + p)),
            pl.BlockSpec((1, s, LANES), lambda b, p: (b, 0, col0 + N_PAIRS + p)),
            pl.BlockSpec((1, s, LANES), lambda b, p: (b, 0, col0 + 2 * N_PAIRS + p)),
        ],
        out_specs=pl.BlockSpec((1, s, LANES), lambda b, p: (b, 0, p)),
        out_shape=jax.ShapeDtypeStruct((bsz, s, W_BRANCH), BF16),
        scratch_shapes=[pltpu.VMEM((rows, LANES), BF16),
                        pltpu.VMEM((rows, LANES), F32),
                        pltpu.VMEM((rows, 1), F32)],
        compiler_params=pltpu.CompilerParams(
            dimension_semantics=("arbitrary", "arbitrary"), vmem_limit_bytes=VMEM_LIMIT),
        name="stick_breaking",
    )(proj, proj, proj)


def _rel_bucket(dist):
    max_exact = REL_BUCKETS // 2
    n = jnp.maximum(dist, 0)
    nf = jnp.maximum(n, 1).astype(F32)
    large = max_exact + (jnp.log(nf / max_exact) / math.log(REL_MAX_DIST / max_exact)
                         * (REL_BUCKETS - max_exact)).astype(jnp.int32)
    large = jnp.minimum(large, REL_BUCKETS - 1)
    return jnp.where(n < max_exact, n, large)


def _moba_bias_tiles(rel_bias):
    assert MOBA_BLOCK >= REL_MAX_DIST
    t = jnp.arange(MOBA_BLOCK)[:, None]
    s = jnp.arange(MOBA_BLOCK)[None, :]
    d_own = t - s
    rel = rel_bias.astype(F32) - rel_bias[REL_BUCKETS - 1].astype(F32)

    def lookup(dist):
        onehot = jax.nn.one_hot(_rel_bucket(dist), REL_BUCKETS, dtype=F32)
        return jnp.einsum("tsk,kh->hts", onehot, rel, precision=lax.Precision.HIGHEST)

    own = jnp.where((d_own >= 0)[None], lookup(d_own), NEG_INF)
    return jnp.stack([own, lookup(d_own + MOBA_BLOCK)], axis=1)


def _moba_kernel(q_ref, k_ref, v_ref, bias_ref, o_ref,
                 ka_ref, va_ref, qa_ref, m_ref, mrep_ref, accl_ref, *, n_blocks):
    tq = MOBA_BLOCK
    s_len = k_ref.shape[1]
    nb_pad = 16
    rows_per_blk = HEADS_PER_TILE * tq
    lane = lax.broadcasted_iota(jnp.int32, (1, LANES), 1)
    head_lanes = [lane // HEAD_DIM == h for h in range(HEADS_PER_TILE)]
    q = q_ref[0]
    k = k_ref[0]

    key_blk = lax.broadcasted_iota(jnp.int32, (s_len, LANES), 0) // tq
    ka_ref[:, :LANES] = k
    ka_ref[:, LANES:] = jnp.where(lane == key_blk, 1.0, 0.0).astype(BF16)
    va_ref[:, :LANES] = v_ref[0]
    va_ref[:, LANES:] = jnp.ones((s_len, LANES), BF16)

    r = lax.broadcasted_iota(jnp.int32, (nb_pad, s_len), 0)
    c = lax.broadcasted_iota(jnp.int32, (nb_pad, s_len), 1)
    pool = jnp.where(c // tq == r, 1.0 / tq, 0.0).astype(BF16)
    km_hi, km_lo = _split_bf16(_dot(pool, k))
    means = [jnp.where(hl, km, jnp.zeros_like(km)) for hl in head_lanes for km in (km_hi, km_lo)]
    gates = _dot_nt(jnp.concatenate(means, axis=0), q)
    own = c // tq
    masks = []
    for h in range(HEADS_PER_TILE):
        gate = (gates[2 * h * nb_pad:(2 * h + 1) * nb_pad]
                + gates[(2 * h + 1) * nb_pad:(2 * h + 2) * nb_pad])
        cnt = jnp.zeros((nb_pad, s_len), F32)
        for n in range(n_blocks):
            gn = gate[n:n + 1, :]
            beats = jnp.where(gn > gate, 1.0, jnp.where((gn == gate) & (n < r), 1.0, 0.0))
            cnt = cnt + jnp.where(n < own, beats, 0.0)
        keep = (r == own) | ((r < own) & (cnt < MOBA_TOPK))
        masks += [jnp.where(keep, 0.0, NEG_INF).astype(BF16),
                  jnp.zeros((LANES - nb_pad, s_len), BF16)]
    masks = jnp.concatenate(masks, axis=0)
    er = lax.broadcasted_iota(jnp.int32, (tq, tq), 0)
    ec = lax.broadcasted_iota(jnp.int32, (tq, tq), 1)
    eye = jnp.where(er == ec, 1.0, 0.0).astype(BF16)
    for i in range(n_blocks):
        qi = q[i * tq:(i + 1) * tq]
        mask_rows = _dot_nt(eye, masks[:, i * tq:(i + 1) * tq]).astype(BF16)
        for h in range(HEADS_PER_TILE):
            row0 = (HEADS_PER_TILE * i + h) * tq
            qa_ref[row0:row0 + tq, :LANES] = jnp.where(head_lanes[h], qi, jnp.zeros_like(qi))
            qa_ref[row0:row0 + tq, LANES:] = mask_rows[:, h * LANES:(h + 1) * LANES]

    bias = [jnp.concatenate([bias_ref[h, which] for h in range(HEADS_PER_TILE)], axis=0)
            for which in range(2)]
    def scores(j):
        row0 = j * rows_per_blk
        s = _dot_nt(qa_ref[row0:, :], ka_ref[j * tq:(j + 1) * tq, :])
        parts = [s[:rows_per_blk] + bias[0]]
        if j + 1 < n_blocks:
            parts.append(s[rows_per_blk:2 * rows_per_blk] + bias[1])
        if j + 2 < n_blocks:
            parts.append(s[2 * rows_per_blk:])
        return jnp.concatenate(parts, axis=0)

    for j in range(n_blocks):
        row0 = j * rows_per_blk
        m = jnp.max(scores(j), axis=1, keepdims=True)
        if j == 0:
            m_ref[...] = m
        else:
            m_ref[row0:, :] = jnp.maximum(m_ref[row0:, :], m)
    mrep_ref[...] = jnp.broadcast_to(m_ref[...], mrep_ref.shape)
    for j in range(n_blocks):
        row0 = j * rows_per_blk
        m = mrep_ref[row0:, :]
        pr = jnp.exp(scores(j) - jnp.concatenate([m] * (tq // LANES), axis=1))
        ov = _dot(pr.astype(BF16), va_ref[j * tq:(j + 1) * tq, :])
        if j == 0:
            accl_ref[...] = ov
        else:
            accl_ref[row0:, :] += ov

    for i in range(n_blocks):
        blk = accl_ref[i * rows_per_blk:(i + 1) * rows_per_blk, :]
        out = blk[:, :LANES] / blk[:, LANES:]
        o_ref[0, i * tq:(i + 1) * tq, :] = jnp.where(lane < HEAD_DIM, out[:tq], out[tq:]).astype(BF16)


def _moba(proj, col0, bias_tiles):
    bsz, s, _ = proj.shape
    tq = MOBA_BLOCK
    assert s % tq == 0
    n_blocks = s // tq
    assert n_blocks <= 16
    rows = HEADS_PER_TILE * s
    return pl.pallas_call(
        functools.partial(_moba_kernel, n_blocks=n_blocks),
        grid=(bsz, N_PAIRS),
        in_specs=[
            pl.BlockSpec((1, s, LANES), lambda b, p: (b, 0, col0 + p)),
            pl.BlockSpec((1, s, LANES), lambda b, p: (b, 0, col0 + N_PAIRS + p)),
            pl.BlockSpec((1, s, LANES), lambda b, p: (b, 0, col0 + 2 * N_PAIRS + p)),
            pl.BlockSpec((HEADS_PER_TILE, 2, tq, tq), lambda b, p: (p, 0, 0, 0)),
        ],
        out_specs=pl.BlockSpec((1, s, LANES), lambda b, p: (b, 0, p)),
        out_shape=jax.ShapeDtypeStruct((bsz, s, W_BRANCH), BF16),
        scratch_shapes=[
            pltpu.VMEM((s, 2 * LANES), BF16),
            pltpu.VMEM((s, 2 * LANES), BF16),
            pltpu.VMEM((rows, 2 * LANES), BF16),
            pltpu.VMEM((rows, 1), F32),
            pltpu.VMEM((rows, LANES), F32),
            pltpu.VMEM((rows, 2 * LANES), F32),
        ],
        compiler_params=pltpu.CompilerParams(
            dimension_semantics=("arbitrary", "arbitrary"), vmem_limit_bytes=VMEM_LIMIT),
        name="moba",
    )(proj, proj, proj, bias_tiles)


def _merge_kernel(oa_ref, ob_ref, ga_ref, gb_ref, x_ref, wa_ref, wb_ref, wo_ref, g_ref, gate_ref,
                  o_ref):
    pa = _dot(oa_ref[0], wa_ref[...])
    pb = _dot(ob_ref[0], wb_ref[...])
    merged = (jax.nn.sigmoid(ga_ref[0].astype(F32)) * pa
              + jax.nn.sigmoid(gb_ref[0].astype(F32)) * pb)
    y = _dot(merged.astype(BF16), wo_ref[...])
    o_ref[0] = x_ref[0] + gate_ref[0] * _rms(y, g_ref[...])


def _merge(o_a, o_b, proj, gate_col, x, w_a, w_b, w_o, g_post, gate):
    bsz, s, d = x.shape
    tm = PROJ_ROWS
    const = lambda b, t: (0, 0)
    return pl.pallas_call(
        _merge_kernel,
        grid=(bsz, s // tm),
        in_specs=[
            pl.BlockSpec((1, tm, W_BRANCH), lambda b, t: (b, t, 0)),
            pl.BlockSpec((1, tm, W_BRANCH), lambda b, t: (b, t, 0)),
            pl.BlockSpec((1, tm, d), lambda b, t: (b, t, gate_col)),
            pl.BlockSpec((1, tm, d), lambda b, t: (b, t, gate_col + 1)),
            pl.BlockSpec((1, tm, d), lambda b, t: (b, t, 0)),
            pl.BlockSpec((W_BRANCH, d), const),
            pl.BlockSpec((W_BRANCH, d), const),
            pl.BlockSpec((d, d), const),
            pl.BlockSpec((1, d), const),
            pl.BlockSpec((1, 1, d), lambda b, t: (b, 0, 0)),
        ],
        out_specs=pl.BlockSpec((1, tm, d), lambda b, t: (b, t, 0)),
        out_shape=jax.ShapeDtypeStruct((bsz, s, d), F32),
        compiler_params=pltpu.CompilerParams(
            dimension_semantics=("arbitrary", "arbitrary"), vmem_limit_bytes=VMEM_LIMIT),
        name="merge",
    )(o_a, o_b, proj, proj, x, w_a, w_b, w_o, g_post.reshape(1, d), gate.reshape(bsz, 1, d))


FFN_ROWS = 1024
FFN_COLS = 512


def _ffn_kernel(x_ref, gpre_ref, sc_ref, sh_ref, w1_ref, w3_ref, w2_ref, gpost_ref, gate_ref,
                o_ref, h_ref, acc_ref):
    k = pl.program_id(2)

    @pl.when(k == 0)
    def _():
        h = _rms(x_ref[0], gpre_ref[...]) * (1.0 + sc_ref[0]) + sh_ref[0]
        h_ref[...] = h.astype(BF16)
        acc_ref[...] = jnp.zeros_like(acc_ref)

    h = h_ref[...]
    u = jax.nn.silu(_dot(h, w1_ref[...].astype(BF16))) * _dot(h, w3_ref[...].astype(BF16))
    acc_ref[...] += _dot(u.astype(BF16), w2_ref[...].astype(BF16))

    @pl.when(k == pl.num_programs(2) - 1)
    def _():
        o_ref[0] = x_ref[0] + gate_ref[0] * _rms(acc_ref[...], gpost_ref[...])


def _ffn(x, g_pre, scale, shift, w1, w3, w2, g_post, gate):
    bsz, s, d = x.shape
    f = w1.shape[1]
    tm, tf = FFN_ROWS, FFN_COLS
    row = lambda b, t, k: (b, t, 0)
    per_b = lambda b, t, k: (b, 0, 0)
    const = lambda b, t, k: (0, 0)
    return pl.pallas_call(
        _ffn_kernel,
        grid=(bsz, s // tm, f // tf),
        in_specs=[
            pl.BlockSpec((1, tm, d), row),
            pl.BlockSpec((1, d), const),
            pl.BlockSpec((1, 1, d), per_b),
            pl.BlockSpec((1, 1, d), per_b),
            pl.BlockSpec((d, tf), lambda b, t, k: (0, k)),
            pl.BlockSpec((d, tf), lambda b, t, k: (0, k)),
            pl.BlockSpec((tf, d), lambda b, t, k: (k, 0)),
            pl.BlockSpec((1, d), const),
            pl.BlockSpec((1, 1, d), per_b),
        ],
        out_specs=pl.BlockSpec((1, tm, d), row),
        out_shape=jax.ShapeDtypeStruct((bsz, s, d), F32),
        scratch_shapes=[pltpu.VMEM((tm, d), BF16), pltpu.VMEM((tm, d), F32)],
        compiler_params=pltpu.CompilerParams(
            dimension_semantics=("arbitrary",) * 3, vmem_limit_bytes=VMEM_LIMIT),
        name="ffn_dense",
    )(x, g_pre.reshape(1, d), scale.reshape(bsz, 1, d), shift.reshape(bsz, 1, d), w1, w3, w2,
      g_post.reshape(1, d), gate.reshape(bsz, 1, d))


ROUTE_ROWS = 1024
COMBINE_ROWS = 512


def _to_token_major(ref, val):
    rows, n = val.shape[0], val.shape[1] // LANES
    for s in range(n):
        ref[pl.ds(s, rows, stride=n), :] = val[:, s * LANES:(s + 1) * LANES]


def _from_token_major(ref, n):
    rows = ref.shape[0] // n
    return jnp.concatenate([ref[pl.ds(s, rows, stride=n), :] for s in range(n)], axis=1)
ROUTE_LANES = ("expert0", "expert1", "weight0", "weight1", "rank0", "rank1")


def _route_kernel(x_ref, gpre_ref, sc_ref, sh_ref, wr_ref, h_ref, route_ref, cnt_ref, *, n_experts):
    first_step = (pl.program_id(0) == 0) & (pl.program_id(1) == 0)

    @pl.when(first_step)
    def _():
        cnt_ref[...] = jnp.zeros_like(cnt_ref)

    h = _rms(x_ref[0], gpre_ref[...]) * (1.0 + sc_ref[0]) + sh_ref[0]
    _to_token_major(h_ref, h)
    h_hi, h_lo = _split_bf16(h)
    w_hi, w_lo = _split_bf16(wr_ref[...])
    logits = _dot(h_hi, w_hi) + (_dot(h_hi, w_lo) + _dot(h_lo, w_hi))
    ts = logits.shape[0]
    lane = lax.broadcasted_iota(jnp.int32, logits.shape, 1)
    lg = jnp.where(lane < n_experts, logits, -jnp.inf)
    m1 = jnp.max(lg, axis=1, keepdims=True)
    i1 = jnp.min(jnp.where(lg == m1, lane, LANES), axis=1, keepdims=True)
    first = lane == i1
    lg2 = jnp.where(first, -jnp.inf, lg)
    m2 = jnp.max(lg2, axis=1, keepdims=True)
    i2 = jnp.min(jnp.where(lg2 == m2, lane, LANES), axis=1, keepdims=True)
    second = lane == i2
    e2 = jnp.exp(m2 - m1)
    w_first = 1.0 / (1.0 + e2)
    w_second = e2 * w_first

    onehot = jnp.where(first | second, 1.0, 0.0)
    r = lax.broadcasted_iota(jnp.int32, (ts, ts), 0)
    c = lax.broadcasted_iota(jnp.int32, (ts, ts), 1)
    lower = jnp.where(c < r, 1.0, 0.0).astype(BF16)
    prefix = _dot(lower, onehot.astype(BF16)) + cnt_ref[...]
    rank_first = jnp.sum(jnp.where(first, prefix, 0.0), axis=1, keepdims=True)
    rank_second = jnp.sum(jnp.where(second, prefix, 0.0), axis=1, keepdims=True)
    cnt_ref[...] += jnp.sum(onehot, axis=0, keepdims=True)

    cols = (i1.astype(F32), i2.astype(F32), w_first, w_second, rank_first, rank_second)
    route = jnp.zeros(logits.shape, F32)
    for n, val in enumerate(cols):
        route = jnp.where(lane == n, val, route)
    route_ref[0] = route


def _route(x, g_pre, scale, shift, w_router):
    bsz, s, d = x.shape
    n_experts = w_router.shape[1]
    ts = ROUTE_ROWS
    row = lambda b, t: (b, t, 0)
    per_b = lambda b, t: (b, 0, 0)
    const = lambda b, t: (0, 0)
    return pl.pallas_call(
        functools.partial(_route_kernel, n_experts=n_experts),
        grid=(bsz, s // ts),
        in_specs=[
            pl.BlockSpec((1, ts, d), row),
            pl.BlockSpec((1, d), const),
            pl.BlockSpec((1, 1, d), per_b),
            pl.BlockSpec((1, 1, d), per_b),
            pl.BlockSpec((d, LANES), const),
        ],
        out_specs=[
            pl.BlockSpec((ts * d // LANES, LANES), lambda b, t: (b * (s // ts) + t, 0)),
            pl.BlockSpec((1, ts, LANES), row),
            pl.BlockSpec((1, LANES), const),
        ],
        out_shape=[
            jax.ShapeDtypeStruct((bsz * s * d // LANES, LANES), F32),
            jax.ShapeDtypeStruct((bsz, s, LANES), F32),
            jax.ShapeDtypeStruct((1, LANES), F32),
        ],
        compiler_params=pltpu.CompilerParams(
            dimension_semantics=("arbitrary", "arbitrary"), vmem_limit_bytes=VMEM_LIMIT),
        name="moe_route",
    )(x, g_pre.reshape(1, d), scale.reshape(bsz, 1, d), shift.reshape(bsz, 1, d),
      jnp.pad(w_router, ((0, 0), (0, LANES - n_experts))))


def _for_each_chunk(start, length, chunk, fn):
    whole = length // chunk

    def loop(c, _):
        fn(start + c * chunk, chunk)
        return 0

    lax.fori_loop(0, whole, loop, 0)
    cur = start + whole * chunk
    size = chunk // 2
    while size >= 1:
        bit = length & size

        @pl.when(bit != 0)
        def _(cur=cur, size=size):
            fn(cur, size)

        cur = cur + bit
        size //= 2


def _scatter_kernel(pos_ref, fill_ref, h_ref, xs_ref, zero_ref, sem, *, n_experts, tile_rows):
    ts = h_ref.shape[0]

    def issue(r, _):
        for slot in range(TOP_K):
            dst = pos_ref[0, 0, slot * ts + r]
            pltpu.make_async_copy(h_ref.at[r], xs_ref.at[dst], sem).start(priority=slot)
        return 0

    lax.fori_loop(0, ts, issue, 0, unroll=4)
    for slot in range(TOP_K):
        pltpu.make_async_copy(h_ref, xs_ref.at[pl.ds(0, ts)], sem).wait()

    @pl.when(pl.program_id(0) == pl.num_programs(0) - 1)
    def _():
        zero_ref[...] = jnp.zeros_like(zero_ref)
        occupied = fill_ref[2 * n_experts] * tile_rows
        spans = [(fill_ref[e], fill_ref[n_experts + e] - fill_ref[e]) for e in range(n_experts)]
        spans.append((occupied, xs_ref.shape[0] - occupied))

        def zero_copy(first, rows):
            return pltpu.make_async_copy(zero_ref.at[pl.ds(0, rows)],
                                         xs_ref.at[pl.ds(first, rows)], sem)

        for first, length in spans:
            _for_each_chunk(first, length, zero_ref.shape[0],
                            lambda row, rows: zero_copy(row, rows).start())
        for first, length in spans:
            _for_each_chunk(first, length, zero_ref.shape[0],
                            lambda row, rows: zero_copy(row, rows).wait())


def _scatter(h, pos, fill, n_rows):
    t = h.shape[0]
    ts = ROUTE_ROWS
    zero_rows = 256
    return pl.pallas_call(
        functools.partial(_scatter_kernel, n_experts=(fill.shape[0] - 1) // 2,
                          tile_rows=FFN_ROWS),
        grid=(t // ts,),
        in_specs=[
            pl.BlockSpec((1, 1, TOP_K * ts), lambda i: (i, 0, 0), memory_space=pltpu.SMEM),
            pl.BlockSpec(memory_space=pltpu.SMEM),
            pl.BlockSpec((ts,) + h.shape[1:], lambda i: (i, 0, 0)),
        ],
        out_specs=pl.BlockSpec(memory_space=pl.ANY),
        out_shape=jax.ShapeDtypeStruct((n_rows,) + h.shape[1:], F32),
        scratch_shapes=[pltpu.VMEM((zero_rows,) + h.shape[1:], F32), pltpu.SemaphoreType.DMA(())],
        compiler_params=pltpu.CompilerParams(
            dimension_semantics=("arbitrary",), vmem_limit_bytes=VMEM_LIMIT),
        name="moe_scatter",
    )(pos, fill, h)


def _experts_kernel(exp_ref, nact_ref, xs_ref, w1_ref, w3_ref, w2_ref, ys_ref, h_ref, acc_ref):
    i = pl.program_id(0)
    k = pl.program_id(1)

    @pl.when(i < nact_ref[0])
    def _():
        @pl.when(k == 0)
        def _():
            h_ref[...] = _from_token_major(xs_ref, h_ref.shape[1] // LANES).astype(BF16)

        h = h_ref[...]
        u = jax.nn.silu(_dot(h, w1_ref[0].astype(BF16))) * _dot(h, w3_ref[0].astype(BF16))
        part = _dot(u.astype(BF16), w2_ref[0].astype(BF16))

        @pl.when(k == 0)
        def _():
            acc_ref[...] = part

        @pl.when(k > 0)
        def _():
            acc_ref[...] += part

        @pl.when(k == pl.num_programs(1) - 1)
        def _():
            _to_token_major(ys_ref, acc_ref[...])

    @pl.when((i >= nact_ref[0]) & (k == 0))
    def _():
        ys_ref[...] = jnp.zeros_like(ys_ref)


def _experts(xs, tile_expert, n_active, w1, w3, w2):
    d, f = w1.shape[1:]
    n_rows = xs.shape[0] * LANES // d
    tm, tf = FFN_ROWS, FFN_COLS
    n_k = f // tf
    n_tiles = n_rows // tm

    def k_of(i, k, nact):
        return jnp.where(i < nact[0], k, n_k - 1)

    return pl.pallas_call(
        _experts_kernel,
        grid_spec=pltpu.PrefetchScalarGridSpec(
            num_scalar_prefetch=2,
            grid=(n_tiles, n_k),
            in_specs=[
                pl.BlockSpec((tm * d // LANES, LANES), lambda i, k, exp, nact: (i, 0)),
                pl.BlockSpec((1, d, tf), lambda i, k, exp, nact: (exp[i], 0, k_of(i, k, nact))),
                pl.BlockSpec((1, d, tf), lambda i, k, exp, nact: (exp[i], 0, k_of(i, k, nact))),
                pl.BlockSpec((1, tf, d), lambda i, k, exp, nact: (exp[i], k_of(i, k, nact), 0)),
            ],
            out_specs=pl.BlockSpec((tm * d // LANES, LANES), lambda i, k, exp, nact: (i, 0)),
            scratch_shapes=[pltpu.VMEM((tm, d), BF16), pltpu.VMEM((tm, d), F32)],
        ),
        out_shape=jax.ShapeDtypeStruct(xs.shape, F32),
        compiler_params=pltpu.CompilerParams(
            dimension_semantics=("arbitrary", "arbitrary"), vmem_limit_bytes=VMEM_LIMIT),
        name="moe_experts",
    )(tile_expert, n_active, xs, w1, w3, w2)


def _combine_kernel(pos_ref, next_pos_ref, route_ref, x_ref, gpost_ref, gate_ref, ys_ref, o_ref,
                    *scratch):
    ts = x_ref.shape[0]
    n = ys_ref.shape[1]
    bufs = (scratch[:TOP_K], scratch[TOP_K:2 * TOP_K])
    sem = scratch[2 * TOP_K]
    i = pl.program_id(0)
    last = pl.num_programs(0) - 1

    def gather(rows_ref, parity):
        def issue(r, _):
            dst = pl.multiple_of(r * n, n)
            for slot in range(TOP_K):
                src = rows_ref[0, 0, slot * ts + r]
                pltpu.make_async_copy(ys_ref.at[src], bufs[parity][slot].at[pl.ds(dst, n)],
                                      sem.at[parity]).start(priority=slot)
            return 0

        lax.fori_loop(0, ts, issue, 0, unroll=4)

    def finish(parity):
        for slot in range(TOP_K):
            buf = bufs[parity][slot]
            pltpu.make_async_copy(buf, buf, sem.at[parity]).wait()
        w0 = ROUTE_LANES.index("weight0")
        route = route_ref[...]
        y = (route[:, w0:w0 + 1] * _from_token_major(bufs[parity][0], n)
             + route[:, w0 + 1:w0 + 2] * _from_token_major(bufs[parity][1], n))
        o_ref[...] = x_ref[...] + gate_ref[0] * _rms(y, gpost_ref[...])

    @pl.when(i == 0)
    def _():
        gather(pos_ref, 0)

    for parity in range(2):
        @pl.when(i % 2 == parity)
        def _(parity=parity):
            @pl.when(i < last)
            def _():
                gather(next_pos_ref, 1 - parity)

            finish(parity)


def _combine(ys, pos, route, x, g_post, gate):
    t, d = x.shape
    bsz = gate.shape[0]
    ts = COMBINE_ROWS
    steps_per_b = t // bsz // ts
    n_steps = t // ts
    return pl.pallas_call(
        _combine_kernel,
        grid=(n_steps,),
        in_specs=[
            pl.BlockSpec((1, 1, TOP_K * ts), lambda i: (i, 0, 0), memory_space=pltpu.SMEM),
            pl.BlockSpec((1, 1, TOP_K * ts), lambda i: (jnp.minimum(i + 1, n_steps - 1), 0, 0),
                         memory_space=pltpu.SMEM),
            pl.BlockSpec((ts, LANES), lambda i: (i, 0)),
            pl.BlockSpec((ts, d), lambda i: (i, 0)),
            pl.BlockSpec((1, d), lambda i: (0, 0)),
            pl.BlockSpec((1, 1, d), lambda i: (i // steps_per_b, 0, 0)),
            pl.BlockSpec(memory_space=pl.ANY),
        ],
        out_specs=pl.BlockSpec((ts, d), lambda i: (i, 0)),
        out_shape=jax.ShapeDtypeStruct((t, d), F32),
        scratch_shapes=[pltpu.VMEM((ts * d // LANES, LANES), F32)] * (2 * TOP_K)
        + [pltpu.SemaphoreType.DMA((2,))],
        compiler_params=pltpu.CompilerParams(
            dimension_semantics=("arbitrary",), vmem_limit_bytes=VMEM_LIMIT),
        name="moe_combine",
    )(pos, pos, route, x, g_post.reshape(1, d), gate, ys)


def _moe(x, g_pre, scale, shift, w_router, w1, w3, w2, g_post, gate):
    bsz, s, d = x.shape
    t = bsz * s
    n_experts = w_router.shape[1]
    tm = FFN_ROWS
    h, route, counts = _route(x, g_pre, scale, shift, w_router)
    route = route.reshape(t, LANES)

    lane_of = ROUTE_LANES.index
    expert = route[:, lane_of("expert0"):lane_of("expert0") + TOP_K].astype(jnp.int32)
    rank = route[:, lane_of("rank0"):lane_of("rank0") + TOP_K].astype(jnp.int32)
    counts = counts[0, :n_experts].astype(jnp.int32)
    tiles_per = (counts + tm - 1) // tm
    tile_end = jnp.cumsum(tiles_per)
    first_row = (tile_end - tiles_per) * tm
    n_active = tile_end[-1]
    n_tiles = TOP_K * t // tm + n_experts
    is_expert = expert[..., None] == jnp.arange(n_experts, dtype=jnp.int32)
    pos = jnp.sum(jnp.where(is_expert, first_row, 0), axis=-1) + rank

    def per_step(rows):
        return pos.reshape(t // rows, rows, TOP_K).transpose(0, 2, 1).reshape(t // rows, 1, -1)

    fill = jnp.concatenate([first_row + counts, tile_end * tm, n_active[None]]).astype(jnp.int32)
    idx = jnp.minimum(jnp.arange(n_tiles, dtype=jnp.int32), n_active - 1)
    tile_expert = jnp.sum(idx[:, None] >= tile_end[None, :], axis=1).astype(jnp.int32)

    token_major = (-1, d // LANES, LANES)
    xs = _scatter(h.reshape(token_major), per_step(ROUTE_ROWS), fill, n_tiles * tm)
    ys = _experts(xs.reshape(-1, LANES), tile_expert, n_active.reshape(1), w1, w3, w2)
    out = _combine(ys.reshape(token_major), per_step(COMBINE_ROWS), route, x.reshape(t, d),
                   g_post, gate.reshape(bsz, 1, d))
    return out.reshape(bsz, s, d)


def kernel(x, c, w_mod, b_mod, g_mix_pre, g_mix_post, g_ffn_pre, g_ffn_post, w_in, w_proj_sb,
           w_proj_moba, w_out, rel_bias, w1_dense, w3_dense, w2_dense, w_router, w1_moe, w3_moe,
           w2_moe):
    depth = w_mod.shape[0]
    d = x.shape[-1]
    qkv = 3 * W_BRANCH
    col = jnp.arange(w_in.shape[-1])
    is_q = (col < W_BRANCH) | ((col >= qkv) & (col < qkv + W_BRANCH))
    col_scale = jnp.where(is_q, HEAD_DIM ** -0.5, 1.0).astype(F32)

    mod = _modulation(c, w_mod, b_mod)
    bias_tiles = _moba_bias_tiles(rel_bias)
    for l in range(depth):
        sh_mix, sc_mix, gt_mix, sh_ffn, sc_ffn, gt_ffn = jnp.split(mod[l], N_MOD, axis=-1)
        proj = _inproj(x, g_mix_pre[l], sc_mix, sh_mix, (w_in[l] * col_scale).astype(BF16))
        o_a = _stick_breaking(proj, 0)
        o_b = _moba(proj, qkv // LANES, bias_tiles)
        x = _merge(o_a, o_b, proj, 2 * qkv // d, x, w_proj_sb[l].astype(BF16),
                   w_proj_moba[l].astype(BF16), w_out[l].astype(BF16), g_mix_post[l], gt_mix)
        j = l // 2
        if l % 2 == 0:
            x = _ffn(x, g_ffn_pre[l], sc_ffn, sh_ffn, w1_dense[j], w3_dense[j], w2_dense[j],
                     g_ffn_post[l], gt_ffn)
        else:
            x = _moe(x, g_ffn_pre[l], sc_ffn, sh_ffn, w_router[j], w1_moe[j], w3_moe[j],
                     w2_moe[j], g_ffn_post[l], gt_ffn)
    return x
```

```python
import functools
import math

import jax
import jax.numpy as jnp
from jax import lax
from jax.experimental import pallas as pl
from jax.experimental.pallas import tpu as pltpu

F32 = jnp.float32
BF16 = jnp.bfloat16

EPS = 1e-6
NEG_INF = -1e30
HEAD_DIM = 64
N_HEADS = 8
W_BRANCH = N_HEADS * HEAD_DIM
LANES = 128
HEADS_PER_TILE = LANES // HEAD_DIM
N_PAIRS = N_HEADS // HEADS_PER_TILE
MOBA_BLOCK = 256
MOBA_TOPK = 3
REL_BUCKETS = 32
REL_MAX_DIST = 128
N_MOD = 6
TOP_K = 2
ATT_BLOCK = 256
PROJ_ROWS = 1024
VMEM_LIMIT = 56 * 1024 * 1024


def _dot(a, b):
    return jnp.dot(a, b, preferred_element_type=F32)


def _dot_nt(a, b):
    return lax.dot_general(a, b, (((1,), (1,)), ((), ())), preferred_element_type=F32)


def _split_bf16(x):
    hi = x.astype(BF16)
    lo = (x - hi.astype(F32)).astype(BF16)
    return hi, lo


def _rms(x, g):
    return x * lax.rsqrt(jnp.mean(x * x, axis=-1, keepdims=True) + EPS) * g


def _mod_kernel(c_ref, w_ref, b_ref, o_ref):
    c = c_ref[...]
    cond = c * jax.nn.sigmoid(c)
    o_ref[0] = jnp.dot(cond, w_ref[0], preferred_element_type=F32,
                       precision=lax.Precision.HIGHEST) + b_ref[0]


def _modulation(c, w_mod, b_mod):
    depth, d, n = w_mod.shape
    bsz = c.shape[0]
    tn = 1536
    return pl.pallas_call(
        _mod_kernel,
        grid=(depth, n // tn),
        in_specs=[
            pl.BlockSpec((bsz, d), lambda l, j: (0, 0)),
            pl.BlockSpec((1, d, tn), lambda l, j: (l, 0, j)),
            pl.BlockSpec((1, 1, tn), lambda l, j: (l, 0, j)),
        ],
        out_specs=pl.BlockSpec((1, bsz, tn), lambda l, j: (l, 0, j)),
        out_shape=jax.ShapeDtypeStruct((depth, bsz, n), F32),
        compiler_params=pltpu.CompilerParams(
            dimension_semantics=("arbitrary", "arbitrary"), vmem_limit_bytes=VMEM_LIMIT),
        name="modulation",
    )(c, w_mod, b_mod.reshape(depth, 1, n))


def _inproj_kernel(x_ref, g_ref, sc_ref, sh_ref, w_ref, o_ref, *, tn):
    h = _rms(x_ref[0], g_ref[...]) * (1.0 + sc_ref[0]) + sh_ref[0]
    hb = h.astype(BF16)
    for n in range(w_ref.shape[1] // tn):
        o_ref[0, :, n * tn:(n + 1) * tn] = _dot(hb, w_ref[:, n * tn:(n + 1) * tn]).astype(BF16)


def _inproj(x, g, scale, shift, w):
    bsz, s, d = x.shape
    n = w.shape[1]
    tm = PROJ_ROWS
    return pl.pallas_call(
        functools.partial(_inproj_kernel, tn=1024),
        grid=(bsz, s // tm),
        in_specs=[
            pl.BlockSpec((1, tm, d), lambda b, t: (b, t, 0)),
            pl.BlockSpec((1, d), lambda b, t: (0, 0)),
            pl.BlockSpec((1, 1, d), lambda b, t: (b, 0, 0)),
            pl.BlockSpec((1, 1, d), lambda b, t: (b, 0, 0)),
            pl.BlockSpec((d, n), lambda b, t: (0, 0), pipeline_mode=pl.Buffered(1)),
        ],
        out_specs=pl.BlockSpec((1, tm, n), lambda b, t: (b, t, 0)),
        out_shape=jax.ShapeDtypeStruct((bsz, s, n), BF16),
        compiler_params=pltpu.CompilerParams(
            dimension_semantics=("arbitrary", "arbitrary"), vmem_limit_bytes=VMEM_LIMIT),
        name="inproj",
    )(x, g.reshape(1, d), scale.reshape(bsz, 1, d), shift.reshape(bsz, 1, d), w)


SOFTPLUS_LINEAR = 40.0


def _softplus(z):
    return jnp.maximum(z, jnp.log(1.0 + jnp.exp(jnp.minimum(z, SOFTPLUS_LINEAR))))


def _sb_kernel(q_ref, k_ref, v_ref, o_ref, qs_ref, acc_ref, carry_ref):
    tq = ATT_BLOCK
    n_blocks = k_ref.shape[1] // tq
    rows_per_blk = HEADS_PER_TILE * tq
    lane = lax.broadcasted_iota(jnp.int32, (1, LANES), 1)
    row = lax.broadcasted_iota(jnp.int32, (tq, tq), 0)
    col = lax.broadcasted_iota(jnp.int32, (tq, tq), 1)
    upper = jnp.where(row > col, 1.0, 0.0).astype(BF16)
    causal = jnp.concatenate([col < row] * HEADS_PER_TILE, axis=0)
    q = q_ref[0]
    for i in range(n_blocks):
        qi = q[i * tq:(i + 1) * tq]
        for h in range(HEADS_PER_TILE):
            row0 = (HEADS_PER_TILE * i + h) * tq
            qs_ref[row0:row0 + tq, :] = jnp.where(lane // HEAD_DIM == h, qi, jnp.zeros_like(qi))

    for j in reversed(range(n_blocks)):
        row0 = j * rows_per_blk
        far0 = row0 + rows_per_blk
        has_far = j + 1 < n_blocks
        z = _dot_nt(qs_ref[row0:, :], k_ref[0, j * tq:(j + 1) * tq, :])
        sp = _softplus(z)
        sp_diag = jnp.where(causal, sp[:rows_per_blk], 0.0)
        sp = jnp.concatenate([sp_diag, sp[rows_per_blk:]], axis=0) if has_far else sp_diag
        logit = z - sp - _dot(sp.astype(BF16), upper)
        att_diag = jnp.where(causal, jnp.exp(logit[:rows_per_blk]), 0.0)
        if has_far:
            att_far = jnp.exp(logit[rows_per_blk:] - carry_ref[far0:, :])
            att = jnp.concatenate([att_diag, att_far], axis=0)
        else:
            att = att_diag
        part = _dot(att.astype(BF16), v_ref[0, j * tq:(j + 1) * tq, :])
        rs = jnp.sum(sp, axis=1, keepdims=True)
        acc_ref[row0:far0, :] = part[:rows_per_blk]
        carry_ref[row0:far0, :] = rs[:rows_per_blk]
        if has_far:
            acc_ref[far0:, :] += part[rows_per_blk:]
            carry_ref[far0:, :] += rs[rows_per_blk:]

    for i in range(n_blocks):
        blk = acc_ref[i * rows_per_blk:(i + 1) * rows_per_blk, :]
        o_ref[0, i * tq:(i + 1) * tq, :] = jnp.where(lane < HEAD_DIM, blk[:tq], blk[tq:]).astype(BF16)


def _stick_breaking(proj, col0):
    bsz, s, _ = proj.shape
    assert s % ATT_BLOCK == 0
    rows = HEADS_PER_TILE * s
    return pl.pallas_call(
        _sb_kernel,
        grid=(bsz, N_PAIRS),
        in_specs=[
            pl.BlockSpec((1, s, LANES), lambda b, p: (b, 0, col0 + p)),
            pl.BlockSpec((1, s, LANES), lambda b, p: (b, 0, col0 + N_PAIRS + p)),
            pl.BlockSpec((1, s, LANES), lambda b, p: (b, 0, col0 + 2 * N_PAIRS + p)),
        ],
        out_specs=pl.BlockSpec((1, s, LANES), lambda b, p: (b, 0, p)),
        out_shape=jax.ShapeDtypeStruct((bsz, s, W_BRANCH), BF16),
        scratch_shapes=[pltpu.VMEM((rows, LANES), BF16),
                        pltpu.VMEM((rows, LANES), F32),
                        pltpu.VMEM((rows, 1), F32)],
        compiler_params=pltpu.CompilerParams(
            dimension_semantics=("arbitrary", "arbitrary"), vmem_limit_bytes=VMEM_LIMIT),
        name="stick_breaking",
    )(proj, proj, proj)


def _rel_bucket(dist):
    max_exact = REL_BUCKETS // 2
    n = jnp.maximum(dist, 0)
    nf = jnp.maximum(n, 1).astype(F32)
    large = max_exact + (jnp.log(nf / max_exact) / math.log(REL_MAX_DIST / max_exact)
                         * (REL_BUCKETS - max_exact)).astype(jnp.int32)
    large = jnp.minimum(large, REL_BUCKETS - 1)
    return jnp.where(n < max_exact, n, large)


def _moba_bias_tiles(rel_bias):
    assert MOBA_BLOCK >= REL_MAX_DIST
    t = jnp.arange(MOBA_BLOCK)[:, None]
    s = jnp.arange(MOBA_BLOCK)[None, :]
    d_own = t - s
    rel = rel_bias.astype(F32) - rel_bias[REL_BUCKETS - 1].astype(F32)

    def lookup(dist):
        onehot = jax.nn.one_hot(_rel_bucket(dist), REL_BUCKETS, dtype=F32)
        return jnp.einsum("tsk,kh->hts", onehot, rel, precision=lax.Precision.HIGHEST)

    own = jnp.where((d_own >= 0)[None], lookup(d_own), NEG_INF)
    return jnp.stack([own, lookup(d_own + MOBA_BLOCK)], axis=1)


def _moba_kernel(q_ref, k_ref, v_ref, bias_ref, o_ref,
                 ka_ref, va_ref, qa_ref, m_ref, mrep_ref, accl_ref, *, n_blocks):
    tq = MOBA_BLOCK
    s_len = k_ref.shape[1]
    nb_pad = 16
    rows_per_blk = HEADS_PER_TILE * tq
    lane = lax.broadcasted_iota(jnp.int32, (1, LANES), 1)
    head_lanes = [lane // HEAD_DIM == h for h in range(HEADS_PER_TILE)]
    q = q_ref[0]
    k = k_ref[0]

    key_blk = lax.broadcasted_iota(jnp.int32, (s_len, LANES), 0) // tq
    ka_ref[:, :LANES] = k
    ka_ref[:, LANES:] = jnp.where(lane == key_blk, 1.0, 0.0).astype(BF16)
    va_ref[:, :LANES] = v_ref[0]
    va_ref[:, LANES:] = jnp.ones((s_len, LANES), BF16)

    r = lax.broadcasted_iota(jnp.int32, (nb_pad, s_len), 0)
    c = lax.broadcasted_iota(jnp.int32, (nb_pad, s_len), 1)
    pool = jnp.where(c // tq == r, 1.0 / tq, 0.0).astype(BF16)
    km_hi, km_lo = _split_bf16(_dot(pool, k))
    means = [jnp.where(hl, km, jnp.zeros_like(km)) for hl in head_lanes for km in (km_hi, km_lo)]
    gates = _dot_nt(jnp.concatenate(means, axis=0), q)
    own = c // tq
    masks = []
    for h in range(HEADS_PER_TILE):
        gate = (gates[2 * h * nb_pad:(2 * h + 1) * nb_pad]
                + gates[(2 * h + 1) * nb_pad:(2 * h + 2) * nb_pad])
        cnt = jnp.zeros((nb_pad, s_len), F32)
        for n in range(n_blocks):
            gn = gate[n:n + 1, :]
            beats = jnp.where(gn > gate, 1.0, jnp.where((gn == gate) & (n < r), 1.0, 0.0))
            cnt = cnt + jnp.where(n < own, beats, 0.0)
        keep = (r == own) | ((r < own) & (cnt < MOBA_TOPK))
        masks += [jnp.where(keep, 0.0, NEG_INF).astype(BF16),
                  jnp.zeros((LANES - nb_pad, s_len), BF16)]
    masks = jnp.concatenate(masks, axis=0)
    er = lax.broadcasted_iota(jnp.int32, (tq, tq), 0)
    ec = lax.broadcasted_iota(jnp.int32, (tq, tq), 1)
    eye = jnp.where(er == ec, 1.0, 0.0).astype(BF16)
    for i in range(n_blocks):
        qi = q[i * tq:(i + 1) * tq]
        mask_rows = _dot_nt(eye, masks[:, i * tq:(i + 1) * tq]).astype(BF16)
        for h in range(HEADS_PER_TILE):
            row0 = (HEADS_PER_TILE * i + h) * tq
            qa_ref[row0:row0 + tq, :LANES] = jnp.where(head_lanes[h], qi, jnp.zeros_like(qi))
            qa_ref[row0:row0 + tq, LANES:] = mask_rows[:, h * LANES:(h + 1) * LANES]

    bias = [jnp.concatenate([bias_ref[h, which] for h in range(HEADS_PER_TILE)], axis=0)
            for which in range(2)]
    def scores(j):
        row0 = j * rows_per_blk
        s = _dot_nt(qa_ref[row0:, :], ka_ref[j * tq:(j + 1) * tq, :])
        parts = [s[:rows_per_blk] + bias[0]]
        if j + 1 < n_blocks:
            parts.append(s[rows_per_blk:2 * rows_per_blk] + bias[1])
        if j + 2 < n_blocks:
            parts.append(s[2 * rows_per_blk:])
        return jnp.concatenate(parts, axis=0)

    for j in range(n_blocks):
        row0 = j * rows_per_blk
        m = jnp.max(scores(j), axis=1, keepdims=True)
        if j == 0:
            m_ref[...] = m
        else:
            m_ref[row0:, :] = jnp.maximum(m_ref[row0:, :], m)
    mrep_ref[...] = jnp.broadcast_to(m_ref[...], mrep_ref.shape)
    for j in range(n_blocks):
        row0 = j * rows_per_blk
        m = mrep_ref[row0:, :]
        pr = jnp.exp(scores(j) - jnp.concatenate([m] * (tq // LANES), axis=1))
        ov = _dot(pr.astype(BF16), va_ref[j * tq:(j + 1) * tq, :])
        if j == 0:
            accl_ref[...] = ov
        else:
            accl_ref[row0:, :] += ov

    for i in range(n_blocks):
        blk = accl_ref[i * rows_per_blk:(i + 1) * rows_per_blk, :]
        out = blk[:, :LANES] / blk[:, LANES:]
        o_ref[0, i * tq:(i + 1) * tq, :] = jnp.where(lane < HEAD_DIM, out[:tq], out[tq:]).astype(BF16)


def _moba(proj, col0, bias_tiles):
    bsz, s, _ = proj.shape
    tq = MOBA_BLOCK
    assert s % tq == 0
    n_blocks = s // tq
    assert n_blocks <= 16
    rows = HEADS_PER_TILE * s
    return pl.pallas_call(
        functools.partial(_moba_kernel, n_blocks=n_blocks),
        grid=(bsz, N_PAIRS),
        in_specs=[
            pl.BlockSpec((1, s, LANES), lambda b, p: (b, 0, col0 + p)),
            pl.BlockSpec((1, s, LANES), lambda b, p: (b, 0, col0 + N_PAIRS + p)),
            pl.BlockSpec((1, s, LANES), lambda b, p: (b, 0, col0 + 2 * N_PAIRS + p)),
            pl.BlockSpec((HEADS_PER_TILE, 2, tq, tq), lambda b, p: (p, 0, 0, 0)),
        ],
        out_specs=pl.BlockSpec((1, s, LANES), lambda b, p: (b, 0, p)),
        out_shape=jax.ShapeDtypeStruct((bsz, s, W_BRANCH), BF16),
        scratch_shapes=[
            pltpu.VMEM((s, 2 * LANES), BF16),
            pltpu.VMEM((s, 2 * LANES), BF16),
            pltpu.VMEM((rows, 2 * LANES), BF16),
            pltpu.VMEM((rows, 1), F32),
            pltpu.VMEM((rows, LANES), F32),
            pltpu.VMEM((rows, 2 * LANES), F32),
        ],
        compiler_params=pltpu.CompilerParams(
            dimension_semantics=("arbitrary", "arbitrary"), vmem_limit_bytes=VMEM_LIMIT),
        name="moba",
    )(proj, proj, proj, bias_tiles)


def _merge_kernel(oa_ref, ob_ref, ga_ref, gb_ref, x_ref, wa_ref, wb_ref, wo_ref, g_ref, gate_ref,
                  o_ref):
    pa = _dot(oa_ref[0], wa_ref[...])
    pb = _dot(ob_ref[0], wb_ref[...])
    merged = (jax.nn.sigmoid(ga_ref[0].astype(F32)) * pa
              + jax.nn.sigmoid(gb_ref[0].astype(F32)) * pb)
    y = _dot(merged.astype(BF16), wo_ref[...])
    o_ref[0] = x_ref[0] + gate_ref[0] * _rms(y, g_ref[...])


def _merge(o_a, o_b, proj, gate_col, x, w_a, w_b, w_o, g_post, gate):
    bsz, s, d = x.shape
    tm = PROJ_ROWS
    const = lambda b, t: (0, 0)
    return pl.pallas_call(
        _merge_kernel,
        grid=(bsz, s // tm),
        in_specs=[
            pl.BlockSpec((1, tm, W_BRANCH), lambda b, t: (b, t, 0)),
            pl.BlockSpec((1, tm, W_BRANCH), lambda b, t: (b, t, 0)),
            pl.BlockSpec((1, tm, d), lambda b, t: (b, t, gate_col)),
            pl.BlockSpec((1, tm, d), lambda b, t: (b, t, gate_col + 1)),
            pl.BlockSpec((1, tm, d), lambda b, t: (b, t, 0)),
            pl.BlockSpec((W_BRANCH, d), const),
            pl.BlockSpec((W_BRANCH, d), const),
            pl.BlockSpec((d, d), const),
            pl.BlockSpec((1, d), const),
            pl.BlockSpec((1, 1, d), lambda b, t: (b, 0, 0)),
        ],
        out_specs=pl.BlockSpec((1, tm, d), lambda b, t: (b, t, 0)),
        out_shape=jax.ShapeDtypeStruct((bsz, s, d), F32),
        compiler_params=pltpu.CompilerParams(
            dimension_semantics=("arbitrary", "arbitrary"), vmem_limit_bytes=VMEM_LIMIT),
        name="merge",
    )(o_a, o_b, proj, proj, x, w_a, w_b, w_o, g_post.reshape(1, d), gate.reshape(bsz, 1, d))


FFN_ROWS = 1024
FFN_COLS = 512


def _ffn_kernel(x_ref, gpre_ref, sc_ref, sh_ref, w1_ref, w3_ref, w2_ref, gpost_ref, gate_ref,
                o_ref, h_ref, acc_ref):
    k = pl.program_id(2)

    @pl.when(k == 0)
    def _():
        h = _rms(x_ref[0], gpre_ref[...]) * (1.0 + sc_ref[0]) + sh_ref[0]
        h_ref[...] = h.astype(BF16)
        acc_ref[...] = jnp.zeros_like(acc_ref)

    h = h_ref[...]
    u = jax.nn.silu(_dot(h, w1_ref[...].astype(BF16))) * _dot(h, w3_ref[...].astype(BF16))
    acc_ref[...] += _dot(u.astype(BF16), w2_ref[...].astype(BF16))

    @pl.when(k == pl.num_programs(2) - 1)
    def _():
        o_ref[0] = x_ref[0] + gate_ref[0] * _rms(acc_ref[...], gpost_ref[...])


def _ffn(x, g_pre, scale, shift, w1, w3, w2, g_post, gate):
    bsz, s, d = x.shape
    f = w1.shape[1]
    tm, tf = FFN_ROWS, FFN_COLS
    row = lambda b, t, k: (b, t, 0)
    per_b = lambda b, t, k: (b, 0, 0)
    const = lambda b, t, k: (0, 0)
    return pl.pallas_call(
        _ffn_kernel,
        grid=(bsz, s // tm, f // tf),
        in_specs=[
            pl.BlockSpec((1, tm, d), row),
            pl.BlockSpec((1, d), const),
            pl.BlockSpec((1, 1, d), per_b),
            pl.BlockSpec((1, 1, d), per_b),
            pl.BlockSpec((d, tf), lambda b, t, k: (0, k)),
            pl.BlockSpec((d, tf), lambda b, t, k: (0, k)),
            pl.BlockSpec((tf, d), lambda b, t, k: (k, 0)),
            pl.BlockSpec((1, d), const),
            pl.BlockSpec((1, 1, d), per_b),
        ],
        out_specs=pl.BlockSpec((1, tm, d), row),
        out_shape=jax.ShapeDtypeStruct((bsz, s, d), F32),
        scratch_shapes=[pltpu.VMEM((tm, d), BF16), pltpu.VMEM((tm, d), F32)],
        compiler_params=pltpu.CompilerParams(
            dimension_semantics=("arbitrary",) * 3, vmem_limit_bytes=VMEM_LIMIT),
        name="ffn_dense",
    )(x, g_pre.reshape(1, d), scale.reshape(bsz, 1, d), shift.reshape(bsz, 1, d), w1, w3, w2,
      g_post.reshape(1, d), gate.reshape(bsz, 1, d))


ROUTE_ROWS = 1024
COMBINE_ROWS = 512


def _to_token_major(ref, val):
    rows, n = val.shape[0], val.shape[1] // LANES
    for s in range(n):
        ref[pl.ds(s, rows, stride=n), :] = val[:, s * LANES:(s + 1) * LANES]


def _from_token_major(ref, n):
    rows = ref.shape[0] // n
    return jnp.concatenate([ref[pl.ds(s, rows, stride=n), :] for s in range(n)], axis=1)
ROUTE_LANES = ("expert0", "expert1", "weight0", "weight1", "rank0", "rank1")


def _route_kernel(x_ref, gpre_ref, sc_ref, sh_ref, wr_ref, h_ref, route_ref, cnt_ref, *, n_experts):
    first_step = (pl.program_id(0) == 0) & (pl.program_id(1) == 0)

    @pl.when(first_step)
    def _():
        cnt_ref[...] = jnp.zeros_like(cnt_ref)

    h = _rms(x_ref[0], gpre_ref[...]) * (1.0 + sc_ref[0]) + sh_ref[0]
    _to_token_major(h_ref, h)
    h_hi, h_lo = _split_bf16(h)
    w_hi, w_lo = _split_bf16(wr_ref[...])
    logits = _dot(h_hi, w_hi) + (_dot(h_hi, w_lo) + _dot(h_lo, w_hi))
    ts = logits.shape[0]
    lane = lax.broadcasted_iota(jnp.int32, logits.shape, 1)
    lg = jnp.where(lane < n_experts, logits, -jnp.inf)
    m1 = jnp.max(lg, axis=1, keepdims=True)
    i1 = jnp.min(jnp.where(lg == m1, lane, LANES), axis=1, keepdims=True)
    first = lane == i1
    lg2 = jnp.where(first, -jnp.inf, lg)
    m2 = jnp.max(lg2, axis=1, keepdims=True)
    i2 = jnp.min(jnp.where(lg2 == m2, lane, LANES), axis=1, keepdims=True)
    second = lane == i2
    e2 = jnp.exp(m2 - m1)
    w_first = 1.0 / (1.0 + e2)
    w_second = e2 * w_first

    onehot = jnp.where(first | second, 1.0, 0.0)
    r = lax.broadcasted_iota(jnp.int32, (ts, ts), 0)
    c = lax.broadcasted_iota(jnp.int32, (ts, ts), 1)
    lower = jnp.where(c < r, 1.0, 0.0).astype(BF16)
    prefix = _dot(lower, onehot.astype(BF16)) + cnt_ref[...]
    rank_first = jnp.sum(jnp.where(first, prefix, 0.0), axis=1, keepdims=True)
    rank_second = jnp.sum(jnp.where(second, prefix, 0.0), axis=1, keepdims=True)
    cnt_ref[...] += jnp.sum(onehot, axis=0, keepdims=True)

    cols = (i1.astype(F32), i2.astype(F32), w_first, w_second, rank_first, rank_second)
    route = jnp.zeros(logits.shape, F32)
    for n, val in enumerate(cols):
        route = jnp.where(lane == n, val, route)
    route_ref[0] = route


def _route(x, g_pre, scale, shift, w_router):
    bsz, s, d = x.shape
    n_experts = w_router.shape[1]
    ts = ROUTE_ROWS
    row = lambda b, t: (b, t, 0)
    per_b = lambda b, t: (b, 0, 0)
    const = lambda b, t: (0, 0)
    return pl.pallas_call(
        functools.partial(_route_kernel, n_experts=n_experts),
        grid=(bsz, s // ts),
        in_specs=[
            pl.BlockSpec((1, ts, d), row),
            pl.BlockSpec((1, d), const),
            pl.BlockSpec((1, 1, d), per_b),
            pl.BlockSpec((1, 1, d), per_b),
            pl.BlockSpec((d, LANES), const),
        ],
        out_specs=[
            pl.BlockSpec((ts * d // LANES, LANES), lambda b, t: (b * (s // ts) + t, 0)),
            pl.BlockSpec((1, ts, LANES), row),
            pl.BlockSpec((1, LANES), const),
        ],
        out_shape=[
            jax.ShapeDtypeStruct((bsz * s * d // LANES, LANES), F32),
            jax.ShapeDtypeStruct((bsz, s, LANES), F32),
            jax.ShapeDtypeStruct((1, LANES), F32),
        ],
        compiler_params=pltpu.CompilerParams(
            dimension_semantics=("arbitrary", "arbitrary"), vmem_limit_bytes=VMEM_LIMIT),
        name="moe_route",
    )(x, g_pre.reshape(1, d), scale.reshape(bsz, 1, d), shift.reshape(bsz, 1, d),
      jnp.pad(w_router, ((0, 0), (0, LANES - n_experts))))


def _for_each_chunk(start, length, chunk, fn):
    whole = length // chunk

    def loop(c, _):
        fn(start + c * chunk, chunk)
        return 0

    lax.fori_loop(0, whole, loop, 0)
    cur = start + whole * chunk
    size = chunk // 2
    while size >= 1:
        bit = length & size

        @pl.when(bit != 0)
        def _(cur=cur, size=size):
            fn(cur, size)

        cur = cur + bit
        size //= 2


def _scatter_kernel(pos_ref, fill_ref, h_ref, xs_ref, zero_ref, sem, *, n_experts, tile_rows):
    ts = h_ref.shape[0]

    def issue(r, _):
        for slot in range(TOP_K):
            dst = pos_ref[0, 0, slot * ts + r]
            pltpu.make_async_copy(h_ref.at[r], xs_ref.at[dst], sem).start(priority=slot)
        return 0

    lax.fori_loop(0, ts, issue, 0, unroll=4)
    for slot in range(TOP_K):
        pltpu.make_async_copy(h_ref, xs_ref.at[pl.ds(0, ts)], sem).wait()

    @pl.when(pl.program_id(0) == pl.num_programs(0) - 1)
    def _():
        zero_ref[...] = jnp.zeros_like(zero_ref)
        occupied = fill_ref[2 * n_experts] * tile_rows
        spans = [(fill_ref[e], fill_ref[n_experts + e] - fill_ref[e]) for e in range(n_experts)]
        spans.append((occupied, xs_ref.shape[0] - occupied))

        def zero_copy(first, rows):
            return pltpu.make_async_copy(zero_ref.at[pl.ds(0, rows)],
                                         xs_ref.at[pl.ds(first, rows)], sem)

        for first, length in spans:
            _for_each_chunk(first, length, zero_ref.shape[0],
                            lambda row, rows: zero_copy(row, rows).start())
        for first, length in spans:
            _for_each_chunk(first, length, zero_ref.shape[0],
                            lambda row, rows: zero_copy(row, rows).wait())


def _scatter(h, pos, fill, n_rows):
    t = h.shape[0]
    ts = ROUTE_ROWS
    zero_rows = 256
    return pl.pallas_call(
        functools.partial(_scatter_kernel, n_experts=(fill.shape[0] - 1) // 2,
                          tile_rows=FFN_ROWS),
        grid=(t // ts,),
        in_specs=[
            pl.BlockSpec((1, 1, TOP_K * ts), lambda i: (i, 0, 0), memory_space=pltpu.SMEM),
            pl.BlockSpec(memory_space=pltpu.SMEM),
            pl.BlockSpec((ts,) + h.shape[1:], lambda i: (i, 0, 0)),
        ],
        out_specs=pl.BlockSpec(memory_space=pl.ANY),
        out_shape=jax.ShapeDtypeStruct((n_rows,) + h.shape[1:], F32),
        scratch_shapes=[pltpu.VMEM((zero_rows,) + h.shape[1:], F32), pltpu.SemaphoreType.DMA(())],
        compiler_params=pltpu.CompilerParams(
            dimension_semantics=("arbitrary",), vmem_limit_bytes=VMEM_LIMIT),
        name="moe_scatter",
    )(pos, fill, h)


def _experts_kernel(exp_ref, nact_ref, xs_ref, w1_ref, w3_ref, w2_ref, ys_ref, h_ref, acc_ref):
    i = pl.program_id(0)
    k = pl.program_id(1)

    @pl.when(i < nact_ref[0])
    def _():
        @pl.when(k == 0)
        def _():
            h_ref[...] = _from_token_major(xs_ref, h_ref.shape[1] // LANES).astype(BF16)

        h = h_ref[...]
        u = jax.nn.silu(_dot(h, w1_ref[0].astype(BF16))) * _dot(h, w3_ref[0].astype(BF16))
        part = _dot(u.astype(BF16), w2_ref[0].astype(BF16))

        @pl.when(k == 0)
        def _():
            acc_ref[...] = part

        @pl.when(k > 0)
        def _():
            acc_ref[...] += part

        @pl.when(k == pl.num_programs(1) - 1)
        def _():
            _to_token_major(ys_ref, acc_ref[...])

    @pl.when((i >= nact_ref[0]) & (k == 0))
    def _():
        ys_ref[...] = jnp.zeros_like(ys_ref)


def _experts(xs, tile_expert, n_active, w1, w3, w2):
    d, f = w1.shape[1:]
    n_rows = xs.shape[0] * LANES // d
    tm, tf = FFN_ROWS, FFN_COLS
    n_k = f // tf
    n_tiles = n_rows // tm

    def k_of(i, k, nact):
        return jnp.where(i < nact[0], k, n_k - 1)

    return pl.pallas_call(
        _experts_kernel,
        grid_spec=pltpu.PrefetchScalarGridSpec(
            num_scalar_prefetch=2,
            grid=(n_tiles, n_k),
            in_specs=[
                pl.BlockSpec((tm * d // LANES, LANES), lambda i, k, exp, nact: (i, 0)),
                pl.BlockSpec((1, d, tf), lambda i, k, exp, nact: (exp[i], 0, k_of(i, k, nact))),
                pl.BlockSpec((1, d, tf), lambda i, k, exp, nact: (exp[i], 0, k_of(i, k, nact))),
                pl.BlockSpec((1, tf, d), lambda i, k, exp, nact: (exp[i], k_of(i, k, nact), 0)),
            ],
            out_specs=pl.BlockSpec((tm * d // LANES, LANES), lambda i, k, exp, nact: (i, 0)),
            scratch_shapes=[pltpu.VMEM((tm, d), BF16), pltpu.VMEM((tm, d), F32)],
        ),
        out_shape=jax.ShapeDtypeStruct(xs.shape, F32),
        compiler_params=pltpu.CompilerParams(
            dimension_semantics=("arbitrary", "arbitrary"), vmem_limit_bytes=VMEM_LIMIT),
        name="moe_experts",
    )(tile_expert, n_active, xs, w1, w3, w2)


def _combine_kernel(pos_ref, next_pos_ref, route_ref, x_ref, gpost_ref, gate_ref, ys_ref, o_ref,
                    *scratch):
    ts = x_ref.shape[0]
    n = ys_ref.shape[1]
    bufs = (scratch[:TOP_K], scratch[TOP_K:2 * TOP_K])
    sem = scratch[2 * TOP_K]
    i = pl.program_id(0)
    last = pl.num_programs(0) - 1

    def gather(rows_ref, parity):
        def issue(r, _):
            dst = pl.multiple_of(r * n, n)
            for slot in range(TOP_K):
                src = rows_ref[0, 0, slot * ts + r]
                pltpu.make_async_copy(ys_ref.at[src], bufs[parity][slot].at[pl.ds(dst, n)],
                                      sem.at[parity]).start(priority=slot)
            return 0

        lax.fori_loop(0, ts, issue, 0, unroll=4)

    def finish(parity):
        for slot in range(TOP_K):
            buf = bufs[parity][slot]
            pltpu.make_async_copy(buf, buf, sem.at[parity]).wait()
        w0 = ROUTE_LANES.index("weight0")
        route = route_ref[...]
        y = (route[:, w0:w0 + 1] * _from_token_major(bufs[parity][0], n)
             + route[:, w0 + 1:w0 + 2] * _from_token_major(bufs[parity][1], n))
        o_ref[...] = x_ref[...] + gate_ref[0] * _rms(y, gpost_ref[...])

    @pl.when(i == 0)
    def _():
        gather(pos_ref, 0)

    for parity in range(2):
        @pl.when(i % 2 == parity)
        def _(parity=parity):
            @pl.when(i < last)
            def _():
                gather(next_pos_ref, 1 - parity)

            finish(parity)


def _combine(ys, pos, route, x, g_post, gate):
    t, d = x.shape
    bsz = gate.shape[0]
    ts = COMBINE_ROWS
    steps_per_b = t // bsz // ts
    n_steps = t // ts
    return pl.pallas_call(
        _combine_kernel,
        grid=(n_steps,),
        in_specs=[
            pl.BlockSpec((1, 1, TOP_K * ts), lambda i: (i, 0, 0), memory_space=pltpu.SMEM),
            pl.BlockSpec((1, 1, TOP_K * ts), lambda i: (jnp.minimum(i + 1, n_steps - 1), 0, 0),
                         memory_space=pltpu.SMEM),
            pl.BlockSpec((ts, LANES), lambda i: (i, 0)),
            pl.BlockSpec((ts, d), lambda i: (i, 0)),
            pl.BlockSpec((1, d), lambda i: (0, 0)),
            pl.BlockSpec((1, 1, d), lambda i: (i // steps_per_b, 0, 0)),
            pl.BlockSpec(memory_space=pl.ANY),
        ],
        out_specs=pl.BlockSpec((ts, d), lambda i: (i, 0)),
        out_shape=jax.ShapeDtypeStruct((t, d), F32),
        scratch_shapes=[pltpu.VMEM((ts * d // LANES, LANES), F32)] * (2 * TOP_K)
        + [pltpu.SemaphoreType.DMA((2,))],
        compiler_params=pltpu.CompilerParams(
            dimension_semantics=("arbitrary",), vmem_limit_bytes=VMEM_LIMIT),
        name="moe_combine",
    )(pos, pos, route, x, g_post.reshape(1, d), gate, ys)


def _moe(x, g_pre, scale, shift, w_router, w1, w3, w2, g_post, gate):
    bsz, s, d = x.shape
    t = bsz * s
    n_experts = w_router.shape[1]
    tm = FFN_ROWS
    h, route, counts = _route(x, g_pre, scale, shift, w_router)
    route = route.reshape(t, LANES)

    lane_of = ROUTE_LANES.index
    expert = route[:, lane_of("expert0"):lane_of("expert0") + TOP_K].astype(jnp.int32)
    rank = route[:, lane_of("rank0"):lane_of("rank0") + TOP_K].astype(jnp.int32)
    counts = counts[0, :n_experts].astype(jnp.int32)
    tiles_per = (counts + tm - 1) // tm
    tile_end = jnp.cumsum(tiles_per)
    first_row = (tile_end - tiles_per) * tm
    n_active = tile_end[-1]
    n_tiles = TOP_K * t // tm + n_experts
    is_expert = expert[..., None] == jnp.arange(n_experts, dtype=jnp.int32)
    pos = jnp.sum(jnp.where(is_expert, first_row, 0), axis=-1) + rank

    def per_step(rows):
        return pos.reshape(t // rows, rows, TOP_K).transpose(0, 2, 1).reshape(t // rows, 1, -1)

    fill = jnp.concatenate([first_row + counts, tile_end * tm, n_active[None]]).astype(jnp.int32)
    idx = jnp.minimum(jnp.arange(n_tiles, dtype=jnp.int32), n_active - 1)
    tile_expert = jnp.sum(idx[:, None] >= tile_end[None, :], axis=1).astype(jnp.int32)

    token_major = (-1, d // LANES, LANES)
    xs = _scatter(h.reshape(token_major), per_step(ROUTE_ROWS), fill, n_tiles * tm)
    ys = _experts(xs.reshape(-1, LANES), tile_expert, n_active.reshape(1), w1, w3, w2)
    out = _combine(ys.reshape(token_major), per_step(COMBINE_ROWS), route, x.reshape(t, d),
                   g_post, gate.reshape(bsz, 1, d))
    return out.reshape(bsz, s, d)


def kernel(x, c, w_mod, b_mod, g_mix_pre, g_mix_post, g_ffn_pre, g_ffn_post, w_in, w_proj_sb,
           w_proj_moba, w_out, rel_bias, w1_dense, w3_dense, w2_dense, w_router, w1_moe, w3_moe,
           w2_moe):
    depth = w_mod.shape[0]
    d = x.shape[-1]
    qkv = 3 * W_BRANCH
    col = jnp.arange(w_in.shape[-1])
    is_q = (col < W_BRANCH) | ((col >= qkv) & (col < qkv + W_BRANCH))
    col_scale = jnp.where(is_q, HEAD_DIM ** -0.5, 1.0).astype(F32)

    mod = _modulation(c, w_mod, b_mod)
    bias_tiles = _moba_bias_tiles(rel_bias)
    for l in range(depth):
        sh_mix, sc_mix, gt_mix, sh_ffn, sc_ffn, gt_ffn = jnp.split(mod[l], N_MOD, axis=-1)
        proj = _inproj(x, g_mix_pre[l], sc_mix, sh_mix, (w_in[l] * col_scale).astype(BF16))
        o_a = _stick_breaking(proj, 0)
        o_b = _moba(proj, qkv // LANES, bias_tiles)
        x = _merge(o_a, o_b, proj, 2 * qkv // d, x, w_proj_sb[l].astype(BF16),
                   w_proj_moba[l].astype(BF16), w_out[l].astype(BF16), g_mix_post[l], gt_mix)
        j = l // 2
        if l % 2 == 0:
            x = _ffn(x, g_ffn_pre[l], sc_ffn, sh_ffn, w1_dense[j], w3_dense[j], w2_dense[j],
                     g_ffn_post[l], gt_ffn)
        else:
            x = _moe(x, g_ffn_pre[l], sc_ffn, sh_ffn, w_router[j], w1_moe[j], w3_moe[j],
                     w2_moe[j], g_ffn_post[l], gt_ffn)
    return x
```

```python
import functools
import math

import jax
import jax.numpy as jnp
from jax import lax
from jax.experimental import pallas as pl
from jax.experimental.pallas import tpu as pltpu

F32 = jnp.float32
BF16 = jnp.bfloat16

EPS = 1e-6
NEG_INF = -1e30
HEAD_DIM = 64
N_HEADS = 8
W_BRANCH = N_HEADS * HEAD_DIM
LANES = 128
HEADS_PER_TILE = LANES // HEAD_DIM
N_PAIRS = N_HEADS // HEADS_PER_TILE
MOBA_BLOCK = 256
MOBA_TOPK = 3
REL_BUCKETS = 32
REL_MAX_DIST = 128
N_MOD = 6
TOP_K = 2
ATT_BLOCK = 256
PROJ_ROWS = 1024
VMEM_LIMIT = 56 * 1024 * 1024


def _dot(a, b):
    return jnp.dot(a, b, preferred_element_type=F32)


def _dot_nt(a, b):
    return lax.dot_general(a, b, (((1,), (1,)), ((), ())), preferred_element_type=F32)


def _split_bf16(x):
    hi = x.astype(BF16)
    lo = (x - hi.astype(F32)).astype(BF16)
    return hi, lo


def _rms(x, g):
    return x * lax.rsqrt(jnp.mean(x * x, axis=-1, keepdims=True) + EPS) * g


def _mod_kernel(c_ref, w_ref, b_ref, o_ref):
    c = c_ref[...]
    cond = c * jax.nn.sigmoid(c)
    o_ref[0] = jnp.dot(cond, w_ref[0], preferred_element_type=F32,
                       precision=lax.Precision.HIGHEST) + b_ref[0]


def _modulation(c, w_mod, b_mod):
    depth, d, n = w_mod.shape
    bsz = c.shape[0]
    tn = 1536
    return pl.pallas_call(
        _mod_kernel,
        grid=(depth, n // tn),
        in_specs=[
            pl.BlockSpec((bsz, d), lambda l, j: (0, 0)),
            pl.BlockSpec((1, d, tn), lambda l, j: (l, 0, j)),
            pl.BlockSpec((1, 1, tn), lambda l, j: (l, 0, j)),
        ],
        out_specs=pl.BlockSpec((1, bsz, tn), lambda l, j: (l, 0, j)),
        out_shape=jax.ShapeDtypeStruct((depth, bsz, n), F32),
        compiler_params=pltpu.CompilerParams(
            dimension_semantics=("arbitrary", "arbitrary"), vmem_limit_bytes=VMEM_LIMIT),
        name="modulation",
    )(c, w_mod, b_mod.reshape(depth, 1, n))


def _inproj_kernel(x_ref, g_ref, sc_ref, sh_ref, w_ref, o_ref, *, tn):
    h = _rms(x_ref[0], g_ref[...]) * (1.0 + sc_ref[0]) + sh_ref[0]
    hb = h.astype(BF16)
    for n in range(w_ref.shape[1] // tn):
        o_ref[0, :, n * tn:(n + 1) * tn] = _dot(hb, w_ref[:, n * tn:(n + 1) * tn]).astype(BF16)


def _inproj(x, g, scale, shift, w):
    bsz, s, d = x.shape
    n = w.shape[1]
    tm = PROJ_ROWS
    return pl.pallas_call(
        functools.partial(_inproj_kernel, tn=1024),
        grid=(bsz, s // tm),
        in_specs=[
            pl.BlockSpec((1, tm, d), lambda b, t: (b, t, 0)),
            pl.BlockSpec((1, d), lambda b, t: (0, 0)),
            pl.BlockSpec((1, 1, d), lambda b, t: (b, 0, 0)),
            pl.BlockSpec((1, 1, d), lambda b, t: (b, 0, 0)),
            pl.BlockSpec((d, n), lambda b, t: (0, 0), pipeline_mode=pl.Buffered(1)),
        ],
        out_specs=pl.BlockSpec((1, tm, n), lambda b, t: (b, t, 0)),
        out_shape=jax.ShapeDtypeStruct((bsz, s, n), BF16),
        compiler_params=pltpu.CompilerParams(
            dimension_semantics=("arbitrary", "arbitrary"), vmem_limit_bytes=VMEM_LIMIT),
        name="inproj",
    )(x, g.reshape(1, d), scale.reshape(bsz, 1, d), shift.reshape(bsz, 1, d), w)


SOFTPLUS_LINEAR = 40.0


def _softplus(z):
    return jnp.maximum(z, jnp.log(1.0 + jnp.exp(jnp.minimum(z, SOFTPLUS_LINEAR))))


def _sb_kernel(q_ref, k_ref, v_ref, o_ref, qs_ref, acc_ref, carry_ref):
    tq = ATT_BLOCK
    n_blocks = k_ref.shape[1] // tq
    rows_per_blk = HEADS_PER_TILE * tq
    lane = lax.broadcasted_iota(jnp.int32, (1, LANES), 1)
    row = lax.broadcasted_iota(jnp.int32, (tq, tq), 0)
    col = lax.broadcasted_iota(jnp.int32, (tq, tq), 1)
    upper = jnp.where(row > col, 1.0, 0.0).astype(BF16)
    causal = jnp.concatenate([col < row] * HEADS_PER_TILE, axis=0)
    q = q_ref[0]
    for i in range(n_blocks):
        qi = q[i * tq:(i + 1) * tq]
        for h in range(HEADS_PER_TILE):
            row0 = (HEADS_PER_TILE * i + h) * tq
            qs_ref[row0:row0 + tq, :] = jnp.where(lane // HEAD_DIM == h, qi, jnp.zeros_like(qi))

    for j in reversed(range(n_blocks)):
        row0 = j * rows_per_blk
        far0 = row0 + rows_per_blk
        has_far = j + 1 < n_blocks
        z = _dot_nt(qs_ref[row0:, :], k_ref[0, j * tq:(j + 1) * tq, :])
        sp = _softplus(z)
        sp_diag = jnp.where(causal, sp[:rows_per_blk], 0.0)
        sp = jnp.concatenate([sp_diag, sp[rows_per_blk:]], axis=0) if has_far else sp_diag
        logit = z - sp - _dot(sp.astype(BF16), upper)
        att_diag = jnp.where(causal, jnp.exp(logit[:rows_per_blk]), 0.0)
        if has_far:
            att_far = jnp.exp(logit[rows_per_blk:] - carry_ref[far0:, :])
            att = jnp.concatenate([att_diag, att_far], axis=0)
        else:
            att = att_diag
        part = _dot(att.astype(BF16), v_ref[0, j * tq:(j + 1) * tq, :])
        rs = jnp.sum(sp, axis=1, keepdims=True)
        acc_ref[row0:far0, :] = part[:rows_per_blk]
        carry_ref[row0:far0, :] = rs[:rows_per_blk]
        if has_far:
            acc_ref[far0:, :] += part[rows_per_blk:]
            carry_ref[far0:, :] += rs[rows_per_blk:]

    for i in range(n_blocks):
        blk = acc_ref[i * rows_per_blk:(i + 1) * rows_per_blk, :]
        o_ref[0, i * tq:(i + 1) * tq, :] = jnp.where(lane < HEAD_DIM, blk[:tq], blk[tq:]).astype(BF16)


def _stick_breaking(proj, col0):
    bsz, s, _ = proj.shape
    assert s % ATT_BLOCK == 0
    rows = HEADS_PER_TILE * s
    return pl.pallas_call(
        _sb_kernel,
        grid=(bsz, N_PAIRS),
        in_specs=[
            pl.BlockSpec((1, s, LANES), lambda b, p: (b, 0, col0 + p)),
            pl.BlockSpec((1, s, LANES), lambda b, p: (b, 0, col0 + N_PAIRS + p)),
            pl.BlockSpec((1, s, LANES), lambda b, p: (b, 0, col0 + 2 * N_PAIRS + p)),
        ],
        out_specs=pl.BlockSpec((1, s, LANES), lambda b, p: (b, 0, p)),
        out_shape=jax.ShapeDtypeStruct((bsz, s, W_BRANCH), BF16),
        scratch_shapes=[pltpu.VMEM((rows, LANES), BF16),
                        pltpu.VMEM((rows, LANES), F32),
                        pltpu.VMEM((rows, 1), F32)],
        compiler_params=pltpu.CompilerParams(
            dimension_semantics=("arbitrary", "arbitrary"), vmem_limit_bytes=VMEM_LIMIT),
        name="stick_breaking",
    )(proj, proj, proj)


def _rel_bucket(dist):
    max_exact = REL_BUCKETS // 2
    n = jnp.maximum(dist, 0)
    nf = jnp.maximum(n, 1).astype(F32)
    large = max_exact + (jnp.log(nf / max_exact) / math.log(REL_MAX_DIST / max_exact)
                         * (REL_BUCKETS - max_exact)).astype(jnp.int32)
    large = jnp.minimum(large, REL_BUCKETS - 1)
    return jnp.where(n < max_exact, n, large)


def _moba_bias_tiles(rel_bias):
    assert MOBA_BLOCK >= REL_MAX_DIST
    t = jnp.arange(MOBA_BLOCK)[:, None]
    s = jnp.arange(MOBA_BLOCK)[None, :]
    d_own = t - s
    rel = rel_bias.astype(F32) - rel_bias[REL_BUCKETS - 1].astype(F32)

    def lookup(dist):
        onehot = jax.nn.one_hot(_rel_bucket(dist), REL_BUCKETS, dtype=F32)
        return jnp.einsum("tsk,kh->hts", onehot, rel, precision=lax.Precision.HIGHEST)

    own = jnp.where((d_own >= 0)[None], lookup(d_own), NEG_INF)
    return jnp.stack([own, lookup(d_own + MOBA_BLOCK)], axis=1)


def _moba_kernel(q_ref, k_ref, v_ref, bias_ref, o_ref,
                 ka_ref, va_ref, qa_ref, m_ref, mrep_ref, accl_ref, *, n_blocks):
    tq = MOBA_BLOCK
    s_len = k_ref.shape[1]
    nb_pad = 16
    rows_per_blk = HEADS_PER_TILE * tq
    lane = lax.broadcasted_iota(jnp.int32, (1, LANES), 1)
    head_lanes = [lane // HEAD_DIM == h for h in range(HEADS_PER_TILE)]
    q = q_ref[0]
    k = k_ref[0]

    key_blk = lax.broadcasted_iota(jnp.int32, (s_len, LANES), 0) // tq
    ka_ref[:, :LANES] = k
    ka_ref[:, LANES:] = jnp.where(lane == key_blk, 1.0, 0.0).astype(BF16)
    va_ref[:, :LANES] = v_ref[0]
    va_ref[:, LANES:] = jnp.ones((s_len, LANES), BF16)

    r = lax.broadcasted_iota(jnp.int32, (nb_pad, s_len), 0)
    c = lax.broadcasted_iota(jnp.int32, (nb_pad, s_len), 1)
    pool = jnp.where(c // tq == r, 1.0 / tq, 0.0).astype(BF16)
    km_hi, km_lo = _split_bf16(_dot(pool, k))
    means = [jnp.where(hl, km, jnp.zeros_like(km)) for hl in head_lanes for km in (km_hi, km_lo)]
    gates = _dot_nt(jnp.concatenate(means, axis=0), q)
    own = c // tq
    masks = []
    for h in range(HEADS_PER_TILE):
        gate = (gates[2 * h * nb_pad:(2 * h + 1) * nb_pad]
                + gates[(2 * h + 1) * nb_pad:(2 * h + 2) * nb_pad])
        cnt = jnp.zeros((nb_pad, s_len), F32)
        for n in range(n_blocks):
            gn = gate[n:n + 1, :]
            beats = jnp.where(gn > gate, 1.0, jnp.where((gn == gate) & (n < r), 1.0, 0.0))
            cnt = cnt + jnp.where(n < own, beats, 0.0)
        keep = (r == own) | ((r < own) & (cnt < MOBA_TOPK))
        masks += [jnp.where(keep, 0.0, NEG_INF).astype(BF16),
                  jnp.zeros((LANES - nb_pad, s_len), BF16)]
    masks = jnp.concatenate(masks, axis=0)
    er = lax.broadcasted_iota(jnp.int32, (tq, tq), 0)
    ec = lax.broadcasted_iota(jnp.int32, (tq, tq), 1)
    eye = jnp.where(er == ec, 1.0, 0.0).astype(BF16)
    for i in range(n_blocks):
        qi = q[i * tq:(i + 1) * tq]
        mask_rows = _dot_nt(eye, masks[:, i * tq:(i + 1) * tq]).astype(BF16)
        for h in range(HEADS_PER_TILE):
            row0 = (HEADS_PER_TILE * i + h) * tq
            qa_ref[row0:row0 + tq, :LANES] = jnp.where(head_lanes[h], qi, jnp.zeros_like(qi))
            qa_ref[row0:row0 + tq, LANES:] = mask_rows[:, h * LANES:(h + 1) * LANES]

    bias = [jnp.concatenate([bias_ref[h, which] for h in range(HEADS_PER_TILE)], axis=0)
            for which in range(2)]
    def scores(j):
        row0 = j * rows_per_blk
        s = _dot_nt(qa_ref[row0:, :], ka_ref[j * tq:(j + 1) * tq, :])
        parts = [s[:rows_per_blk] + bias[0]]
        if j + 1 < n_blocks:
            parts.append(s[rows_per_blk:2 * rows_per_blk] + bias[1])
        if j + 2 < n_blocks:
            parts.append(s[2 * rows_per_blk:])
        return jnp.concatenate(parts, axis=0)

    for j in range(n_blocks):
        row0 = j * rows_per_blk
        s = scores(j)
        m = jnp.maximum(s[:, :LANES], s[:, LANES:])
        if j == 0:
            mrep_ref[...] = m
        else:
            mrep_ref[row0:, :] = jnp.maximum(mrep_ref[row0:, :], m)
    m_ref[...] = jnp.max(mrep_ref[...], axis=1, keepdims=True)
    mrep_ref[...] = jnp.broadcast_to(m_ref[...], mrep_ref.shape)
    for j in range(n_blocks):
        row0 = j * rows_per_blk
        m = mrep_ref[row0:, :]
        pr = jnp.exp(scores(j) - jnp.concatenate([m] * (tq // LANES), axis=1))
        ov = _dot(pr.astype(BF16), va_ref[j * tq:(j + 1) * tq, :])
        if j == 0:
            accl_ref[...] = ov
        else:
            accl_ref[row0:, :] += ov

    for i in range(n_blocks):
        blk = accl_ref[i * rows_per_blk:(i + 1) * rows_per_blk, :]
        out = blk[:, :LANES] / blk[:, LANES:]
        o_ref[0, i * tq:(i + 1) * tq, :] = jnp.where(lane < HEAD_DIM, out[:tq], out[tq:]).astype(BF16)


def _moba(proj, col0, bias_tiles):
    bsz, s, _ = proj.shape
    tq = MOBA_BLOCK
    assert s % tq == 0
    n_blocks = s // tq
    assert n_blocks <= 16
    rows = HEADS_PER_TILE * s
    return pl.pallas_call(
        functools.partial(_moba_kernel, n_blocks=n_blocks),
        grid=(bsz, N_PAIRS),
        in_specs=[
            pl.BlockSpec((1, s, LANES), lambda b, p: (b, 0, col0 + p)),
            pl.BlockSpec((1, s, LANES), lambda b, p: (b, 0, col0 + N_PAIRS + p)),
            pl.BlockSpec((1, s, LANES), lambda b, p: (b, 0, col0 + 2 * N_PAIRS + p)),
            pl.BlockSpec((HEADS_PER_TILE, 2, tq, tq), lambda b, p: (p, 0, 0, 0)),
        ],
        out_specs=pl.BlockSpec((1, s, LANES), lambda b, p: (b, 0, p)),
        out_shape=jax.ShapeDtypeStruct((bsz, s, W_BRANCH), BF16),
        scratch_shapes=[
            pltpu.VMEM((s, 2 * LANES), BF16),
            pltpu.VMEM((s, 2 * LANES), BF16),
            pltpu.VMEM((rows, 2 * LANES), BF16),
            pltpu.VMEM((rows, 1), F32),
            pltpu.VMEM((rows, LANES), F32),
            pltpu.VMEM((rows, 2 * LANES), F32),
        ],
        compiler_params=pltpu.CompilerParams(
            dimension_semantics=("arbitrary", "arbitrary"), vmem_limit_bytes=VMEM_LIMIT),
        name="moba",
    )(proj, proj, proj, bias_tiles)


def _merge_kernel(oa_ref, ob_ref, ga_ref, gb_ref, x_ref, wa_ref, wb_ref, wo_ref, g_ref, gate_ref,
                  o_ref):
    pa = _dot(oa_ref[0], wa_ref[...])
    pb = _dot(ob_ref[0], wb_ref[...])
    merged = (jax.nn.sigmoid(ga_ref[0].astype(F32)) * pa
              + jax.nn.sigmoid(gb_ref[0].astype(F32)) * pb)
    y = _dot(merged.astype(BF16), wo_ref[...])
    o_ref[0] = x_ref[0] + gate_ref[0] * _rms(y, g_ref[...])


def _merge(o_a, o_b, proj, gate_col, x, w_a, w_b, w_o, g_post, gate):
    bsz, s, d = x.shape
    tm = PROJ_ROWS
    const = lambda b, t: (0, 0)
    return pl.pallas_call(
        _merge_kernel,
        grid=(bsz, s // tm),
        in_specs=[
            pl.BlockSpec((1, tm, W_BRANCH), lambda b, t: (b, t, 0)),
            pl.BlockSpec((1, tm, W_BRANCH), lambda b, t: (b, t, 0)),
            pl.BlockSpec((1, tm, d), lambda b, t: (b, t, gate_col)),
            pl.BlockSpec((1, tm, d), lambda b, t: (b, t, gate_col + 1)),
            pl.BlockSpec((1, tm, d), lambda b, t: (b, t, 0)),
            pl.BlockSpec((W_BRANCH, d), const),
            pl.BlockSpec((W_BRANCH, d), const),
            pl.BlockSpec((d, d), const),
            pl.BlockSpec((1, d), const),
            pl.BlockSpec((1, 1, d), lambda b, t: (b, 0, 0)),
        ],
        out_specs=pl.BlockSpec((1, tm, d), lambda b, t: (b, t, 0)),
        out_shape=jax.ShapeDtypeStruct((bsz, s, d), F32),
        compiler_params=pltpu.CompilerParams(
            dimension_semantics=("arbitrary", "arbitrary"), vmem_limit_bytes=VMEM_LIMIT),
        name="merge",
    )(o_a, o_b, proj, proj, x, w_a, w_b, w_o, g_post.reshape(1, d), gate.reshape(bsz, 1, d))


FFN_ROWS = 1024
FFN_COLS = 512


def _ffn_kernel(x_ref, gpre_ref, sc_ref, sh_ref, w1_ref, w3_ref, w2_ref, gpost_ref, gate_ref,
                o_ref, h_ref, acc_ref):
    k = pl.program_id(2)

    @pl.when(k == 0)
    def _():
        h = _rms(x_ref[0], gpre_ref[...]) * (1.0 + sc_ref[0]) + sh_ref[0]
        h_ref[...] = h.astype(BF16)
        acc_ref[...] = jnp.zeros_like(acc_ref)

    h = h_ref[...]
    u = jax.nn.silu(_dot(h, w1_ref[...].astype(BF16))) * _dot(h, w3_ref[...].astype(BF16))
    acc_ref[...] += _dot(u.astype(BF16), w2_ref[...].astype(BF16))

    @pl.when(k == pl.num_programs(2) - 1)
    def _():
        o_ref[0] = x_ref[0] + gate_ref[0] * _rms(acc_ref[...], gpost_ref[...])


def _ffn(x, g_pre, scale, shift, w1, w3, w2, g_post, gate):
    bsz, s, d = x.shape
    f = w1.shape[1]
    tm, tf = FFN_ROWS, FFN_COLS
    row = lambda b, t, k: (b, t, 0)
    per_b = lambda b, t, k: (b, 0, 0)
    const = lambda b, t, k: (0, 0)
    return pl.pallas_call(
        _ffn_kernel,
        grid=(bsz, s // tm, f // tf),
        in_specs=[
            pl.BlockSpec((1, tm, d), row),
            pl.BlockSpec((1, d), const),
            pl.BlockSpec((1, 1, d), per_b),
            pl.BlockSpec((1, 1, d), per_b),
            pl.BlockSpec((d, tf), lambda b, t, k: (0, k)),
            pl.BlockSpec((d, tf), lambda b, t, k: (0, k)),
            pl.BlockSpec((tf, d), lambda b, t, k: (k, 0)),
            pl.BlockSpec((1, d), const),
            pl.BlockSpec((1, 1, d), per_b),
        ],
        out_specs=pl.BlockSpec((1, tm, d), row),
        out_shape=jax.ShapeDtypeStruct((bsz, s, d), F32),
        scratch_shapes=[pltpu.VMEM((tm, d), BF16), pltpu.VMEM((tm, d), F32)],
        compiler_params=pltpu.CompilerParams(
            dimension_semantics=("arbitrary",) * 3, vmem_limit_bytes=VMEM_LIMIT),
        name="ffn_dense",
    )(x, g_pre.reshape(1, d), scale.reshape(bsz, 1, d), shift.reshape(bsz, 1, d), w1, w3, w2,
      g_post.reshape(1, d), gate.reshape(bsz, 1, d))


ROUTE_ROWS = 1024
COMBINE_ROWS = 512


def _to_token_major(ref, val):
    rows, n = val.shape[0], val.shape[1] // LANES
    for s in range(n):
        ref[pl.ds(s, rows, stride=n), :] = val[:, s * LANES:(s + 1) * LANES]


def _from_token_major(ref, n):
    rows = ref.shape[0] // n
    return jnp.concatenate([ref[pl.ds(s, rows, stride=n), :] for s in range(n)], axis=1)
ROUTE_LANES = ("expert0", "expert1", "weight0", "weight1", "rank0", "rank1")


def _route_kernel(x_ref, gpre_ref, sc_ref, sh_ref, wr_ref, h_ref, route_ref, cnt_ref, *, n_experts):
    first_step = (pl.program_id(0) == 0) & (pl.program_id(1) == 0)

    @pl.when(first_step)
    def _():
        cnt_ref[...] = jnp.zeros_like(cnt_ref)

    h = _rms(x_ref[0], gpre_ref[...]) * (1.0 + sc_ref[0]) + sh_ref[0]
    _to_token_major(h_ref, h)
    h_hi, h_lo = _split_bf16(h)
    w_hi, w_lo = _split_bf16(wr_ref[...])
    logits = _dot(h_hi, w_hi) + (_dot(h_hi, w_lo) + _dot(h_lo, w_hi))
    ts = logits.shape[0]
    lane = lax.broadcasted_iota(jnp.int32, logits.shape, 1)
    lg = jnp.where(lane < n_experts, logits, -jnp.inf)
    m1 = jnp.max(lg, axis=1, keepdims=True)
    i1 = jnp.min(jnp.where(lg == m1, lane, LANES), axis=1, keepdims=True)
    first = lane == i1
    lg2 = jnp.where(first, -jnp.inf, lg)
    m2 = jnp.max(lg2, axis=1, keepdims=True)
    i2 = jnp.min(jnp.where(lg2 == m2, lane, LANES), axis=1, keepdims=True)
    second = lane == i2
    e2 = jnp.exp(m2 - m1)
    w_first = 1.0 / (1.0 + e2)
    w_second = e2 * w_first

    onehot = jnp.where(first | second, 1.0, 0.0)
    r = lax.broadcasted_iota(jnp.int32, (ts, ts), 0)
    c = lax.broadcasted_iota(jnp.int32, (ts, ts), 1)
    lower = jnp.where(c < r, 1.0, 0.0).astype(BF16)
    prefix = _dot(lower, onehot.astype(BF16)) + cnt_ref[...]
    rank_first = jnp.sum(jnp.where(first, prefix, 0.0), axis=1, keepdims=True)
    rank_second = jnp.sum(jnp.where(second, prefix, 0.0), axis=1, keepdims=True)
    cnt_ref[...] += jnp.sum(onehot, axis=0, keepdims=True)

    cols = (i1.astype(F32), i2.astype(F32), w_first, w_second, rank_first, rank_second)
    route = jnp.zeros(logits.shape, F32)
    for n, val in enumerate(cols):
        route = jnp.where(lane == n, val, route)
    route_ref[0] = route


def _route(x, g_pre, scale, shift, w_router):
    bsz, s, d = x.shape
    n_experts = w_router.shape[1]
    ts = ROUTE_ROWS
    row = lambda b, t: (b, t, 0)
    per_b = lambda b, t: (b, 0, 0)
    const = lambda b, t: (0, 0)
    return pl.pallas_call(
        functools.partial(_route_kernel, n_experts=n_experts),
        grid=(bsz, s // ts),
        in_specs=[
            pl.BlockSpec((1, ts, d), row),
            pl.BlockSpec((1, d), const),
            pl.BlockSpec((1, 1, d), per_b),
            pl.BlockSpec((1, 1, d), per_b),
            pl.BlockSpec((d, LANES), const),
        ],
        out_specs=[
            pl.BlockSpec((ts * d // LANES, LANES), lambda b, t: (b * (s // ts) + t, 0)),
            pl.BlockSpec((1, ts, LANES), row),
            pl.BlockSpec((1, LANES), const),
        ],
        out_shape=[
            jax.ShapeDtypeStruct((bsz * s * d // LANES, LANES), F32),
            jax.ShapeDtypeStruct((bsz, s, LANES), F32),
            jax.ShapeDtypeStruct((1, LANES), F32),
        ],
        compiler_params=pltpu.CompilerParams(
            dimension_semantics=("arbitrary", "arbitrary"), vmem_limit_bytes=VMEM_LIMIT),
        name="moe_route",
    )(x, g_pre.reshape(1, d), scale.reshape(bsz, 1, d), shift.reshape(bsz, 1, d),
      jnp.pad(w_router, ((0, 0), (0, LANES - n_experts))))


def _for_each_chunk(start, length, chunk, fn):
    whole = length // chunk

    def loop(c, _):
        fn(start + c * chunk, chunk)
        return 0

    lax.fori_loop(0, whole, loop, 0)
    cur = start + whole * chunk
    size = chunk // 2
    while size >= 1:
        bit = length & size

        @pl.when(bit != 0)
        def _(cur=cur, size=size):
            fn(cur, size)

        cur = cur + bit
        size //= 2


def _scatter_kernel(pos_ref, fill_ref, h_ref, xs_ref, zero_ref, sem, *, n_experts, tile_rows):
    ts = h_ref.shape[0]

    def issue(r, _):
        for slot in range(TOP_K):
            dst = pos_ref[0, 0, slot * ts + r]
            pltpu.make_async_copy(h_ref.at[r], xs_ref.at[dst], sem).start(priority=slot)
        return 0

    lax.fori_loop(0, ts, issue, 0, unroll=4)
    for slot in range(TOP_K):
        pltpu.make_async_copy(h_ref, xs_ref.at[pl.ds(0, ts)], sem).wait()

    @pl.when(pl.program_id(0) == pl.num_programs(0) - 1)
    def _():
        zero_ref[...] = jnp.zeros_like(zero_ref)
        occupied = fill_ref[2 * n_experts] * tile_rows
        spans = [(fill_ref[e], fill_ref[n_experts + e] - fill_ref[e]) for e in range(n_experts)]
        spans.append((occupied, xs_ref.shape[0] - occupied))

        def zero_copy(first, rows):
            return pltpu.make_async_copy(zero_ref.at[pl.ds(0, rows)],
                                         xs_ref.at[pl.ds(first, rows)], sem)

        for first, length in spans:
            _for_each_chunk(first, length, zero_ref.shape[0],
                            lambda row, rows: zero_copy(row, rows).start())
        for first, length in spans:
            _for_each_chunk(first, length, zero_ref.shape[0],
                            lambda row, rows: zero_copy(row, rows).wait())


def _scatter(h, pos, fill, n_rows):
    t = h.shape[0]
    ts = ROUTE_ROWS
    zero_rows = 256
    return pl.pallas_call(
        functools.partial(_scatter_kernel, n_experts=(fill.shape[0] - 1) // 2,
                          tile_rows=FFN_ROWS),
        grid=(t // ts,),
        in_specs=[
            pl.BlockSpec((1, 1, TOP_K * ts), lambda i: (i, 0, 0), memory_space=pltpu.SMEM),
            pl.BlockSpec(memory_space=pltpu.SMEM),
            pl.BlockSpec((ts,) + h.shape[1:], lambda i: (i, 0, 0)),
        ],
        out_specs=pl.BlockSpec(memory_space=pl.ANY),
        out_shape=jax.ShapeDtypeStruct((n_rows,) + h.shape[1:], F32),
        scratch_shapes=[pltpu.VMEM((zero_rows,) + h.shape[1:], F32), pltpu.SemaphoreType.DMA(())],
        compiler_params=pltpu.CompilerParams(
            dimension_semantics=("arbitrary",), vmem_limit_bytes=VMEM_LIMIT),
        name="moe_scatter",
    )(pos, fill, h)


def _experts_kernel(exp_ref, nact_ref, xs_ref, w1_ref, w3_ref, w2_ref, ys_ref, h_ref, acc_ref):
    i = pl.program_id(0)
    k = pl.program_id(1)

    @pl.when(i < nact_ref[0])
    def _():
        @pl.when(k == 0)
        def _():
            h_ref[...] = _from_token_major(xs_ref, h_ref.shape[1] // LANES).astype(BF16)

        h = h_ref[...]
        u = jax.nn.silu(_dot(h, w1_ref[0].astype(BF16))) * _dot(h, w3_ref[0].astype(BF16))
        part = _dot(u.astype(BF16), w2_ref[0].astype(BF16))

        @pl.when(k == 0)
        def _():
            acc_ref[...] = part

        @pl.when(k > 0)
        def _():
            acc_ref[...] += part

        @pl.when(k == pl.num_programs(1) - 1)
        def _():
            _to_token_major(ys_ref, acc_ref[...])

    @pl.when((i >= nact_ref[0]) & (k == 0))
    def _():
        ys_ref[...] = jnp.zeros_like(ys_ref)


def _experts(xs, tile_expert, n_active, w1, w3, w2):
    d, f = w1.shape[1:]
    n_rows = xs.shape[0] * LANES // d
    tm, tf = FFN_ROWS, FFN_COLS
    n_k = f // tf
    n_tiles = n_rows // tm

    def k_of(i, k, nact):
        return jnp.where(i < nact[0], k, n_k - 1)

    return pl.pallas_call(
        _experts_kernel,
        grid_spec=pltpu.PrefetchScalarGridSpec(
            num_scalar_prefetch=2,
            grid=(n_tiles, n_k),
            in_specs=[
                pl.BlockSpec((tm * d // LANES, LANES), lambda i, k, exp, nact: (i, 0)),
                pl.BlockSpec((1, d, tf), lambda i, k, exp, nact: (exp[i], 0, k_of(i, k, nact))),
                pl.BlockSpec((1, d, tf), lambda i, k, exp, nact: (exp[i], 0, k_of(i, k, nact))),
                pl.BlockSpec((1, tf, d), lambda i, k, exp, nact: (exp[i], k_of(i, k, nact), 0)),
            ],
            out_specs=pl.BlockSpec((tm * d // LANES, LANES), lambda i, k, exp, nact: (i, 0)),
            scratch_shapes=[pltpu.VMEM((tm, d), BF16), pltpu.VMEM((tm, d), F32)],
        ),
        out_shape=jax.ShapeDtypeStruct(xs.shape, F32),
        compiler_params=pltpu.CompilerParams(
            dimension_semantics=("arbitrary", "arbitrary"), vmem_limit_bytes=VMEM_LIMIT),
        name="moe_experts",
    )(tile_expert, n_active, xs, w1, w3, w2)


def _combine_kernel(pos_ref, next_pos_ref, route_ref, x_ref, gpost_ref, gate_ref, ys_ref, o_ref,
                    *scratch):
    ts = x_ref.shape[0]
    n = ys_ref.shape[1]
    bufs = (scratch[:TOP_K], scratch[TOP_K:2 * TOP_K])
    sem = scratch[2 * TOP_K]
    i = pl.program_id(0)
    last = pl.num_programs(0) - 1

    def gather(rows_ref, parity):
        def issue(r, _):
            dst = pl.multiple_of(r * n, n)
            for slot in range(TOP_K):
                src = rows_ref[0, 0, slot * ts + r]
                pltpu.make_async_copy(ys_ref.at[src], bufs[parity][slot].at[pl.ds(dst, n)],
                                      sem.at[parity]).start(priority=slot)
            return 0

        lax.fori_loop(0, ts, issue, 0, unroll=4)

    def finish(parity):
        for slot in range(TOP_K):
            buf = bufs[parity][slot]
            pltpu.make_async_copy(buf, buf, sem.at[parity]).wait()
        w0 = ROUTE_LANES.index("weight0")
        route = route_ref[...]
        y = (route[:, w0:w0 + 1] * _from_token_major(bufs[parity][0], n)
             + route[:, w0 + 1:w0 + 2] * _from_token_major(bufs[parity][1], n))
        o_ref[...] = x_ref[...] + gate_ref[0] * _rms(y, gpost_ref[...])

    @pl.when(i == 0)
    def _():
        gather(pos_ref, 0)

    for parity in range(2):
        @pl.when(i % 2 == parity)
        def _(parity=parity):
            @pl.when(i < last)
            def _():
                gather(next_pos_ref, 1 - parity)

            finish(parity)


def _combine(ys, pos, route, x, g_post, gate):
    t, d = x.shape
    bsz = gate.shape[0]
    ts = COMBINE_ROWS
    steps_per_b = t // bsz // ts
    n_steps = t // ts
    return pl.pallas_call(
        _combine_kernel,
        grid=(n_steps,),
        in_specs=[
            pl.BlockSpec((1, 1, TOP_K * ts), lambda i: (i, 0, 0), memory_space=pltpu.SMEM),
            pl.BlockSpec((1, 1, TOP_K * ts), lambda i: (jnp.minimum(i + 1, n_steps - 1), 0, 0),
                         memory_space=pltpu.SMEM),
            pl.BlockSpec((ts, LANES), lambda i: (i, 0)),
            pl.BlockSpec((ts, d), lambda i: (i, 0)),
            pl.BlockSpec((1, d), lambda i: (0, 0)),
            pl.BlockSpec((1, 1, d), lambda i: (i // steps_per_b, 0, 0)),
            pl.BlockSpec(memory_space=pl.ANY),
        ],
        out_specs=pl.BlockSpec((ts, d), lambda i: (i, 0)),
        out_shape=jax.ShapeDtypeStruct((t, d), F32),
        scratch_shapes=[pltpu.VMEM((ts * d // LANES, LANES), F32)] * (2 * TOP_K)
        + [pltpu.SemaphoreType.DMA((2,))],
        compiler_params=pltpu.CompilerParams(
            dimension_semantics=("arbitrary",), vmem_limit_bytes=VMEM_LIMIT),
        name="moe_combine",
    )(pos, pos, route, x, g_post.reshape(1, d), gate, ys)


def _moe(x, g_pre, scale, shift, w_router, w1, w3, w2, g_post, gate):
    bsz, s, d = x.shape
    t = bsz * s
    n_experts = w_router.shape[1]
    tm = FFN_ROWS
    h, route, counts = _route(x, g_pre, scale, shift, w_router)
    route = route.reshape(t, LANES)

    lane_of = ROUTE_LANES.index
    expert = route[:, lane_of("expert0"):lane_of("expert0") + TOP_K].astype(jnp.int32)
    rank = route[:, lane_of("rank0"):lane_of("rank0") + TOP_K].astype(jnp.int32)
    counts = counts[0, :n_experts].astype(jnp.int32)
    tiles_per = (counts + tm - 1) // tm
    tile_end = jnp.cumsum(tiles_per)
    first_row = (tile_end - tiles_per) * tm
    n_active = tile_end[-1]
    n_tiles = TOP_K * t // tm + n_experts
    is_expert = expert[..., None] == jnp.arange(n_experts, dtype=jnp.int32)
    pos = jnp.sum(jnp.where(is_expert, first_row, 0), axis=-1) + rank

    def per_step(rows):
        return pos.reshape(t // rows, rows, TOP_K).transpose(0, 2, 1).reshape(t // rows, 1, -1)

    fill = jnp.concatenate([first_row + counts, tile_end * tm, n_active[None]]).astype(jnp.int32)
    idx = jnp.minimum(jnp.arange(n_tiles, dtype=jnp.int32), n_active - 1)
    tile_expert = jnp.sum(idx[:, None] >= tile_end[None, :], axis=1).astype(jnp.int32)

    token_major = (-1, d // LANES, LANES)
    xs = _scatter(h.reshape(token_major), per_step(ROUTE_ROWS), fill, n_tiles * tm)
    ys = _experts(xs.reshape(-1, LANES), tile_expert, n_active.reshape(1), w1, w3, w2)
    out = _combine(ys.reshape(token_major), per_step(COMBINE_ROWS), route, x.reshape(t, d),
                   g_post, gate.reshape(bsz, 1, d))
    return out.reshape(bsz, s, d)


def kernel(x, c, w_mod, b_mod, g_mix_pre, g_mix_post, g_ffn_pre, g_ffn_post, w_in, w_proj_sb,
           w_proj_moba, w_out, rel_bias, w1_dense, w3_dense, w2_dense, w_router, w1_moe, w3_moe,
           w2_moe):
    depth = w_mod.shape[0]
    d = x.shape[-1]
    qkv = 3 * W_BRANCH
    col = jnp.arange(w_in.shape[-1])
    is_q = (col < W_BRANCH) | ((col >= qkv) & (col < qkv + W_BRANCH))
    col_scale = jnp.where(is_q, HEAD_DIM ** -0.5, 1.0).astype(F32)

    mod = _modulation(c, w_mod, b_mod)
    bias_tiles = _moba_bias_tiles(rel_bias)
    for l in range(depth):
        sh_mix, sc_mix, gt_mix, sh_ffn, sc_ffn, gt_ffn = jnp.split(mod[l], N_MOD, axis=-1)
        proj = _inproj(x, g_mix_pre[l], sc_mix, sh_mix, (w_in[l] * col_scale).astype(BF16))
        o_a = _stick_breaking(proj, 0)
        o_b = _moba(proj, qkv // LANES, bias_tiles)
        x = _merge(o_a, o_b, proj, 2 * qkv // d, x, w_proj_sb[l].astype(BF16),
                   w_proj_moba[l].astype(BF16), w_out[l].astype(BF16), g_mix_post[l], gt_mix)
        j = l // 2
        if l % 2 == 0:
            x = _ffn(x, g_ffn_pre[l], sc_ffn, sh_ffn, w1_dense[j], w3_dense[j], w2_dense[j],
                     g_ffn_post[l], gt_ffn)
        else:
            x = _moe(x, g_ffn_pre[l], sc_ffn, sh_ffn, w_router[j], w1_moe[j], w3_moe[j],
                     w2_moe[j], g_ffn_post[l], gt_ffn)
    return x
```
